```python
import math
import jax, jax.numpy as jnp
from jax import lax
import numpy as np

D_MODEL = 1024
BATCH = 8
SEQ = 4096
DEPTH = 2

DA_HEADS = 4
DA_HEAD_DIM = 64
HG_HEADS = 4
HG_DK = 128
HG_DV = 128
HG_CHUNK = 64
LRU_WIDTH = 512
LRU_BLOCKS = 8
CONV_WIDTH = 4
LRU_C = 8.0
SB_HEADS = 8
SB_HEAD_DIM = 64
Q_BLOCK = 128
N_EXPERTS = 16
N_GROUPS = 4
E_PER_GROUP = N_EXPERTS // N_GROUPS
TOP_K = 2
D_FF = 512

DA_QK = DA_HEADS * 2 * DA_HEAD_DIM
DA_V = DA_HEADS * 2 * DA_HEAD_DIM
HG_W = HG_HEADS * HG_DK
HG_VW = HG_HEADS * HG_DV
EVEN_SPLITS = [DA_QK, DA_QK, DA_V, HG_W, HG_W, HG_VW, HG_VW]
EVEN_IN = sum(EVEN_SPLITS)
EVEN_OUT = DA_V + HG_VW
SB_W = SB_HEADS * SB_HEAD_DIM
ODD_SPLITS = [LRU_WIDTH, LRU_WIDTH, SB_W, SB_W, SB_W]
ODD_IN = sum(ODD_SPLITS)
ODD_OUT = LRU_WIDTH + SB_W
DEEPNORM_ALPHA = (2.0 * DEPTH) ** 0.25
DEEPNORM_BETA = (8.0 * DEPTH) ** -0.25

kernel_name = "hybrid_diffattn_hgrn2_rglru_stickbreak_grouped_moe"


def _split_idx(widths):
    return [int(v) for v in np.cumsum(widths)[:-1]]


def layer_norm(x, g, b, eps=1e-5):
    xf = x.astype(jnp.float32)
    mu = jnp.mean(xf, axis=-1, keepdims=True)
    var = jnp.mean(jnp.square(xf - mu), axis=-1, keepdims=True)
    return ((xf - mu) * lax.rsqrt(var + eps)).astype(x.dtype) * g + b


def rms_norm(x, g, eps=1e-6):
    xf = x.astype(jnp.float32)
    return (xf * lax.rsqrt(jnp.mean(xf * xf, axis=-1, keepdims=True) + eps)).astype(x.dtype) * g


def diff_attention(q, k, v, lam, gain, lam_init):
    B, H, T, _, dh = q.shape
    nb = T // Q_BLOCK
    scale = dh ** -0.5
    slopes = 2.0 ** (-8.0 * jnp.arange(1, H + 1, dtype=jnp.float32) / H)
    kpos = jnp.arange(T)
    qb = q.reshape(B, H, nb, Q_BLOCK, 2, dh).transpose(2, 0, 1, 3, 4, 5)
    starts = jnp.arange(nb) * Q_BLOCK

    def block(args):
        qi, start = args
        s = jnp.einsum('bhqcd,bhkcd->bhcqk', qi, k).astype(jnp.float32) * scale
        dist = ((start + jnp.arange(Q_BLOCK))[:, None] - kpos[None, :]).astype(jnp.float32)
        s = s - slopes[None, :, None, None, None] * dist[None, None, None]
        s = jnp.where((dist >= 0)[None, None, None], s, -jnp.inf)
        p = jax.nn.softmax(s, axis=-1)
        a = p[:, :, 0] - lam * p[:, :, 1]
        return jnp.einsum('bhqk,bhkv->bhqv', a.astype(v.dtype), v)

    o = lax.map(block, (qb, starts))
    o = o.transpose(1, 0, 3, 2, 4).reshape(B, T, H, 2 * dh)
    return rms_norm(o, gain) * (1.0 - lam_init)


def hgrn2_chunked(q, log_f, k, v):
    B, H, T, dk = q.shape
    dv = v.shape[-1]
    n = T // HG_CHUNK
    q = q.reshape(B, H, n, HG_CHUNK, dk)
    log_f = log_f.reshape(B, H, n, HG_CHUNK, dk)
    k = k.reshape(B, H, n, HG_CHUNK, dk)
    v = v.reshape(B, H, n, HG_CHUNK, dv)
    b = jnp.cumsum(log_f, axis=3)
    b_mid = b[:, :, :, HG_CHUNK // 2 - 1:HG_CHUNK // 2]
    b_last = b[:, :, :, -1:]
    causal = jnp.tril(jnp.ones((HG_CHUNK, HG_CHUNK), dtype=bool))
    att = jnp.einsum('bhncd,bhnsd->bhncs', q * jnp.exp(b - b_mid), k * jnp.exp(b_mid - b))
    att = jnp.where(causal, att, 0.0)
    o_intra = jnp.einsum('bhncs,bhnsv->bhncv', att, v)
    dS = jnp.einsum('bhncd,bhncv->bhndv', k * jnp.exp(b_last - b), v)
    decay = jnp.exp(b_last[:, :, :, 0])

    def step(S, inp):
        d, ds = inp
        return d[..., None] * S + ds, S

    S0 = jnp.zeros((B, H, dk, dv), dS.dtype)
    _, S_prev = lax.scan(step, S0, (jnp.moveaxis(decay, 2, 0), jnp.moveaxis(dS, 2, 0)))
    S_prev = jnp.moveaxis(S_prev, 0, 2)
    o_inter = jnp.einsum('bhncd,bhndv->bhncv', q * jnp.exp(b), S_prev)
    return (o_intra + o_inter).reshape(B, H, T, dv)


def causal_depthwise_conv(x, w, b):
    C = x.shape[-1]
    W = w.shape[0]
    y = lax.conv_general_dilated(x, w[:, None, :].astype(x.dtype), window_strides=(1,),
                                 padding=[(W - 1, 0)], dimension_numbers=('NWC', 'WIO', 'NWC'),
                                 feature_group_count=C)
    return y + b


def rg_lru(x, w_a, b_a, w_x, b_x, lam):
    B, T, C = x.shape
    xf = x.astype(jnp.float32)
    xb = xf.reshape(B, T, LRU_BLOCKS, C // LRU_BLOCKS)
    r = jax.nn.sigmoid(jnp.einsum('btgi,gij->btgj', xb, w_a).reshape(B, T, C) + b_a)
    i = jax.nn.sigmoid(jnp.einsum('btgi,gij->btgj', xb, w_x).reshape(B, T, C) + b_x)
    log_a = -LRU_C * r * jax.nn.softplus(-lam.astype(jnp.float32))
    a = jnp.exp(log_a)
    u = jnp.sqrt(jnp.maximum(-jnp.expm1(2.0 * log_a), 1e-12)) * (i * xf)

    def combine(e, l):
        a1, h1 = e
        a2, h2 = l
        return a1 * a2, a2 * h1 + h2

    _, h = lax.associative_scan(combine, (a, u), axis=1)
    return h.astype(x.dtype)


def stick_breaking_attention(q, k, v):
    B, H, T, d = q.shape
    nb = T // Q_BLOCK
    kpos = jnp.arange(T)
    qb = q.reshape(B, H, nb, Q_BLOCK, d).transpose(2, 0, 1, 3, 4)
    starts = jnp.arange(nb) * Q_BLOCK
    scale = d ** -0.5

    def block(args):
        qi, start = args
        z = jnp.einsum('bhqd,bhkd->bhqk', qi, k).astype(jnp.float32) * scale
        strict = (kpos[None, :] < (start + jnp.arange(Q_BLOCK))[:, None])[None, None]
        log_1m = jnp.where(strict, jax.nn.log_sigmoid(-z), 0.0)
        after = lax.cumsum(log_1m, axis=3, reverse=True) - log_1m
        a = jnp.where(strict, jnp.exp(jax.nn.log_sigmoid(z) + after), 0.0)
        return jnp.einsum('bhqk,bhkd->bhqd', a.astype(v.dtype), v)

    o = lax.map(block, (qb, starts))
    return o.transpose(1, 0, 3, 2, 4).reshape(B, T, H * d)


def even_mixer(h, w_in, w_out, lam_vecs, diff_gain, hg_gain, lb, lam_init):
    B, T, _ = h.shape
    proj = h @ w_in
    qa, ka, va, qh, fh, ih, gh = jnp.split(proj, _split_idx(EVEN_SPLITS), axis=-1)
    qa = qa.reshape(B, T, DA_HEADS, 2, DA_HEAD_DIM).transpose(0, 2, 1, 3, 4)
    ka = ka.reshape(B, T, DA_HEADS, 2, DA_HEAD_DIM).transpose(0, 2, 1, 3, 4)
    va = va.reshape(B, T, DA_HEADS, 2 * DA_HEAD_DIM).transpose(0, 2, 1, 3)
    lv = lam_vecs.astype(jnp.float32)
    lam = jnp.exp(jnp.sum(lv[0] * lv[1])) - jnp.exp(jnp.sum(lv[2] * lv[3])) + lam_init
    oa = diff_attention(qa, ka, va, lam, diff_gain, lam_init).reshape(B, T, DA_V)
    fl = fh.astype(jnp.float32)
    f = lb + (1.0 - lb) * jax.nn.sigmoid(fl)
    kh = (1.0 - lb) * jax.nn.sigmoid(-fl)
    heads = lambda t: t.reshape(B, T, HG_HEADS, -1).transpose(0, 2, 1, 3)
    ob = hgrn2_chunked(heads(jax.nn.silu(qh.astype(jnp.float32))), heads(jnp.log(f)), heads(kh),
                       heads(ih.astype(jnp.float32)))
    ob = ob.transpose(0, 2, 1, 3).astype(h.dtype)
    ob = rms_norm(ob, hg_gain) * jax.nn.silu(gh.reshape(B, T, HG_HEADS, HG_DV))
    return jnp.concatenate([oa, ob.reshape(B, T, HG_VW)], axis=-1) @ w_out


def odd_mixer(h, w_in, w_out, conv_w, conv_b, w_a, b_a, w_x, b_x, lam):
    B, T, _ = h.shape
    proj = h @ w_in
    xr, gr, qs, ks, vs = jnp.split(proj, _split_idx(ODD_SPLITS), axis=-1)
    xr = causal_depthwise_conv(xr, conv_w, conv_b)
    yr = jax.nn.gelu(gr) * rg_lru(xr, w_a, b_a, w_x, b_x, lam)
    heads = lambda t: t.reshape(B, T, SB_HEADS, SB_HEAD_DIM).transpose(0, 2, 1, 3)
    ys = stick_breaking_attention(heads(qs), heads(ks), heads(vs))
    return jnp.concatenate([yr, ys], axis=-1) @ w_out


def grouped_moe(h, router_w, router_b, w_gate, w_up, w_down):
    B, T, D = h.shape
    xt = h.reshape(B * T, D)
    logits = (xt @ router_w).astype(jnp.float32) + router_b
    probs = jax.nn.softmax(logits, axis=-1)
    pg = probs.reshape(-1, N_GROUPS, E_PER_GROUP)
    group_score = jnp.sum(lax.top_k(pg, TOP_K)[0], axis=-1)
    g_sel = jnp.argmax(group_score, axis=-1)
    in_group = jax.nn.one_hot(g_sel, N_GROUPS, dtype=jnp.float32)[:, :, None] > 0
    masked = jnp.where(in_group, pg, -1.0).reshape(-1, N_EXPERTS)
    top_p, top_i = lax.top_k(masked, TOP_K)
    weights = top_p / jnp.sum(top_p, axis=-1, keepdims=True)
    gates = jnp.sum(jax.nn.one_hot(top_i, N_EXPERTS, dtype=jnp.float32) * weights[..., None], axis=1)
    y = jnp.zeros_like(xt)
    for e in range(N_EXPERTS):
        he = jax.nn.silu(xt @ w_gate[e]) * (xt @ w_up[e])
        y = y + gates[:, e:e + 1].astype(xt.dtype) * (he @ w_down[e])
    return y.reshape(B, T, D)


def setup_inputs(seed: int = 0) -> dict:
    key = jax.random.key(seed)
    ks = jax.random.split(key, 32)
    n_even = (DEPTH + 1) // 2
    n_odd = DEPTH // 2
    nrm = lambda k, shape, s: jax.random.normal(k, shape, jnp.float32) * s
    gb = LRU_WIDTH // LRU_BLOCKS
    u = jax.random.uniform(ks[20], (n_odd, LRU_WIDTH), jnp.float32, 0.9, 0.999)
    s = u ** (1.0 / LRU_C)
    lru_lambda = jnp.log(s) - jnp.log1p(-s)
    return {
        "x": nrm(ks[0], (BATCH, SEQ, D_MODEL), 1.0),
        "c": nrm(ks[1], (BATCH, D_MODEL), 1.0),
        "ada_w": nrm(ks[2], (DEPTH, D_MODEL, 6 * D_MODEL), 0.1 * D_MODEL ** -0.5),
        "ada_b": nrm(ks[3], (DEPTH, 6 * D_MODEL), 0.01),
        "ln_g": 1.0 + nrm(ks[4], (DEPTH, 2, D_MODEL), 0.02),
        "ln_b": nrm(ks[5], (DEPTH, 2, D_MODEL), 0.02),
        "even_w_in": nrm(ks[6], (n_even, D_MODEL, EVEN_IN), D_MODEL ** -0.5),
        "even_w_out": nrm(ks[7], (n_even, EVEN_OUT, D_MODEL), DEEPNORM_BETA * EVEN_OUT ** -0.5),
        "diff_lambda": nrm(ks[8], (n_even, 4, DA_HEAD_DIM), 0.1),
        "diff_gain": 1.0 + nrm(ks[9], (n_even, 2 * DA_HEAD_DIM), 0.02),
        "hgrn_gamma": nrm(ks[10], (DEPTH + 1, HG_W), 0.1),
        "hgrn_gain": 1.0 + nrm(ks[11], (n_even, HG_DV), 0.02),
        "odd_w_in": nrm(ks[12], (n_odd, D_MODEL, ODD_IN), D_MODEL ** -0.5),
        "odd_w_out": nrm(ks[13], (n_odd, ODD_OUT, D_MODEL), DEEPNORM_BETA * ODD_OUT ** -0.5),
        "conv_w": nrm(ks[14], (n_odd, CONV_WIDTH, LRU_WIDTH), CONV_WIDTH ** -0.5),
        "conv_b": nrm(ks[15], (n_odd, LRU_WIDTH), 0.01),
        "lru_wa": nrm(ks[16], (n_odd, LRU_BLOCKS, gb, gb), gb ** -0.5),
        "lru_ba": nrm(ks[17], (n_odd, LRU_WIDTH), 0.01),
        "lru_wx": nrm(ks[18], (n_odd, LRU_BLOCKS, gb, gb), gb ** -0.5),
        "lru_bx": nrm(ks[19], (n_odd, LRU_WIDTH), 0.01),
        "lru_lambda": lru_lambda,
        "router_w": nrm(ks[21], (D_MODEL, N_EXPERTS), D_MODEL ** -0.5),
        "router_b": nrm(ks[22], (N_EXPERTS,), 0.01),
        "moe_w_gate": nrm(ks[23], (DEPTH, N_EXPERTS, D_MODEL, D_FF), D_MODEL ** -0.5),
        "moe_w_up": nrm(ks[24], (DEPTH, N_EXPERTS, D_MODEL, D_FF), D_MODEL ** -0.5),
        "moe_w_down": nrm(ks[25], (DEPTH, N_EXPERTS, D_FF, D_MODEL), DEEPNORM_BETA * D_FF ** -0.5),
    }


def reference(x, c, ada_w, ada_b, ln_g, ln_b, even_w_in, even_w_out, diff_lambda, diff_gain,
              hgrn_gamma, hgrn_gain, odd_w_in, odd_w_out, conv_w, conv_b, lru_wa, lru_ba, lru_wx,
              lru_bx, lru_lambda, router_w, router_b, moe_w_gate, moe_w_up, moe_w_down):
    lower_bounds = jnp.cumsum(jax.nn.softmax(hgrn_gamma.astype(jnp.float32), axis=0), axis=0)
    cond = jax.nn.silu(c)
    for l in range(DEPTH):
        mod = (cond @ ada_w[l] + ada_b[l])[:, None, :]
        sh1, sc1, g1, sh2, sc2, g2 = jnp.split(mod, 6, axis=-1)
        h = x * (1.0 + sc1) + sh1
        if l % 2 == 0:
            j = l // 2
            lam_init = 0.8 - 0.6 * math.exp(-0.3 * l)
            y = even_mixer(h, even_w_in[j], even_w_out[j], diff_lambda[j], diff_gain[j],
                           hgrn_gain[j], lower_bounds[l], lam_init)
        else:
            j = l // 2
            y = odd_mixer(h, odd_w_in[j], odd_w_out[j], conv_w[j], conv_b[j], lru_wa[j], lru_ba[j],
                          lru_wx[j], lru_bx[j], lru_lambda[j])
        x = layer_norm(DEEPNORM_ALPHA * x + (1.0 + g1) * y, ln_g[l, 0], ln_b[l, 0])
        h = x * (1.0 + sc2) + sh2
        y = grouped_moe(h, router_w, router_b, moe_w_gate[l], moe_w_up[l], moe_w_down[l])
        x = layer_norm(DEEPNORM_ALPHA * x + (1.0 + g2) * y, ln_g[l, 1], ln_b[l, 1])
    return x
```

```python
import functools
import math

import jax
import jax.numpy as jnp
from jax import lax
from jax.experimental import pallas as pl
from jax.experimental.pallas import tpu as pltpu

F32 = jnp.float32
BF16 = jnp.bfloat16

DA_HEADS = 4
DA_HEAD_DIM = 64
HG_HEADS = 4
HG_DK = 128
HG_CHUNK = 64
LRU_WIDTH = 512
LRU_BLOCKS = 8
CONV_WIDTH = 4
LRU_C = 8.0
SB_HEADS = 8
SB_HEAD_DIM = 64
N_EXPERTS = 16
N_GROUPS = 4
E_PER_GROUP = N_EXPERTS // N_GROUPS
D_FF = 512

LANES = 128
NEG_BIG = -1e30
VMEM_LIMIT = 56 * 1024 * 1024

TM_PROJ = 512
TQ_ATT = 512
TQ_SB = 512
TK_SB = 256
T_HG = 512
T_LRU = 256
TM_MOE = 1024


def _params(sem):
    return pltpu.CompilerParams(dimension_semantics=sem, vmem_limit_bytes=VMEM_LIMIT)


def _sigmoid(x):
    return 1.0 / (1.0 + jnp.exp(-x))


def _dot(a, b):
    return jnp.dot(a, b, preferred_element_type=F32)


def _dot_nt(a, b):
    return lax.dot_general(a, b, (((1,), (1,)), ((), ())), preferred_element_type=F32)


def _split3(x):
    hi = x.astype(BF16)
    r1 = x - hi.astype(F32)
    mid = r1.astype(BF16)
    lo = (r1 - mid.astype(F32)).astype(BF16)
    return hi, mid, lo


def _ada_kernel(c_ref, w_ref, b_ref, o_ref):
    c = c_ref[...]
    cond = c * _sigmoid(c)
    hi, mid, _ = _split3(cond)
    w = w_ref[...].astype(BF16)
    o_ref[...] = _dot(hi, w) + _dot(mid, w) + b_ref[...]


def _ada_mod(c, ada_w, ada_b):
    depth, d, d6 = ada_w.shape
    bsz = c.shape[0]
    n_col = d6 // d
    return pl.pallas_call(
        _ada_kernel,
        grid=(depth, n_col),
        in_specs=[
            pl.BlockSpec((bsz, d), lambda l, j: (0, 0)),
            pl.BlockSpec((None, d, d), lambda l, j: (l, 0, j)),
            pl.BlockSpec((None, 1, d), lambda l, j: (l, 0, j)),
        ],
        out_specs=pl.BlockSpec((None, bsz, d), lambda l, j: (l, 0, j)),
        out_shape=jax.ShapeDtypeStruct((depth, bsz, d6), F32),
        compiler_params=_params(("arbitrary", "arbitrary")),
        name="ada_mod",
    )(c, ada_w, ada_b.reshape(depth, 1, d6))


def _inproj_kernel(x_ref, mod_ref, w_ref, o_ref, *, col_chunk):
    sh = mod_ref[0:1, :]
    sc = mod_ref[1:2, :]
    h = (x_ref[...] * (1.0 + sc) + sh).astype(BF16)
    for j in range(o_ref.shape[1] // col_chunk):
        cols = slice(j * col_chunk, (j + 1) * col_chunk)
        o_ref[:, cols] = _dot(h, w_ref[:, cols]).astype(o_ref.dtype)


def _inproj(x, mod_l, w_bf16):
    bsz, t, d = x.shape
    width = w_bf16.shape[1]
    tm = min(TM_PROJ, t)
    return pl.pallas_call(
        functools.partial(_inproj_kernel, col_chunk=512),
        grid=(bsz, t // tm),
        in_specs=[
            pl.BlockSpec((None, tm, d), lambda b, i: (b, i, 0)),
            pl.BlockSpec((None, 6, d), lambda b, i: (b, 0, 0)),
            pl.BlockSpec((d, width), lambda b, i: (0, 0)),
        ],
        out_specs=pl.BlockSpec((None, tm, width), lambda b, i: (b, i, 0)),
        out_shape=jax.ShapeDtypeStruct((bsz, t, width), BF16),
        compiler_params=_params(("arbitrary", "arbitrary")),
        name="inproj",
    )(x, mod_l, w_bf16)


def _diffattn_kernel(q_ref, k_ref, v_ref, lam_ref, gain_ref, o_ref, m_sc, l_sc, acc_sc,
                     *, tile, lam_init):
    h = pl.program_id(1)
    qi = pl.program_id(2)
    dh = DA_HEAD_DIM

    lane = lax.broadcasted_iota(jnp.int32, (1, 2 * dh), 1)
    q = q_ref[...] * jnp.asarray(dh ** -0.5, BF16)
    q_halves = (jnp.where(lane < dh, q, jnp.zeros_like(q)),
                jnp.where(lane >= dh, q, jnp.zeros_like(q)))

    hf = jnp.full((1, 1), h + 1, jnp.int32).astype(F32)
    slope = jnp.exp2(hf * (-8.0 / DA_HEADS))
    col = lax.broadcasted_iota(jnp.int32, (1, tile), 1)

    m_sc[...] = jnp.full(m_sc.shape, NEG_BIG, F32)
    l_sc[...] = jnp.zeros(l_sc.shape, F32)
    acc_sc[...] = jnp.zeros(acc_sc.shape, F32)

    def sweep(kt_idx, masked):
        ks = pl.multiple_of(kt_idx * tile, tile)
        kt = k_ref[pl.ds(ks, tile), :]
        vt = v_ref[pl.ds(ks, tile), :]
        bias = (col + (kt_idx - qi) * tile).astype(F32) * slope
        if masked:
            row2 = lax.broadcasted_iota(jnp.int32, (tile, tile), 0)
            col2 = lax.broadcasted_iota(jnp.int32, (tile, tile), 1)
            keep = col2 <= row2
        for c in range(2):
            s = _dot_nt(q_halves[c], kt) + bias
            if masked:
                s = jnp.where(keep, s, NEG_BIG)
            m_old = m_sc[c]
            m_new = jnp.maximum(m_old, jnp.max(s, axis=1, keepdims=True))
            p = jnp.exp(s - m_new)
            alpha = jnp.exp(m_old - m_new)
            l_sc[c] = alpha * l_sc[c] + jnp.sum(p, axis=1, keepdims=True)
            acc_sc[c] = alpha * acc_sc[c] + _dot(p.astype(BF16), vt)
            m_sc[c] = m_new

    def body(kt_idx, carry):
        sweep(kt_idx, False)
        return carry

    lax.fori_loop(0, qi, body, 0)
    sweep(qi, True)

    lv = lam_ref[...].astype(F32)
    dots = jnp.sum(lv[0:1, :] * lv[1:2, :], axis=1, keepdims=True)
    dots2 = jnp.sum(lv[2:3, :] * lv[3:4, :], axis=1, keepdims=True)
    lam = jnp.exp(dots) - jnp.exp(dots2) + lam_init
    o = acc_sc[0] / l_sc[0] - lam * (acc_sc[1] / l_sc[1])
    ms = jnp.mean(o * o, axis=1, keepdims=True)
    o = o * lax.rsqrt(ms + 1e-6) * gain_ref[...] * (1.0 - lam_init)
    o_ref[...] = o.astype(o_ref.dtype)


def _diff_attention(proj, diff_lambda, diff_gain, lam_init):
    bsz, t, _ = proj.shape
    tile = min(TQ_ATT, t)
    hd = 2 * DA_HEAD_DIM
    kern = functools.partial(_diffattn_kernel, tile=tile, lam_init=lam_init)
    return pl.pallas_call(
        kern,
        grid=(bsz, DA_HEADS, t // tile),
        in_specs=[
            pl.BlockSpec((None, tile, hd), lambda b, h, i: (b, i, h)),
            pl.BlockSpec((None, t, hd), lambda b, h, i: (b, 0, DA_HEADS + h)),
            pl.BlockSpec((None, t, hd), lambda b, h, i: (b, 0, 2 * DA_HEADS + h)),
            pl.BlockSpec((4, DA_HEAD_DIM), lambda b, h, i: (0, 0)),
            pl.BlockSpec((1, hd), lambda b, h, i: (0, 0)),
        ],
        out_specs=pl.BlockSpec((None, tile, hd), lambda b, h, i: (b, i, h)),
        out_shape=jax.ShapeDtypeStruct((bsz, t, DA_HEADS * hd), BF16),
        scratch_shapes=[
            pltpu.VMEM((2, tile, 1), F32),
            pltpu.VMEM((2, tile, 1), F32),
            pltpu.VMEM((2, tile, hd), F32),
        ],
        compiler_params=_params(("arbitrary", "arbitrary", "arbitrary")),
        name="diff_attn",
    )(proj, proj, proj, diff_lambda, diff_gain.reshape(1, hd))


def _hgrn_kernel(q_ref, f_ref, i_ref, g_ref, gamma_ref, gain_ref, o_ref, st_sc, *, layer):
    @pl.when(pl.program_id(2) == 0)
    def _():
        st_sc[...] = jnp.zeros(st_sc.shape, F32)

    gam = gamma_ref[...].astype(F32)
    e = jnp.exp(gam - jnp.max(gam, axis=0, keepdims=True))
    sm = e / jnp.sum(e, axis=0, keepdims=True)
    lb = jnp.sum(sm[0:layer + 1, :], axis=0, keepdims=True)

    c = HG_CHUNK
    row = lax.broadcasted_iota(jnp.int32, (c, c), 0)
    col = lax.broadcasted_iota(jnp.int32, (c, c), 1)
    tril = col <= row
    tril_bf = jnp.where(tril, 1.0, 0.0).astype(BF16)
    gain = gain_ref[...]

    for n in range(q_ref.shape[0] // c):
        rows = slice(n * c, (n + 1) * c)
        fl = f_ref[rows, :].astype(F32)
        qh = q_ref[rows, :].astype(F32)
        v = i_ref[rows, :]
        gh = g_ref[rows, :].astype(F32)
        sig = _sigmoid(fl)
        logf = jnp.log(lb + (1.0 - lb) * sig)
        kk = (1.0 - lb) * (1.0 - sig)
        hi, mid, lo = _split3(logf)
        b = _dot(tril_bf, hi) + _dot(tril_bf, mid) + _dot(tril_bf, lo)
        b_mid = b[c // 2 - 1:c // 2, :]
        b_last = b[c - 1:c, :]
        qs = qh * _sigmoid(qh)
        att = _dot_nt((qs * jnp.exp(b - b_mid)).astype(BF16),
                      (kk * jnp.exp(b_mid - b)).astype(BF16))
        att = jnp.where(tril, att, 0.0)
        o_intra = _dot(att.astype(BF16), v)
        kd = (kk * jnp.exp(b_last - b)).astype(BF16)
        ds_t = _dot(v.T, kd)
        st = st_sc[...]
        o_inter = _dot_nt((qs * jnp.exp(b)).astype(BF16), st.astype(BF16))
        st_sc[...] = st * jnp.exp(b_last) + ds_t
        o = o_intra + o_inter
        ms = jnp.mean(o * o, axis=1, keepdims=True)
        o = o * lax.rsqrt(ms + 1e-6) * gain * (gh * _sigmoid(gh))
        o_ref[rows, :] = o.astype(o_ref.dtype)


def _hgrn2(proj, hgrn_gamma, hgrn_gain, layer):
    bsz, t, _ = proj.shape
    tt = min(T_HG, t)
    base = 3 * DA_HEADS
    spec = lambda k: pl.BlockSpec((None, tt, HG_DK), lambda b, h, i: (b, i, base + k * HG_HEADS + h))
    return pl.pallas_call(
        functools.partial(_hgrn_kernel, layer=layer),
        grid=(bsz, HG_HEADS, t // tt),
        in_specs=[
            spec(0), spec(1), spec(2), spec(3),
            pl.BlockSpec((hgrn_gamma.shape[0], HG_DK), lambda b, h, i: (0, h)),
            pl.BlockSpec((1, HG_DK), lambda b, h, i: (0, 0)),
        ],
        out_specs=pl.BlockSpec((None, tt, HG_DK), lambda b, h, i: (b, i, h)),
        out_shape=jax.ShapeDtypeStruct((bsz, t, HG_HEADS * HG_DK), BF16),
        scratch_shapes=[pltpu.VMEM((HG_DK, HG_DK), F32)],
        compiler_params=_params(("arbitrary", "arbitrary", "arbitrary")),
        name="hgrn2",
    )(proj, proj, proj, proj, hgrn_gamma, hgrn_gain.reshape(1, HG_DK))


def _route(logits_t):
    mx = jnp.max(logits_t, axis=0, keepdims=True)
    ex = jnp.exp(logits_t - mx)
    probs = ex / jnp.sum(ex, axis=0, keepdims=True)
    p = [probs[e:e + 1, :] for e in range(N_EXPERTS)]
    g = E_PER_GROUP
    scores = []
    for gi in range(N_GROUPS):
        pg = p[gi * g:(gi + 1) * g]
        best = None
        for a in range(g):
            for b in range(a + 1, g):
                pair = pg[a] + pg[b]
                best = pair if best is None else jnp.maximum(best, pair)
        scores.append(best)
    gates = []
    for gi in range(N_GROUPS):
        sel = None
        for gj in range(N_GROUPS):
            if gj == gi:
                continue
            cond = (scores[gi] > scores[gj]) if gj < gi else (scores[gi] >= scores[gj])
            sel = cond if sel is None else (sel & cond)
        pg = p[gi * g:(gi + 1) * g]
        chosen = []
        for a in range(g):
            rank = jnp.zeros_like(pg[a])
            for b in range(g):
                if b == a:
                    continue
                ahead = (pg[b] >= pg[a]) if b < a else (pg[b] > pg[a])
                rank = rank + jnp.where(ahead, 1.0, 0.0)
            chosen.append(sel & (rank < 2.0))
        denom = None
        for a in range(g):
            term = jnp.where(chosen[a], pg[a], 0.0)
            denom = term if denom is None else denom + term
        for a in range(g):
            gates.append(jnp.where(chosen[a], pg[a] / denom, 0.0))
    return jnp.concatenate(gates, axis=0)


def _outproj_kernel(a_ref, b_ref, x_ref, mod_ref, w_ref, lng_ref, lnb_ref, rwt_ref, rb_ref,
                    xo_ref, h_ref, gate_ref, *, alpha):
    half = a_ref.shape[1]
    y = _dot(a_ref[...], w_ref[0:half, :]) + _dot(b_ref[...], w_ref[half:2 * half, :])
    g1 = mod_ref[2:3, :]
    sh2 = mod_ref[3:4, :]
    sc2 = mod_ref[4:5, :]
    r = alpha * x_ref[...] + (1.0 + g1) * y
    mu = jnp.mean(r, axis=1, keepdims=True)
    rc = r - mu
    var = jnp.mean(rc * rc, axis=1, keepdims=True)
    xn = rc * lax.rsqrt(var + 1e-5) * lng_ref[...] + lnb_ref[...]
    xo_ref[...] = xn
    h2 = xn * (1.0 + sc2) + sh2
    h_ref[...] = h2.astype(BF16)
    h_hi, h_mid, _ = _split3(h2)
    rw = rwt_ref[...]
    w_hi, w_mid, _ = _split3(rw)
    logits_t = _dot_nt(w_hi, h_hi) + _dot_nt(w_hi, h_mid) + _dot_nt(w_mid, h_hi) + rb_ref[...]
    gates_t = _route(logits_t)
    pad = jnp.zeros((LANES - N_EXPERTS, gates_t.shape[1]), F32)
    gate_ref[...] = jnp.concatenate([gates_t, pad], axis=0).T


def _outproj(a, b, x, mod_l, w_bf16, ln_g, ln_b, router_w, router_b, alpha):
    bsz, t, d = x.shape
    half = a.shape[2]
    tm = min(TM_PROJ, t)
    tok = lambda width: pl.BlockSpec((None, tm, width), lambda bi, i: (bi, i, 0))
    full = lambda r, c: pl.BlockSpec((r, c), lambda bi, i: (0, 0))
    return pl.pallas_call(
        functools.partial(_outproj_kernel, alpha=alpha),
        grid=(bsz, t // tm),
        in_specs=[
            tok(half), tok(half), tok(d),
            pl.BlockSpec((None, 6, d), lambda bi, i: (bi, 0, 0)),
            full(2 * half, d), full(1, d), full(1, d), full(N_EXPERTS, d), full(N_EXPERTS, 1),
        ],
        out_specs=[tok(d), tok(d), tok(LANES)],
        out_shape=[
            jax.ShapeDtypeStruct((bsz, t, d), F32),
            jax.ShapeDtypeStruct((bsz, t, d), BF16),
            jax.ShapeDtypeStruct((bsz, t, LANES), F32),
        ],
        compiler_params=_params(("arbitrary", "arbitrary")),
        name="outproj_ln_route",
    )(a, b, x, mod_l, w_bf16, ln_g.reshape(1, d), ln_b.reshape(1, d), router_w.T,
      router_b.reshape(N_EXPERTS, 1))


def _moe_kernel(h_ref, gate_ref, x_ref, mod_ref, wg_ref, wu_ref, wd_ref, lng_ref, lnb_ref,
                o_ref, acc_sc, *, alpha):
    e = pl.program_id(2)

    @pl.when(e == 0)
    def _():
        acc_sc[...] = jnp.zeros(acc_sc.shape, F32)

    h = h_ref[...]
    a = _dot(h, wg_ref[...])
    u = _dot(h, wu_ref[...])
    he = (a * _sigmoid(a) * u).astype(BF16)
    lane = lax.broadcasted_iota(jnp.int32, (1, LANES), 1)
    gcol = jnp.sum(jnp.where(lane == e, gate_ref[...], 0.0), axis=1, keepdims=True)
    acc_sc[...] += gcol * _dot(he, wd_ref[...])

    @pl.when(e == pl.num_programs(2) - 1)
    def _():
        g2 = mod_ref[5:6, :]
        r = alpha * x_ref[...] + (1.0 + g2) * acc_sc[...]
        mu = jnp.mean(r, axis=1, keepdims=True)
        rc = r - mu
        var = jnp.mean(rc * rc, axis=1, keepdims=True)
        o_ref[...] = rc * lax.rsqrt(var + 1e-5) * lng_ref[...] + lnb_ref[...]


def _moe(h2, gates, x, mod_l, wg, wu, wd, ln_g, ln_b, alpha):
    bsz, t, d = x.shape
    tm = min(TM_MOE, t)
    n_e, _, dff = wg.shape
    tok = lambda width: pl.BlockSpec((None, tm, width), lambda bi, i, e: (bi, i, 0))
    return pl.pallas_call(
        functools.partial(_moe_kernel, alpha=alpha),
        grid=(bsz, t // tm, n_e),
        in_specs=[
            tok(d), tok(LANES), tok(d),
            pl.BlockSpec((None, 6, d), lambda bi, i, e: (bi, 0, 0)),
            pl.BlockSpec((None, d, dff), lambda bi, i, e: (e, 0, 0)),
            pl.BlockSpec((None, d, dff), lambda bi, i, e: (e, 0, 0)),
            pl.BlockSpec((None, dff, d), lambda bi, i, e: (e, 0, 0)),
            pl.BlockSpec((1, d), lambda bi, i, e: (0, 0)),
            pl.BlockSpec((1, d), lambda bi, i, e: (0, 0)),
        ],
        out_specs=tok(d),
        out_shape=jax.ShapeDtypeStruct((bsz, t, d), F32),
        scratch_shapes=[pltpu.VMEM((tm, d), F32)],
        compiler_params=_params(("arbitrary", "arbitrary", "arbitrary")),
        name="moe_dense",
    )(h2, gates, x, mod_l, wg, wu, wd, ln_g.reshape(1, d), ln_b.reshape(1, d))


def _lru_kernel(x_ref, g_ref, cw_ref, cb_ref, wa_ref, ba_ref, wx_ref, bx_ref, lam_ref,
                o_ref, xpad_sc, h_sc):
    tt = x_ref.shape[0]
    pad = 8

    @pl.when(pl.program_id(1) == 0)
    def _():
        xpad_sc[0:pad, :] = jnp.zeros((pad, xpad_sc.shape[1]), F32)
        h_sc[...] = jnp.zeros(h_sc.shape, F32)

    xpad_sc[pad:pad + tt, :] = x_ref[...].astype(F32)
    xc = cb_ref[...] + jnp.zeros((tt, x_ref.shape[1]), F32)
    for j in range(CONV_WIDTH):
        off = pad - (CONV_WIDTH - 1) + j
        xc = xc + cw_ref[j:j + 1, :] * xpad_sc[off:off + tt, :]
    xpad_sc[0:pad, :] = xpad_sc[tt:tt + pad, :]

    xb = xc.astype(BF16)
    r = _sigmoid(_dot(xb, wa_ref[...]) + ba_ref[...])
    i = _sigmoid(_dot(xb, wx_ref[...]) + bx_ref[...])
    lam = lam_ref[...].astype(F32)
    softplus_neg = jnp.maximum(-lam, 0.0) + jnp.log(1.0 + jnp.exp(-jnp.abs(lam)))
    log_a = -LRU_C * r * softplus_neg
    a = jnp.exp(log_a)
    u = jnp.sqrt(jnp.maximum(1.0 - jnp.exp(2.0 * log_a), 1e-12)) * (i * xc)

    rowi = lax.broadcasted_iota(jnp.int32, (tt, 1), 0)
    d = 1
    while d < tt:
        a_sh = jnp.where(rowi >= d, pltpu.roll(a, d, 0), 1.0)
        u_sh = jnp.where(rowi >= d, pltpu.roll(u, d, 0), 0.0)
        u = u + a * u_sh
        a = a * a_sh
        d *= 2
    hcur = u + a * h_sc[...]
    h_sc[...] = hcur[tt - 1:tt, :]

    gr = g_ref[...].astype(F32)
    gelu = 0.5 * gr * (1.0 + jnp.tanh(0.7978845608028654 * (gr + 0.044715 * gr * gr * gr)))
    o_ref[...] = (gelu * hcur).astype(o_ref.dtype)


def _rg_lru(proj, conv_w, conv_b, wa_dense, ba, wx_dense, bx, lam):
    bsz, t, _ = proj.shape
    tt = min(T_LRU, t)
    w = LRU_WIDTH
    nblk = w // LANES
    row = lambda a: a.reshape(1, w)
    full = lambda r, c: pl.BlockSpec((r, c), lambda b, i: (0, 0))
    return pl.pallas_call(
        _lru_kernel,
        grid=(bsz, t // tt),
        in_specs=[
            pl.BlockSpec((None, tt, w), lambda b, i: (b, i, 0)),
            pl.BlockSpec((None, tt, w), lambda b, i: (b, i, 1)),
            full(CONV_WIDTH, w), full(1, w), full(w, w), full(1, w), full(w, w), full(1, w),
            full(1, w),
        ],
        out_specs=pl.BlockSpec((None, tt, w), lambda b, i: (b, i, 0)),
        out_shape=jax.ShapeDtypeStruct((bsz, t, w), BF16),
        scratch_shapes=[pltpu.VMEM((tt + 8, w), F32), pltpu.VMEM((1, w), F32)],
        compiler_params=_params(("arbitrary", "arbitrary")),
        name="rg_lru",
    )(proj, proj, conv_w, row(conv_b), wa_dense, row(ba), wx_dense, row(bx), row(lam))


def _sb_kernel(q_ref, k_ref, v_ref, o_ref, r_sc, acc_sc, *, tq, tk):
    qi = pl.program_id(2)
    d = SB_HEAD_DIM
    lane = lax.broadcasted_iota(jnp.int32, (1, 2 * d), 1)
    q = q_ref[...] * jnp.asarray(d ** -0.5, BF16)
    q_heads = (jnp.where(lane < d, q, jnp.zeros_like(q)),
               jnp.where(lane >= d, q, jnp.zeros_like(q)))
    rj = lax.broadcasted_iota(jnp.int32, (tk, tk), 0)
    cs = lax.broadcasted_iota(jnp.int32, (tk, tk), 1)
    upper = jnp.where(rj > cs, 1.0, 0.0).astype(BF16)

    r_sc[...] = jnp.zeros(r_sc.shape, F32)
    acc_sc[...] = jnp.zeros(acc_sc.shape, F32)
    per_q = tq // tk

    def sweep(ks, strict):
        kt = k_ref[pl.ds(ks, tk), :]
        vt = v_ref[pl.ds(ks, tk), :]
        for c in range(2):
            z = _dot_nt(q_heads[c], kt)
            sp = jnp.maximum(z, 0.0) + jnp.log(1.0 + jnp.exp(-jnp.abs(z)))
            log_1m = -sp
            if strict is not None:
                log_1m = jnp.where(strict, log_1m, 0.0)
            after = _dot(log_1m.astype(BF16), upper) + r_sc[c]
            w = jnp.exp(z - sp + after)
            if strict is not None:
                w = jnp.where(strict, w, 0.0)
            acc_sc[c] += _dot(w.astype(BF16), vt)
            r_sc[c] = after[:, 0:1] + log_1m[:, 0:1]

    for j in reversed(range(per_q)):
        ks = pl.multiple_of(qi * tq + j * tk, tk)
        rowp = lax.broadcasted_iota(jnp.int32, (tq, tk), 0)
        colp = lax.broadcasted_iota(jnp.int32, (tq, tk), 1) + j * tk
        sweep(ks, colp < rowp)

    def body(n, carry):
        ks = pl.multiple_of((qi * per_q - 1 - n) * tk, tk)
        sweep(ks, None)
        return carry

    lax.fori_loop(0, qi * per_q, body, 0)
    o_ref[...] = jnp.where(lane < d, acc_sc[0], acc_sc[1]).astype(o_ref.dtype)


def _sb_attention(proj):
    bsz, t, _ = proj.shape
    tq = min(TQ_SB, t)
    tk = min(TK_SB, tq)
    pairs = SB_HEADS // 2
    wblk = 2 * SB_HEAD_DIM
    base = 2 * LRU_WIDTH // wblk
    return pl.pallas_call(
        functools.partial(_sb_kernel, tq=tq, tk=tk),
        grid=(bsz, pairs, t // tq),
        in_specs=[
            pl.BlockSpec((None, tq, wblk), lambda b, h, i: (b, i, base + h)),
            pl.BlockSpec((None, t, wblk), lambda b, h, i: (b, 0, base + pairs + h)),
            pl.BlockSpec((None, t, wblk), lambda b, h, i: (b, 0, base + 2 * pairs + h)),
        ],
        out_specs=pl.BlockSpec((None, tq, wblk), lambda b, h, i: (b, i, h)),
        out_shape=jax.ShapeDtypeStruct((bsz, t, SB_HEADS * SB_HEAD_DIM), BF16),
        scratch_shapes=[pltpu.VMEM((2, tq, 1), F32), pltpu.VMEM((2, tq, wblk), F32)],
        compiler_params=_params(("arbitrary", "arbitrary", "arbitrary")),
        name="sb_attn",
    )(proj, proj, proj)


def _block_diag(w):
    g, n, _ = w.shape
    eye = jnp.eye(g, dtype=w.dtype)
    return (eye[:, None, :, None] * w[:, :, None, :]).reshape(g * n, g * n)


def kernel(x, c, ada_w, ada_b, ln_g, ln_b, even_w_in, even_w_out, diff_lambda, diff_gain, hgrn_gamma, hgrn_gain, odd_w_in, odd_w_out, conv_w, conv_b, lru_wa, lru_ba, lru_wx, lru_bx, lru_lambda, router_w, router_b, moe_w_gate, moe_w_up, moe_w_down):
    depth = ada_w.shape[0]
    bsz, t, d = x.shape
    alpha = (2.0 * depth) ** 0.25
    mod = _ada_mod(c, ada_w, ada_b).reshape(depth, bsz, 6, d)
    for l in range(depth):
        j = l // 2
        mod_l = mod[l]
        if l % 2 == 0:
            lam_init = 0.8 - 0.6 * math.exp(-0.3 * l)
            proj = _inproj(x, mod_l, even_w_in[j].astype(BF16))
            mix_a = _diff_attention(proj, diff_lambda[j], diff_gain[j], lam_init)
            mix_b = _hgrn2(proj, hgrn_gamma, hgrn_gain[j], l)
            w_out = even_w_out[j]
        else:
            proj = _inproj(x, mod_l, odd_w_in[j].astype(BF16))
            mix_a = _rg_lru(proj, conv_w[j], conv_b[j], _block_diag(lru_wa[j]).astype(BF16),
                            lru_ba[j], _block_diag(lru_wx[j]).astype(BF16), lru_bx[j],
                            lru_lambda[j])
            mix_b = _sb_attention(proj)
            w_out = odd_w_out[j]
        x, h2, gates = _outproj(mix_a, mix_b, x, mod_l, w_out.astype(BF16), ln_g[l, 0],
                                ln_b[l, 0], router_w, router_b, alpha)
        x = _moe(h2, gates, x, mod_l, moe_w_gate[l].astype(BF16), moe_w_up[l].astype(BF16),
                 moe_w_down[l].astype(BF16), ln_g[l, 1], ln_b[l, 1], alpha)
    return x
```

```python
import functools
import math

import jax
import jax.numpy as jnp
from jax import lax
from jax.experimental import pallas as pl
from jax.experimental.pallas import tpu as pltpu

F32 = jnp.float32
BF16 = jnp.bfloat16

DA_HEADS = 4
DA_HEAD_DIM = 64
HG_HEADS = 4
HG_DK = 128
HG_CHUNK = 64
LRU_WIDTH = 512
LRU_BLOCKS = 8
CONV_WIDTH = 4
LRU_C = 8.0
SB_HEADS = 8
SB_HEAD_DIM = 64
N_EXPERTS = 16
N_GROUPS = 4
E_PER_GROUP = N_EXPERTS // N_GROUPS
D_FF = 512

LANES = 128
NEG_BIG = -1e30
LOG2E = 1.4426950408889634
Q_PRESCALE = DA_HEAD_DIM ** -0.5 * LOG2E
PROJ_CHUNK = 512
VMEM_LIMIT = 56 * 1024 * 1024

TM_PROJ = 512
TQ_ATT = 512
T_SB = 512
T_HG = 512
T_LRU = 256
TM_MOE = 1024


def _params(sem):
    return pltpu.CompilerParams(dimension_semantics=sem, vmem_limit_bytes=VMEM_LIMIT)


def _sigmoid(x):
    return 1.0 / (1.0 + jnp.exp(-x))


def _dot(a, b):
    return jnp.dot(a, b, preferred_element_type=F32)


def _dot_nt(a, b):
    return lax.dot_general(a, b, (((1,), (1,)), ((), ())), preferred_element_type=F32)


def _split3(x):
    hi = x.astype(BF16)
    r1 = x - hi.astype(F32)
    mid = r1.astype(BF16)
    lo = (r1 - mid.astype(F32)).astype(BF16)
    return hi, mid, lo


def _ada_kernel(c_ref, w_ref, b_ref, o_ref):
    c = c_ref[...]
    cond = c * _sigmoid(c)
    hi, mid, _ = _split3(cond)
    w = w_ref[...].astype(BF16)
    o_ref[...] = _dot(hi, w) + _dot(mid, w) + b_ref[...]


def _ada_mod(c, ada_w, ada_b):
    depth, d, d6 = ada_w.shape
    bsz = c.shape[0]
    n_col = d6 // d
    return pl.pallas_call(
        _ada_kernel,
        grid=(depth, n_col),
        in_specs=[
            pl.BlockSpec((bsz, d), lambda l, j: (0, 0)),
            pl.BlockSpec((None, d, d), lambda l, j: (l, 0, j)),
            pl.BlockSpec((None, 1, d), lambda l, j: (l, 0, j)),
        ],
        out_specs=pl.BlockSpec((None, bsz, d), lambda l, j: (l, 0, j)),
        out_shape=jax.ShapeDtypeStruct((depth, bsz, d6), F32),
        compiler_params=_params(("arbitrary", "arbitrary")),
        name="ada_mod",
    )(c, ada_w, ada_b.reshape(depth, 1, d6))


def _inproj_kernel(x_ref, mod_ref, w_ref, o_ref, *, col_chunk, q_chunk):
    sh = mod_ref[0:1, :]
    sc = mod_ref[1:2, :]
    h = (x_ref[...] * (1.0 + sc) + sh).astype(BF16)
    for j in range(o_ref.shape[1] // col_chunk):
        cols = slice(j * col_chunk, (j + 1) * col_chunk)
        y = _dot(h, w_ref[:, cols])
        if j == q_chunk:
            y = y * Q_PRESCALE
        o_ref[:, cols] = y.astype(o_ref.dtype)


def _inproj(x, mod_l, w_bf16, q_chunk):
    bsz, t, d = x.shape
    width = w_bf16.shape[1]
    tm = min(TM_PROJ, t)
    return pl.pallas_call(
        functools.partial(_inproj_kernel, col_chunk=PROJ_CHUNK, q_chunk=q_chunk),
        grid=(bsz, t // tm),
        in_specs=[
            pl.BlockSpec((None, tm, d), lambda b, i: (b, i, 0)),
            pl.BlockSpec((None, 6, d), lambda b, i: (b, 0, 0)),
            pl.BlockSpec((d, width), lambda b, i: (0, 0)),
        ],
        out_specs=pl.BlockSpec((None, tm, width), lambda b, i: (b, i, 0)),
        out_shape=jax.ShapeDtypeStruct((bsz, t, width), BF16),
        compiler_params=_params(("arbitrary", "arbitrary")),
        name="inproj",
    )(x, mod_l, w_bf16)


def _diffattn_kernel(q_ref, k_ref, v_ref, lam_ref, gain_ref, o_ref, m_sc, acc_sc, s0_sc, s1_sc,
                     p0_sc, p1_sc, a0_sc, a1_sc, *, tile, lam_init):
    h = pl.program_id(1)
    qi = pl.program_id(2)
    dh = DA_HEAD_DIM
    hd = 2 * dh
    tk = tile // 2
    reps = tk // LANES
    s_bufs, p_bufs, a_bufs = (s0_sc, s1_sc), (p0_sc, p1_sc), (a0_sc, a1_sc)

    lane = lax.broadcasted_iota(jnp.int32, (1, hd), 1)
    q = q_ref[...]
    zero = jnp.zeros_like(q)
    q2 = jnp.concatenate([jnp.where(lane < dh, q, zero), jnp.where(lane >= dh, q, zero)], axis=0)

    hf = jnp.full((1, 1), h + 1, jnp.int32).astype(F32)
    slope = jnp.exp2(hf * (-8.0 / DA_HEADS)) * LOG2E
    col = lax.broadcasted_iota(jnp.int32, (1, tk), 1)
    ones = jnp.ones((tk, hd), BF16)

    m_sc[...] = jnp.full(m_sc.shape, NEG_BIG, F32)
    acc_sc[...] = jnp.zeros(acc_sc.shape, F32)

    def scores(j, slot):
        ks = pl.multiple_of(j * tk, tk)
        bias = (col + (j * tk - qi * tile)).astype(F32) * slope
        s_bufs[slot][...] = _dot_nt(q2, k_ref[pl.ds(ks, tk), :]) + bias

    def softmax(j, slot, masked):
        s = s_bufs[slot][...]
        if masked:
            rowp = lax.broadcasted_iota(jnp.int32, (tile, tk), 0) + qi * tile
            colp = lax.broadcasted_iota(jnp.int32, (tile, tk), 1) + j * tk
            keep = colp <= rowp
            s = jnp.where(jnp.concatenate([keep, keep], axis=0), s, NEG_BIG)
        m_old = m_sc[...]
        m_new = jnp.maximum(m_old, jnp.max(s, axis=1, keepdims=True))
        p_bufs[slot][...] = jnp.exp2(s - pltpu.repeat(m_new, reps, axis=1)).astype(BF16)
        a_bufs[slot][...] = jnp.exp2(m_old - m_new)
        m_sc[...] = m_new

    def values(j, slot):
        ks = pl.multiple_of(j * tk, tk)
        v_aug = jnp.concatenate([v_ref[pl.ds(ks, tk), :], ones], axis=1)
        acc_sc[...] = (pltpu.repeat(a_bufs[slot][...], 2, axis=1) * acc_sc[...]
                       + _dot(p_bufs[slot][...], v_aug))

    scores(0, 0)
    scores(1, 1)
    softmax(0, 0, True)

    @pl.when(qi > 0)
    def _():
        def body(i, carry):
            t = 2 * i
            scores(t, 0)
            softmax(t - 1, 1, False)
            values(t - 2, 0)
            scores(t + 1, 1)
            softmax(t, 0, False)
            values(t - 1, 1)
            return carry

        lax.fori_loop(1, qi, body, 0)
        t = 2 * qi
        scores(t, 0)
        softmax(t - 1, 1, False)
        values(t - 2, 0)
        scores(t + 1, 1)
        softmax(t, 0, True)
        values(t - 1, 1)

    softmax(2 * qi + 1, 1, True)
    values(2 * qi, 0)
    values(2 * qi + 1, 1)

    lv = lam_ref[...].astype(F32)
    dots = jnp.sum(lv[0:1, :] * lv[1:2, :], axis=1, keepdims=True)
    dots2 = jnp.sum(lv[2:3, :] * lv[3:4, :], axis=1, keepdims=True)
    lam = jnp.exp(dots) - jnp.exp(dots2) + lam_init
    acc = acc_sc[...]
    o0 = acc[0:tile, 0:hd] / acc[0:tile, hd:2 * hd]
    o1 = acc[tile:2 * tile, 0:hd] / acc[tile:2 * tile, hd:2 * hd]
    o = o0 - lam * o1
    ms = jnp.mean(o * o, axis=1, keepdims=True)
    o = o * lax.rsqrt(ms + 1e-6) * gain_ref[...] * (1.0 - lam_init)
    o_ref[...] = o.astype(o_ref.dtype)


def _diff_attention(proj, diff_lambda, diff_gain, lam_init):
    bsz, t, _ = proj.shape
    tile = min(TQ_ATT, t)
    hd = 2 * DA_HEAD_DIM
    kern = functools.partial(_diffattn_kernel, tile=tile, lam_init=lam_init)
    return pl.pallas_call(
        kern,
        grid=(bsz, DA_HEADS, t // tile),
        in_specs=[
            pl.BlockSpec((None, tile, hd), lambda b, h, i: (b, i, h)),
            pl.BlockSpec((None, t, hd), lambda b, h, i: (b, 0, DA_HEADS + h)),
            pl.BlockSpec((None, t, hd), lambda b, h, i: (b, 0, 2 * DA_HEADS + h)),
            pl.BlockSpec((4, DA_HEAD_DIM), lambda b, h, i: (0, 0)),
            pl.BlockSpec((1, hd), lambda b, h, i: (0, 0)),
        ],
        out_specs=pl.BlockSpec((None, tile, hd), lambda b, h, i: (b, i, h)),
        out_shape=jax.ShapeDtypeStruct((bsz, t, DA_HEADS * hd), BF16),
        scratch_shapes=[
            pltpu.VMEM((2 * tile, LANES), F32),
            pltpu.VMEM((2 * tile, 2 * hd), F32),
            pltpu.VMEM((2 * tile, tile // 2), F32),
            pltpu.VMEM((2 * tile, tile // 2), F32),
            pltpu.VMEM((2 * tile, tile // 2), BF16),
            pltpu.VMEM((2 * tile, tile // 2), BF16),
            pltpu.VMEM((2 * tile, LANES), F32),
            pltpu.VMEM((2 * tile, LANES), F32),
        ],
        compiler_params=_params(("arbitrary", "arbitrary", "arbitrary")),
        name="diff_attn",
    )(proj, proj, proj, diff_lambda, diff_gain.reshape(1, hd))


def _hgrn_kernel(q_ref, f_ref, i_ref, g_ref, gamma_ref, gain_ref, o_ref, st_sc, *, layer):
    @pl.when(pl.program_id(2) == 0)
    def _():
        st_sc[...] = jnp.zeros(st_sc.shape, F32)

    gam = gamma_ref[...].astype(F32)
    e = jnp.exp(gam - jnp.max(gam, axis=0, keepdims=True))
    sm = e / jnp.sum(e, axis=0, keepdims=True)
    lb = jnp.sum(sm[0:layer + 1, :], axis=0, keepdims=True)

    c = HG_CHUNK
    row = lax.broadcasted_iota(jnp.int32, (c, c), 0)
    col = lax.broadcasted_iota(jnp.int32, (c, c), 1)
    tril = col <= row
    tril_bf = jnp.where(tril, 1.0, 0.0).astype(BF16)
    gain = gain_ref[...]

    for n in range(q_ref.shape[0] // c):
        rows = slice(n * c, (n + 1) * c)
        fl = f_ref[rows, :].astype(F32)
        qh = q_ref[rows, :].astype(F32)
        v = i_ref[rows, :]
        gh = g_ref[rows, :].astype(F32)
        sig = _sigmoid(fl)
        logf = jnp.log(lb + (1.0 - lb) * sig)
        kk = (1.0 - lb) * (1.0 - sig)
        hi, mid, lo = _split3(logf)
        b = _dot(tril_bf, hi) + _dot(tril_bf, mid) + _dot(tril_bf, lo)
        b_mid = b[c // 2 - 1:c // 2, :]
        b_last = b[c - 1:c, :]
        qs = qh * _sigmoid(qh)
        att = _dot_nt((qs * jnp.exp(b - b_mid)).astype(BF16),
                      (kk * jnp.exp(b_mid - b)).astype(BF16))
        att = jnp.where(tril, att, 0.0)
        o_intra = _dot(att.astype(BF16), v)
        kd = (kk * jnp.exp(b_last - b)).astype(BF16)
        ds_t = _dot(v.T, kd)
        st = st_sc[...]
        o_inter = _dot_nt((qs * jnp.exp(b)).astype(BF16), st.astype(BF16))
        st_sc[...] = st * jnp.exp(b_last) + ds_t
        o = o_intra + o_inter
        ms = jnp.mean(o * o, axis=1, keepdims=True)
        o = o * lax.rsqrt(ms + 1e-6) * gain * (gh * _sigmoid(gh))
        o_ref[rows, :] = o.astype(o_ref.dtype)


def _hgrn2(proj, hgrn_gamma, hgrn_gain, layer):
    bsz, t, _ = proj.shape
    tt = min(T_HG, t)
    base = 3 * DA_HEADS
    spec = lambda k: pl.BlockSpec((None, tt, HG_DK), lambda b, h, i: (b, i, base + k * HG_HEADS + h))
    return pl.pallas_call(
        functools.partial(_hgrn_kernel, layer=layer),
        grid=(bsz, HG_HEADS, t // tt),
        in_specs=[
            spec(0), spec(1), spec(2), spec(3),
            pl.BlockSpec((hgrn_gamma.shape[0], HG_DK), lambda b, h, i: (0, h)),
            pl.BlockSpec((1, HG_DK), lambda b, h, i: (0, 0)),
        ],
        out_specs=pl.BlockSpec((None, tt, HG_DK), lambda b, h, i: (b, i, h)),
        out_shape=jax.ShapeDtypeStruct((bsz, t, HG_HEADS * HG_DK), BF16),
        scratch_shapes=[pltpu.VMEM((HG_DK, HG_DK), F32)],
        compiler_params=_params(("arbitrary", "arbitrary", "arbitrary")),
        name="hgrn2",
    )(proj, proj, proj, proj, hgrn_gamma, hgrn_gain.reshape(1, HG_DK))


def _route(logits_t):
    mx = jnp.max(logits_t, axis=0, keepdims=True)
    ex = jnp.exp(logits_t - mx)
    probs = ex / jnp.sum(ex, axis=0, keepdims=True)
    p = [probs[e:e + 1, :] for e in range(N_EXPERTS)]
    g = E_PER_GROUP
    scores = []
    for gi in range(N_GROUPS):
        pg = p[gi * g:(gi + 1) * g]
        best = None
        for a in range(g):
            for b in range(a + 1, g):
                pair = pg[a] + pg[b]
                best = pair if best is None else jnp.maximum(best, pair)
        scores.append(best)
    gates = []
    for gi in range(N_GROUPS):
        sel = None
        for gj in range(N_GROUPS):
            if gj == gi:
                continue
            cond = (scores[gi] > scores[gj]) if gj < gi else (scores[gi] >= scores[gj])
            sel = cond if sel is None else (sel & cond)
        pg = p[gi * g:(gi + 1) * g]
        chosen = []
        for a in range(g):
            rank = jnp.zeros_like(pg[a])
            for b in range(g):
                if b == a:
                    continue
                ahead = (pg[b] >= pg[a]) if b < a else (pg[b] > pg[a])
                rank = rank + jnp.where(ahead, 1.0, 0.0)
            chosen.append(sel & (rank < 2.0))
        denom = None
        for a in range(g):
            term = jnp.where(chosen[a], pg[a], 0.0)
            denom = term if denom is None else denom + term
        for a in range(g):
            gates.append(jnp.where(chosen[a], pg[a] / denom, 0.0))
    return jnp.concatenate(gates, axis=0)


def _outproj_kernel(a_ref, b_ref, x_ref, mod_ref, w_ref, lng_ref, lnb_ref, rwt_ref, rb_ref,
                    xo_ref, h_ref, gate_ref, *, alpha):
    half = a_ref.shape[1]
    y = _dot(a_ref[...], w_ref[0:half, :]) + _dot(b_ref[...], w_ref[half:2 * half, :])
    g1 = mod_ref[2:3, :]
    sh2 = mod_ref[3:4, :]
    sc2 = mod_ref[4:5, :]
    r = alpha * x_ref[...] + (1.0 + g1) * y
    mu = jnp.mean(r, axis=1, keepdims=True)
    rc = r - mu
    var = jnp.mean(rc * rc, axis=1, keepdims=True)
    xn = rc * lax.rsqrt(var + 1e-5) * lng_ref[...] + lnb_ref[...]
    xo_ref[...] = xn
    h2 = xn * (1.0 + sc2) + sh2
    h_ref[...] = h2.astype(BF16)
    h_hi, h_mid, _ = _split3(h2)
    rw = rwt_ref[...]
    w_hi, w_mid, _ = _split3(rw)
    logits_t = _dot_nt(w_hi, h_hi) + _dot_nt(w_hi, h_mid) + _dot_nt(w_mid, h_hi) + rb_ref[...]
    gates_t = _route(logits_t)
    pad = jnp.zeros((LANES - N_EXPERTS, gates_t.shape[1]), F32)
    gate_ref[...] = jnp.concatenate([gates_t, pad], axis=0).T


def _outproj(a, b, x, mod_l, w_bf16, ln_g, ln_b, router_w, router_b, alpha):
    bsz, t, d = x.shape
    half = a.shape[2]
    tm = min(TM_PROJ, t)
    tok = lambda width: pl.BlockSpec((None, tm, width), lambda bi, i: (bi, i, 0))
    full = lambda r, c: pl.BlockSpec((r, c), lambda bi, i: (0, 0))
    return pl.pallas_call(
        functools.partial(_outproj_kernel, alpha=alpha),
        grid=(bsz, t // tm),
        in_specs=[
            tok(half), tok(half), tok(d),
            pl.BlockSpec((None, 6, d), lambda bi, i: (bi, 0, 0)),
            full(2 * half, d), full(1, d), full(1, d), full(N_EXPERTS, d), full(N_EXPERTS, 1),
        ],
        out_specs=[tok(d), tok(d), tok(LANES)],
        out_shape=[
            jax.ShapeDtypeStruct((bsz, t, d), F32),
            jax.ShapeDtypeStruct((bsz, t, d), BF16),
            jax.ShapeDtypeStruct((bsz, t, LANES), F32),
        ],
        compiler_params=_params(("arbitrary", "arbitrary")),
        name="outproj_ln_route",
    )(a, b, x, mod_l, w_bf16, ln_g.reshape(1, d), ln_b.reshape(1, d), router_w.T,
      router_b.reshape(N_EXPERTS, 1))


def _moe_kernel(h_ref, gate_ref, x_ref, mod_ref, wg_ref, wu_ref, wd_ref, lng_ref, lnb_ref,
                o_ref, acc_sc, *, alpha):
    e = pl.program_id(2)

    @pl.when(e == 0)
    def _():
        acc_sc[...] = jnp.zeros(acc_sc.shape, F32)

    h = h_ref[...]
    a = _dot(h, wg_ref[...])
    u = _dot(h, wu_ref[...])
    he = (a * _sigmoid(a) * u).astype(BF16)
    lane = lax.broadcasted_iota(jnp.int32, (1, LANES), 1)
    gcol = jnp.sum(jnp.where(lane == e, gate_ref[...], 0.0), axis=1, keepdims=True)
    acc_sc[...] += gcol * _dot(he, wd_ref[...])

    @pl.when(e == pl.num_programs(2) - 1)
    def _():
        g2 = mod_ref[5:6, :]
        r = alpha * x_ref[...] + (1.0 + g2) * acc_sc[...]
        mu = jnp.mean(r, axis=1, keepdims=True)
        rc = r - mu
        var = jnp.mean(rc * rc, axis=1, keepdims=True)
        o_ref[...] = rc * lax.rsqrt(var + 1e-5) * lng_ref[...] + lnb_ref[...]


def _moe(h2, gates, x, mod_l, wg, wu, wd, ln_g, ln_b, alpha):
    bsz, t, d = x.shape
    tm = min(TM_MOE, t)
    n_e, _, dff = wg.shape
    tok = lambda width: pl.BlockSpec((None, tm, width), lambda bi, i, e: (bi, i, 0))
    return pl.pallas_call(
        functools.partial(_moe_kernel, alpha=alpha),
        grid=(bsz, t // tm, n_e),
        in_specs=[
            tok(d), tok(LANES), tok(d),
            pl.BlockSpec((None, 6, d), lambda bi, i, e: (bi, 0, 0)),
            pl.BlockSpec((None, d, dff), lambda bi, i, e: (e, 0, 0)),
            pl.BlockSpec((None, d, dff), lambda bi, i, e: (e, 0, 0)),
            pl.BlockSpec((None, dff, d), lambda bi, i, e: (e, 0, 0)),
            pl.BlockSpec((1, d), lambda bi, i, e: (0, 0)),
            pl.BlockSpec((1, d), lambda bi, i, e: (0, 0)),
        ],
        out_specs=tok(d),
        out_shape=jax.ShapeDtypeStruct((bsz, t, d), F32),
        scratch_shapes=[pltpu.VMEM((tm, d), F32)],
        compiler_params=_params(("arbitrary", "arbitrary", "arbitrary")),
        name="moe_dense",
    )(h2, gates, x, mod_l, wg, wu, wd, ln_g.reshape(1, d), ln_b.reshape(1, d))


def _lru_kernel(x_ref, g_ref, cw_ref, cb_ref, wa_ref, ba_ref, wx_ref, bx_ref, lam_ref,
                o_ref, xpad_sc, h_sc):
    tt = x_ref.shape[0]
    pad = 8

    @pl.when(pl.program_id(1) == 0)
    def _():
        xpad_sc[0:pad, :] = jnp.zeros((pad, xpad_sc.shape[1]), F32)
        h_sc[...] = jnp.zeros(h_sc.shape, F32)

    xpad_sc[pad:pad + tt, :] = x_ref[...].astype(F32)
    xc = cb_ref[...] + jnp.zeros((tt, x_ref.shape[1]), F32)
    for j in range(CONV_WIDTH):
        off = pad - (CONV_WIDTH - 1) + j
        xc = xc + cw_ref[j:j + 1, :] * xpad_sc[off:off + tt, :]
    xpad_sc[0:pad, :] = xpad_sc[tt:tt + pad, :]

    xb = xc.astype(BF16)
    r = _sigmoid(_dot(xb, wa_ref[...]) + ba_ref[...])
    i = _sigmoid(_dot(xb, wx_ref[...]) + bx_ref[...])
    lam = lam_ref[...].astype(F32)
    softplus_neg = jnp.maximum(-lam, 0.0) + jnp.log(1.0 + jnp.exp(-jnp.abs(lam)))
    log_a = -LRU_C * r * softplus_neg
    a = jnp.exp(log_a)
    u = jnp.sqrt(jnp.maximum(1.0 - jnp.exp(2.0 * log_a), 1e-12)) * (i * xc)

    rowi = lax.broadcasted_iota(jnp.int32, (tt, 1), 0)
    d = 1
    while d < tt:
        a_sh = jnp.where(rowi >= d, pltpu.roll(a, d, 0), 1.0)
        u_sh = jnp.where(rowi >= d, pltpu.roll(u, d, 0), 0.0)
        u = u + a * u_sh
        a = a * a_sh
        d *= 2
    hcur = u + a * h_sc[...]
    h_sc[...] = hcur[tt - 1:tt, :]

    gr = g_ref[...].astype(F32)
    gelu = 0.5 * gr * (1.0 + jnp.tanh(0.7978845608028654 * (gr + 0.044715 * gr * gr * gr)))
    o_ref[...] = (gelu * hcur).astype(o_ref.dtype)


def _rg_lru(proj, conv_w, conv_b, wa_dense, ba, wx_dense, bx, lam):
    bsz, t, _ = proj.shape
    tt = min(T_LRU, t)
    w = LRU_WIDTH
    nblk = w // LANES
    row = lambda a: a.reshape(1, w)
    full = lambda r, c: pl.BlockSpec((r, c), lambda b, i: (0, 0))
    return pl.pallas_call(
        _lru_kernel,
        grid=(bsz, t // tt),
        in_specs=[
            pl.BlockSpec((None, tt, w), lambda b, i: (b, i, 0)),
            pl.BlockSpec((None, tt, w), lambda b, i: (b, i, 1)),
            full(CONV_WIDTH, w), full(1, w), full(w, w), full(1, w), full(w, w), full(1, w),
            full(1, w),
        ],
        out_specs=pl.BlockSpec((None, tt, w), lambda b, i: (b, i, 0)),
        out_shape=jax.ShapeDtypeStruct((bsz, t, w), BF16),
        scratch_shapes=[pltpu.VMEM((tt + 8, w), F32), pltpu.VMEM((1, w), F32)],
        compiler_params=_params(("arbitrary", "arbitrary")),
        name="rg_lru",
    )(proj, proj, conv_w, row(conv_b), wa_dense, row(ba), wx_dense, row(bx), row(lam))


def _sb_kernel(q_ref, k_ref, v_ref, o_ref, r_sc, acc_sc, z0_sc, z1_sc, lb0_sc, lb1_sc, l0_sc,
               l1_sc, w0_sc, w1_sc, *, tile):
    qi = pl.program_id(2)
    d = SB_HEAD_DIM
    tk = tile // 2
    lane = lax.broadcasted_iota(jnp.int32, (1, 2 * d), 1)
    q = q_ref[...]
    zero = jnp.zeros_like(q)
    q2 = jnp.concatenate([jnp.where(lane < d, q, zero), jnp.where(lane >= d, q, zero)], axis=0)
    rj = lax.broadcasted_iota(jnp.int32, (tk, tk), 0)
    cs = lax.broadcasted_iota(jnp.int32, (tk, tk), 1)
    upper = jnp.where(rj > cs, 1.0, 0.0).astype(BF16)

    r_sc[...] = jnp.zeros(r_sc.shape, F32)
    acc_sc[...] = jnp.zeros(acc_sc.shape, F32)
    z_bufs, lb_bufs, l_bufs, w_bufs = (z0_sc, z1_sc), (lb0_sc, lb1_sc), (l0_sc, l1_sc), (w0_sc, w1_sc)
    n_sub = 2 * qi + 2

    def key_start(j):
        return pl.multiple_of((n_sub - 1 - j) * tk, tk)

    def strict_mask(j):
        rowp = lax.broadcasted_iota(jnp.int32, (tile, tk), 0)
        colp = lax.broadcasted_iota(jnp.int32, (tile, tk), 1) + (1 - j) * tk
        strict = colp < rowp
        return jnp.concatenate([strict, strict], axis=0)

    def logits(j, slot):
        z_bufs[slot][...] = _dot_nt(q2, k_ref[pl.ds(key_start(j), tk), :])

    def gates(j, slot, masked):
        z = z_bufs[slot][...]
        log_1m = jnp.log(1.0 + jnp.exp2(-jnp.abs(z))) * (-LOG2E) - jnp.maximum(z, 0.0)
        lb_bufs[slot][...] = z + log_1m
        if masked:
            log_1m = jnp.where(strict_mask(j), log_1m, 0.0)
        l_bufs[slot][...] = log_1m.astype(BF16)

    def weights(j, slot, masked):
        log_1m = l_bufs[slot][...]
        after = _dot(log_1m, upper) + r_sc[...]
        w = jnp.exp2(lb_bufs[slot][...] + after)
        if masked:
            w = jnp.where(strict_mask(j), w, 0.0)
        w_bufs[slot][...] = w.astype(BF16)
        r_sc[...] = after[:, 0:1] + log_1m[:, 0:1].astype(F32)

    def values(j, slot):
        acc_sc[...] += _dot(w_bufs[slot][...], v_ref[pl.ds(key_start(j), tk), :])

    @pl.when(qi == 0)
    def _():
        for j in range(2):
            logits(j, j)
            gates(j, j, True)
            weights(j, j, True)
            values(j, j)

    @pl.when(qi > 0)
    def _():
        logits(0, 0)
        logits(1, 1)
        gates(0, 0, True)
        logits(2, 0)
        gates(1, 1, True)
        weights(0, 0, True)
        logits(3, 1)
        gates(2, 0, False)
        weights(1, 1, True)
        values(0, 0)

        def body(i, carry):
            t = 2 * i
            logits(t, 0)
            gates(t - 1, 1, False)
            weights(t - 2, 0, False)
            values(t - 3, 1)
            logits(t + 1, 1)
            gates(t, 0, False)
            weights(t - 1, 1, False)
            values(t - 2, 0)
            return carry

        lax.fori_loop(2, qi + 1, body, 0)
        n = n_sub
        gates(n - 1, 1, False)
        weights(n - 2, 0, False)
        values(n - 3, 1)
        weights(n - 1, 1, False)
        values(n - 2, 0)
        values(n - 1, 1)

    acc = acc_sc[...]
    o_ref[...] = jnp.where(lane < d, acc[0:tile, :], acc[tile:2 * tile, :]).astype(o_ref.dtype)


def _sb_attention(proj):
    bsz, t, _ = proj.shape
    tq = min(T_SB, t)
    pairs = SB_HEADS // 2
    wblk = 2 * SB_HEAD_DIM
    base = 2 * LRU_WIDTH // wblk
    return pl.pallas_call(
        functools.partial(_sb_kernel, tile=tq),
        grid=(bsz, pairs, t // tq),
        in_specs=[
            pl.BlockSpec((None, tq, wblk), lambda b, h, i: (b, i, base + h)),
            pl.BlockSpec((None, t, wblk), lambda b, h, i: (b, 0, base + pairs + h)),
            pl.BlockSpec((None, t, wblk), lambda b, h, i: (b, 0, base + 2 * pairs + h)),
        ],
        out_specs=pl.BlockSpec((None, tq, wblk), lambda b, h, i: (b, i, h)),
        out_shape=jax.ShapeDtypeStruct((bsz, t, SB_HEADS * SB_HEAD_DIM), BF16),
        scratch_shapes=[pltpu.VMEM((2 * tq, 1), F32), pltpu.VMEM((2 * tq, wblk), F32)]
        + [pltpu.VMEM((2 * tq, tq // 2), F32)] * 4 + [pltpu.VMEM((2 * tq, tq // 2), BF16)] * 4,
        compiler_params=_params(("arbitrary", "arbitrary", "arbitrary")),
        name="sb_attn",
    )(proj, proj, proj)


def _block_diag(w):
    g, n, _ = w.shape
    eye = jnp.eye(g, dtype=w.dtype)
    return (eye[:, None, :, None] * w[:, :, None, :]).reshape(g * n, g * n)


def kernel(x, c, ada_w, ada_b, ln_g, ln_b, even_w_in, even_w_out, diff_lambda, diff_gain, hgrn_gamma, hgrn_gain, odd_w_in, odd_w_out, conv_w, conv_b, lru_wa, lru_ba, lru_wx, lru_bx, lru_lambda, router_w, router_b, moe_w_gate, moe_w_up, moe_w_down):
    depth = ada_w.shape[0]
    bsz, t, d = x.shape
    alpha = (2.0 * depth) ** 0.25
    mod = _ada_mod(c, ada_w, ada_b).reshape(depth, bsz, 6, d)
    for l in range(depth):
        j = l // 2
        mod_l = mod[l]
        if l % 2 == 0:
            lam_init = 0.8 - 0.6 * math.exp(-0.3 * l)
            proj = _inproj(x, mod_l, even_w_in[j].astype(BF16), q_chunk=0)
            mix_a = _diff_attention(proj, diff_lambda[j], diff_gain[j], lam_init)
            mix_b = _hgrn2(proj, hgrn_gamma, hgrn_gain[j], l)
            w_out = even_w_out[j]
        else:
            proj = _inproj(x, mod_l, odd_w_in[j].astype(BF16), q_chunk=2 * LRU_WIDTH // PROJ_CHUNK)
            mix_a = _rg_lru(proj, conv_w[j], conv_b[j], _block_diag(lru_wa[j]).astype(BF16),
                            lru_ba[j], _block_diag(lru_wx[j]).astype(BF16), lru_bx[j],
                            lru_lambda[j])
            mix_b = _sb_attention(proj)
            w_out = odd_w_out[j]
        x, h2, gates = _outproj(mix_a, mix_b, x, mod_l, w_out.astype(BF16), ln_g[l, 0],
                                ln_b[l, 0], router_w, router_b, alpha)
        x = _moe(h2, gates, x, mod_l, moe_w_gate[l].astype(BF16), moe_w_up[l].astype(BF16),
                 moe_w_down[l].astype(BF16), ln_g[l, 1], ln_b[l, 1], alpha)
    return x
```

```python
import functools
import math

import jax
import jax.numpy as jnp
from jax import lax
from jax.experimental import pallas as pl
from jax.experimental.pallas import tpu as pltpu

F32 = jnp.float32
BF16 = jnp.bfloat16

DA_HEADS = 4
DA_HEAD_DIM = 64
HG_HEADS = 4
HG_DK = 128
HG_CHUNK = 64
LRU_WIDTH = 512
LRU_BLOCKS = 8
CONV_WIDTH = 4
LRU_C = 8.0
SB_HEADS = 8
SB_HEAD_DIM = 64
N_EXPERTS = 16
N_GROUPS = 4
E_PER_GROUP = N_EXPERTS // N_GROUPS
D_FF = 512

LANES = 128
NEG_BIG = -1e30
LOG2E = 1.4426950408889634
Q_PRESCALE = DA_HEAD_DIM ** -0.5 * LOG2E
PROJ_CHUNK = 512
VMEM_LIMIT = 56 * 1024 * 1024

TM_PROJ = 512
TQ_ATT = 512
T_SB = 512
T_HG = 512
T_LRU = 256
T_BLK = 1024
MOE_ROWS_MAIN = 320
MOE_ROWS_EXTRA = 128


def _params(sem):
    return pltpu.CompilerParams(dimension_semantics=sem, vmem_limit_bytes=VMEM_LIMIT)


def _sigmoid(x):
    return 1.0 / (1.0 + jnp.exp(-x))


def _dot(a, b):
    return jnp.dot(a, b, preferred_element_type=F32)


def _dot_nt(a, b):
    return lax.dot_general(a, b, (((1,), (1,)), ((), ())), preferred_element_type=F32)


def _onehot(mask):
    return jnp.where(mask, 1.0, 0.0).astype(BF16)


def _split3(x):
    hi = x.astype(BF16)
    r1 = x - hi.astype(F32)
    mid = r1.astype(BF16)
    lo = (r1 - mid.astype(F32)).astype(BF16)
    return hi, mid, lo


def _ada_kernel(c_ref, w_ref, b_ref, o_ref):
    c = c_ref[...]
    cond = c * _sigmoid(c)
    hi, mid, _ = _split3(cond)
    w = w_ref[...].astype(BF16)
    o_ref[...] = _dot(hi, w) + _dot(mid, w) + b_ref[...]


def _ada_mod(c, ada_w, ada_b):
    depth, d, d6 = ada_w.shape
    bsz = c.shape[0]
    n_col = d6 // d
    return pl.pallas_call(
        _ada_kernel,
        grid=(depth, n_col),
        in_specs=[
            pl.BlockSpec((bsz, d), lambda l, j: (0, 0)),
            pl.BlockSpec((None, d, d), lambda l, j: (l, 0, j)),
            pl.BlockSpec((None, 1, d), lambda l, j: (l, 0, j)),
        ],
        out_specs=pl.BlockSpec((None, bsz, d), lambda l, j: (l, 0, j)),
        out_shape=jax.ShapeDtypeStruct((depth, bsz, d6), F32),
        compiler_params=_params(("arbitrary", "arbitrary")),
        name="ada_mod",
    )(c, ada_w, ada_b.reshape(depth, 1, d6))


def _inproj_kernel(x_ref, mod_ref, w_ref, o_ref, *, col_chunk, q_chunk):
    sh = mod_ref[0:1, :]
    sc = mod_ref[1:2, :]
    h = (x_ref[...] * (1.0 + sc) + sh).astype(BF16)
    for j in range(o_ref.shape[1] // col_chunk):
        cols = slice(j * col_chunk, (j + 1) * col_chunk)
        y = _dot(h, w_ref[:, cols])
        if j == q_chunk:
            y = y * Q_PRESCALE
        o_ref[:, cols] = y.astype(o_ref.dtype)


def _inproj(x, mod_l, w_bf16, q_chunk):
    bsz, t, d = x.shape
    width = w_bf16.shape[1]
    tm = min(TM_PROJ, t)
    return pl.pallas_call(
        functools.partial(_inproj_kernel, col_chunk=PROJ_CHUNK, q_chunk=q_chunk),
        grid=(bsz, t // tm),
        in_specs=[
            pl.BlockSpec((None, tm, d), lambda b, i: (b, i, 0)),
            pl.BlockSpec((None, 6, d), lambda b, i: (b, 0, 0)),
            pl.BlockSpec((d, width), lambda b, i: (0, 0)),
        ],
        out_specs=pl.BlockSpec((None, tm, width), lambda b, i: (b, i, 0)),
        out_shape=jax.ShapeDtypeStruct((bsz, t, width), BF16),
        compiler_params=_params(("arbitrary", "arbitrary")),
        name="inproj",
    )(x, mod_l, w_bf16)


def _diffattn_kernel(q_ref, k_ref, v_ref, lam_ref, gain_ref, o_ref, m_sc, acc_sc, s0_sc, s1_sc,
                     p0_sc, p1_sc, a0_sc, a1_sc, *, tile, lam_init):
    h = pl.program_id(1)
    qi = pl.program_id(2)
    dh = DA_HEAD_DIM
    hd = 2 * dh
    tk = tile // 2
    reps = tk // LANES
    s_bufs, p_bufs, a_bufs = (s0_sc, s1_sc), (p0_sc, p1_sc), (a0_sc, a1_sc)

    lane = lax.broadcasted_iota(jnp.int32, (1, hd), 1)
    q = q_ref[...]
    zero = jnp.zeros_like(q)
    q2 = jnp.concatenate([jnp.where(lane < dh, q, zero), jnp.where(lane >= dh, q, zero)], axis=0)

    hf = jnp.full((1, 1), h + 1, jnp.int32).astype(F32)
    slope = jnp.exp2(hf * (-8.0 / DA_HEADS)) * LOG2E
    col = lax.broadcasted_iota(jnp.int32, (1, tk), 1)
    ones = jnp.ones((tk, hd), BF16)

    m_sc[...] = jnp.full(m_sc.shape, NEG_BIG, F32)
    acc_sc[...] = jnp.zeros(acc_sc.shape, F32)

    def scores(j, slot):
        ks = pl.multiple_of(j * tk, tk)
        bias = (col + (j * tk - qi * tile)).astype(F32) * slope
        s_bufs[slot][...] = _dot_nt(q2, k_ref[pl.ds(ks, tk), :]) + bias

    def softmax(j, slot, masked):
        s = s_bufs[slot][...]
        if masked:
            rowp = lax.broadcasted_iota(jnp.int32, (tile, tk), 0) + qi * tile
            colp = lax.broadcasted_iota(jnp.int32, (tile, tk), 1) + j * tk
            keep = colp <= rowp
            s = jnp.where(jnp.concatenate([keep, keep], axis=0), s, NEG_BIG)
        m_old = m_sc[...]
        m_new = jnp.maximum(m_old, jnp.max(s, axis=1, keepdims=True))
        p_bufs[slot][...] = jnp.exp2(s - jnp.concatenate([m_new] * reps, axis=1)).astype(BF16)
        a_bufs[slot][...] = jnp.exp2(m_old - m_new)
        m_sc[...] = m_new

    def values(j, slot):
        ks = pl.multiple_of(j * tk, tk)
        v_aug = jnp.concatenate([v_ref[pl.ds(ks, tk), :], ones], axis=1)
        alpha = a_bufs[slot][...]
        acc_sc[...] = (jnp.concatenate([alpha, alpha], axis=1) * acc_sc[...]
                       + _dot(p_bufs[slot][...], v_aug))

    scores(0, 0)
    scores(1, 1)
    softmax(0, 0, True)

    @pl.when(qi > 0)
    def _():
        def body(i, carry):
            t = 2 * i
            scores(t, 0)
            softmax(t - 1, 1, False)
            values(t - 2, 0)
            scores(t + 1, 1)
            softmax(t, 0, False)
            values(t - 1, 1)
            return carry

        lax.fori_loop(1, qi, body, 0)
        t = 2 * qi
        scores(t, 0)
        softmax(t - 1, 1, False)
        values(t - 2, 0)
        scores(t + 1, 1)
        softmax(t, 0, True)
        values(t - 1, 1)

    softmax(2 * qi + 1, 1, True)
    values(2 * qi, 0)
    values(2 * qi + 1, 1)

    lv = lam_ref[...].astype(F32)
    dots = jnp.sum(lv[0:1, :] * lv[1:2, :], axis=1, keepdims=True)
    dots2 = jnp.sum(lv[2:3, :] * lv[3:4, :], axis=1, keepdims=True)
    lam = jnp.exp(dots) - jnp.exp(dots2) + lam_init
    acc = acc_sc[...]
    o0 = acc[0:tile, 0:hd] / acc[0:tile, hd:2 * hd]
    o1 = acc[tile:2 * tile, 0:hd] / acc[tile:2 * tile, hd:2 * hd]
    o = o0 - lam * o1
    ms = jnp.mean(o * o, axis=1, keepdims=True)
    o = o * lax.rsqrt(ms + 1e-6) * gain_ref[...] * (1.0 - lam_init)
    o_ref[...] = o.astype(o_ref.dtype)


def _diff_attention(proj, diff_lambda, diff_gain, lam_init):
    bsz, t, _ = proj.shape
    tile = min(TQ_ATT, t)
    hd = 2 * DA_HEAD_DIM
    kern = functools.partial(_diffattn_kernel, tile=tile, lam_init=lam_init)
    return pl.pallas_call(
        kern,
        grid=(bsz, DA_HEADS, t // tile),
        in_specs=[
            pl.BlockSpec((None, tile, hd), lambda b, h, i: (b, i, h)),
            pl.BlockSpec((None, t, hd), lambda b, h, i: (b, 0, DA_HEADS + h)),
            pl.BlockSpec((None, t, hd), lambda b, h, i: (b, 0, 2 * DA_HEADS + h)),
            pl.BlockSpec((4, DA_HEAD_DIM), lambda b, h, i: (0, 0)),
            pl.BlockSpec((1, hd), lambda b, h, i: (0, 0)),
        ],
        out_specs=pl.BlockSpec((None, tile, hd), lambda b, h, i: (b, i, h)),
        out_shape=jax.ShapeDtypeStruct((bsz, t, DA_HEADS * hd), BF16),
        scratch_shapes=[
            pltpu.VMEM((2 * tile, LANES), F32),
            pltpu.VMEM((2 * tile, 2 * hd), F32),
            pltpu.VMEM((2 * tile, tile // 2), F32),
            pltpu.VMEM((2 * tile, tile // 2), F32),
            pltpu.VMEM((2 * tile, tile // 2), BF16),
            pltpu.VMEM((2 * tile, tile // 2), BF16),
            pltpu.VMEM((2 * tile, LANES), F32),
            pltpu.VMEM((2 * tile, LANES), F32),
        ],
        compiler_params=_params(("arbitrary", "arbitrary", "arbitrary")),
        name="diff_attn",
    )(proj, proj, proj, diff_lambda, diff_gain.reshape(1, hd))


def _hgrn_kernel(q_ref, f_ref, i_ref, g_ref, gamma_ref, gain_ref, o_ref, st_sc, *, layer):
    @pl.when(pl.program_id(2) == 0)
    def _():
        st_sc[...] = jnp.zeros(st_sc.shape, F32)

    gam = gamma_ref[...].astype(F32)
    e = jnp.exp(gam - jnp.max(gam, axis=0, keepdims=True))
    sm = e / jnp.sum(e, axis=0, keepdims=True)
    lb = jnp.sum(sm[0:layer + 1, :], axis=0, keepdims=True)

    c = HG_CHUNK
    row = lax.broadcasted_iota(jnp.int32, (c, c), 0)
    col = lax.broadcasted_iota(jnp.int32, (c, c), 1)
    tril = col <= row
    tril_bf = jnp.where(tril, 1.0, 0.0).astype(BF16)
    gain = gain_ref[...]

    for n in range(q_ref.shape[0] // c):
        rows = slice(n * c, (n + 1) * c)
        fl = f_ref[rows, :].astype(F32)
        qh = q_ref[rows, :].astype(F32)
        v = i_ref[rows, :]
        gh = g_ref[rows, :].astype(F32)
        sig = _sigmoid(fl)
        logf = jnp.log(lb + (1.0 - lb) * sig)
        kk = (1.0 - lb) * (1.0 - sig)
        hi, mid, lo = _split3(logf)
        b = _dot(tril_bf, hi) + _dot(tril_bf, mid) + _dot(tril_bf, lo)
        b_mid = b[c // 2 - 1:c // 2, :]
        b_last = b[c - 1:c, :]
        qs = qh * _sigmoid(qh)
        att = _dot_nt((qs * jnp.exp(b - b_mid)).astype(BF16),
                      (kk * jnp.exp(b_mid - b)).astype(BF16))
        att = jnp.where(tril, att, 0.0)
        o_intra = _dot(att.astype(BF16), v)
        kd = (kk * jnp.exp(b_last - b)).astype(BF16)
        ds_t = _dot(v.T, kd)
        st = st_sc[...]
        o_inter = _dot_nt((qs * jnp.exp(b)).astype(BF16), st.astype(BF16))
        st_sc[...] = st * jnp.exp(b_last) + ds_t
        o = o_intra + o_inter
        ms = jnp.mean(o * o, axis=1, keepdims=True)
        o = o * lax.rsqrt(ms + 1e-6) * gain * (gh * _sigmoid(gh))
        o_ref[rows, :] = o.astype(o_ref.dtype)


def _hgrn2(proj, hgrn_gamma, hgrn_gain, layer):
    bsz, t, _ = proj.shape
    tt = min(T_HG, t)
    base = 3 * DA_HEADS
    spec = lambda k: pl.BlockSpec((None, tt, HG_DK), lambda b, h, i: (b, i, base + k * HG_HEADS + h))
    return pl.pallas_call(
        functools.partial(_hgrn_kernel, layer=layer),
        grid=(bsz, HG_HEADS, t // tt),
        in_specs=[
            spec(0), spec(1), spec(2), spec(3),
            pl.BlockSpec((hgrn_gamma.shape[0], HG_DK), lambda b, h, i: (0, h)),
            pl.BlockSpec((1, HG_DK), lambda b, h, i: (0, 0)),
        ],
        out_specs=pl.BlockSpec((None, tt, HG_DK), lambda b, h, i: (b, i, h)),
        out_shape=jax.ShapeDtypeStruct((bsz, t, HG_HEADS * HG_DK), BF16),
        scratch_shapes=[pltpu.VMEM((HG_DK, HG_DK), F32)],
        compiler_params=_params(("arbitrary", "arbitrary", "arbitrary")),
        name="hgrn2",
    )(proj, proj, proj, proj, hgrn_gamma, hgrn_gain.reshape(1, HG_DK))


def _route(logits_t):
    mx = jnp.max(logits_t, axis=0, keepdims=True)
    ex = jnp.exp(logits_t - mx)
    probs = ex / jnp.sum(ex, axis=0, keepdims=True)
    p = [probs[e:e + 1, :] for e in range(N_EXPERTS)]
    g = E_PER_GROUP
    scores = []
    for gi in range(N_GROUPS):
        pg = p[gi * g:(gi + 1) * g]
        best = None
        for a in range(g):
            for b in range(a + 1, g):
                pair = pg[a] + pg[b]
                best = pair if best is None else jnp.maximum(best, pair)
        scores.append(best)
    group_id = jnp.zeros_like(p[0])
    gates = [jnp.zeros_like(p[0]) for _ in range(g)]
    for gi in range(N_GROUPS):
        sel = None
        for gj in range(N_GROUPS):
            if gj == gi:
                continue
            cond = (scores[gi] > scores[gj]) if gj < gi else (scores[gi] >= scores[gj])
            sel = cond if sel is None else (sel & cond)
        group_id = jnp.where(sel, float(gi), group_id)
        pg = p[gi * g:(gi + 1) * g]
        chosen = []
        for a in range(g):
            rank = jnp.zeros_like(pg[a])
            for b in range(g):
                if b == a:
                    continue
                ahead = (pg[b] >= pg[a]) if b < a else (pg[b] > pg[a])
                rank = rank + jnp.where(ahead, 1.0, 0.0)
            chosen.append(sel & (rank < 2.0))
        denom = None
        for a in range(g):
            term = jnp.where(chosen[a], pg[a], 0.0)
            denom = term if denom is None else denom + term
        for a in range(g):
            gates[a] = jnp.where(chosen[a], pg[a] / denom, gates[a])
    return group_id, gates


def _outproj_kernel(a_ref, b_ref, x_ref, mod_ref, w_ref, lng_ref, lnb_ref, rwt_ref, rb_ref,
                    xo_ref, h_ref, row_ref, col_ref, *, alpha):
    half = a_ref.shape[1]
    tm = a_ref.shape[0]
    y = _dot(a_ref[...], w_ref[0:half, :]) + _dot(b_ref[...], w_ref[half:2 * half, :])
    g1 = mod_ref[2:3, :]
    sh2 = mod_ref[3:4, :]
    sc2 = mod_ref[4:5, :]
    r = alpha * x_ref[...] + (1.0 + g1) * y
    mu = jnp.mean(r, axis=1, keepdims=True)
    rc = r - mu
    var = jnp.mean(rc * rc, axis=1, keepdims=True)
    xn = rc * lax.rsqrt(var + 1e-5) * lng_ref[...] + lnb_ref[...]
    xo_ref[...] = xn
    h2 = xn * (1.0 + sc2) + sh2
    h_ref[...] = h2.astype(BF16)
    h_hi, h_mid, _ = _split3(h2)
    rw = rwt_ref[...]
    w_hi, w_mid, _ = _split3(rw)
    logits_t = _dot_nt(w_hi, h_hi) + _dot_nt(w_hi, h_mid) + _dot_nt(w_mid, h_hi) + rb_ref[...]
    group_id, gates = _route(logits_t)
    sel = [jnp.where(group_id == float(gi), 1.0, 0.0) for gi in range(N_GROUPS)]
    onehot = jnp.concatenate(sel + [jnp.zeros((8 - N_GROUPS, tm), F32)], axis=0).astype(BF16)
    src = lax.broadcasted_iota(jnp.int32, (tm, tm), 0)
    dst = lax.broadcasted_iota(jnp.int32, (tm, tm), 1)
    earlier = jnp.where(src < dst, 1.0, 0.0).astype(BF16)
    counts = _dot(onehot, earlier)
    rank = sel[0] * counts[0:1, :]
    for gi in range(1, N_GROUPS):
        rank = rank + sel[gi] * counts[gi:gi + 1, :]
    info = jnp.concatenate(gates + [group_id, rank], axis=0)
    row_ref[...] = jnp.concatenate(
        [group_id, rank, jnp.zeros((8 - 2, tm), F32)], axis=0)
    pad = jnp.zeros((LANES - info.shape[0], tm), F32)
    col_ref[...] = jnp.concatenate([info, pad], axis=0).T


def _outproj(a, b, x, mod_l, w_bf16, ln_g, ln_b, router_w, router_b, alpha):
    bsz, t, d = x.shape
    half = a.shape[2]
    tm = min(T_BLK, t)
    tok = lambda width: pl.BlockSpec((None, tm, width), lambda bi, i: (bi, i, 0))
    full = lambda r, c: pl.BlockSpec((r, c), lambda bi, i: (0, 0))
    return pl.pallas_call(
        functools.partial(_outproj_kernel, alpha=alpha),
        grid=(bsz, t // tm),
        in_specs=[
            tok(half), tok(half), tok(d),
            pl.BlockSpec((None, 6, d), lambda bi, i: (bi, 0, 0)),
            full(2 * half, d), full(1, d), full(1, d), full(N_EXPERTS, d), full(N_EXPERTS, 1),
        ],
        out_specs=[tok(d), tok(d), pl.BlockSpec((None, 8, tm), lambda bi, i: (bi, 0, i)),
                   tok(LANES)],
        out_shape=[
            jax.ShapeDtypeStruct((bsz, t, d), F32),
            jax.ShapeDtypeStruct((bsz, t, d), BF16),
            jax.ShapeDtypeStruct((bsz, 8, t), F32),
            jax.ShapeDtypeStruct((bsz, t, LANES), F32),
        ],
        compiler_params=_params(("arbitrary", "arbitrary")),
        name="outproj_ln_route",
    )(a, b, x, mod_l, w_bf16, ln_g.reshape(1, d), ln_b.reshape(1, d), router_w.T,
      router_b.reshape(N_EXPERTS, 1))


def _slab_rows(tm):
    extra = -(-(tm - MOE_ROWS_MAIN) // MOE_ROWS_EXTRA)
    return MOE_ROWS_MAIN + max(extra, 0) * MOE_ROWS_EXTRA


def _extra_chunks(count):
    return (jnp.maximum(count - MOE_ROWS_MAIN, 0) + MOE_ROWS_EXTRA - 1) // MOE_ROWS_EXTRA


def _moe_expert_kernel(cnt_ref, h_ref, row_ref, col_ref, wg_ref, wu_ref, wd_ref, z_ref):
    g = pl.program_id(0)
    blk = pl.program_id(1)
    count = cnt_ref[g * pl.num_programs(1) + blk]
    mine = row_ref[0:1, :] == g.astype(F32)
    rank = row_ref[1:2, :]
    info = col_ref[...]
    info_hi = info.astype(BF16)
    info_lo = (info - info_hi.astype(F32)).astype(BF16)

    def run_rows(r0, m):
        rid = (lax.broadcasted_iota(jnp.int32, (m, 1), 0) + r0).astype(F32)
        pick = _onehot((rank == rid) & mine)
        xs = _dot(pick, h_ref[...]).astype(BF16)
        gm = _dot(pick, info_hi) + _dot(pick, info_lo)
        acc = None
        for j in range(E_PER_GROUP):
            a = _dot(xs, wg_ref[j])
            u = _dot(xs, wu_ref[j])
            he = (a * _sigmoid(a) * u * gm[:, j:j + 1]).astype(BF16)
            part = _dot(he, wd_ref[j])
            acc = part if acc is None else acc + part
        z_ref[pl.ds(r0, m), :] = acc.astype(z_ref.dtype)

    run_rows(0, MOE_ROWS_MAIN)
    rest = z_ref.shape[0] - MOE_ROWS_MAIN
    if rest:
        z_ref[MOE_ROWS_MAIN:, :] = jnp.zeros((rest, z_ref.shape[1]), z_ref.dtype)

        def body(i, carry):
            run_rows(pl.multiple_of(MOE_ROWS_MAIN + i * MOE_ROWS_EXTRA, 16), MOE_ROWS_EXTRA)
            return carry

        lax.fori_loop(0, _extra_chunks(count), body, 0)


def _moe_combine_kernel(cnt_ref, z0_ref, z1_ref, z2_ref, z3_ref, col_ref, x_ref, mod_ref, lng_ref,
                        lnb_ref, o_ref, y_sc, *, alpha):
    blk = pl.program_id(0) * pl.num_programs(1) + pl.program_id(1)
    n_blk = pl.num_programs(0) * pl.num_programs(1)
    z_refs = (z0_ref, z1_ref, z2_ref, z3_ref)
    main = MOE_ROWS_MAIN
    grp = col_ref[:, E_PER_GROUP:E_PER_GROUP + 1]
    rank = col_ref[:, E_PER_GROUP + 1:E_PER_GROUP + 2]
    where_to = jnp.where(rank < float(main), grp * float(main) + rank, -1.0)
    lane = lax.broadcasted_iota(jnp.int32, (1, N_GROUPS * main), 1).astype(F32)
    pick = _onehot(where_to == lane)
    z_all = jnp.concatenate([zr[0:main, :] for zr in z_refs], axis=0)
    y_sc[...] = _dot(pick, z_all)
    lane_x = lax.broadcasted_iota(jnp.int32, (1, MOE_ROWS_EXTRA), 1).astype(F32)
    for gi in range(N_GROUPS):
        def body(i, carry, gi=gi):
            r0 = pl.multiple_of(main + i * MOE_ROWS_EXTRA, 16)
            hit = (grp == float(gi)) & ((rank - r0.astype(F32)) == lane_x)
            y_sc[...] += _dot(_onehot(hit), z_refs[gi][pl.ds(r0, MOE_ROWS_EXTRA), :])
            return carry

        lax.fori_loop(0, _extra_chunks(cnt_ref[gi * n_blk + blk]), body, 0)

    g2 = mod_ref[5:6, :]
    r = alpha * x_ref[...] + (1.0 + g2) * y_sc[...]
    mu = jnp.mean(r, axis=1, keepdims=True)
    rc = r - mu
    var = jnp.mean(rc * rc, axis=1, keepdims=True)
    o_ref[...] = rc * lax.rsqrt(var + 1e-5) * lng_ref[...] + lnb_ref[...]


def _moe(h2, rowinfo, colinfo, x, mod_l, wg, wu, wd, ln_g, ln_b, alpha):
    bsz, t, d = x.shape
    tm = min(T_BLK, t)
    nb = t // tm
    n_blk = bsz * nb
    dff = wg.shape[2]
    slab = _slab_rows(tm)
    group_of = rowinfo[:, 0, :].reshape(1, n_blk, tm)
    counts = jnp.sum(group_of == jnp.arange(N_GROUPS, dtype=F32).reshape(N_GROUPS, 1, 1), axis=2)
    counts = counts.astype(jnp.int32).reshape(N_GROUPS * n_blk)

    z = pl.pallas_call(
        _moe_expert_kernel,
        grid_spec=pltpu.PrefetchScalarGridSpec(
            num_scalar_prefetch=1,
            grid=(N_GROUPS, n_blk),
            in_specs=[
                pl.BlockSpec((None, tm, d), lambda g, i, c: (i // nb, i % nb, 0)),
                pl.BlockSpec((None, 8, tm), lambda g, i, c: (i // nb, 0, i % nb)),
                pl.BlockSpec((None, tm, LANES), lambda g, i, c: (i // nb, i % nb, 0)),
                pl.BlockSpec((E_PER_GROUP, d, dff), lambda g, i, c: (g, 0, 0)),
                pl.BlockSpec((E_PER_GROUP, d, dff), lambda g, i, c: (g, 0, 0)),
                pl.BlockSpec((E_PER_GROUP, dff, d), lambda g, i, c: (g, 0, 0)),
            ],
            out_specs=pl.BlockSpec((None, None, slab, d), lambda g, i, c: (g, i, 0, 0)),
        ),
        out_shape=jax.ShapeDtypeStruct((N_GROUPS, n_blk, slab, d), BF16),
        compiler_params=_params(("arbitrary", "arbitrary")),
        name="moe_experts",
    )(counts, h2, rowinfo, colinfo, wg, wu, wd)

    zspec = lambda gi: pl.BlockSpec((None, None, slab, d), lambda b, i, c: (gi, b * nb + i, 0, 0))
    tok = lambda width: pl.BlockSpec((None, tm, width), lambda b, i, c: (b, i, 0))
    return pl.pallas_call(
        functools.partial(_moe_combine_kernel, alpha=alpha),
        grid_spec=pltpu.PrefetchScalarGridSpec(
            num_scalar_prefetch=1,
            grid=(bsz, nb),
            in_specs=[
                zspec(0), zspec(1), zspec(2), zspec(3), tok(LANES), tok(d),
                pl.BlockSpec((None, 6, d), lambda b, i, c: (b, 0, 0)),
                pl.BlockSpec((1, d), lambda b, i, c: (0, 0)),
                pl.BlockSpec((1, d), lambda b, i, c: (0, 0)),
            ],
            out_specs=tok(d),
            scratch_shapes=[pltpu.VMEM((tm, d), F32)],
        ),
        out_shape=jax.ShapeDtypeStruct((bsz, t, d), F32),
        compiler_params=_params(("arbitrary", "arbitrary")),
        name="moe_combine_ln",
    )(counts, z, z, z, z, colinfo, x, mod_l, ln_g.reshape(1, d), ln_b.reshape(1, d))


def _lru_kernel(x_ref, g_ref, cw_ref, cb_ref, wa_ref, ba_ref, wx_ref, bx_ref, lam_ref,
                o_ref, xpad_sc, h_sc):
    tt = x_ref.shape[0]
    pad = 8

    @pl.when(pl.program_id(1) == 0)
    def _():
        xpad_sc[0:pad, :] = jnp.zeros((pad, xpad_sc.shape[1]), F32)
        h_sc[...] = jnp.zeros(h_sc.shape, F32)

    xpad_sc[pad:pad + tt, :] = x_ref[...].astype(F32)
    xc = cb_ref[...] + jnp.zeros((tt, x_ref.shape[1]), F32)
    for j in range(CONV_WIDTH):
        off = pad - (CONV_WIDTH - 1) + j
        xc = xc + cw_ref[j:j + 1, :] * xpad_sc[off:off + tt, :]
    xpad_sc[0:pad, :] = xpad_sc[tt:tt + pad, :]

    xb = xc.astype(BF16)
    r = _sigmoid(_dot(xb, wa_ref[...]) + ba_ref[...])
    i = _sigmoid(_dot(xb, wx_ref[...]) + bx_ref[...])
    lam = lam_ref[...].astype(F32)
    softplus_neg = jnp.maximum(-lam, 0.0) + jnp.log(1.0 + jnp.exp(-jnp.abs(lam)))
    log_a = -LRU_C * r * softplus_neg
    a = jnp.exp(log_a)
    u = jnp.sqrt(jnp.maximum(1.0 - jnp.exp(2.0 * log_a), 1e-12)) * (i * xc)

    rowi = lax.broadcasted_iota(jnp.int32, (tt, 1), 0)
    d = 1
    while d < tt:
        a_sh = jnp.where(rowi >= d, pltpu.roll(a, d, 0), 1.0)
        u_sh = jnp.where(rowi >= d, pltpu.roll(u, d, 0), 0.0)
        u = u + a * u_sh
        a = a * a_sh
        d *= 2
    hcur = u + a * h_sc[...]
    h_sc[...] = hcur[tt - 1:tt, :]

    gr = g_ref[...].astype(F32)
    gelu = 0.5 * gr * (1.0 + jnp.tanh(0.7978845608028654 * (gr + 0.044715 * gr * gr * gr)))
    o_ref[...] = (gelu * hcur).astype(o_ref.dtype)


def _rg_lru(proj, conv_w, conv_b, wa_dense, ba, wx_dense, bx, lam):
    bsz, t, _ = proj.shape
    tt = min(T_LRU, t)
    w = LRU_WIDTH
    nblk = w // LANES
    row = lambda a: a.reshape(1, w)
    full = lambda r, c: pl.BlockSpec((r, c), lambda b, i: (0, 0))
    return pl.pallas_call(
        _lru_kernel,
        grid=(bsz, t // tt),
        in_specs=[
            pl.BlockSpec((None, tt, w), lambda b, i: (b, i, 0)),
            pl.BlockSpec((None, tt, w), lambda b, i: (b, i, 1)),
            full(CONV_WIDTH, w), full(1, w), full(w, w), full(1, w), full(w, w), full(1, w),
            full(1, w),
        ],
        out_specs=pl.BlockSpec((None, tt, w), lambda b, i: (b, i, 0)),
        out_shape=jax.ShapeDtypeStruct((bsz, t, w), BF16),
        scratch_shapes=[pltpu.VMEM((tt + 8, w), F32), pltpu.VMEM((1, w), F32)],
        compiler_params=_params(("arbitrary", "arbitrary")),
        name="rg_lru",
    )(proj, proj, conv_w, row(conv_b), wa_dense, row(ba), wx_dense, row(bx), row(lam))


def _sb_kernel(q_ref, k_ref, v_ref, o_ref, r_sc, acc_sc, z0_sc, z1_sc, lb0_sc, lb1_sc, l0_sc,
               l1_sc, w0_sc, w1_sc, *, tile):
    qi = pl.program_id(2)
    d = SB_HEAD_DIM
    tk = tile // 2
    lane = lax.broadcasted_iota(jnp.int32, (1, 2 * d), 1)
    q = q_ref[...]
    zero = jnp.zeros_like(q)
    q2 = jnp.concatenate([jnp.where(lane < d, q, zero), jnp.where(lane >= d, q, zero)], axis=0)
    rj = lax.broadcasted_iota(jnp.int32, (tk, tk), 0)
    cs = lax.broadcasted_iota(jnp.int32, (tk, tk), 1)
    upper = jnp.where(rj > cs, 1.0, 0.0).astype(BF16)

    r_sc[...] = jnp.zeros(r_sc.shape, F32)
    acc_sc[...] = jnp.zeros(acc_sc.shape, F32)
    z_bufs, lb_bufs, l_bufs, w_bufs = (z0_sc, z1_sc), (lb0_sc, lb1_sc), (l0_sc, l1_sc), (w0_sc, w1_sc)
    n_sub = 2 * qi + 2

    def key_start(j):
        return pl.multiple_of((n_sub - 1 - j) * tk, tk)

    def strict_mask(j):
        rowp = lax.broadcasted_iota(jnp.int32, (tile, tk), 0)
        colp = lax.broadcasted_iota(jnp.int32, (tile, tk), 1) + (1 - j) * tk
        strict = colp < rowp
        return jnp.concatenate([strict, strict], axis=0)

    def logits(j, slot):
        z_bufs[slot][...] = _dot_nt(q2, k_ref[pl.ds(key_start(j), tk), :])

    def gates(j, slot, masked):
        z = z_bufs[slot][...]
        log_1m = jnp.log(1.0 + jnp.exp2(-jnp.abs(z))) * (-LOG2E) - jnp.maximum(z, 0.0)
        lb_bufs[slot][...] = z + log_1m
        if masked:
            log_1m = jnp.where(strict_mask(j), log_1m, 0.0)
        l_bufs[slot][...] = log_1m.astype(BF16)

    def weights(j, slot, masked):
        log_1m = l_bufs[slot][...]
        after = _dot(log_1m, upper) + r_sc[...]
        w = jnp.exp2(lb_bufs[slot][...] + after)
        if masked:
            w = jnp.where(strict_mask(j), w, 0.0)
        w_bufs[slot][...] = w.astype(BF16)
        r_sc[...] = after[:, 0:1] + log_1m[:, 0:1].astype(F32)

    def values(j, slot):
        acc_sc[...] += _dot(w_bufs[slot][...], v_ref[pl.ds(key_start(j), tk), :])

    @pl.when(qi == 0)
    def _():
        for j in range(2):
            logits(j, j)
            gates(j, j, True)
            weights(j, j, True)
            values(j, j)

    @pl.when(qi > 0)
    def _():
        logits(0, 0)
        logits(1, 1)
        gates(0, 0, True)
        logits(2, 0)
        gates(1, 1, True)
        weights(0, 0, True)
        logits(3, 1)
        gates(2, 0, False)
        weights(1, 1, True)
        values(0, 0)

        def body(i, carry):
            t = 2 * i
            logits(t, 0)
            gates(t - 1, 1, False)
            weights(t - 2, 0, False)
            values(t - 3, 1)
            logits(t + 1, 1)
            gates(t, 0, False)
            weights(t - 1, 1, False)
            values(t - 2, 0)
            return carry

        lax.fori_loop(2, qi + 1, body, 0)
        n = n_sub
        gates(n - 1, 1, False)
        weights(n - 2, 0, False)
        values(n - 3, 1)
        weights(n - 1, 1, False)
        values(n - 2, 0)
        values(n - 1, 1)

    acc = acc_sc[...]
    o_ref[...] = jnp.where(lane < d, acc[0:tile, :], acc[tile:2 * tile, :]).astype(o_ref.dtype)


def _sb_attention(proj):
    bsz, t, _ = proj.shape
    tq = min(T_SB, t)
    pairs = SB_HEADS // 2
    wblk = 2 * SB_HEAD_DIM
    base = 2 * LRU_WIDTH // wblk
    return pl.pallas_call(
        functools.partial(_sb_kernel, tile=tq),
        grid=(bsz, pairs, t // tq),
        in_specs=[
            pl.BlockSpec((None, tq, wblk), lambda b, h, i: (b, i, base + h)),
            pl.BlockSpec((None, t, wblk), lambda b, h, i: (b, 0, base + pairs + h)),
            pl.BlockSpec((None, t, wblk), lambda b, h, i: (b, 0, base + 2 * pairs + h)),
        ],
        out_specs=pl.BlockSpec((None, tq, wblk), lambda b, h, i: (b, i, h)),
        out_shape=jax.ShapeDtypeStruct((bsz, t, SB_HEADS * SB_HEAD_DIM), BF16),
        scratch_shapes=[pltpu.VMEM((2 * tq, 1), F32), pltpu.VMEM((2 * tq, wblk), F32)]
        + [pltpu.VMEM((2 * tq, tq // 2), F32)] * 4 + [pltpu.VMEM((2 * tq, tq // 2), BF16)] * 4,
        compiler_params=_params(("arbitrary", "arbitrary", "arbitrary")),
        name="sb_attn",
    )(proj, proj, proj)


def _block_diag(w):
    g, n, _ = w.shape
    eye = jnp.eye(g, dtype=w.dtype)
    return (eye[:, None, :, None] * w[:, :, None, :]).reshape(g * n, g * n)


def kernel(x, c, ada_w, ada_b, ln_g, ln_b, even_w_in, even_w_out, diff_lambda, diff_gain, hgrn_gamma, hgrn_gain, odd_w_in, odd_w_out, conv_w, conv_b, lru_wa, lru_ba, lru_wx, lru_bx, lru_lambda, router_w, router_b, moe_w_gate, moe_w_up, moe_w_down):
    depth = ada_w.shape[0]
    bsz, t, d = x.shape
    alpha = (2.0 * depth) ** 0.25
    mod = _ada_mod(c, ada_w, ada_b).reshape(depth, bsz, 6, d)
    for l in range(depth):
        j = l // 2
        mod_l = mod[l]
        if l % 2 == 0:
            lam_init = 0.8 - 0.6 * math.exp(-0.3 * l)
            proj = _inproj(x, mod_l, even_w_in[j].astype(BF16), q_chunk=0)
            mix_a = _diff_attention(proj, diff_lambda[j], diff_gain[j], lam_init)
            mix_b = _hgrn2(proj, hgrn_gamma, hgrn_gain[j], l)
            w_out = even_w_out[j]
        else:
            proj = _inproj(x, mod_l, odd_w_in[j].astype(BF16), q_chunk=2 * LRU_WIDTH // PROJ_CHUNK)
            mix_a = _rg_lru(proj, conv_w[j], conv_b[j], _block_diag(lru_wa[j]).astype(BF16),
                            lru_ba[j], _block_diag(lru_wx[j]).astype(BF16), lru_bx[j],
                            lru_lambda[j])
            mix_b = _sb_attention(proj)
            w_out = odd_w_out[j]
        x, h2, rowinfo, colinfo = _outproj(mix_a, mix_b, x, mod_l, w_out.astype(BF16),
                                           ln_g[l, 0], ln_b[l, 0], router_w, router_b, alpha)
        x = _moe(h2, rowinfo, colinfo, x, mod_l, moe_w_gate[l].astype(BF16),
                 moe_w_up[l].astype(BF16), moe_w_down[l].astype(BF16), ln_g[l, 1], ln_b[l, 1],
                 alpha)
    return x
```

```python
import functools
import math

import jax
import jax.numpy as jnp
from jax import lax
from jax.experimental import pallas as pl
from jax.experimental.pallas import tpu as pltpu

F32 = jnp.float32
BF16 = jnp.bfloat16

DA_HEADS = 4
DA_HEAD_DIM = 64
HG_HEADS = 4
HG_DK = 128
HG_CHUNK = 64
LRU_WIDTH = 512
LRU_BLOCKS = 8
CONV_WIDTH = 4
LRU_C = 8.0
SB_HEADS = 8
SB_HEAD_DIM = 64
N_EXPERTS = 16
N_GROUPS = 4
E_PER_GROUP = N_EXPERTS // N_GROUPS
D_FF = 512

LANES = 128
NEG_BIG = -1e30
LOG2E = 1.4426950408889634
Q_PRESCALE = DA_HEAD_DIM ** -0.5 * LOG2E
PROJ_CHUNK = 512
VMEM_LIMIT = 56 * 1024 * 1024

TM_PROJ = 512
TQ_ATT = 512
T_SB = 512
SB_SETS = 3
SB_DEAD_LOG2 = 160.0
T_HG = 512
T_LRU = 256
T_BLK = 1024
MOE_ROWS_MAIN = 320
MOE_ROWS_EXTRA = 128


def _params(sem):
    return pltpu.CompilerParams(dimension_semantics=sem, vmem_limit_bytes=VMEM_LIMIT)


def _sigmoid(x):
    return 1.0 / (1.0 + jnp.exp(-x))


def _dot(a, b):
    return jnp.dot(a, b, preferred_element_type=F32)


def _dot_nt(a, b):
    return lax.dot_general(a, b, (((1,), (1,)), ((), ())), preferred_element_type=F32)


def _onehot(mask):
    return jnp.where(mask, 1.0, 0.0).astype(BF16)


def _split3(x):
    hi = x.astype(BF16)
    r1 = x - hi.astype(F32)
    mid = r1.astype(BF16)
    lo = (r1 - mid.astype(F32)).astype(BF16)
    return hi, mid, lo


def _ada_kernel(c_ref, w_ref, b_ref, o_ref):
    c = c_ref[...]
    cond = c * _sigmoid(c)
    hi, mid, _ = _split3(cond)
    w = w_ref[...].astype(BF16)
    o_ref[...] = _dot(hi, w) + _dot(mid, w) + b_ref[...]


def _ada_mod(c, ada_w, ada_b):
    depth, d, d6 = ada_w.shape
    bsz = c.shape[0]
    n_col = d6 // d
    return pl.pallas_call(
        _ada_kernel,
        grid=(depth, n_col),
        in_specs=[
            pl.BlockSpec((bsz, d), lambda l, j: (0, 0)),
            pl.BlockSpec((None, d, d), lambda l, j: (l, 0, j)),
            pl.BlockSpec((None, 1, d), lambda l, j: (l, 0, j)),
        ],
        out_specs=pl.BlockSpec((None, bsz, d), lambda l, j: (l, 0, j)),
        out_shape=jax.ShapeDtypeStruct((depth, bsz, d6), F32),
        compiler_params=_params(("arbitrary", "arbitrary")),
        name="ada_mod",
    )(c, ada_w, ada_b.reshape(depth, 1, d6))


def _inproj_kernel(x_ref, mod_ref, w_ref, o_ref, *, col_chunk, q_chunk):
    sh = mod_ref[0:1, :]
    sc = mod_ref[1:2, :]
    h = (x_ref[...] * (1.0 + sc) + sh).astype(BF16)
    for j in range(o_ref.shape[1] // col_chunk):
        cols = slice(j * col_chunk, (j + 1) * col_chunk)
        y = _dot(h, w_ref[:, cols])
        if j == q_chunk:
            y = y * Q_PRESCALE
        o_ref[:, cols] = y.astype(o_ref.dtype)


def _inproj(x, mod_l, w_bf16, q_chunk):
    bsz, t, d = x.shape
    width = w_bf16.shape[1]
    tm = min(TM_PROJ, t)
    return pl.pallas_call(
        functools.partial(_inproj_kernel, col_chunk=PROJ_CHUNK, q_chunk=q_chunk),
        grid=(bsz, t // tm),
        in_specs=[
            pl.BlockSpec((None, tm, d), lambda b, i: (b, i, 0)),
            pl.BlockSpec((None, 6, d), lambda b, i: (b, 0, 0)),
            pl.BlockSpec((d, width), lambda b, i: (0, 0)),
        ],
        out_specs=pl.BlockSpec((None, tm, width), lambda b, i: (b, i, 0)),
        out_shape=jax.ShapeDtypeStruct((bsz, t, width), BF16),
        compiler_params=_params(("arbitrary", "arbitrary")),
        name="inproj",
    )(x, mod_l, w_bf16)


def _diffattn_kernel(q_ref, k_ref, v_ref, lam_ref, gain_ref, o_ref, m_sc, acc_sc, s0_sc, s1_sc,
                     p0_sc, p1_sc, a0_sc, a1_sc, *, tile, lam_init):
    h = pl.program_id(1)
    qi = pl.program_id(2)
    dh = DA_HEAD_DIM
    hd = 2 * dh
    tk = tile // 2
    reps = tk // LANES
    s_bufs, p_bufs, a_bufs = (s0_sc, s1_sc), (p0_sc, p1_sc), (a0_sc, a1_sc)

    lane = lax.broadcasted_iota(jnp.int32, (1, hd), 1)
    q = q_ref[...]
    zero = jnp.zeros_like(q)
    q2 = jnp.concatenate([jnp.where(lane < dh, q, zero), jnp.where(lane >= dh, q, zero)], axis=0)

    hf = jnp.full((1, 1), h + 1, jnp.int32).astype(F32)
    slope = jnp.exp2(hf * (-8.0 / DA_HEADS)) * LOG2E
    col = lax.broadcasted_iota(jnp.int32, (1, tk), 1)
    ones = jnp.ones((tk, hd), BF16)

    m_sc[...] = jnp.full(m_sc.shape, NEG_BIG, F32)
    acc_sc[...] = jnp.zeros(acc_sc.shape, F32)

    def scores(j, slot):
        ks = pl.multiple_of(j * tk, tk)
        bias = (col + (j * tk - qi * tile)).astype(F32) * slope
        s_bufs[slot][...] = _dot_nt(q2, k_ref[pl.ds(ks, tk), :]) + bias

    def softmax(j, slot, masked):
        s = s_bufs[slot][...]
        if masked:
            rowp = lax.broadcasted_iota(jnp.int32, (tile, tk), 0) + qi * tile
            colp = lax.broadcasted_iota(jnp.int32, (tile, tk), 1) + j * tk
            keep = colp <= rowp
            s = jnp.where(jnp.concatenate([keep, keep], axis=0), s, NEG_BIG)
        m_old = m_sc[...]
        m_new = jnp.maximum(m_old, jnp.max(s, axis=1, keepdims=True))
        p_bufs[slot][...] = jnp.exp2(s - jnp.concatenate([m_new] * reps, axis=1)).astype(BF16)
        a_bufs[slot][...] = jnp.exp2(m_old - m_new)
        m_sc[...] = m_new

    def values(j, slot):
        ks = pl.multiple_of(j * tk, tk)
        v_aug = jnp.concatenate([v_ref[pl.ds(ks, tk), :], ones], axis=1)
        alpha = a_bufs[slot][...]
        acc_sc[...] = (jnp.concatenate([alpha, alpha], axis=1) * acc_sc[...]
                       + _dot(p_bufs[slot][...], v_aug))

    scores(0, 0)
    scores(1, 1)
    softmax(0, 0, True)

    @pl.when(qi > 0)
    def _():
        def body(i, carry):
            t = 2 * i
            scores(t, 0)
            softmax(t - 1, 1, False)
            values(t - 2, 0)
            scores(t + 1, 1)
            softmax(t, 0, False)
            values(t - 1, 1)
            return carry

        lax.fori_loop(1, qi, body, 0)
        t = 2 * qi
        scores(t, 0)
        softmax(t - 1, 1, False)
        values(t - 2, 0)
        scores(t + 1, 1)
        softmax(t, 0, True)
        values(t - 1, 1)

    softmax(2 * qi + 1, 1, True)
    values(2 * qi, 0)
    values(2 * qi + 1, 1)

    lv = lam_ref[...].astype(F32)
    dots = jnp.sum(lv[0:1, :] * lv[1:2, :], axis=1, keepdims=True)
    dots2 = jnp.sum(lv[2:3, :] * lv[3:4, :], axis=1, keepdims=True)
    lam = jnp.exp(dots) - jnp.exp(dots2) + lam_init
    acc = acc_sc[...]
    o0 = acc[0:tile, 0:hd] / acc[0:tile, hd:2 * hd]
    o1 = acc[tile:2 * tile, 0:hd] / acc[tile:2 * tile, hd:2 * hd]
    o = o0 - lam * o1
    ms = jnp.mean(o * o, axis=1, keepdims=True)
    o = o * lax.rsqrt(ms + 1e-6) * gain_ref[...] * (1.0 - lam_init)
    o_ref[...] = o.astype(o_ref.dtype)


def _diff_attention(proj, diff_lambda, diff_gain, lam_init):
    bsz, t, _ = proj.shape
    tile = min(TQ_ATT, t)
    hd = 2 * DA_HEAD_DIM
    kern = functools.partial(_diffattn_kernel, tile=tile, lam_init=lam_init)
    return pl.pallas_call(
        kern,
        grid=(bsz, DA_HEADS, t // tile),
        in_specs=[
            pl.BlockSpec((None, tile, hd), lambda b, h, i: (b, i, h)),
            pl.BlockSpec((None, t, hd), lambda b, h, i: (b, 0, DA_HEADS + h)),
            pl.BlockSpec((None, t, hd), lambda b, h, i: (b, 0, 2 * DA_HEADS + h)),
            pl.BlockSpec((4, DA_HEAD_DIM), lambda b, h, i: (0, 0)),
            pl.BlockSpec((1, hd), lambda b, h, i: (0, 0)),
        ],
        out_specs=pl.BlockSpec((None, tile, hd), lambda b, h, i: (b, i, h)),
        out_shape=jax.ShapeDtypeStruct((bsz, t, DA_HEADS * hd), BF16),
        scratch_shapes=[
            pltpu.VMEM((2 * tile, LANES), F32),
            pltpu.VMEM((2 * tile, 2 * hd), F32),
            pltpu.VMEM((2 * tile, tile // 2), F32),
            pltpu.VMEM((2 * tile, tile // 2), F32),
            pltpu.VMEM((2 * tile, tile // 2), BF16),
            pltpu.VMEM((2 * tile, tile // 2), BF16),
            pltpu.VMEM((2 * tile, LANES), F32),
            pltpu.VMEM((2 * tile, LANES), F32),
        ],
        compiler_params=_params(("arbitrary", "arbitrary", "arbitrary")),
        name="diff_attn",
    )(proj, proj, proj, diff_lambda, diff_gain.reshape(1, hd))


def _hgrn_kernel(q_ref, f_ref, i_ref, g_ref, gamma_ref, gain_ref, o_ref, st_sc, *, layer):
    @pl.when(pl.program_id(2) == 0)
    def _():
        st_sc[...] = jnp.zeros(st_sc.shape, F32)

    gam = gamma_ref[...].astype(F32)
    e = jnp.exp(gam - jnp.max(gam, axis=0, keepdims=True))
    sm = e / jnp.sum(e, axis=0, keepdims=True)
    lb = jnp.sum(sm[0:layer + 1, :], axis=0, keepdims=True)

    c = HG_CHUNK
    row = lax.broadcasted_iota(jnp.int32, (c, c), 0)
    col = lax.broadcasted_iota(jnp.int32, (c, c), 1)
    tril = col <= row
    tril_bf = jnp.where(tril, 1.0, 0.0).astype(BF16)
    gain = gain_ref[...]

    for n in range(q_ref.shape[0] // c):
        rows = slice(n * c, (n + 1) * c)
        fl = f_ref[rows, :].astype(F32)
        qh = q_ref[rows, :].astype(F32)
        v = i_ref[rows, :]
        gh = g_ref[rows, :].astype(F32)
        sig = _sigmoid(fl)
        logf = jnp.log(lb + (1.0 - lb) * sig)
        kk = (1.0 - lb) * (1.0 - sig)
        hi, mid, lo = _split3(logf)
        b = _dot(tril_bf, hi) + _dot(tril_bf, mid) + _dot(tril_bf, lo)
        b_mid = b[c // 2 - 1:c // 2, :]
        b_last = b[c - 1:c, :]
        qs = qh * _sigmoid(qh)
        att = _dot_nt((qs * jnp.exp(b - b_mid)).astype(BF16),
                      (kk * jnp.exp(b_mid - b)).astype(BF16))
        att = jnp.where(tril, att, 0.0)
        o_intra = _dot(att.astype(BF16), v)
        kd = (kk * jnp.exp(b_last - b)).astype(BF16)
        ds_t = _dot(v.T, kd)
        st = st_sc[...]
        o_inter = _dot_nt((qs * jnp.exp(b)).astype(BF16), st.astype(BF16))
        st_sc[...] = st * jnp.exp(b_last) + ds_t
        o = o_intra + o_inter
        ms = jnp.mean(o * o, axis=1, keepdims=True)
        o = o * lax.rsqrt(ms + 1e-6) * gain * (gh * _sigmoid(gh))
        o_ref[rows, :] = o.astype(o_ref.dtype)


def _hgrn2(proj, hgrn_gamma, hgrn_gain, layer):
    bsz, t, _ = proj.shape
    tt = min(T_HG, t)
    base = 3 * DA_HEADS
    spec = lambda k: pl.BlockSpec((None, tt, HG_DK), lambda b, h, i: (b, i, base + k * HG_HEADS + h))
    return pl.pallas_call(
        functools.partial(_hgrn_kernel, layer=layer),
        grid=(bsz, HG_HEADS, t // tt),
        in_specs=[
            spec(0), spec(1), spec(2), spec(3),
            pl.BlockSpec((hgrn_gamma.shape[0], HG_DK), lambda b, h, i: (0, h)),
            pl.BlockSpec((1, HG_DK), lambda b, h, i: (0, 0)),
        ],
        out_specs=pl.BlockSpec((None, tt, HG_DK), lambda b, h, i: (b, i, h)),
        out_shape=jax.ShapeDtypeStruct((bsz, t, HG_HEADS * HG_DK), BF16),
        scratch_shapes=[pltpu.VMEM((HG_DK, HG_DK), F32)],
        compiler_params=_params(("arbitrary", "arbitrary", "arbitrary")),
        name="hgrn2",
    )(proj, proj, proj, proj, hgrn_gamma, hgrn_gain.reshape(1, HG_DK))


def _route(logits_t):
    mx = jnp.max(logits_t, axis=0, keepdims=True)
    ex = jnp.exp(logits_t - mx)
    probs = ex / jnp.sum(ex, axis=0, keepdims=True)
    p = [probs[e:e + 1, :] for e in range(N_EXPERTS)]
    g = E_PER_GROUP
    scores = []
    for gi in range(N_GROUPS):
        pg = p[gi * g:(gi + 1) * g]
        best = None
        for a in range(g):
            for b in range(a + 1, g):
                pair = pg[a] + pg[b]
                best = pair if best is None else jnp.maximum(best, pair)
        scores.append(best)
    group_id = jnp.zeros_like(p[0])
    gates = [jnp.zeros_like(p[0]) for _ in range(g)]
    for gi in range(N_GROUPS):
        sel = None
        for gj in range(N_GROUPS):
            if gj == gi:
                continue
            cond = (scores[gi] > scores[gj]) if gj < gi else (scores[gi] >= scores[gj])
            sel = cond if sel is None else (sel & cond)
        group_id = jnp.where(sel, float(gi), group_id)
        pg = p[gi * g:(gi + 1) * g]
        chosen = []
        for a in range(g):
            rank = jnp.zeros_like(pg[a])
            for b in range(g):
                if b == a:
                    continue
                ahead = (pg[b] >= pg[a]) if b < a else (pg[b] > pg[a])
                rank = rank + jnp.where(ahead, 1.0, 0.0)
            chosen.append(sel & (rank < 2.0))
        denom = None
        for a in range(g):
            term = jnp.where(chosen[a], pg[a], 0.0)
            denom = term if denom is None else denom + term
        for a in range(g):
            gates[a] = jnp.where(chosen[a], pg[a] / denom, gates[a])
    return group_id, gates


def _outproj_kernel(a_ref, b_ref, x_ref, mod_ref, w_ref, lng_ref, lnb_ref, rwt_ref, rb_ref,
                    xo_ref, h_ref, row_ref, col_ref, *, alpha):
    half = a_ref.shape[1]
    tm = a_ref.shape[0]
    y = _dot(a_ref[...], w_ref[0:half, :]) + _dot(b_ref[...], w_ref[half:2 * half, :])
    g1 = mod_ref[2:3, :]
    sh2 = mod_ref[3:4, :]
    sc2 = mod_ref[4:5, :]
    r = alpha * x_ref[...] + (1.0 + g1) * y
    mu = jnp.mean(r, axis=1, keepdims=True)
    rc = r - mu
    var = jnp.mean(rc * rc, axis=1, keepdims=True)
    xn = rc * lax.rsqrt(var + 1e-5) * lng_ref[...] + lnb_ref[...]
    xo_ref[...] = xn
    h2 = xn * (1.0 + sc2) + sh2
    h_ref[...] = h2.astype(BF16)
    h_hi, h_mid, _ = _split3(h2)
    rw = rwt_ref[...]
    w_hi, w_mid, _ = _split3(rw)
    logits_t = _dot_nt(w_hi, h_hi) + _dot_nt(w_hi, h_mid) + _dot_nt(w_mid, h_hi) + rb_ref[...]
    group_id, gates = _route(logits_t)
    sel = [jnp.where(group_id == float(gi), 1.0, 0.0) for gi in range(N_GROUPS)]
    onehot = jnp.concatenate(sel + [jnp.zeros((8 - N_GROUPS, tm), F32)], axis=0).astype(BF16)
    src = lax.broadcasted_iota(jnp.int32, (tm, tm), 0)
    dst = lax.broadcasted_iota(jnp.int32, (tm, tm), 1)
    earlier = jnp.where(src < dst, 1.0, 0.0).astype(BF16)
    counts = _dot(onehot, earlier)
    rank = sel[0] * counts[0:1, :]
    for gi in range(1, N_GROUPS):
        rank = rank + sel[gi] * counts[gi:gi + 1, :]
    info = jnp.concatenate(gates + [group_id, rank], axis=0)
    row_ref[...] = jnp.concatenate(
        [group_id, rank, jnp.zeros((8 - 2, tm), F32)], axis=0)
    pad = jnp.zeros((LANES - info.shape[0], tm), F32)
    col_ref[...] = jnp.concatenate([info, pad], axis=0).T


def _outproj(a, b, x, mod_l, w_bf16, ln_g, ln_b, router_w, router_b, alpha):
    bsz, t, d = x.shape
    half = a.shape[2]
    tm = min(T_BLK, t)
    tok = lambda width: pl.BlockSpec((None, tm, width), lambda bi, i: (bi, i, 0))
    full = lambda r, c: pl.BlockSpec((r, c), lambda bi, i: (0, 0))
    return pl.pallas_call(
        functools.partial(_outproj_kernel, alpha=alpha),
        grid=(bsz, t // tm),
        in_specs=[
            tok(half), tok(half), tok(d),
            pl.BlockSpec((None, 6, d), lambda bi, i: (bi, 0, 0)),
            full(2 * half, d), full(1, d), full(1, d), full(N_EXPERTS, d), full(N_EXPERTS, 1),
        ],
        out_specs=[tok(d), tok(d), pl.BlockSpec((None, 8, tm), lambda bi, i: (bi, 0, i)),
                   tok(LANES)],
        out_shape=[
            jax.ShapeDtypeStruct((bsz, t, d), F32),
            jax.ShapeDtypeStruct((bsz, t, d), BF16),
            jax.ShapeDtypeStruct((bsz, 8, t), F32),
            jax.ShapeDtypeStruct((bsz, t, LANES), F32),
        ],
        compiler_params=_params(("arbitrary", "arbitrary")),
        name="outproj_ln_route",
    )(a, b, x, mod_l, w_bf16, ln_g.reshape(1, d), ln_b.reshape(1, d), router_w.T,
      router_b.reshape(N_EXPERTS, 1))


def _slab_rows(tm):
    extra = -(-(tm - MOE_ROWS_MAIN) // MOE_ROWS_EXTRA)
    return MOE_ROWS_MAIN + max(extra, 0) * MOE_ROWS_EXTRA


def _extra_chunks(count):
    return (jnp.maximum(count - MOE_ROWS_MAIN, 0) + MOE_ROWS_EXTRA - 1) // MOE_ROWS_EXTRA


def _moe_expert_kernel(cnt_ref, h_ref, row_ref, col_ref, wg_ref, wu_ref, wd_ref, z_ref):
    g = pl.program_id(0)
    blk = pl.program_id(1)
    count = cnt_ref[g * pl.num_programs(1) + blk]
    mine = row_ref[0:1, :] == g.astype(F32)
    rank = row_ref[1:2, :]
    info = col_ref[...]
    info_hi = info.astype(BF16)
    info_lo = (info - info_hi.astype(F32)).astype(BF16)

    def run_rows(r0, m):
        rid = (lax.broadcasted_iota(jnp.int32, (m, 1), 0) + r0).astype(F32)
        pick = _onehot((rank == rid) & mine)
        xs = _dot(pick, h_ref[...]).astype(BF16)
        gm = _dot(pick, info_hi) + _dot(pick, info_lo)
        acc = None
        for j in range(E_PER_GROUP):
            a = _dot(xs, wg_ref[j])
            u = _dot(xs, wu_ref[j])
            he = (a * _sigmoid(a) * u * gm[:, j:j + 1]).astype(BF16)
            part = _dot(he, wd_ref[j])
            acc = part if acc is None else acc + part
        z_ref[pl.ds(r0, m), :] = acc.astype(z_ref.dtype)

    run_rows(0, MOE_ROWS_MAIN)
    rest = z_ref.shape[0] - MOE_ROWS_MAIN
    if rest:
        z_ref[MOE_ROWS_MAIN:, :] = jnp.zeros((rest, z_ref.shape[1]), z_ref.dtype)

        def body(i, carry):
            run_rows(pl.multiple_of(MOE_ROWS_MAIN + i * MOE_ROWS_EXTRA, 16), MOE_ROWS_EXTRA)
            return carry

        lax.fori_loop(0, _extra_chunks(count), body, 0)


def _moe_combine_kernel(cnt_ref, z0_ref, z1_ref, z2_ref, z3_ref, col_ref, x_ref, mod_ref, lng_ref,
                        lnb_ref, o_ref, y_sc, *, alpha):
    blk = pl.program_id(0) * pl.num_programs(1) + pl.program_id(1)
    n_blk = pl.num_programs(0) * pl.num_programs(1)
    z_refs = (z0_ref, z1_ref, z2_ref, z3_ref)
    main = MOE_ROWS_MAIN
    grp = col_ref[:, E_PER_GROUP:E_PER_GROUP + 1]
    rank = col_ref[:, E_PER_GROUP + 1:E_PER_GROUP + 2]
    where_to = jnp.where(rank < float(main), grp * float(main) + rank, -1.0)
    lane = lax.broadcasted_iota(jnp.int32, (1, N_GROUPS * main), 1).astype(F32)
    pick = _onehot(where_to == lane)
    z_all = jnp.concatenate([zr[0:main, :] for zr in z_refs], axis=0)
    y_sc[...] = _dot(pick, z_all)
    lane_x = lax.broadcasted_iota(jnp.int32, (1, MOE_ROWS_EXTRA), 1).astype(F32)
    for gi in range(N_GROUPS):
        def body(i, carry, gi=gi):
            r0 = pl.multiple_of(main + i * MOE_ROWS_EXTRA, 16)
            hit = (grp == float(gi)) & ((rank - r0.astype(F32)) == lane_x)
            y_sc[...] += _dot(_onehot(hit), z_refs[gi][pl.ds(r0, MOE_ROWS_EXTRA), :])
            return carry

        lax.fori_loop(0, _extra_chunks(cnt_ref[gi * n_blk + blk]), body, 0)

    g2 = mod_ref[5:6, :]
    r = alpha * x_ref[...] + (1.0 + g2) * y_sc[...]
    mu = jnp.mean(r, axis=1, keepdims=True)
    rc = r - mu
    var = jnp.mean(rc * rc, axis=1, keepdims=True)
    o_ref[...] = rc * lax.rsqrt(var + 1e-5) * lng_ref[...] + lnb_ref[...]


def _moe(h2, rowinfo, colinfo, x, mod_l, wg, wu, wd, ln_g, ln_b, alpha):
    bsz, t, d = x.shape
    tm = min(T_BLK, t)
    nb = t // tm
    n_blk = bsz * nb
    dff = wg.shape[2]
    slab = _slab_rows(tm)
    group_of = rowinfo[:, 0, :].reshape(1, n_blk, tm)
    counts = jnp.sum(group_of == jnp.arange(N_GROUPS, dtype=F32).reshape(N_GROUPS, 1, 1), axis=2)
    counts = counts.astype(jnp.int32).reshape(N_GROUPS * n_blk)

    z = pl.pallas_call(
        _moe_expert_kernel,
        grid_spec=pltpu.PrefetchScalarGridSpec(
            num_scalar_prefetch=1,
            grid=(N_GROUPS, n_blk),
            in_specs=[
                pl.BlockSpec((None, tm, d), lambda g, i, c: (i // nb, i % nb, 0)),
                pl.BlockSpec((None, 8, tm), lambda g, i, c: (i // nb, 0, i % nb)),
                pl.BlockSpec((None, tm, LANES), lambda g, i, c: (i // nb, i % nb, 0)),
                pl.BlockSpec((E_PER_GROUP, d, dff), lambda g, i, c: (g, 0, 0)),
                pl.BlockSpec((E_PER_GROUP, d, dff), lambda g, i, c: (g, 0, 0)),
                pl.BlockSpec((E_PER_GROUP, dff, d), lambda g, i, c: (g, 0, 0)),
            ],
            out_specs=pl.BlockSpec((None, None, slab, d), lambda g, i, c: (g, i, 0, 0)),
        ),
        out_shape=jax.ShapeDtypeStruct((N_GROUPS, n_blk, slab, d), BF16),
        compiler_params=_params(("arbitrary", "arbitrary")),
        name="moe_experts",
    )(counts, h2, rowinfo, colinfo, wg, wu, wd)

    zspec = lambda gi: pl.BlockSpec((None, None, slab, d), lambda b, i, c: (gi, b * nb + i, 0, 0))
    tok = lambda width: pl.BlockSpec((None, tm, width), lambda b, i, c: (b, i, 0))
    return pl.pallas_call(
        functools.partial(_moe_combine_kernel, alpha=alpha),
        grid_spec=pltpu.PrefetchScalarGridSpec(
            num_scalar_prefetch=1,
            grid=(bsz, nb),
            in_specs=[
                zspec(0), zspec(1), zspec(2), zspec(3), tok(LANES), tok(d),
                pl.BlockSpec((None, 6, d), lambda b, i, c: (b, 0, 0)),
                pl.BlockSpec((1, d), lambda b, i, c: (0, 0)),
                pl.BlockSpec((1, d), lambda b, i, c: (0, 0)),
            ],
            out_specs=tok(d),
            scratch_shapes=[pltpu.VMEM((tm, d), F32)],
        ),
        out_shape=jax.ShapeDtypeStruct((bsz, t, d), F32),
        compiler_params=_params(("arbitrary", "arbitrary")),
        name="moe_combine_ln",
    )(counts, z, z, z, z, colinfo, x, mod_l, ln_g.reshape(1, d), ln_b.reshape(1, d))


def _lru_kernel(x_ref, g_ref, cw_ref, cb_ref, wa_ref, ba_ref, wx_ref, bx_ref, lam_ref,
                o_ref, xpad_sc, h_sc):
    tt = x_ref.shape[0]
    pad = 8

    @pl.when(pl.program_id(1) == 0)
    def _():
        xpad_sc[0:pad, :] = jnp.zeros((pad, xpad_sc.shape[1]), F32)
        h_sc[...] = jnp.zeros(h_sc.shape, F32)

    xpad_sc[pad:pad + tt, :] = x_ref[...].astype(F32)
    xc = cb_ref[...] + jnp.zeros((tt, x_ref.shape[1]), F32)
    for j in range(CONV_WIDTH):
        off = pad - (CONV_WIDTH - 1) + j
        xc = xc + cw_ref[j:j + 1, :] * xpad_sc[off:off + tt, :]
    xpad_sc[0:pad, :] = xpad_sc[tt:tt + pad, :]

    xb = xc.astype(BF16)
    r = _sigmoid(_dot(xb, wa_ref[...]) + ba_ref[...])
    i = _sigmoid(_dot(xb, wx_ref[...]) + bx_ref[...])
    lam = lam_ref[...].astype(F32)
    softplus_neg = jnp.maximum(-lam, 0.0) + jnp.log(1.0 + jnp.exp(-jnp.abs(lam)))
    log_a = -LRU_C * r * softplus_neg
    a = jnp.exp(log_a)
    u = jnp.sqrt(jnp.maximum(1.0 - jnp.exp(2.0 * log_a), 1e-12)) * (i * xc)

    rowi = lax.broadcasted_iota(jnp.int32, (tt, 1), 0)
    d = 1
    while d < tt:
        a_sh = jnp.where(rowi >= d, pltpu.roll(a, d, 0), 1.0)
        u_sh = jnp.where(rowi >= d, pltpu.roll(u, d, 0), 0.0)
        u = u + a * u_sh
        a = a * a_sh
        d *= 2
    hcur = u + a * h_sc[...]
    h_sc[...] = hcur[tt - 1:tt, :]

    gr = g_ref[...].astype(F32)
    gelu = 0.5 * gr * (1.0 + jnp.tanh(0.7978845608028654 * (gr + 0.044715 * gr * gr * gr)))
    o_ref[...] = (gelu * hcur).astype(o_ref.dtype)


def _rg_lru(proj, conv_w, conv_b, wa_dense, ba, wx_dense, bx, lam):
    bsz, t, _ = proj.shape
    tt = min(T_LRU, t)
    w = LRU_WIDTH
    nblk = w // LANES
    row = lambda a: a.reshape(1, w)
    full = lambda r, c: pl.BlockSpec((r, c), lambda b, i: (0, 0))
    return pl.pallas_call(
        _lru_kernel,
        grid=(bsz, t // tt),
        in_specs=[
            pl.BlockSpec((None, tt, w), lambda b, i: (b, i, 0)),
            pl.BlockSpec((None, tt, w), lambda b, i: (b, i, 1)),
            full(CONV_WIDTH, w), full(1, w), full(w, w), full(1, w), full(w, w), full(1, w),
            full(1, w),
        ],
        out_specs=pl.BlockSpec((None, tt, w), lambda b, i: (b, i, 0)),
        out_shape=jax.ShapeDtypeStruct((bsz, t, w), BF16),
        scratch_shapes=[pltpu.VMEM((tt + 8, w), F32), pltpu.VMEM((1, w), F32)],
        compiler_params=_params(("arbitrary", "arbitrary")),
        name="rg_lru",
    )(proj, proj, conv_w, row(conv_b), wa_dense, row(ba), wx_dense, row(bx), row(lam))


def _sb_kernel(q_ref, k_ref, v_ref, o_ref, r_sc, acc_sc, *bufs, tile):
    qi = pl.program_id(2)
    d = SB_HEAD_DIM
    tk = tile // 2
    z_bufs, lb_bufs, l_bufs, w_bufs = (bufs[i * SB_SETS:(i + 1) * SB_SETS] for i in range(4))
    lane = lax.broadcasted_iota(jnp.int32, (1, 2 * d), 1)
    q = q_ref[...]
    zero = jnp.zeros_like(q)
    q2 = jnp.concatenate([jnp.where(lane < d, q, zero), jnp.where(lane >= d, q, zero)], axis=0)
    rj = lax.broadcasted_iota(jnp.int32, (tk, tk), 0)
    cs = lax.broadcasted_iota(jnp.int32, (tk, tk), 1)
    upper = jnp.where(rj > cs, 1.0, 0.0).astype(BF16)

    r_sc[...] = jnp.zeros(r_sc.shape, F32)
    acc_sc[...] = jnp.zeros(acc_sc.shape, F32)
    n_sub = 2 * qi + 2
    every = slice(0, 2 * tile)
    per_head = (slice(0, tile), slice(tile, 2 * tile))

    def key_start(j):
        return pl.multiple_of((n_sub - 1 - j) * tk, tk)

    def strict_mask(j):
        rowp = lax.broadcasted_iota(jnp.int32, (tile, tk), 0)
        colp = lax.broadcasted_iota(jnp.int32, (tile, tk), 1) + (1 - j) * tk
        strict = colp < rowp
        return jnp.concatenate([strict, strict], axis=0)

    def logits(j, b, rows=every):
        z_bufs[b][rows, :] = _dot_nt(q2[rows, :], k_ref[pl.ds(key_start(j), tk), :])

    def gates(j, b, rows=every, masked=False):
        z = z_bufs[b][rows, :]
        log_1m = jnp.log(1.0 + jnp.exp2(-jnp.abs(z))) * (-LOG2E) - jnp.maximum(z, 0.0)
        lb_bufs[b][rows, :] = z + log_1m
        if masked:
            log_1m = jnp.where(strict_mask(j), log_1m, 0.0)
        l_bufs[b][rows, :] = log_1m.astype(BF16)

    def weights(j, b, rows=every, masked=False):
        log_1m = l_bufs[b][rows, :]
        after = _dot(log_1m, upper) + r_sc[rows, :]
        w = jnp.exp2(lb_bufs[b][rows, :] + after)
        if masked:
            w = jnp.where(strict_mask(j), w, 0.0)
        w_bufs[b][rows, :] = w.astype(BF16)
        r_sc[rows, :] = after[:, 0:1] + log_1m[:, 0:1].astype(F32)

    def values(j, b, rows=every):
        acc_sc[rows, :] += _dot(w_bufs[b][rows, :], v_ref[pl.ds(key_start(j), tk), :])

    @pl.when(qi == 0)
    def _():
        logits(0, 0)
        logits(1, 1)
        gates(0, 0, masked=True)
        gates(1, 1, masked=True)
        weights(0, 0, masked=True)
        weights(1, 1, masked=True)
        values(0, 0)
        values(1, 1)

    @pl.when(qi > 0)
    def _():
        logits(0, 0)
        logits(1, 1)
        gates(0, 0, masked=True)
        logits(2, 2)
        gates(1, 1, masked=True)
        weights(0, 0, masked=True)
        gates(2, 2)
        weights(1, 1, masked=True)
        values(0, 0)
        weights(2, 2)
        values(1, 1)
        values(2, 2)

        def alive():
            return (jnp.max(r_sc[...]) > -SB_DEAD_LOG2).astype(jnp.int32)

        def cond(carry):
            j, live = carry
            return (j < n_sub) & (live > 0)

        def body(carry):
            j, _ = carry
            for rows in per_head:
                logits(j, 0, rows)
            for rows in per_head:
                gates(j, 0, rows)
            for rows in per_head:
                weights(j, 0, rows)
            for rows in per_head:
                values(j, 0, rows)
            return j + 1, alive()

        lax.while_loop(cond, body, (jnp.int32(3), alive()))

    acc = acc_sc[...]
    o_ref[...] = jnp.where(lane < d, acc[0:tile, :], acc[tile:2 * tile, :]).astype(o_ref.dtype)


def _sb_attention(proj):
    bsz, t, _ = proj.shape
    tq = min(T_SB, t)
    pairs = SB_HEADS // 2
    wblk = 2 * SB_HEAD_DIM
    base = 2 * LRU_WIDTH // wblk
    return pl.pallas_call(
        functools.partial(_sb_kernel, tile=tq),
        grid=(bsz, pairs, t // tq),
        in_specs=[
            pl.BlockSpec((None, tq, wblk), lambda b, h, i: (b, i, base + h)),
            pl.BlockSpec((None, t, wblk), lambda b, h, i: (b, 0, base + pairs + h)),
            pl.BlockSpec((None, t, wblk), lambda b, h, i: (b, 0, base + 2 * pairs + h)),
        ],
        out_specs=pl.BlockSpec((None, tq, wblk), lambda b, h, i: (b, i, h)),
        out_shape=jax.ShapeDtypeStruct((bsz, t, SB_HEADS * SB_HEAD_DIM), BF16),
        scratch_shapes=[pltpu.VMEM((2 * tq, 1), F32), pltpu.VMEM((2 * tq, wblk), F32)]
        + [pltpu.VMEM((2 * tq, tq // 2), F32)] * (2 * SB_SETS)
        + [pltpu.VMEM((2 * tq, tq // 2), BF16)] * (2 * SB_SETS),
        compiler_params=_params(("arbitrary", "arbitrary", "arbitrary")),
        name="sb_attn",
    )(proj, proj, proj)


def _block_diag(w):
    g, n, _ = w.shape
    eye = jnp.eye(g, dtype=w.dtype)
    return (eye[:, None, :, None] * w[:, :, None, :]).reshape(g * n, g * n)


def kernel(x, c, ada_w, ada_b, ln_g, ln_b, even_w_in, even_w_out, diff_lambda, diff_gain, hgrn_gamma, hgrn_gain, odd_w_in, odd_w_out, conv_w, conv_b, lru_wa, lru_ba, lru_wx, lru_bx, lru_lambda, router_w, router_b, moe_w_gate, moe_w_up, moe_w_down):
    depth = ada_w.shape[0]
    bsz, t, d = x.shape
    alpha = (2.0 * depth) ** 0.25
    mod = _ada_mod(c, ada_w, ada_b).reshape(depth, bsz, 6, d)
    for l in range(depth):
        j = l // 2
        mod_l = mod[l]
        if l % 2 == 0:
            lam_init = 0.8 - 0.6 * math.exp(-0.3 * l)
            proj = _inproj(x, mod_l, even_w_in[j].astype(BF16), q_chunk=0)
            mix_a = _diff_attention(proj, diff_lambda[j], diff_gain[j], lam_init)
            mix_b = _hgrn2(proj, hgrn_gamma, hgrn_gain[j], l)
            w_out = even_w_out[j]
        else:
            proj = _inproj(x, mod_l, odd_w_in[j].astype(BF16), q_chunk=2 * LRU_WIDTH // PROJ_CHUNK)
            mix_a = _rg_lru(proj, conv_w[j], conv_b[j], _block_diag(lru_wa[j]).astype(BF16),
                            lru_ba[j], _block_diag(lru_wx[j]).astype(BF16), lru_bx[j],
                            lru_lambda[j])
            mix_b = _sb_attention(proj)
            w_out = odd_w_out[j]
        x, h2, rowinfo, colinfo = _outproj(mix_a, mix_b, x, mod_l, w_out.astype(BF16),
                                           ln_g[l, 0], ln_b[l, 0], router_w, router_b, alpha)
        x = _moe(h2, rowinfo, colinfo, x, mod_l, moe_w_gate[l].astype(BF16),
                 moe_w_up[l].astype(BF16), moe_w_down[l].astype(BF16), ln_g[l, 1], ln_b[l, 1],
                 alpha)
    return x
```

```python
import functools
import math

import jax
import jax.numpy as jnp
from jax import lax
from jax.experimental import pallas as pl
from jax.experimental.pallas import tpu as pltpu

F32 = jnp.float32
BF16 = jnp.bfloat16

DA_HEADS = 4
DA_HEAD_DIM = 64
HG_HEADS = 4
HG_DK = 128
HG_CHUNK = 64
LRU_WIDTH = 512
LRU_BLOCKS = 8
CONV_WIDTH = 4
LRU_C = 8.0
SB_HEADS = 8
SB_HEAD_DIM = 64
N_EXPERTS = 16
N_GROUPS = 4
E_PER_GROUP = N_EXPERTS // N_GROUPS
D_FF = 512

LANES = 128
NEG_BIG = -1e30
LOG2E = 1.4426950408889634
Q_PRESCALE = DA_HEAD_DIM ** -0.5 * LOG2E
PROJ_CHUNK = 512
VMEM_LIMIT = 56 * 1024 * 1024

TM_PROJ = 512
TQ_ATT = 512
T_SB = 512
SB_SETS = 3
SB_DEAD_LOG2 = 160.0
DA_DEAD_LOG2 = 160.0
T_HG = 512
T_LRU = 256
T_BLK = 1024
MOE_ROWS_MAIN = 320
MOE_ROWS_EXTRA = 128


def _params(sem):
    return pltpu.CompilerParams(dimension_semantics=sem, vmem_limit_bytes=VMEM_LIMIT)


def _sigmoid(x):
    return 1.0 / (1.0 + jnp.exp(-x))


def _dot(a, b):
    return jnp.dot(a, b, preferred_element_type=F32)


def _dot_nt(a, b):
    return lax.dot_general(a, b, (((1,), (1,)), ((), ())), preferred_element_type=F32)


def _onehot(mask):
    return jnp.where(mask, 1.0, 0.0).astype(BF16)


def _split3(x):
    hi = x.astype(BF16)
    r1 = x - hi.astype(F32)
    mid = r1.astype(BF16)
    lo = (r1 - mid.astype(F32)).astype(BF16)
    return hi, mid, lo


def _ada_kernel(c_ref, w_ref, b_ref, o_ref):
    c = c_ref[...]
    cond = c * _sigmoid(c)
    hi, mid, _ = _split3(cond)
    w = w_ref[...].astype(BF16)
    o_ref[...] = _dot(hi, w) + _dot(mid, w) + b_ref[...]


def _ada_mod(c, ada_w, ada_b):
    depth, d, d6 = ada_w.shape
    bsz = c.shape[0]
    n_col = d6 // d
    return pl.pallas_call(
        _ada_kernel,
        grid=(depth, n_col),
        in_specs=[
            pl.BlockSpec((bsz, d), lambda l, j: (0, 0)),
            pl.BlockSpec((None, d, d), lambda l, j: (l, 0, j)),
            pl.BlockSpec((None, 1, d), lambda l, j: (l, 0, j)),
        ],
        out_specs=pl.BlockSpec((None, bsz, d), lambda l, j: (l, 0, j)),
        out_shape=jax.ShapeDtypeStruct((depth, bsz, d6), F32),
        compiler_params=_params(("arbitrary", "arbitrary")),
        name="ada_mod",
    )(c, ada_w, ada_b.reshape(depth, 1, d6))


def _inproj_kernel(x_ref, mod_ref, w_ref, o_ref, *, col_chunk, q_chunk):
    sh = mod_ref[0:1, :]
    sc = mod_ref[1:2, :]
    h = (x_ref[...] * (1.0 + sc) + sh).astype(BF16)
    for j in range(o_ref.shape[1] // col_chunk):
        cols = slice(j * col_chunk, (j + 1) * col_chunk)
        y = _dot(h, w_ref[:, cols])
        if j == q_chunk:
            y = y * Q_PRESCALE
        o_ref[:, cols] = y.astype(o_ref.dtype)


def _inproj(x, mod_l, w_bf16, q_chunk):
    bsz, t, d = x.shape
    width = w_bf16.shape[1]
    tm = min(TM_PROJ, t)
    return pl.pallas_call(
        functools.partial(_inproj_kernel, col_chunk=PROJ_CHUNK, q_chunk=q_chunk),
        grid=(bsz, t // tm),
        in_specs=[
            pl.BlockSpec((None, tm, d), lambda b, i: (b, i, 0)),
            pl.BlockSpec((None, 6, d), lambda b, i: (b, 0, 0)),
            pl.BlockSpec((d, width), lambda b, i: (0, 0)),
        ],
        out_specs=pl.BlockSpec((None, tm, width), lambda b, i: (b, i, 0)),
        out_shape=jax.ShapeDtypeStruct((bsz, t, width), BF16),
        compiler_params=_params(("arbitrary", "arbitrary")),
        name="inproj",
    )(x, mod_l, w_bf16)


def _diffattn_kernel(q_ref, k_ref, v_ref, lam_ref, gain_ref, o_ref, m_sc, acc_sc, s0_sc, s1_sc,
                     p0_sc, p1_sc, a0_sc, a1_sc, kn_sc, *, tile, lam_init):
    h = pl.program_id(1)
    qi = pl.program_id(2)
    dh = DA_HEAD_DIM
    hd = 2 * dh
    tk = tile // 2
    reps = tk // LANES
    s_bufs, p_bufs, a_bufs = (s0_sc, s1_sc), (p0_sc, p1_sc), (a0_sc, a1_sc)

    lane = lax.broadcasted_iota(jnp.int32, (1, hd), 1)
    q = q_ref[...]
    zero = jnp.zeros_like(q)
    q2 = jnp.concatenate([jnp.where(lane < dh, q, zero), jnp.where(lane >= dh, q, zero)], axis=0)

    hf = jnp.full((1, 1), h + 1, jnp.int32).astype(F32)
    slope = jnp.exp2(hf * (-8.0 / DA_HEADS)) * LOG2E
    col = lax.broadcasted_iota(jnp.int32, (1, tk), 1)
    ones = jnp.ones((tk, hd), BF16)

    m_sc[...] = jnp.full(m_sc.shape, NEG_BIG, F32)
    acc_sc[...] = jnp.zeros(acc_sc.shape, F32)

    @pl.when(qi == 0)
    def _():
        kf = k_ref[...].astype(F32)
        k_sq = jnp.max(jnp.sum(kf * kf, axis=1, keepdims=True), axis=0, keepdims=True)
        kn_sc[...] = jnp.broadcast_to(jnp.sqrt(k_sq), kn_sc.shape)

    n_sub = 2 * qi + 2

    def key_start(j):
        return pl.multiple_of((n_sub - 1 - j) * tk, tk)

    def scores(j, slot):
        ks = key_start(j)
        bias = (col + (ks - qi * tile)).astype(F32) * slope
        s_bufs[slot][...] = _dot_nt(q2, k_ref[pl.ds(ks, tk), :]) + bias

    def softmax(j, slot, masked):
        s = s_bufs[slot][...]
        if masked:
            rowp = lax.broadcasted_iota(jnp.int32, (tile, tk), 0) + qi * tile
            colp = lax.broadcasted_iota(jnp.int32, (tile, tk), 1) + key_start(j)
            keep = colp <= rowp
            s = jnp.where(jnp.concatenate([keep, keep], axis=0), s, NEG_BIG)
        m_old = m_sc[...]
        m_new = jnp.maximum(m_old, jnp.max(s, axis=1, keepdims=True))
        p_bufs[slot][...] = jnp.exp2(s - jnp.concatenate([m_new] * reps, axis=1)).astype(BF16)
        a_bufs[slot][...] = jnp.exp2(m_old - m_new)
        m_sc[...] = m_new

    def values(j, slot):
        v_aug = jnp.concatenate([v_ref[pl.ds(key_start(j), tk), :], ones], axis=1)
        alpha = a_bufs[slot][...]
        acc_sc[...] = (jnp.concatenate([alpha, alpha], axis=1) * acc_sc[...]
                       + _dot(p_bufs[slot][...], v_aug))

    scores(0, 0)
    scores(1, 1)
    softmax(0, 0, True)

    @pl.when(qi == 0)
    def _():
        softmax(1, 1, True)
        values(0, 0)
        values(1, 1)

    @pl.when(qi > 0)
    def _():
        scores(2, 0)
        softmax(1, 1, True)
        values(0, 0)
        scores(3, 1)
        softmax(2, 0, False)
        values(1, 1)
        qf = q.astype(F32)
        q_norm = jnp.sqrt(jnp.max(jnp.sum(qf * qf, axis=1, keepdims=True), axis=0, keepdims=True))
        m_min = jnp.min(m_sc[...], axis=0, keepdims=True)[:, 0:1]
        reach = (q_norm * kn_sc[0:1, 0:1] - m_min + DA_DEAD_LOG2) / slope
        first_dead = jnp.floor((reach - 1.0) / tk) + 3.0
        first_dead = jnp.max(jnp.clip(first_dead, 0.0, 1e6)).astype(jnp.int32)
        pairs_end = jnp.maximum(2, jnp.minimum(qi + 1, (first_dead + 1) // 2))

        def body(i, carry):
            t = 2 * i
            scores(t, 0)
            softmax(t - 1, 1, False)
            values(t - 2, 0)
            scores(t + 1, 1)
            softmax(t, 0, False)
            values(t - 1, 1)
            return carry

        lax.fori_loop(2, pairs_end, body, 0)
        t = 2 * pairs_end
        softmax(t - 1, 1, False)
        values(t - 2, 0)
        values(t - 1, 1)

    lv = lam_ref[...].astype(F32)
    dots = jnp.sum(lv[0:1, :] * lv[1:2, :], axis=1, keepdims=True)
    dots2 = jnp.sum(lv[2:3, :] * lv[3:4, :], axis=1, keepdims=True)
    lam = jnp.exp(dots) - jnp.exp(dots2) + lam_init
    acc = acc_sc[...]
    o0 = acc[0:tile, 0:hd] / acc[0:tile, hd:2 * hd]
    o1 = acc[tile:2 * tile, 0:hd] / acc[tile:2 * tile, hd:2 * hd]
    o = o0 - lam * o1
    ms = jnp.mean(o * o, axis=1, keepdims=True)
    o = o * lax.rsqrt(ms + 1e-6) * gain_ref[...] * (1.0 - lam_init)
    o_ref[...] = o.astype(o_ref.dtype)


def _diff_attention(proj, diff_lambda, diff_gain, lam_init):
    bsz, t, _ = proj.shape
    tile = min(TQ_ATT, t)
    hd = 2 * DA_HEAD_DIM
    kern = functools.partial(_diffattn_kernel, tile=tile, lam_init=lam_init)
    return pl.pallas_call(
        kern,
        grid=(bsz, DA_HEADS, t // tile),
        in_specs=[
            pl.BlockSpec((None, tile, hd), lambda b, h, i: (b, i, h)),
            pl.BlockSpec((None, t, hd), lambda b, h, i: (b, 0, DA_HEADS + h)),
            pl.BlockSpec((None, t, hd), lambda b, h, i: (b, 0, 2 * DA_HEADS + h)),
            pl.BlockSpec((4, DA_HEAD_DIM), lambda b, h, i: (0, 0)),
            pl.BlockSpec((1, hd), lambda b, h, i: (0, 0)),
        ],
        out_specs=pl.BlockSpec((None, tile, hd), lambda b, h, i: (b, i, h)),
        out_shape=jax.ShapeDtypeStruct((bsz, t, DA_HEADS * hd), BF16),
        scratch_shapes=[
            pltpu.VMEM((2 * tile, LANES), F32),
            pltpu.VMEM((2 * tile, 2 * hd), F32),
            pltpu.VMEM((2 * tile, tile // 2), F32),
            pltpu.VMEM((2 * tile, tile // 2), F32),
            pltpu.VMEM((2 * tile, tile // 2), BF16),
            pltpu.VMEM((2 * tile, tile // 2), BF16),
            pltpu.VMEM((2 * tile, LANES), F32),
            pltpu.VMEM((2 * tile, LANES), F32),
            pltpu.VMEM((8, LANES), F32),
        ],
        compiler_params=_params(("arbitrary", "arbitrary", "arbitrary")),
        name="diff_attn",
    )(proj, proj, proj, diff_lambda, diff_gain.reshape(1, hd))


def _hgrn_kernel(q_ref, f_ref, i_ref, g_ref, gamma_ref, gain_ref, o_ref, st_sc, *, layer):
    @pl.when(pl.program_id(1) == 0)
    def _():
        st_sc[...] = jnp.zeros(st_sc.shape, F32)

    gam = gamma_ref[...].astype(F32)
    e = jnp.exp(gam - jnp.max(gam, axis=0, keepdims=True))
    sm = e / jnp.sum(e, axis=0, keepdims=True)
    lb_all = jnp.sum(sm[0:layer + 1, :], axis=0, keepdims=True)

    c = HG_CHUNK
    row = lax.broadcasted_iota(jnp.int32, (c, c), 0)
    col = lax.broadcasted_iota(jnp.int32, (c, c), 1)
    tril = col <= row
    tril_bf = jnp.where(tril, 1.0, 0.0).astype(BF16)
    gain = gain_ref[...]

    heads = range(HG_HEADS)
    hcols = [slice(h * HG_DK, (h + 1) * HG_DK) for h in heads]
    lbs = [lb_all[:, hc] for hc in hcols]
    for n in range(q_ref.shape[0] // c):
        rows = slice(n * c, (n + 1) * c)
        sig = [_sigmoid(f_ref[rows, hc].astype(F32)) for hc in hcols]
        logf = [jnp.log(lbs[h] + (1.0 - lbs[h]) * sig[h]) for h in heads]
        kk = [(1.0 - lbs[h]) * (1.0 - sig[h]) for h in heads]
        parts = [_split3(x) for x in logf]
        b = [_dot(tril_bf, p[0]) + _dot(tril_bf, p[1]) + _dot(tril_bf, p[2]) for p in parts]
        b_mid = [x[c // 2 - 1:c // 2, :] for x in b]
        b_last = [x[c - 1:c, :] for x in b]
        qh = [q_ref[rows, hc].astype(F32) for hc in hcols]
        qs = [x * _sigmoid(x) for x in qh]
        v = [i_ref[rows, hc] for hc in hcols]
        att = [_dot_nt((qs[h] * jnp.exp(b[h] - b_mid[h])).astype(BF16),
                       (kk[h] * jnp.exp(b_mid[h] - b[h])).astype(BF16)) for h in heads]
        att = [jnp.where(tril, x, 0.0).astype(BF16) for x in att]
        o_intra = [_dot(att[h], v[h]) for h in heads]
        kd = [(kk[h] * jnp.exp(b_last[h] - b[h])).astype(BF16) for h in heads]
        ds_t = [_dot(v[h].T, kd[h]) for h in heads]
        st = [st_sc[h] for h in heads]
        o_inter = [_dot_nt((qs[h] * jnp.exp(b[h])).astype(BF16), st[h].astype(BF16))
                   for h in heads]
        for h in heads:
            st_sc[h] = st[h] * jnp.exp(b_last[h]) + ds_t[h]
        for h in heads:
            o = o_intra[h] + o_inter[h]
            gh = g_ref[rows, hcols[h]].astype(F32)
            ms = jnp.mean(o * o, axis=1, keepdims=True)
            o = o * lax.rsqrt(ms + 1e-6) * gain * (gh * _sigmoid(gh))
            o_ref[rows, hcols[h]] = o.astype(o_ref.dtype)


def _hgrn2(proj, hgrn_gamma, hgrn_gain, layer):
    bsz, t, _ = proj.shape
    tt = min(T_HG, t)
    width = HG_HEADS * HG_DK
    base = 3 * DA_HEADS * 2 * DA_HEAD_DIM // width
    spec = lambda k: pl.BlockSpec((None, tt, width), lambda b, i: (b, i, base + k))
    return pl.pallas_call(
        functools.partial(_hgrn_kernel, layer=layer),
        grid=(bsz, t // tt),
        in_specs=[
            spec(0), spec(1), spec(2), spec(3),
            pl.BlockSpec((hgrn_gamma.shape[0], width), lambda b, i: (0, 0)),
            pl.BlockSpec((1, HG_DK), lambda b, i: (0, 0)),
        ],
        out_specs=pl.BlockSpec((None, tt, width), lambda b, i: (b, i, 0)),
        out_shape=jax.ShapeDtypeStruct((bsz, t, width), BF16),
        scratch_shapes=[pltpu.VMEM((HG_HEADS, HG_DK, HG_DK), F32)],
        compiler_params=_params(("arbitrary", "arbitrary")),
        name="hgrn2",
    )(proj, proj, proj, proj, hgrn_gamma, hgrn_gain.reshape(1, HG_DK))


def _route(logits_t):
    mx = jnp.max(logits_t, axis=0, keepdims=True)
    ex = jnp.exp(logits_t - mx)
    probs = ex / jnp.sum(ex, axis=0, keepdims=True)
    p = [probs[e:e + 1, :] for e in range(N_EXPERTS)]
    g = E_PER_GROUP
    scores = []
    for gi in range(N_GROUPS):
        pg = p[gi * g:(gi + 1) * g]
        best = None
        for a in range(g):
            for b in range(a + 1, g):
                pair = pg[a] + pg[b]
                best = pair if best is None else jnp.maximum(best, pair)
        scores.append(best)
    group_id = jnp.zeros_like(p[0])
    gates = [jnp.zeros_like(p[0]) for _ in range(g)]
    for gi in range(N_GROUPS):
        sel = None
        for gj in range(N_GROUPS):
            if gj == gi:
                continue
            cond = (scores[gi] > scores[gj]) if gj < gi else (scores[gi] >= scores[gj])
            sel = cond if sel is None else (sel & cond)
        group_id = jnp.where(sel, float(gi), group_id)
        pg = p[gi * g:(gi + 1) * g]
        chosen = []
        for a in range(g):
            rank = jnp.zeros_like(pg[a])
            for b in range(g):
                if b == a:
                    continue
                ahead = (pg[b] >= pg[a]) if b < a else (pg[b] > pg[a])
                rank = rank + jnp.where(ahead, 1.0, 0.0)
            chosen.append(sel & (rank < 2.0))
        denom = None
        for a in range(g):
            term = jnp.where(chosen[a], pg[a], 0.0)
            denom = term if denom is None else denom + term
        for a in range(g):
            gates[a] = jnp.where(chosen[a], pg[a] / denom, gates[a])
    return group_id, gates


def _outproj_kernel(a_ref, b_ref, x_ref, mod_ref, w_ref, lng_ref, lnb_ref, rwt_ref, rb_ref,
                    xo_ref, h_ref, row_ref, col_ref, *, alpha):
    half = a_ref.shape[1]
    tm = a_ref.shape[0]
    y = _dot(a_ref[...], w_ref[0:half, :]) + _dot(b_ref[...], w_ref[half:2 * half, :])
    g1 = mod_ref[2:3, :]
    sh2 = mod_ref[3:4, :]
    sc2 = mod_ref[4:5, :]
    r = alpha * x_ref[...] + (1.0 + g1) * y
    mu = jnp.mean(r, axis=1, keepdims=True)
    rc = r - mu
    var = jnp.mean(rc * rc, axis=1, keepdims=True)
    xn = rc * lax.rsqrt(var + 1e-5) * lng_ref[...] + lnb_ref[...]
    xo_ref[...] = xn
    h2 = xn * (1.0 + sc2) + sh2
    h_ref[...] = h2.astype(BF16)
    h_hi, h_mid, _ = _split3(h2)
    rw = rwt_ref[...]
    w_hi, w_mid, _ = _split3(rw)
    logits_t = _dot_nt(w_hi, h_hi) + _dot_nt(w_hi, h_mid) + _dot_nt(w_mid, h_hi) + rb_ref[...]
    group_id, gates = _route(logits_t)
    sel = [jnp.where(group_id == float(gi), 1.0, 0.0) for gi in range(N_GROUPS)]
    onehot = jnp.concatenate(sel + [jnp.zeros((8 - N_GROUPS, tm), F32)], axis=0).astype(BF16)
    src = lax.broadcasted_iota(jnp.int32, (tm, tm), 0)
    dst = lax.broadcasted_iota(jnp.int32, (tm, tm), 1)
    earlier = jnp.where(src < dst, 1.0, 0.0).astype(BF16)
    counts = _dot(onehot, earlier)
    rank = sel[0] * counts[0:1, :]
    for gi in range(1, N_GROUPS):
        rank = rank + sel[gi] * counts[gi:gi + 1, :]
    info = jnp.concatenate(gates + [group_id, rank], axis=0)
    row_ref[...] = jnp.concatenate(
        [group_id, rank, jnp.zeros((8 - 2, tm), F32)], axis=0)
    pad = jnp.zeros((LANES - info.shape[0], tm), F32)
    col_ref[...] = jnp.concatenate([info, pad], axis=0).T


def _outproj(a, b, x, mod_l, w_bf16, ln_g, ln_b, router_w, router_b, alpha):
    bsz, t, d = x.shape
    half = a.shape[2]
    tm = min(T_BLK, t)
    tok = lambda width: pl.BlockSpec((None, tm, width), lambda bi, i: (bi, i, 0))
    full = lambda r, c: pl.BlockSpec((r, c), lambda bi, i: (0, 0))
    return pl.pallas_call(
        functools.partial(_outproj_kernel, alpha=alpha),
        grid=(bsz, t // tm),
        in_specs=[
            tok(half), tok(half), tok(d),
            pl.BlockSpec((None, 6, d), lambda bi, i: (bi, 0, 0)),
            full(2 * half, d), full(1, d), full(1, d), full(N_EXPERTS, d), full(N_EXPERTS, 1),
        ],
        out_specs=[tok(d), tok(d), pl.BlockSpec((None, 8, tm), lambda bi, i: (bi, 0, i)),
                   tok(LANES)],
        out_shape=[
            jax.ShapeDtypeStruct((bsz, t, d), F32),
            jax.ShapeDtypeStruct((bsz, t, d), BF16),
            jax.ShapeDtypeStruct((bsz, 8, t), F32),
            jax.ShapeDtypeStruct((bsz, t, LANES), F32),
        ],
        compiler_params=_params(("arbitrary", "arbitrary")),
        name="outproj_ln_route",
    )(a, b, x, mod_l, w_bf16, ln_g.reshape(1, d), ln_b.reshape(1, d), router_w.T,
      router_b.reshape(N_EXPERTS, 1))


def _slab_rows(tm):
    extra = -(-(tm - MOE_ROWS_MAIN) // MOE_ROWS_EXTRA)
    return MOE_ROWS_MAIN + max(extra, 0) * MOE_ROWS_EXTRA


def _extra_chunks(count):
    return (jnp.maximum(count - MOE_ROWS_MAIN, 0) + MOE_ROWS_EXTRA - 1) // MOE_ROWS_EXTRA


def _moe_expert_kernel(cnt_ref, h_ref, row_ref, col_ref, wg_ref, wu_ref, wd_ref, z_ref):
    g = pl.program_id(0)
    blk = pl.program_id(1)
    count = cnt_ref[g * pl.num_programs(1) + blk]
    mine = row_ref[0:1, :] == g.astype(F32)
    rank = row_ref[1:2, :]
    info = col_ref[...]
    info_hi = info.astype(BF16)
    info_lo = (info - info_hi.astype(F32)).astype(BF16)

    def run_rows(r0, m):
        rid = (lax.broadcasted_iota(jnp.int32, (m, 1), 0) + r0).astype(F32)
        pick = _onehot((rank == rid) & mine)
        xs = _dot(pick, h_ref[...]).astype(BF16)
        gm = _dot(pick, info_hi) + _dot(pick, info_lo)
        acc = None
        for j in range(E_PER_GROUP):
            a = _dot(xs, wg_ref[j])
            u = _dot(xs, wu_ref[j])
            he = (a * _sigmoid(a) * u * gm[:, j:j + 1]).astype(BF16)
            part = _dot(he, wd_ref[j])
            acc = part if acc is None else acc + part
        z_ref[pl.ds(r0, m), :] = acc.astype(z_ref.dtype)

    run_rows(0, MOE_ROWS_MAIN)
    rest = z_ref.shape[0] - MOE_ROWS_MAIN
    if rest:
        z_ref[MOE_ROWS_MAIN:, :] = jnp.zeros((rest, z_ref.shape[1]), z_ref.dtype)

        def body(i, carry):
            run_rows(pl.multiple_of(MOE_ROWS_MAIN + i * MOE_ROWS_EXTRA, 16), MOE_ROWS_EXTRA)
            return carry

        lax.fori_loop(0, _extra_chunks(count), body, 0)


def _moe_combine_kernel(cnt_ref, z0_ref, z1_ref, z2_ref, z3_ref, col_ref, x_ref, mod_ref, lng_ref,
                        lnb_ref, o_ref, y_sc, *, alpha):
    blk = pl.program_id(0) * pl.num_programs(1) + pl.program_id(1)
    n_blk = pl.num_programs(0) * pl.num_programs(1)
    z_refs = (z0_ref, z1_ref, z2_ref, z3_ref)
    main = MOE_ROWS_MAIN
    grp = col_ref[:, E_PER_GROUP:E_PER_GROUP + 1]
    rank = col_ref[:, E_PER_GROUP + 1:E_PER_GROUP + 2]
    where_to = jnp.where(rank < float(main), grp * float(main) + rank, -1.0)
    lane = lax.broadcasted_iota(jnp.int32, (1, N_GROUPS * main), 1).astype(F32)
    pick = _onehot(where_to == lane)
    z_all = jnp.concatenate([zr[0:main, :] for zr in z_refs], axis=0)
    y_sc[...] = _dot(pick, z_all)
    lane_x = lax.broadcasted_iota(jnp.int32, (1, MOE_ROWS_EXTRA), 1).astype(F32)
    for gi in range(N_GROUPS):
        def body(i, carry, gi=gi):
            r0 = pl.multiple_of(main + i * MOE_ROWS_EXTRA, 16)
            hit = (grp == float(gi)) & ((rank - r0.astype(F32)) == lane_x)
            y_sc[...] += _dot(_onehot(hit), z_refs[gi][pl.ds(r0, MOE_ROWS_EXTRA), :])
            return carry

        lax.fori_loop(0, _extra_chunks(cnt_ref[gi * n_blk + blk]), body, 0)

    g2 = mod_ref[5:6, :]
    r = alpha * x_ref[...] + (1.0 + g2) * y_sc[...]
    mu = jnp.mean(r, axis=1, keepdims=True)
    rc = r - mu
    var = jnp.mean(rc * rc, axis=1, keepdims=True)
    o_ref[...] = rc * lax.rsqrt(var + 1e-5) * lng_ref[...] + lnb_ref[...]


def _moe(h2, rowinfo, colinfo, x, mod_l, wg, wu, wd, ln_g, ln_b, alpha):
    bsz, t, d = x.shape
    tm = min(T_BLK, t)
    nb = t // tm
    n_blk = bsz * nb
    dff = wg.shape[2]
    slab = _slab_rows(tm)
    group_of = rowinfo[:, 0, :].reshape(1, n_blk, tm)
    counts = jnp.sum(group_of == jnp.arange(N_GROUPS, dtype=F32).reshape(N_GROUPS, 1, 1), axis=2)
    counts = counts.astype(jnp.int32).reshape(N_GROUPS * n_blk)

    z = pl.pallas_call(
        _moe_expert_kernel,
        grid_spec=pltpu.PrefetchScalarGridSpec(
            num_scalar_prefetch=1,
            grid=(N_GROUPS, n_blk),
            in_specs=[
                pl.BlockSpec((None, tm, d), lambda g, i, c: (i // nb, i % nb, 0)),
                pl.BlockSpec((None, 8, tm), lambda g, i, c: (i // nb, 0, i % nb)),
                pl.BlockSpec((None, tm, LANES), lambda g, i, c: (i // nb, i % nb, 0)),
                pl.BlockSpec((E_PER_GROUP, d, dff), lambda g, i, c: (g, 0, 0)),
                pl.BlockSpec((E_PER_GROUP, d, dff), lambda g, i, c: (g, 0, 0)),
                pl.BlockSpec((E_PER_GROUP, dff, d), lambda g, i, c: (g, 0, 0)),
            ],
            out_specs=pl.BlockSpec((None, None, slab, d), lambda g, i, c: (g, i, 0, 0)),
        ),
        out_shape=jax.ShapeDtypeStruct((N_GROUPS, n_blk, slab, d), BF16),
        compiler_params=_params(("arbitrary", "arbitrary")),
        name="moe_experts",
    )(counts, h2, rowinfo, colinfo, wg, wu, wd)

    zspec = lambda gi: pl.BlockSpec((None, None, slab, d), lambda b, i, c: (gi, b * nb + i, 0, 0))
    tok = lambda width: pl.BlockSpec((None, tm, width), lambda b, i, c: (b, i, 0))
    return pl.pallas_call(
        functools.partial(_moe_combine_kernel, alpha=alpha),
        grid_spec=pltpu.PrefetchScalarGridSpec(
            num_scalar_prefetch=1,
            grid=(bsz, nb),
            in_specs=[
                zspec(0), zspec(1), zspec(2), zspec(3), tok(LANES), tok(d),
                pl.BlockSpec((None, 6, d), lambda b, i, c: (b, 0, 0)),
                pl.BlockSpec((1, d), lambda b, i, c: (0, 0)),
                pl.BlockSpec((1, d), lambda b, i, c: (0, 0)),
            ],
            out_specs=tok(d),
            scratch_shapes=[pltpu.VMEM((tm, d), F32)],
        ),
        out_shape=jax.ShapeDtypeStruct((bsz, t, d), F32),
        compiler_params=_params(("arbitrary", "arbitrary")),
        name="moe_combine_ln",
    )(counts, z, z, z, z, colinfo, x, mod_l, ln_g.reshape(1, d), ln_b.reshape(1, d))


def _lru_kernel(x_ref, g_ref, cw_ref, cb_ref, wa_ref, ba_ref, wx_ref, bx_ref, lam_ref,
                o_ref, xpad_sc, h_sc):
    tt = x_ref.shape[0]
    pad = 8

    @pl.when(pl.program_id(1) == 0)
    def _():
        xpad_sc[0:pad, :] = jnp.zeros((pad, xpad_sc.shape[1]), F32)
        h_sc[...] = jnp.zeros(h_sc.shape, F32)

    xpad_sc[pad:pad + tt, :] = x_ref[...].astype(F32)
    xc = cb_ref[...] + jnp.zeros((tt, x_ref.shape[1]), F32)
    for j in range(CONV_WIDTH):
        off = pad - (CONV_WIDTH - 1) + j
        xc = xc + cw_ref[j:j + 1, :] * xpad_sc[off:off + tt, :]
    xpad_sc[0:pad, :] = xpad_sc[tt:tt + pad, :]

    xb = xc.astype(BF16)
    r = _sigmoid(_dot(xb, wa_ref[...]) + ba_ref[...])
    i = _sigmoid(_dot(xb, wx_ref[...]) + bx_ref[...])
    lam = lam_ref[...].astype(F32)
    softplus_neg = jnp.maximum(-lam, 0.0) + jnp.log(1.0 + jnp.exp(-jnp.abs(lam)))
    log_a = -LRU_C * r * softplus_neg
    a = jnp.exp(log_a)
    u = jnp.sqrt(jnp.maximum(1.0 - jnp.exp(2.0 * log_a), 1e-12)) * (i * xc)

    rowi = lax.broadcasted_iota(jnp.int32, (tt, 1), 0)
    d = 1
    while d < tt:
        a_sh = jnp.where(rowi >= d, pltpu.roll(a, d, 0), 1.0)
        u_sh = jnp.where(rowi >= d, pltpu.roll(u, d, 0), 0.0)
        u = u + a * u_sh
        a = a * a_sh
        d *= 2
    hcur = u + a * h_sc[...]
    h_sc[...] = hcur[tt - 1:tt, :]

    gr = g_ref[...].astype(F32)
    gelu = 0.5 * gr * (1.0 + jnp.tanh(0.7978845608028654 * (gr + 0.044715 * gr * gr * gr)))
    o_ref[...] = (gelu * hcur).astype(o_ref.dtype)


def _rg_lru(proj, conv_w, conv_b, wa_dense, ba, wx_dense, bx, lam):
    bsz, t, _ = proj.shape
    tt = min(T_LRU, t)
    w = LRU_WIDTH
    nblk = w // LANES
    row = lambda a: a.reshape(1, w)
    full = lambda r, c: pl.BlockSpec((r, c), lambda b, i: (0, 0))
    return pl.pallas_call(
        _lru_kernel,
        grid=(bsz, t // tt),
        in_specs=[
            pl.BlockSpec((None, tt, w), lambda b, i: (b, i, 0)),
            pl.BlockSpec((None, tt, w), lambda b, i: (b, i, 1)),
            full(CONV_WIDTH, w), full(1, w), full(w, w), full(1, w), full(w, w), full(1, w),
            full(1, w),
        ],
        out_specs=pl.BlockSpec((None, tt, w), lambda b, i: (b, i, 0)),
        out_shape=jax.ShapeDtypeStruct((bsz, t, w), BF16),
        scratch_shapes=[pltpu.VMEM((tt + 8, w), F32), pltpu.VMEM((1, w), F32)],
        compiler_params=_params(("arbitrary", "arbitrary")),
        name="rg_lru",
    )(proj, proj, conv_w, row(conv_b), wa_dense, row(ba), wx_dense, row(bx), row(lam))


def _sb_kernel(q_ref, k_ref, v_ref, o_ref, r_sc, acc_sc, *bufs, tile):
    qi = pl.program_id(2)
    d = SB_HEAD_DIM
    tk = tile // 2
    z_bufs, lb_bufs, l_bufs, w_bufs = (bufs[i * SB_SETS:(i + 1) * SB_SETS] for i in range(4))
    lane = lax.broadcasted_iota(jnp.int32, (1, 2 * d), 1)
    q = q_ref[...]
    zero = jnp.zeros_like(q)
    q2 = jnp.concatenate([jnp.where(lane < d, q, zero), jnp.where(lane >= d, q, zero)], axis=0)
    rj = lax.broadcasted_iota(jnp.int32, (tk, tk), 0)
    cs = lax.broadcasted_iota(jnp.int32, (tk, tk), 1)
    upper = jnp.where(rj > cs, 1.0, 0.0).astype(BF16)

    r_sc[...] = jnp.zeros(r_sc.shape, F32)
    acc_sc[...] = jnp.zeros(acc_sc.shape, F32)
    n_sub = 2 * qi + 2
    every = slice(0, 2 * tile)
    per_head = (slice(0, tile), slice(tile, 2 * tile))

    def key_start(j):
        return pl.multiple_of((n_sub - 1 - j) * tk, tk)

    def strict_mask(j):
        rowp = lax.broadcasted_iota(jnp.int32, (tile, tk), 0)
        colp = lax.broadcasted_iota(jnp.int32, (tile, tk), 1) + (1 - j) * tk
        strict = colp < rowp
        return jnp.concatenate([strict, strict], axis=0)

    def logits(j, b, rows=every):
        z_bufs[b][rows, :] = _dot_nt(q2[rows, :], k_ref[pl.ds(key_start(j), tk), :])

    def gates(j, b, rows=every, masked=False):
        z = z_bufs[b][rows, :]
        log_1m = jnp.log(1.0 + jnp.exp2(-jnp.abs(z))) * (-LOG2E) - jnp.maximum(z, 0.0)
        lb_bufs[b][rows, :] = z + log_1m
        if masked:
            log_1m = jnp.where(strict_mask(j), log_1m, 0.0)
        l_bufs[b][rows, :] = log_1m.astype(BF16)

    def weights(j, b, rows=every, masked=False):
        log_1m = l_bufs[b][rows, :]
        after = _dot(log_1m, upper) + r_sc[rows, :]
        w = jnp.exp2(lb_bufs[b][rows, :] + after)
        if masked:
            w = jnp.where(strict_mask(j), w, 0.0)
        w_bufs[b][rows, :] = w.astype(BF16)
        r_sc[rows, :] = after[:, 0:1] + log_1m[:, 0:1].astype(F32)

    def values(j, b, rows=every):
        acc_sc[rows, :] += _dot(w_bufs[b][rows, :], v_ref[pl.ds(key_start(j), tk), :])

    @pl.when(qi == 0)
    def _():
        logits(0, 0)
        logits(1, 1)
        gates(0, 0, masked=True)
        gates(1, 1, masked=True)
        weights(0, 0, masked=True)
        weights(1, 1, masked=True)
        values(0, 0)
        values(1, 1)

    @pl.when(qi > 0)
    def _():
        logits(0, 0)
        logits(1, 1)
        gates(0, 0, masked=True)
        logits(2, 2)
        gates(1, 1, masked=True)
        weights(0, 0, masked=True)
        gates(2, 2)
        weights(1, 1, masked=True)
        values(0, 0)
        weights(2, 2)
        values(1, 1)
        values(2, 2)

        def alive():
            return (jnp.max(r_sc[...]) > -SB_DEAD_LOG2).astype(jnp.int32)

        def cond(carry):
            j, live = carry
            return (j < n_sub) & (live > 0)

        def body(carry):
            j, _ = carry
            for rows in per_head:
                logits(j, 0, rows)
            for rows in per_head:
                gates(j, 0, rows)
            for rows in per_head:
                weights(j, 0, rows)
            for rows in per_head:
                values(j, 0, rows)
            return j + 1, alive()

        lax.while_loop(cond, body, (jnp.int32(3), alive()))

    acc = acc_sc[...]
    o_ref[...] = jnp.where(lane < d, acc[0:tile, :], acc[tile:2 * tile, :]).astype(o_ref.dtype)


def _sb_attention(proj):
    bsz, t, _ = proj.shape
    tq = min(T_SB, t)
    pairs = SB_HEADS // 2
    wblk = 2 * SB_HEAD_DIM
    base = 2 * LRU_WIDTH // wblk
    return pl.pallas_call(
        functools.partial(_sb_kernel, tile=tq),
        grid=(bsz, pairs, t // tq),
        in_specs=[
            pl.BlockSpec((None, tq, wblk), lambda b, h, i: (b, i, base + h)),
            pl.BlockSpec((None, t, wblk), lambda b, h, i: (b, 0, base + pairs + h)),
            pl.BlockSpec((None, t, wblk), lambda b, h, i: (b, 0, base + 2 * pairs + h)),
        ],
        out_specs=pl.BlockSpec((None, tq, wblk), lambda b, h, i: (b, i, h)),
        out_shape=jax.ShapeDtypeStruct((bsz, t, SB_HEADS * SB_HEAD_DIM), BF16),
        scratch_shapes=[pltpu.VMEM((2 * tq, 1), F32), pltpu.VMEM((2 * tq, wblk), F32)]
        + [pltpu.VMEM((2 * tq, tq // 2), F32)] * (2 * SB_SETS)
        + [pltpu.VMEM((2 * tq, tq // 2), BF16)] * (2 * SB_SETS),
        compiler_params=_params(("arbitrary", "arbitrary", "arbitrary")),
        name="sb_attn",
    )(proj, proj, proj)


def _block_diag(w):
    g, n, _ = w.shape
    eye = jnp.eye(g, dtype=w.dtype)
    return (eye[:, None, :, None] * w[:, :, None, :]).reshape(g * n, g * n)


def kernel(x, c, ada_w, ada_b, ln_g, ln_b, even_w_in, even_w_out, diff_lambda, diff_gain, hgrn_gamma, hgrn_gain, odd_w_in, odd_w_out, conv_w, conv_b, lru_wa, lru_ba, lru_wx, lru_bx, lru_lambda, router_w, router_b, moe_w_gate, moe_w_up, moe_w_down):
    depth = ada_w.shape[0]
    bsz, t, d = x.shape
    alpha = (2.0 * depth) ** 0.25
    mod = _ada_mod(c, ada_w, ada_b).reshape(depth, bsz, 6, d)
    for l in range(depth):
        j = l // 2
        mod_l = mod[l]
        if l % 2 == 0:
            lam_init = 0.8 - 0.6 * math.exp(-0.3 * l)
            proj = _inproj(x, mod_l, even_w_in[j].astype(BF16), q_chunk=0)
            mix_a = _diff_attention(proj, diff_lambda[j], diff_gain[j], lam_init)
            mix_b = _hgrn2(proj, hgrn_gamma, hgrn_gain[j], l)
            w_out = even_w_out[j]
        else:
            proj = _inproj(x, mod_l, odd_w_in[j].astype(BF16), q_chunk=2 * LRU_WIDTH // PROJ_CHUNK)
            mix_a = _rg_lru(proj, conv_w[j], conv_b[j], _block_diag(lru_wa[j]).astype(BF16),
                            lru_ba[j], _block_diag(lru_wx[j]).astype(BF16), lru_bx[j],
                            lru_lambda[j])
            mix_b = _sb_attention(proj)
            w_out = odd_w_out[j]
        x, h2, rowinfo, colinfo = _outproj(mix_a, mix_b, x, mod_l, w_out.astype(BF16),
                                           ln_g[l, 0], ln_b[l, 0], router_w, router_b, alpha)
        x = _moe(h2, rowinfo, colinfo, x, mod_l, moe_w_gate[l].astype(BF16),
                 moe_w_up[l].astype(BF16), moe_w_down[l].astype(BF16), ln_g[l, 1], ln_b[l, 1],
                 alpha)
    return x
```

```python
import functools
import math

import jax
import jax.numpy as jnp
from jax import lax
from jax.experimental import pallas as pl
from jax.experimental.pallas import tpu as pltpu

F32 = jnp.float32
BF16 = jnp.bfloat16

DA_HEADS = 4
DA_HEAD_DIM = 64
HG_HEADS = 4
HG_DK = 128
HG_CHUNK = 64
LRU_WIDTH = 512
LRU_BLOCKS = 8
CONV_WIDTH = 4
LRU_C = 8.0
SB_HEADS = 8
SB_HEAD_DIM = 64
N_EXPERTS = 16
N_GROUPS = 4
E_PER_GROUP = N_EXPERTS // N_GROUPS
D_FF = 512

LANES = 128
SUBLANES = 8
NEG_BIG = -1e30
LOG2E = 1.4426950408889634
Q_PRESCALE = DA_HEAD_DIM ** -0.5 * LOG2E
PROJ_CHUNK = 512
VMEM_LIMIT = 56 * 1024 * 1024

TM_PROJ = 512
TQ_ATT = 512
T_SB = 512
SB_SETS = 3
SB_DEAD_LOG2 = 160.0
DA_DEAD_LOG2 = 152.0
DA_FREEZE_LOG2 = 64.0
T_HG = 512
T_LRU = 256
T_BLK = 1024
MOE_ROWS_MAIN = 320
MOE_ROWS_SMALL = 256
MOE_ROWS_EXTRA = 128


def _params(sem):
    return pltpu.CompilerParams(dimension_semantics=sem, vmem_limit_bytes=VMEM_LIMIT)


def _sigmoid(x):
    return 1.0 / (1.0 + jnp.exp(-x))


def _dot(a, b):
    return jnp.dot(a, b, preferred_element_type=F32)


def _dot_nt(a, b):
    return lax.dot_general(a, b, (((1,), (1,)), ((), ())), preferred_element_type=F32)


def _onehot(mask):
    return jnp.where(mask, 1.0, 0.0).astype(BF16)


def _split3(x):
    hi = x.astype(BF16)
    r1 = x - hi.astype(F32)
    mid = r1.astype(BF16)
    lo = (r1 - mid.astype(F32)).astype(BF16)
    return hi, mid, lo


def _ada_kernel(c_ref, w_ref, b_ref, o_ref):
    c = c_ref[...]
    cond = c * _sigmoid(c)
    hi, mid, _ = _split3(cond)
    w = w_ref[...].astype(BF16)
    o_ref[...] = _dot(hi, w) + _dot(mid, w) + b_ref[...]


def _ada_mod(c, ada_w, ada_b):
    depth, d, d6 = ada_w.shape
    bsz = c.shape[0]
    n_col = d6 // d
    return pl.pallas_call(
        _ada_kernel,
        grid=(depth, n_col),
        in_specs=[
            pl.BlockSpec((bsz, d), lambda l, j: (0, 0)),
            pl.BlockSpec((None, d, d), lambda l, j: (l, 0, j)),
            pl.BlockSpec((None, 1, d), lambda l, j: (l, 0, j)),
        ],
        out_specs=pl.BlockSpec((None, bsz, d), lambda l, j: (l, 0, j)),
        out_shape=jax.ShapeDtypeStruct((depth, bsz, d6), F32),
        compiler_params=_params(("arbitrary", "arbitrary")),
        name="ada_mod",
    )(c, ada_w, ada_b.reshape(depth, 1, d6))


def _inproj_kernel(x_ref, mod_ref, w_ref, o_ref, *, col_chunk, q_chunk):
    sh = mod_ref[0:1, :]
    sc = mod_ref[1:2, :]
    h = (x_ref[...] * (1.0 + sc) + sh).astype(BF16)
    for j in range(o_ref.shape[1] // col_chunk):
        cols = slice(j * col_chunk, (j + 1) * col_chunk)
        y = _dot(h, w_ref[:, cols])
        if j == q_chunk:
            y = y * Q_PRESCALE
        o_ref[:, cols] = y.astype(o_ref.dtype)


def _inproj(x, mod_l, w_bf16, q_chunk):
    bsz, t, d = x.shape
    width = w_bf16.shape[1]
    tm = min(TM_PROJ, t)
    return pl.pallas_call(
        functools.partial(_inproj_kernel, col_chunk=PROJ_CHUNK, q_chunk=q_chunk),
        grid=(bsz, t // tm),
        in_specs=[
            pl.BlockSpec((None, tm, d), lambda b, i: (b, i, 0)),
            pl.BlockSpec((None, 6, d), lambda b, i: (b, 0, 0)),
            pl.BlockSpec((d, width), lambda b, i: (0, 0)),
        ],
        out_specs=pl.BlockSpec((None, tm, width), lambda b, i: (b, i, 0)),
        out_shape=jax.ShapeDtypeStruct((bsz, t, width), BF16),
        compiler_params=_params(("arbitrary", "arbitrary")),
        name="inproj",
    )(x, mod_l, w_bf16)


def _diffattn_kernel(q_ref, k_ref, v_ref, lam_ref, gain_ref, o_ref, m_sc, acc_sc, s0_sc, s1_sc,
                     p0_sc, p1_sc, a0_sc, a1_sc, kn_sc, *, tile, lam_init):
    h = pl.program_id(1)
    qi = pl.program_id(2)
    dh = DA_HEAD_DIM
    hd = 2 * dh
    tk = tile // 2
    reps = tk // LANES
    s_bufs, p_bufs, a_bufs = (s0_sc, s1_sc), (p0_sc, p1_sc), (a0_sc, a1_sc)

    lane = lax.broadcasted_iota(jnp.int32, (1, hd), 1)
    q = q_ref[...]
    zero = jnp.zeros_like(q)
    q2 = jnp.concatenate([jnp.where(lane < dh, q, zero), jnp.where(lane >= dh, q, zero)], axis=0)

    hf = jnp.full((1, 1), h + 1, jnp.int32).astype(F32)
    slope = jnp.exp2(hf * (-8.0 / DA_HEADS)) * LOG2E
    col = lax.broadcasted_iota(jnp.int32, (1, tk), 1)
    ones = jnp.ones((tk, hd), BF16)

    m_sc[...] = jnp.full(m_sc.shape, NEG_BIG, F32)
    acc_sc[...] = jnp.zeros(acc_sc.shape, F32)

    def max_half_norms(x):
        xf = x.astype(F32)
        sq = xf * xf
        out = []
        for keep in (lane < dh, lane >= dh):
            rows = jnp.sum(jnp.where(keep, sq, 0.0), axis=1, keepdims=True)
            out.append(jnp.sqrt(jnp.max(rows, axis=0, keepdims=True)))
        return out

    @pl.when(qi == 0)
    def _():
        kn_sc[...] = jnp.concatenate([jnp.broadcast_to(n, (4, LANES))
                                      for n in max_half_norms(k_ref[...])], axis=0)

    n_sub = 2 * qi + 2

    def key_start(j):
        return pl.multiple_of((n_sub - 1 - j) * tk, tk)

    def scores(j, slot):
        ks = key_start(j)
        bias = (col + (ks - qi * tile)).astype(F32) * slope
        s_bufs[slot][...] = _dot_nt(q2, k_ref[pl.ds(ks, tk), :]) + bias

    def softmax(j, slot, masked):
        s = s_bufs[slot][...]
        if masked:
            rowp = lax.broadcasted_iota(jnp.int32, (tile, tk), 0) + qi * tile
            colp = lax.broadcasted_iota(jnp.int32, (tile, tk), 1) + key_start(j)
            keep = colp <= rowp
            s = jnp.where(jnp.concatenate([keep, keep], axis=0), s, NEG_BIG)
        m_old = m_sc[...]
        m_new = jnp.maximum(m_old, jnp.max(s, axis=1, keepdims=True))
        p_bufs[slot][...] = jnp.exp2(s - jnp.concatenate([m_new] * reps, axis=1)).astype(BF16)
        a_bufs[slot][...] = jnp.exp2(m_old - m_new)
        m_sc[...] = m_new

    def values(j, slot):
        v_aug = jnp.concatenate([v_ref[pl.ds(key_start(j), tk), :], ones], axis=1)
        alpha = a_bufs[slot][...]
        acc_sc[...] = (jnp.concatenate([alpha, alpha], axis=1) * acc_sc[...]
                       + _dot(p_bufs[slot][...], v_aug))

    def softmax_frozen(slot):
        m_rep = jnp.concatenate([m_sc[...]] * reps, axis=1)
        p_bufs[slot][...] = jnp.exp2(s_bufs[slot][...] - m_rep).astype(BF16)

    def values_frozen(j, slot):
        v_aug = jnp.concatenate([v_ref[pl.ds(key_start(j), tk), :], ones], axis=1)
        acc_sc[...] += _dot(p_bufs[slot][...], v_aug)

    scores(0, 0)
    scores(1, 1)
    softmax(0, 0, True)

    @pl.when(qi == 0)
    def _():
        softmax(1, 1, True)
        values(0, 0)
        values(1, 1)

    @pl.when(qi > 0)
    def _():
        scores(2, 0)
        softmax(1, 1, True)
        values(0, 0)
        scores(3, 1)
        softmax(2, 0, False)
        values(1, 1)
        qn = max_half_norms(q)
        qk_max = jnp.maximum(qn[0] * kn_sc[0:1, 0:1], qn[1] * kn_sc[4:5, 0:1])
        m_min = jnp.min(m_sc[...], axis=0, keepdims=True)[:, 0:1]
        reach = (qk_max - m_min + DA_DEAD_LOG2) / slope
        first_dead = jnp.floor((reach - 1.0) / tk) + 3.0
        first_dead = jnp.max(jnp.clip(first_dead, 0.0, 1e6)).astype(jnp.int32)
        pairs_end = jnp.maximum(2, jnp.minimum(qi + 1, (first_dead + 1) // 2))

        freeze = jnp.max(jnp.where(qk_max - m_min <= DA_FREEZE_LOG2, 1.0, 0.0)) > 0.5
        freeze = jnp.logical_and(freeze, pairs_end > 2)

        @pl.when(jnp.logical_not(freeze))
        def _():
            def body(i, carry):
                t = 2 * i
                scores(t, 0)
                softmax(t - 1, 1, False)
                values(t - 2, 0)
                scores(t + 1, 1)
                softmax(t, 0, False)
                values(t - 1, 1)
                return carry

            lax.fori_loop(2, pairs_end, body, 0)
            t = 2 * pairs_end
            softmax(t - 1, 1, False)
            values(t - 2, 0)
            values(t - 1, 1)

        @pl.when(freeze)
        def _():
            scores(4, 0)
            softmax_frozen(1)
            values(2, 0)
            scores(5, 1)
            softmax_frozen(0)
            values_frozen(3, 1)

            def body(i, carry):
                t = 2 * i
                scores(t, 0)
                softmax_frozen(1)
                values_frozen(t - 2, 0)
                scores(t + 1, 1)
                softmax_frozen(0)
                values_frozen(t - 1, 1)
                return carry

            lax.fori_loop(3, pairs_end, body, 0)
            t = 2 * pairs_end
            softmax_frozen(1)
            values_frozen(t - 2, 0)
            values_frozen(t - 1, 1)

    lv = lam_ref[...].astype(F32)
    dots = jnp.sum(lv[0:1, :] * lv[1:2, :], axis=1, keepdims=True)
    dots2 = jnp.sum(lv[2:3, :] * lv[3:4, :], axis=1, keepdims=True)
    lam = jnp.exp(dots) - jnp.exp(dots2) + lam_init
    acc = acc_sc[...]
    o0 = acc[0:tile, 0:hd] / acc[0:tile, hd:2 * hd]
    o1 = acc[tile:2 * tile, 0:hd] / acc[tile:2 * tile, hd:2 * hd]
    o = o0 - lam * o1
    ms = jnp.mean(o * o, axis=1, keepdims=True)
    o = o * lax.rsqrt(ms + 1e-6) * gain_ref[...] * (1.0 - lam_init)
    o_ref[...] = o.astype(o_ref.dtype)


def _diff_attention(proj, diff_lambda, diff_gain, lam_init):
    bsz, t, _ = proj.shape
    tile = min(TQ_ATT, t)
    hd = 2 * DA_HEAD_DIM
    kern = functools.partial(_diffattn_kernel, tile=tile, lam_init=lam_init)
    return pl.pallas_call(
        kern,
        grid=(bsz, DA_HEADS, t // tile),
        in_specs=[
            pl.BlockSpec((None, tile, hd), lambda b, h, i: (b, i, h)),
            pl.BlockSpec((None, t, hd), lambda b, h, i: (b, 0, DA_HEADS + h)),
            pl.BlockSpec((None, t, hd), lambda b, h, i: (b, 0, 2 * DA_HEADS + h)),
            pl.BlockSpec((4, DA_HEAD_DIM), lambda b, h, i: (0, 0)),
            pl.BlockSpec((1, hd), lambda b, h, i: (0, 0)),
        ],
        out_specs=pl.BlockSpec((None, tile, hd), lambda b, h, i: (b, i, h)),
        out_shape=jax.ShapeDtypeStruct((bsz, t, DA_HEADS * hd), BF16),
        scratch_shapes=[
            pltpu.VMEM((2 * tile, LANES), F32),
            pltpu.VMEM((2 * tile, 2 * hd), F32),
            pltpu.VMEM((2 * tile, tile // 2), F32),
            pltpu.VMEM((2 * tile, tile // 2), F32),
            pltpu.VMEM((2 * tile, tile // 2), BF16),
            pltpu.VMEM((2 * tile, tile // 2), BF16),
            pltpu.VMEM((2 * tile, LANES), F32),
            pltpu.VMEM((2 * tile, LANES), F32),
            pltpu.VMEM((8, LANES), F32),
        ],
        compiler_params=_params(("arbitrary", "arbitrary", "arbitrary")),
        name="diff_attn",
    )(proj, proj, proj, diff_lambda, diff_gain.reshape(1, hd))


def _hgrn_kernel(q_ref, f_ref, i_ref, g_ref, gamma_ref, gain_ref, o_ref, st_sc, *, layer):
    @pl.when(pl.program_id(1) == 0)
    def _():
        st_sc[...] = jnp.zeros(st_sc.shape, F32)

    gam = gamma_ref[...].astype(F32)
    e = jnp.exp(gam - jnp.max(gam, axis=0, keepdims=True))
    sm = e / jnp.sum(e, axis=0, keepdims=True)
    lb_all = jnp.sum(sm[0:layer + 1, :], axis=0, keepdims=True)

    c = HG_CHUNK
    row = lax.broadcasted_iota(jnp.int32, (c, c), 0)
    col = lax.broadcasted_iota(jnp.int32, (c, c), 1)
    tril = col <= row
    tril_bf = jnp.where(tril, 1.0, 0.0).astype(BF16)
    gain = gain_ref[...]

    heads = range(HG_HEADS)
    hcols = [slice(h * HG_DK, (h + 1) * HG_DK) for h in heads]
    lbs = [lb_all[:, hc] for hc in hcols]
    for n in range(q_ref.shape[0] // c):
        rows = slice(n * c, (n + 1) * c)
        sig = [_sigmoid(f_ref[rows, hc].astype(F32)) for hc in hcols]
        logf = [jnp.log(lbs[h] + (1.0 - lbs[h]) * sig[h]) for h in heads]
        kk = [(1.0 - lbs[h]) * (1.0 - sig[h]) for h in heads]
        parts = [_split3(x) for x in logf]
        b = [_dot(tril_bf, p[0]) + _dot(tril_bf, p[1]) + _dot(tril_bf, p[2]) for p in parts]
        b_mid = [x[c // 2 - 1:c // 2, :] for x in b]
        b_last = [x[c - 1:c, :] for x in b]
        qh = [q_ref[rows, hc].astype(F32) for hc in hcols]
        qs = [x * _sigmoid(x) for x in qh]
        v = [i_ref[rows, hc] for hc in hcols]
        att = [_dot_nt((qs[h] * jnp.exp(b[h] - b_mid[h])).astype(BF16),
                       (kk[h] * jnp.exp(b_mid[h] - b[h])).astype(BF16)) for h in heads]
        att = [jnp.where(tril, x, 0.0).astype(BF16) for x in att]
        o_intra = [_dot(att[h], v[h]) for h in heads]
        kd = [(kk[h] * jnp.exp(b_last[h] - b[h])).astype(BF16) for h in heads]
        ds_t = [_dot(v[h].T, kd[h]) for h in heads]
        st = [st_sc[h] for h in heads]
        o_inter = [_dot_nt((qs[h] * jnp.exp(b[h])).astype(BF16), st[h].astype(BF16))
                   for h in heads]
        for h in heads:
            st_sc[h] = st[h] * jnp.exp(b_last[h]) + ds_t[h]
        for h in heads:
            o = o_intra[h] + o_inter[h]
            gh = g_ref[rows, hcols[h]].astype(F32)
            ms = jnp.mean(o * o, axis=1, keepdims=True)
            o = o * lax.rsqrt(ms + 1e-6) * gain * (gh * _sigmoid(gh))
            o_ref[rows, hcols[h]] = o.astype(o_ref.dtype)


def _hgrn2(proj, hgrn_gamma, hgrn_gain, layer):
    bsz, t, _ = proj.shape
    tt = min(T_HG, t)
    width = HG_HEADS * HG_DK
    base = 3 * DA_HEADS * 2 * DA_HEAD_DIM // width
    spec = lambda k: pl.BlockSpec((None, tt, width), lambda b, i: (b, i, base + k))
    return pl.pallas_call(
        functools.partial(_hgrn_kernel, layer=layer),
        grid=(bsz, t // tt),
        in_specs=[
            spec(0), spec(1), spec(2), spec(3),
            pl.BlockSpec((hgrn_gamma.shape[0], width), lambda b, i: (0, 0)),
            pl.BlockSpec((1, HG_DK), lambda b, i: (0, 0)),
        ],
        out_specs=pl.BlockSpec((None, tt, width), lambda b, i: (b, i, 0)),
        out_shape=jax.ShapeDtypeStruct((bsz, t, width), BF16),
        scratch_shapes=[pltpu.VMEM((HG_HEADS, HG_DK, HG_DK), F32)],
        compiler_params=_params(("arbitrary", "arbitrary")),
        name="hgrn2",
    )(proj, proj, proj, proj, hgrn_gamma, hgrn_gain.reshape(1, HG_DK))


def _route(logits_t):
    mx = jnp.max(logits_t, axis=0, keepdims=True)
    ex = jnp.exp(logits_t - mx)
    probs = ex / jnp.sum(ex, axis=0, keepdims=True)
    p = [probs[e:e + 1, :] for e in range(N_EXPERTS)]
    g = E_PER_GROUP
    scores = []
    for gi in range(N_GROUPS):
        pg = p[gi * g:(gi + 1) * g]
        best = None
        for a in range(g):
            for b in range(a + 1, g):
                pair = pg[a] + pg[b]
                best = pair if best is None else jnp.maximum(best, pair)
        scores.append(best)
    group_id = jnp.zeros_like(p[0])
    gates = [jnp.zeros_like(p[0]) for _ in range(g)]
    for gi in range(N_GROUPS):
        sel = None
        for gj in range(N_GROUPS):
            if gj == gi:
                continue
            cond = (scores[gi] > scores[gj]) if gj < gi else (scores[gi] >= scores[gj])
            sel = cond if sel is None else (sel & cond)
        group_id = jnp.where(sel, float(gi), group_id)
        pg = p[gi * g:(gi + 1) * g]
        chosen = []
        for a in range(g):
            rank = jnp.zeros_like(pg[a])
            for b in range(g):
                if b == a:
                    continue
                ahead = (pg[b] >= pg[a]) if b < a else (pg[b] > pg[a])
                rank = rank + jnp.where(ahead, 1.0, 0.0)
            chosen.append(sel & (rank < 2.0))
        denom = None
        for a in range(g):
            term = jnp.where(chosen[a], pg[a], 0.0)
            denom = term if denom is None else denom + term
        for a in range(g):
            gates[a] = jnp.where(chosen[a], pg[a] / denom, gates[a])
    return group_id, gates


def _outproj_kernel(a_ref, b_ref, x_ref, mod_ref, w_ref, lng_ref, lnb_ref, rwt_ref, rb_ref,
                    xo_ref, h_ref, row_ref, col_ref, *, alpha):
    half = a_ref.shape[1]
    tm = a_ref.shape[0]
    y = _dot(a_ref[...], w_ref[0:half, :]) + _dot(b_ref[...], w_ref[half:2 * half, :])
    g1 = mod_ref[2:3, :]
    sh2 = mod_ref[3:4, :]
    sc2 = mod_ref[4:5, :]
    r = alpha * x_ref[...] + (1.0 + g1) * y
    mu = jnp.mean(r, axis=1, keepdims=True)
    rc = r - mu
    var = jnp.mean(rc * rc, axis=1, keepdims=True)
    xn = rc * lax.rsqrt(var + 1e-5) * lng_ref[...] + lnb_ref[...]
    xo_ref[...] = xn
    h2 = xn * (1.0 + sc2) + sh2
    h_ref[...] = h2.astype(BF16)
    h_hi, h_mid, _ = _split3(h2)
    rw = rwt_ref[...]
    w_hi, w_mid, _ = _split3(rw)
    logits_t = _dot_nt(w_hi, h_hi) + _dot_nt(w_hi, h_mid) + _dot_nt(w_mid, h_hi) + rb_ref[...]
    group_id, gates = _route(logits_t)
    sel = [jnp.where(group_id == float(gi), 1.0, 0.0) for gi in range(N_GROUPS)]
    onehot = jnp.concatenate(sel + [jnp.zeros((8 - N_GROUPS, tm), F32)], axis=0).astype(BF16)
    src = lax.broadcasted_iota(jnp.int32, (tm, tm), 0)
    dst = lax.broadcasted_iota(jnp.int32, (tm, tm), 1)
    earlier = jnp.where(src < dst, 1.0, 0.0).astype(BF16)
    counts = _dot(onehot, earlier)
    rank = sel[0] * counts[0:1, :]
    for gi in range(1, N_GROUPS):
        rank = rank + sel[gi] * counts[gi:gi + 1, :]
    info = jnp.concatenate(gates + [group_id, rank], axis=0)
    row_ref[...] = jnp.concatenate(
        [group_id, rank, jnp.zeros((8 - 2, tm), F32)], axis=0)
    pad = jnp.zeros((LANES - info.shape[0], tm), F32)
    col_ref[...] = jnp.concatenate([info, pad], axis=0).T


def _outproj(a, b, x, mod_l, w_bf16, ln_g, ln_b, router_w, router_b, alpha):
    bsz, t, d = x.shape
    half = a.shape[2]
    tm = min(T_BLK, t)
    tok = lambda width: pl.BlockSpec((None, tm, width), lambda bi, i: (bi, i, 0))
    full = lambda r, c: pl.BlockSpec((r, c), lambda bi, i: (0, 0))
    return pl.pallas_call(
        functools.partial(_outproj_kernel, alpha=alpha),
        grid=(bsz, t // tm),
        in_specs=[
            tok(half), tok(half), tok(d),
            pl.BlockSpec((None, 6, d), lambda bi, i: (bi, 0, 0)),
            full(2 * half, d), full(1, d), full(1, d), full(N_EXPERTS, d), full(N_EXPERTS, 1),
        ],
        out_specs=[tok(d), tok(d), pl.BlockSpec((None, 8, tm), lambda bi, i: (bi, 0, i)),
                   tok(LANES)],
        out_shape=[
            jax.ShapeDtypeStruct((bsz, t, d), F32),
            jax.ShapeDtypeStruct((bsz, t, d), BF16),
            jax.ShapeDtypeStruct((bsz, 8, t), F32),
            jax.ShapeDtypeStruct((bsz, t, LANES), F32),
        ],
        compiler_params=_params(("arbitrary", "arbitrary")),
        name="outproj_ln_route",
    )(a, b, x, mod_l, w_bf16, ln_g.reshape(1, d), ln_b.reshape(1, d), router_w.T,
      router_b.reshape(N_EXPERTS, 1))


def _slab_rows(tm):
    extra = -(-(tm - MOE_ROWS_MAIN) // MOE_ROWS_EXTRA)
    return MOE_ROWS_MAIN + max(extra, 0) * MOE_ROWS_EXTRA


def _extra_chunks(count):
    return (jnp.maximum(count - MOE_ROWS_MAIN, 0) + MOE_ROWS_EXTRA - 1) // MOE_ROWS_EXTRA


def _moe_expert_kernel(cnt_ref, h_ref, row_ref, col_ref, wg_ref, wu_ref, wd_ref, z_ref):
    g = pl.program_id(0)
    blk = pl.program_id(1)
    count = cnt_ref[g * pl.num_programs(1) + blk]
    mine = row_ref[0:1, :] == g.astype(F32)
    rank = row_ref[1:2, :]
    info = col_ref[...]
    info_hi = info.astype(BF16)
    info_lo = (info - info_hi.astype(F32)).astype(BF16)

    def run_rows(r0, m):
        rid = (lax.broadcasted_iota(jnp.int32, (m, 1), 0) + r0).astype(F32)
        pick = _onehot((rank == rid) & mine)
        xs = _dot(pick, h_ref[...]).astype(BF16)
        gm = _dot(pick, info_hi) + _dot(pick, info_lo)
        acc = None
        for j in range(E_PER_GROUP):
            a = _dot(xs, wg_ref[j])
            u = _dot(xs, wu_ref[j])
            he = (a * _sigmoid(a) * u * gm[:, j:j + 1]).astype(BF16)
            part = _dot(he, wd_ref[j])
            acc = part if acc is None else acc + part
        z_ref[pl.ds(r0, m), :] = acc.astype(z_ref.dtype)

    @pl.when(count <= MOE_ROWS_SMALL)
    def _():
        run_rows(0, MOE_ROWS_SMALL)
        z_ref[MOE_ROWS_SMALL:MOE_ROWS_MAIN, :] = jnp.zeros(
            (MOE_ROWS_MAIN - MOE_ROWS_SMALL, z_ref.shape[1]), z_ref.dtype)

    @pl.when(count > MOE_ROWS_SMALL)
    def _():
        run_rows(0, MOE_ROWS_MAIN)

    rest = z_ref.shape[0] - MOE_ROWS_MAIN
    if rest:
        z_ref[MOE_ROWS_MAIN:, :] = jnp.zeros((rest, z_ref.shape[1]), z_ref.dtype)

        def body(i, carry):
            run_rows(pl.multiple_of(MOE_ROWS_MAIN + i * MOE_ROWS_EXTRA, 16), MOE_ROWS_EXTRA)
            return carry

        lax.fori_loop(0, _extra_chunks(count), body, 0)


def _moe_combine_kernel(cnt_ref, z0_ref, z1_ref, z2_ref, z3_ref, col_ref, x_ref, mod_ref, lng_ref,
                        lnb_ref, o_ref, y_sc, *, alpha):
    blk = pl.program_id(0) * pl.num_programs(1) + pl.program_id(1)
    n_blk = pl.num_programs(0) * pl.num_programs(1)
    z_refs = (z0_ref, z1_ref, z2_ref, z3_ref)
    main = MOE_ROWS_MAIN
    grp = col_ref[:, E_PER_GROUP:E_PER_GROUP + 1]
    rank = col_ref[:, E_PER_GROUP + 1:E_PER_GROUP + 2]
    where_to = jnp.where(rank < float(main), grp * float(main) + rank, -1.0)
    lane = lax.broadcasted_iota(jnp.int32, (1, N_GROUPS * main), 1).astype(F32)
    pick = _onehot(where_to == lane)
    z_all = jnp.concatenate([zr[0:main, :] for zr in z_refs], axis=0)
    y_sc[...] = _dot(pick, z_all)
    lane_x = lax.broadcasted_iota(jnp.int32, (1, MOE_ROWS_EXTRA), 1).astype(F32)
    for gi in range(N_GROUPS):
        def body(i, carry, gi=gi):
            r0 = pl.multiple_of(main + i * MOE_ROWS_EXTRA, 16)
            hit = (grp == float(gi)) & ((rank - r0.astype(F32)) == lane_x)
            y_sc[...] += _dot(_onehot(hit), z_refs[gi][pl.ds(r0, MOE_ROWS_EXTRA), :])
            return carry

        lax.fori_loop(0, _extra_chunks(cnt_ref[gi * n_blk + blk]), body, 0)

    g2 = mod_ref[5:6, :]
    r = alpha * x_ref[...] + (1.0 + g2) * y_sc[...]
    mu = jnp.mean(r, axis=1, keepdims=True)
    rc = r - mu
    var = jnp.mean(rc * rc, axis=1, keepdims=True)
    o_ref[...] = rc * lax.rsqrt(var + 1e-5) * lng_ref[...] + lnb_ref[...]


def _moe(h2, rowinfo, colinfo, x, mod_l, wg, wu, wd, ln_g, ln_b, alpha):
    bsz, t, d = x.shape
    tm = min(T_BLK, t)
    nb = t // tm
    n_blk = bsz * nb
    dff = wg.shape[2]
    slab = _slab_rows(tm)
    group_of = rowinfo[:, 0, :].reshape(1, n_blk, tm)
    counts = jnp.sum(group_of == jnp.arange(N_GROUPS, dtype=F32).reshape(N_GROUPS, 1, 1), axis=2)
    counts = counts.astype(jnp.int32).reshape(N_GROUPS * n_blk)

    z = pl.pallas_call(
        _moe_expert_kernel,
        grid_spec=pltpu.PrefetchScalarGridSpec(
            num_scalar_prefetch=1,
            grid=(N_GROUPS, n_blk),
            in_specs=[
                pl.BlockSpec((None, tm, d), lambda g, i, c: (i // nb, i % nb, 0)),
                pl.BlockSpec((None, 8, tm), lambda g, i, c: (i // nb, 0, i % nb)),
                pl.BlockSpec((None, tm, LANES), lambda g, i, c: (i // nb, i % nb, 0)),
                pl.BlockSpec((E_PER_GROUP, d, dff), lambda g, i, c: (g, 0, 0)),
                pl.BlockSpec((E_PER_GROUP, d, dff), lambda g, i, c: (g, 0, 0)),
                pl.BlockSpec((E_PER_GROUP, dff, d), lambda g, i, c: (g, 0, 0)),
            ],
            out_specs=pl.BlockSpec((None, None, slab, d), lambda g, i, c: (g, i, 0, 0)),
        ),
        out_shape=jax.ShapeDtypeStruct((N_GROUPS, n_blk, slab, d), BF16),
        compiler_params=_params(("arbitrary", "arbitrary")),
        name="moe_experts",
    )(counts, h2, rowinfo, colinfo, wg, wu, wd)

    zspec = lambda gi: pl.BlockSpec((None, None, slab, d), lambda b, i, c: (gi, b * nb + i, 0, 0))
    tok = lambda width: pl.BlockSpec((None, tm, width), lambda b, i, c: (b, i, 0))
    return pl.pallas_call(
        functools.partial(_moe_combine_kernel, alpha=alpha),
        grid_spec=pltpu.PrefetchScalarGridSpec(
            num_scalar_prefetch=1,
            grid=(bsz, nb),
            in_specs=[
                zspec(0), zspec(1), zspec(2), zspec(3), tok(LANES), tok(d),
                pl.BlockSpec((None, 6, d), lambda b, i, c: (b, 0, 0)),
                pl.BlockSpec((1, d), lambda b, i, c: (0, 0)),
                pl.BlockSpec((1, d), lambda b, i, c: (0, 0)),
            ],
            out_specs=tok(d),
            scratch_shapes=[pltpu.VMEM((tm, d), F32)],
        ),
        out_shape=jax.ShapeDtypeStruct((bsz, t, d), F32),
        compiler_params=_params(("arbitrary", "arbitrary")),
        name="moe_combine_ln",
    )(counts, z, z, z, z, colinfo, x, mod_l, ln_g.reshape(1, d), ln_b.reshape(1, d))


def _lru_kernel(x_ref, g_ref, cw_ref, cb_ref, wa_ref, ba_ref, wx_ref, bx_ref, lam_ref,
                o_ref, xpad_sc, h_sc):
    tt = x_ref.shape[0]
    pad = 8

    @pl.when(pl.program_id(1) == 0)
    def _():
        xpad_sc[0:pad, :] = jnp.zeros((pad, xpad_sc.shape[1]), F32)
        h_sc[...] = jnp.zeros(h_sc.shape, F32)

    xpad_sc[pad:pad + tt, :] = x_ref[...].astype(F32)
    xc = cb_ref[...] + jnp.zeros((tt, x_ref.shape[1]), F32)
    for j in range(CONV_WIDTH):
        off = pad - (CONV_WIDTH - 1) + j
        xc = xc + cw_ref[j:j + 1, :] * xpad_sc[off:off + tt, :]
    xpad_sc[0:pad, :] = xpad_sc[tt:tt + pad, :]

    xb = xc.astype(BF16)
    r = _sigmoid(_dot(xb, wa_ref[...]) + ba_ref[...])
    i = _sigmoid(_dot(xb, wx_ref[...]) + bx_ref[...])
    lam = lam_ref[...].astype(F32)
    softplus_neg = jnp.maximum(-lam, 0.0) + jnp.log(1.0 + jnp.exp(-jnp.abs(lam)))
    log_a = -LRU_C * r * softplus_neg
    a = jnp.exp(log_a)
    gain_sq = jnp.maximum(1.0 - jnp.exp(2.0 * log_a), 1e-12)
    u = gain_sq * lax.rsqrt(gain_sq) * (i * xc)

    groups = (tt // SUBLANES, SUBLANES, a.shape[1])
    a = a.reshape(groups)
    u = u.reshape(groups)
    rowi = lax.broadcasted_iota(jnp.int32, (1, SUBLANES, 1), 1)
    d = 1
    while d < SUBLANES:
        a_sh = jnp.where(rowi >= d, pltpu.roll(a, d, 1), 1.0)
        u_sh = jnp.where(rowi >= d, pltpu.roll(u, d, 1), 0.0)
        u = u + a * u_sh
        a = a * a_sh
        d *= 2
    a = a.reshape(tt, groups[2])
    u = u.reshape(tt, groups[2])
    gr = g_ref[...].astype(F32)
    gelu = 0.5 * gr * (1.0 + jnp.tanh(0.7978845608028654 * (gr + 0.044715 * gr * gr * gr)))
    h_prev = h_sc[...]
    out = []
    for grp in range(tt // SUBLANES):
        rows = slice(grp * SUBLANES, (grp + 1) * SUBLANES)
        h_grp = u[rows, :] + a[rows, :] * h_prev
        out.append(gelu[rows, :] * h_grp)
        h_prev = h_grp[SUBLANES - 1:SUBLANES, :]
    h_sc[...] = h_prev
    o_ref[...] = jnp.concatenate(out, axis=0).astype(o_ref.dtype)


def _rg_lru(proj, conv_w, conv_b, wa_dense, ba, wx_dense, bx, lam):
    bsz, t, _ = proj.shape
    tt = min(T_LRU, t)
    w = LRU_WIDTH
    nblk = w // LANES
    row = lambda a: a.reshape(1, w)
    full = lambda r, c: pl.BlockSpec((r, c), lambda b, i: (0, 0))
    return pl.pallas_call(
        _lru_kernel,
        grid=(bsz, t // tt),
        in_specs=[
            pl.BlockSpec((None, tt, w), lambda b, i: (b, i, 0)),
            pl.BlockSpec((None, tt, w), lambda b, i: (b, i, 1)),
            full(CONV_WIDTH, w), full(1, w), full(w, w), full(1, w), full(w, w), full(1, w),
            full(1, w),
        ],
        out_specs=pl.BlockSpec((None, tt, w), lambda b, i: (b, i, 0)),
        out_shape=jax.ShapeDtypeStruct((bsz, t, w), BF16),
        scratch_shapes=[pltpu.VMEM((tt + SUBLANES, w), F32), pltpu.VMEM((1, w), F32)],
        compiler_params=_params(("arbitrary", "arbitrary")),
        name="rg_lru",
    )(proj, proj, conv_w, row(conv_b), wa_dense, row(ba), wx_dense, row(bx), row(lam))


def _sb_kernel(q_ref, k_ref, v_ref, o_ref, r_sc, acc_sc, *bufs, tile):
    qi = pl.program_id(2)
    d = SB_HEAD_DIM
    tk = tile // 2
    z_bufs, lb_bufs, l_bufs, w_bufs = (bufs[i * SB_SETS:(i + 1) * SB_SETS] for i in range(4))
    lane = lax.broadcasted_iota(jnp.int32, (1, 2 * d), 1)
    q = q_ref[...]
    zero = jnp.zeros_like(q)
    q2 = jnp.concatenate([jnp.where(lane < d, q, zero), jnp.where(lane >= d, q, zero)], axis=0)
    rj = lax.broadcasted_iota(jnp.int32, (tk, tk), 0)
    cs = lax.broadcasted_iota(jnp.int32, (tk, tk), 1)
    upper = jnp.where(rj > cs, 1.0, 0.0).astype(BF16)

    r_sc[...] = jnp.zeros(r_sc.shape, F32)
    acc_sc[...] = jnp.zeros(acc_sc.shape, F32)
    n_sub = 2 * qi + 2
    every = slice(0, 2 * tile)
    per_head = (slice(0, tile), slice(tile, 2 * tile))

    def key_start(j):
        return pl.multiple_of((n_sub - 1 - j) * tk, tk)

    def strict_mask(j, rows):
        n_rows = rows.stop - rows.start
        rowp = (lax.broadcasted_iota(jnp.int32, (n_rows, tk), 0) + rows.start) & (tile - 1)
        colp = lax.broadcasted_iota(jnp.int32, (n_rows, tk), 1) + (1 - j) * tk
        return colp < rowp

    def logits(j, b, rows=every):
        z_bufs[b][rows, :] = _dot_nt(q2[rows, :], k_ref[pl.ds(key_start(j), tk), :])

    def gates(j, b, rows=every, masked=False):
        z = z_bufs[b][rows, :]
        log_1m = jnp.log(1.0 + jnp.exp2(-jnp.abs(z))) * (-LOG2E) - jnp.maximum(z, 0.0)
        lb_bufs[b][rows, :] = z + log_1m
        if masked:
            log_1m = jnp.where(strict_mask(j, rows), log_1m, 0.0)
        l_bufs[b][rows, :] = log_1m.astype(BF16)

    def weights(j, b, rows=every, masked=False):
        log_1m = l_bufs[b][rows, :]
        after = _dot(log_1m, upper) + r_sc[rows, :]
        w = jnp.exp2(lb_bufs[b][rows, :] + after)
        if masked:
            w = jnp.where(strict_mask(j, rows), w, 0.0)
        w_bufs[b][rows, :] = w.astype(BF16)
        r_sc[rows, :] = after[:, 0:1] + log_1m[:, 0:1].astype(F32)

    def values(j, b, rows=every):
        acc_sc[rows, :] += _dot(w_bufs[b][rows, :], v_ref[pl.ds(key_start(j), tk), :])

    late = tuple(slice(r.start + tile // 2, r.stop) for r in per_head)

    def first(stage, **kw):
        for rows in late:
            stage(0, 0, rows, **kw)

    @pl.when(qi == 0)
    def _():
        first(logits)
        logits(1, 1)
        first(gates, masked=True)
        gates(1, 1, masked=True)
        first(weights, masked=True)
        weights(1, 1, masked=True)
        first(values)
        values(1, 1)

    @pl.when(qi > 0)
    def _():
        first(logits)
        logits(1, 1)
        first(gates, masked=True)
        logits(2, 2)
        gates(1, 1, masked=True)
        first(weights, masked=True)
        gates(2, 2)
        weights(1, 1, masked=True)
        first(values)
        weights(2, 2)
        values(1, 1)
        values(2, 2)

        def alive():
            return (jnp.max(r_sc[...]) > -SB_DEAD_LOG2).astype(jnp.int32)

        def cond(carry):
            j, live = carry
            return (j < n_sub) & (live > 0)

        def body(carry):
            j, _ = carry
            for rows in per_head:
                logits(j, 0, rows)
            for rows in per_head:
                gates(j, 0, rows)
            for rows in per_head:
                weights(j, 0, rows)
            for rows in per_head:
                values(j, 0, rows)
            return j + 1, alive()

        lax.while_loop(cond, body, (jnp.int32(3), alive()))

    acc = acc_sc[...]
    o_ref[...] = jnp.where(lane < d, acc[0:tile, :], acc[tile:2 * tile, :]).astype(o_ref.dtype)


def _sb_attention(proj):
    bsz, t, _ = proj.shape
    tq = min(T_SB, t)
    pairs = SB_HEADS // 2
    wblk = 2 * SB_HEAD_DIM
    base = 2 * LRU_WIDTH // wblk
    return pl.pallas_call(
        functools.partial(_sb_kernel, tile=tq),
        grid=(bsz, pairs, t // tq),
        in_specs=[
            pl.BlockSpec((None, tq, wblk), lambda b, h, i: (b, i, base + h)),
            pl.BlockSpec((None, t, wblk), lambda b, h, i: (b, 0, base + pairs + h)),
            pl.BlockSpec((None, t, wblk), lambda b, h, i: (b, 0, base + 2 * pairs + h)),
        ],
        out_specs=pl.BlockSpec((None, tq, wblk), lambda b, h, i: (b, i, h)),
        out_shape=jax.ShapeDtypeStruct((bsz, t, SB_HEADS * SB_HEAD_DIM), BF16),
        scratch_shapes=[pltpu.VMEM((2 * tq, 1), F32), pltpu.VMEM((2 * tq, wblk), F32)]
        + [pltpu.VMEM((2 * tq, tq // 2), F32)] * (2 * SB_SETS)
        + [pltpu.VMEM((2 * tq, tq // 2), BF16)] * (2 * SB_SETS),
        compiler_params=_params(("arbitrary", "arbitrary", "arbitrary")),
        name="sb_attn",
    )(proj, proj, proj)


def _block_diag(w):
    g, n, _ = w.shape
    eye = jnp.eye(g, dtype=w.dtype)
    return (eye[:, None, :, None] * w[:, :, None, :]).reshape(g * n, g * n)


def kernel(x, c, ada_w, ada_b, ln_g, ln_b, even_w_in, even_w_out, diff_lambda, diff_gain, hgrn_gamma, hgrn_gain, odd_w_in, odd_w_out, conv_w, conv_b, lru_wa, lru_ba, lru_wx, lru_bx, lru_lambda, router_w, router_b, moe_w_gate, moe_w_up, moe_w_down):
    depth = ada_w.shape[0]
    bsz, t, d = x.shape
    alpha = (2.0 * depth) ** 0.25
    mod = _ada_mod(c, ada_w, ada_b).reshape(depth, bsz, 6, d)
    for l in range(depth):
        j = l // 2
        mod_l = mod[l]
        if l % 2 == 0:
            lam_init = 0.8 - 0.6 * math.exp(-0.3 * l)
            proj = _inproj(x, mod_l, even_w_in[j].astype(BF16), q_chunk=0)
            mix_a = _diff_attention(proj, diff_lambda[j], diff_gain[j], lam_init)
            mix_b = _hgrn2(proj, hgrn_gamma, hgrn_gain[j], l)
            w_out = even_w_out[j]
        else:
            proj = _inproj(x, mod_l, odd_w_in[j].astype(BF16), q_chunk=2 * LRU_WIDTH // PROJ_CHUNK)
            mix_a = _rg_lru(proj, conv_w[j], conv_b[j], _block_diag(lru_wa[j]).astype(BF16),
                            lru_ba[j], _block_diag(lru_wx[j]).astype(BF16), lru_bx[j],
                            lru_lambda[j])
            mix_b = _sb_attention(proj)
            w_out = odd_w_out[j]
        x, h2, rowinfo, colinfo = _outproj(mix_a, mix_b, x, mod_l, w_out.astype(BF16),
                                           ln_g[l, 0], ln_b[l, 0], router_w, router_b, alpha)
        x = _moe(h2, rowinfo, colinfo, x, mod_l, moe_w_gate[l].astype(BF16),
                 moe_w_up[l].astype(BF16), moe_w_down[l].astype(BF16), ln_g[l, 1], ln_b[l, 1],
                 alpha)
    return x
```

```python
import functools
import math

import jax
import jax.numpy as jnp
from jax import lax
from jax.experimental import pallas as pl
from jax.experimental.pallas import tpu as pltpu

F32 = jnp.float32
BF16 = jnp.bfloat16

DA_HEADS = 4
DA_HEAD_DIM = 64
HG_HEADS = 4
HG_DK = 128
HG_CHUNK = 64
LRU_WIDTH = 512
LRU_BLOCKS = 8
CONV_WIDTH = 4
LRU_C = 8.0
SB_HEADS = 8
SB_HEAD_DIM = 64
N_EXPERTS = 16
N_GROUPS = 4
E_PER_GROUP = N_EXPERTS // N_GROUPS
D_FF = 512

LANES = 128
SUBLANES = 8
NEG_BIG = -1e30
LOG2E = 1.4426950408889634
Q_PRESCALE = DA_HEAD_DIM ** -0.5 * LOG2E
PROJ_CHUNK = 512
VMEM_LIMIT = 56 * 1024 * 1024

TM_PROJ = 1024
TQ_ATT = 512
T_SB = 512
SB_SETS = 3
SB_DEAD_LOG2 = 160.0
DA_DEAD_LOG2 = 152.0
DA_FREEZE_LOG2 = 64.0
T_HG = 512
T_LRU = 256
T_BLK = 1024
MOE_ROWS_MAIN = 320
MOE_ROWS_SMALL = 256
MOE_ROWS_EXTRA = 128


def _params(sem):
    return pltpu.CompilerParams(dimension_semantics=sem, vmem_limit_bytes=VMEM_LIMIT)


def _sigmoid(x):
    return 1.0 / (1.0 + jnp.exp(-x))


def _dot(a, b):
    return jnp.dot(a, b, preferred_element_type=F32)


def _dot_nt(a, b):
    return lax.dot_general(a, b, (((1,), (1,)), ((), ())), preferred_element_type=F32)


def _onehot(mask):
    return jnp.where(mask, 1.0, 0.0).astype(BF16)


def _split3(x):
    hi = x.astype(BF16)
    r1 = x - hi.astype(F32)
    mid = r1.astype(BF16)
    lo = (r1 - mid.astype(F32)).astype(BF16)
    return hi, mid, lo


def _ada_kernel(c_ref, w_ref, b_ref, o_ref):
    c = c_ref[...]
    cond = c * _sigmoid(c)
    hi, mid, _ = _split3(cond)
    w = w_ref[...].astype(BF16)
    o_ref[...] = _dot(hi, w) + _dot(mid, w) + b_ref[...]


def _ada_mod(c, ada_w, ada_b):
    depth, d, d6 = ada_w.shape
    bsz = c.shape[0]
    n_col = d6 // d
    return pl.pallas_call(
        _ada_kernel,
        grid=(depth, n_col),
        in_specs=[
            pl.BlockSpec((bsz, d), lambda l, j: (0, 0)),
            pl.BlockSpec((None, d, d), lambda l, j: (l, 0, j)),
            pl.BlockSpec((None, 1, d), lambda l, j: (l, 0, j)),
        ],
        out_specs=pl.BlockSpec((None, bsz, d), lambda l, j: (l, 0, j)),
        out_shape=jax.ShapeDtypeStruct((depth, bsz, d6), F32),
        compiler_params=_params(("arbitrary", "arbitrary")),
        name="ada_mod",
    )(c, ada_w, ada_b.reshape(depth, 1, d6))


def _inproj_kernel(x_ref, mod_ref, w_ref, o_ref, *, col_chunk, q_chunk):
    sh = mod_ref[0:1, :]
    sc = mod_ref[1:2, :]
    h = (x_ref[...] * (1.0 + sc) + sh).astype(BF16)
    for j in range(o_ref.shape[1] // col_chunk):
        cols = slice(j * col_chunk, (j + 1) * col_chunk)
        y = _dot(h, w_ref[:, cols])
        if j == q_chunk:
            y = y * Q_PRESCALE
        o_ref[:, cols] = y.astype(o_ref.dtype)


def _inproj(x, mod_l, w_bf16, q_chunk):
    bsz, t, d = x.shape
    width = w_bf16.shape[1]
    tm = min(TM_PROJ, t)
    return pl.pallas_call(
        functools.partial(_inproj_kernel, col_chunk=PROJ_CHUNK, q_chunk=q_chunk),
        grid=(bsz, t // tm),
        in_specs=[
            pl.BlockSpec((None, tm, d), lambda b, i: (b, i, 0)),
            pl.BlockSpec((None, 6, d), lambda b, i: (b, 0, 0)),
            pl.BlockSpec((d, width), lambda b, i: (0, 0)),
        ],
        out_specs=pl.BlockSpec((None, tm, width), lambda b, i: (b, i, 0)),
        out_shape=jax.ShapeDtypeStruct((bsz, t, width), BF16),
        compiler_params=_params(("arbitrary", "arbitrary")),
        name="inproj",
    )(x, mod_l, w_bf16)


def _diffattn_kernel(q_ref, k_ref, v_ref, lam_ref, gain_ref, o_ref, m_sc, acc_sc, s0_sc, s1_sc,
                     p0_sc, p1_sc, a0_sc, a1_sc, kn_sc, *, tile, lam_init):
    h = pl.program_id(1)
    qi = pl.program_id(2)
    dh = DA_HEAD_DIM
    hd = 2 * dh
    tk = tile // 2
    reps = tk // LANES
    s_bufs, p_bufs, a_bufs = (s0_sc, s1_sc), (p0_sc, p1_sc), (a0_sc, a1_sc)

    lane = lax.broadcasted_iota(jnp.int32, (1, hd), 1)
    q = q_ref[...]
    zero = jnp.zeros_like(q)
    q2 = jnp.concatenate([jnp.where(lane < dh, q, zero), jnp.where(lane >= dh, q, zero)], axis=0)

    hf = jnp.full((1, 1), h + 1, jnp.int32).astype(F32)
    slope = jnp.exp2(hf * (-8.0 / DA_HEADS)) * LOG2E
    col = lax.broadcasted_iota(jnp.int32, (1, tk), 1)
    ones = jnp.ones((tk, hd), BF16)

    m_sc[...] = jnp.full(m_sc.shape, NEG_BIG, F32)
    acc_sc[...] = jnp.zeros(acc_sc.shape, F32)

    def max_half_norms(x):
        xf = x.astype(F32)
        sq = xf * xf
        out = []
        for keep in (lane < dh, lane >= dh):
            rows = jnp.sum(jnp.where(keep, sq, 0.0), axis=1, keepdims=True)
            out.append(jnp.sqrt(jnp.max(rows, axis=0, keepdims=True)))
        return out

    @pl.when(qi == 0)
    def _():
        kn_sc[...] = jnp.concatenate([jnp.broadcast_to(n, (4, LANES))
                                      for n in max_half_norms(k_ref[...])], axis=0)

    n_sub = 2 * qi + 2

    def key_start(j):
        return pl.multiple_of((n_sub - 1 - j) * tk, tk)

    every = slice(0, 2 * tile)

    def scores(j, slot, rows=every):
        ks = key_start(j)
        bias = (col + (ks - qi * tile)).astype(F32) * slope
        s_bufs[slot][rows, :] = _dot_nt(q2[rows, :], k_ref[pl.ds(ks, tk), :]) + bias

    def softmax(j, slot, masked, rows=every):
        s = s_bufs[slot][rows, :]
        if masked:
            n_rows = rows.stop - rows.start
            rowp = ((lax.broadcasted_iota(jnp.int32, (n_rows, tk), 0) + rows.start) & (tile - 1)) \
                + qi * tile
            colp = lax.broadcasted_iota(jnp.int32, (n_rows, tk), 1) + key_start(j)
            s = jnp.where(colp <= rowp, s, NEG_BIG)
        m_old = m_sc[rows, :]
        m_new = jnp.maximum(m_old, jnp.max(s, axis=1, keepdims=True))
        p_bufs[slot][rows, :] = jnp.exp2(s - jnp.concatenate([m_new] * reps, axis=1)).astype(BF16)
        a_bufs[slot][rows, :] = jnp.exp2(m_old - m_new)
        m_sc[rows, :] = m_new

    def values(j, slot, rows=every):
        v_aug = jnp.concatenate([v_ref[pl.ds(key_start(j), tk), :], ones], axis=1)
        alpha = a_bufs[slot][rows, :]
        acc_sc[rows, :] = (jnp.concatenate([alpha, alpha], axis=1) * acc_sc[rows, :]
                           + _dot(p_bufs[slot][rows, :], v_aug))

    late = (slice(tile // 2, tile), slice(tile + tile // 2, 2 * tile))

    def first(stage, *args):
        for rows in late:
            stage(0, 0, *args, rows)

    def softmax_frozen(slot):
        m_rep = jnp.concatenate([m_sc[...]] * reps, axis=1)
        p_bufs[slot][...] = jnp.exp2(s_bufs[slot][...] - m_rep).astype(BF16)

    def values_frozen(j, slot):
        v_aug = jnp.concatenate([v_ref[pl.ds(key_start(j), tk), :], ones], axis=1)
        acc_sc[...] += _dot(p_bufs[slot][...], v_aug)

    first(scores)
    scores(1, 1)
    first(softmax, True)

    @pl.when(qi == 0)
    def _():
        softmax(1, 1, True)
        first(values)
        values(1, 1)

    @pl.when(qi > 0)
    def _():
        scores(2, 0)
        softmax(1, 1, True)
        first(values)
        scores(3, 1)
        softmax(2, 0, False)
        values(1, 1)
        qn = max_half_norms(q)
        qk_max = jnp.maximum(qn[0] * kn_sc[0:1, 0:1], qn[1] * kn_sc[4:5, 0:1])
        m_min = jnp.min(m_sc[...], axis=0, keepdims=True)[:, 0:1]
        reach = (qk_max - m_min + DA_DEAD_LOG2) / slope
        first_dead = jnp.floor((reach - 1.0) / tk) + 3.0
        first_dead = jnp.max(jnp.clip(first_dead, 0.0, 1e6)).astype(jnp.int32)
        pairs_end = jnp.maximum(2, jnp.minimum(qi + 1, (first_dead + 1) // 2))

        freeze = jnp.max(jnp.where(qk_max - m_min <= DA_FREEZE_LOG2, 1.0, 0.0)) > 0.5
        freeze = jnp.logical_and(freeze, pairs_end > 2)

        @pl.when(jnp.logical_not(freeze))
        def _():
            def body(i, carry):
                t = 2 * i
                scores(t, 0)
                softmax(t - 1, 1, False)
                values(t - 2, 0)
                scores(t + 1, 1)
                softmax(t, 0, False)
                values(t - 1, 1)
                return carry

            lax.fori_loop(2, pairs_end, body, 0)
            t = 2 * pairs_end
            softmax(t - 1, 1, False)
            values(t - 2, 0)
            values(t - 1, 1)

        @pl.when(freeze)
        def _():
            scores(4, 0)
            softmax_frozen(1)
            values(2, 0)
            scores(5, 1)
            softmax_frozen(0)
            values_frozen(3, 1)

            def body(i, carry):
                t = 2 * i
                scores(t, 0)
                softmax_frozen(1)
                values_frozen(t - 2, 0)
                scores(t + 1, 1)
                softmax_frozen(0)
                values_frozen(t - 1, 1)
                return carry

            lax.fori_loop(3, pairs_end, body, 0)
            t = 2 * pairs_end
            softmax_frozen(1)
            values_frozen(t - 2, 0)
            values_frozen(t - 1, 1)

    lv = lam_ref[...].astype(F32)
    dots = jnp.sum(lv[0:1, :] * lv[1:2, :], axis=1, keepdims=True)
    dots2 = jnp.sum(lv[2:3, :] * lv[3:4, :], axis=1, keepdims=True)
    lam = jnp.exp(dots) - jnp.exp(dots2) + lam_init
    acc = acc_sc[...]
    o0 = acc[0:tile, 0:hd] / acc[0:tile, hd:2 * hd]
    o1 = acc[tile:2 * tile, 0:hd] / acc[tile:2 * tile, hd:2 * hd]
    o = o0 - lam * o1
    ms = jnp.mean(o * o, axis=1, keepdims=True)
    o = o * lax.rsqrt(ms + 1e-6) * gain_ref[...] * (1.0 - lam_init)
    o_ref[...] = o.astype(o_ref.dtype)


def _diff_attention(proj, diff_lambda, diff_gain, lam_init):
    bsz, t, _ = proj.shape
    tile = min(TQ_ATT, t)
    hd = 2 * DA_HEAD_DIM
    kern = functools.partial(_diffattn_kernel, tile=tile, lam_init=lam_init)
    return pl.pallas_call(
        kern,
        grid=(bsz, DA_HEADS, t // tile),
        in_specs=[
            pl.BlockSpec((None, tile, hd), lambda b, h, i: (b, i, h)),
            pl.BlockSpec((None, t, hd), lambda b, h, i: (b, 0, DA_HEADS + h)),
            pl.BlockSpec((None, t, hd), lambda b, h, i: (b, 0, 2 * DA_HEADS + h)),
            pl.BlockSpec((4, DA_HEAD_DIM), lambda b, h, i: (0, 0)),
            pl.BlockSpec((1, hd), lambda b, h, i: (0, 0)),
        ],
        out_specs=pl.BlockSpec((None, tile, hd), lambda b, h, i: (b, i, h)),
        out_shape=jax.ShapeDtypeStruct((bsz, t, DA_HEADS * hd), BF16),
        scratch_shapes=[
            pltpu.VMEM((2 * tile, LANES), F32),
            pltpu.VMEM((2 * tile, 2 * hd), F32),
            pltpu.VMEM((2 * tile, tile // 2), F32),
            pltpu.VMEM((2 * tile, tile // 2), F32),
            pltpu.VMEM((2 * tile, tile // 2), BF16),
            pltpu.VMEM((2 * tile, tile // 2), BF16),
            pltpu.VMEM((2 * tile, LANES), F32),
            pltpu.VMEM((2 * tile, LANES), F32),
            pltpu.VMEM((8, LANES), F32),
        ],
        compiler_params=_params(("arbitrary", "arbitrary", "arbitrary")),
        name="diff_attn",
    )(proj, proj, proj, diff_lambda, diff_gain.reshape(1, hd))


def _hgrn_kernel(q_ref, f_ref, i_ref, g_ref, gamma_ref, gain_ref, o_ref, st_sc, *, layer):
    @pl.when(pl.program_id(1) == 0)
    def _():
        st_sc[...] = jnp.zeros(st_sc.shape, F32)

    gam = gamma_ref[...].astype(F32)
    e = jnp.exp(gam - jnp.max(gam, axis=0, keepdims=True))
    sm = e / jnp.sum(e, axis=0, keepdims=True)
    lb_all = jnp.sum(sm[0:layer + 1, :], axis=0, keepdims=True)

    c = HG_CHUNK
    row = lax.broadcasted_iota(jnp.int32, (c, c), 0)
    col = lax.broadcasted_iota(jnp.int32, (c, c), 1)
    tril = col <= row
    tril_bf = jnp.where(tril, 1.0, 0.0).astype(BF16)
    gain = gain_ref[...]

    heads = range(HG_HEADS)
    hcols = [slice(h * HG_DK, (h + 1) * HG_DK) for h in heads]
    lbs = [lb_all[:, hc] for hc in hcols]
    for n in range(q_ref.shape[0] // c):
        rows = slice(n * c, (n + 1) * c)
        sig = [_sigmoid(f_ref[rows, hc].astype(F32)) for hc in hcols]
        logf = [jnp.log(lbs[h] + (1.0 - lbs[h]) * sig[h]) for h in heads]
        kk = [(1.0 - lbs[h]) * (1.0 - sig[h]) for h in heads]
        parts = [_split3(x) for x in logf]
        b = [_dot(tril_bf, p[0]) + _dot(tril_bf, p[1]) + _dot(tril_bf, p[2]) for p in parts]
        b_mid = [x[c // 2 - 1:c // 2, :] for x in b]
        b_last = [x[c - 1:c, :] for x in b]
        qh = [q_ref[rows, hc].astype(F32) for hc in hcols]
        qs = [x * _sigmoid(x) for x in qh]
        v = [i_ref[rows, hc] for hc in hcols]
        att = [_dot_nt((qs[h] * jnp.exp(b[h] - b_mid[h])).astype(BF16),
                       (kk[h] * jnp.exp(b_mid[h] - b[h])).astype(BF16)) for h in heads]
        att = [jnp.where(tril, x, 0.0).astype(BF16) for x in att]
        o_intra = [_dot(att[h], v[h]) for h in heads]
        kd = [(kk[h] * jnp.exp(b_last[h] - b[h])).astype(BF16) for h in heads]
        ds_t = [_dot(v[h].T, kd[h]) for h in heads]
        st = [st_sc[h] for h in heads]
        o_inter = [_dot_nt((qs[h] * jnp.exp(b[h])).astype(BF16), st[h].astype(BF16))
                   for h in heads]
        for h in heads:
            st_sc[h] = st[h] * jnp.exp(b_last[h]) + ds_t[h]
        for h in heads:
            o = o_intra[h] + o_inter[h]
            gh = g_ref[rows, hcols[h]].astype(F32)
            ms = jnp.mean(o * o, axis=1, keepdims=True)
            o = o * lax.rsqrt(ms + 1e-6) * gain * (gh * _sigmoid(gh))
            o_ref[rows, hcols[h]] = o.astype(o_ref.dtype)


def _hgrn2(proj, hgrn_gamma, hgrn_gain, layer):
    bsz, t, _ = proj.shape
    tt = min(T_HG, t)
    width = HG_HEADS * HG_DK
    base = 3 * DA_HEADS * 2 * DA_HEAD_DIM // width
    spec = lambda k: pl.BlockSpec((None, tt, width), lambda b, i: (b, i, base + k))
    return pl.pallas_call(
        functools.partial(_hgrn_kernel, layer=layer),
        grid=(bsz, t // tt),
        in_specs=[
            spec(0), spec(1), spec(2), spec(3),
            pl.BlockSpec((hgrn_gamma.shape[0], width), lambda b, i: (0, 0)),
            pl.BlockSpec((1, HG_DK), lambda b, i: (0, 0)),
        ],
        out_specs=pl.BlockSpec((None, tt, width), lambda b, i: (b, i, 0)),
        out_shape=jax.ShapeDtypeStruct((bsz, t, width), BF16),
        scratch_shapes=[pltpu.VMEM((HG_HEADS, HG_DK, HG_DK), F32)],
        compiler_params=_params(("arbitrary", "arbitrary")),
        name="hgrn2",
    )(proj, proj, proj, proj, hgrn_gamma, hgrn_gain.reshape(1, HG_DK))


def _route(logits_t):
    mx = jnp.max(logits_t, axis=0, keepdims=True)
    ex = jnp.exp(logits_t - mx)
    probs = ex / jnp.sum(ex, axis=0, keepdims=True)
    p = [probs[e:e + 1, :] for e in range(N_EXPERTS)]
    g = E_PER_GROUP
    scores = []
    for gi in range(N_GROUPS):
        pg = p[gi * g:(gi + 1) * g]
        best = None
        for a in range(g):
            for b in range(a + 1, g):
                pair = pg[a] + pg[b]
                best = pair if best is None else jnp.maximum(best, pair)
        scores.append(best)
    group_id = jnp.zeros_like(p[0])
    gates = [jnp.zeros_like(p[0]) for _ in range(g)]
    for gi in range(N_GROUPS):
        sel = None
        for gj in range(N_GROUPS):
            if gj == gi:
                continue
            cond = (scores[gi] > scores[gj]) if gj < gi else (scores[gi] >= scores[gj])
            sel = cond if sel is None else (sel & cond)
        group_id = jnp.where(sel, float(gi), group_id)
        pg = p[gi * g:(gi + 1) * g]
        chosen = []
        for a in range(g):
            rank = jnp.zeros_like(pg[a])
            for b in range(g):
                if b == a:
                    continue
                ahead = (pg[b] >= pg[a]) if b < a else (pg[b] > pg[a])
                rank = rank + jnp.where(ahead, 1.0, 0.0)
            chosen.append(sel & (rank < 2.0))
        denom = None
        for a in range(g):
            term = jnp.where(chosen[a], pg[a], 0.0)
            denom = term if denom is None else denom + term
        for a in range(g):
            gates[a] = jnp.where(chosen[a], pg[a] / denom, gates[a])
    return group_id, gates


def _outproj_kernel(a_ref, b_ref, x_ref, mod_ref, w_ref, lng_ref, lnb_ref, rwt_ref, rb_ref,
                    xo_ref, h_ref, row_ref, col_ref, *, alpha):
    half = a_ref.shape[1]
    tm = a_ref.shape[0]
    y = _dot(a_ref[...], w_ref[0:half, :]) + _dot(b_ref[...], w_ref[half:2 * half, :])
    g1 = mod_ref[2:3, :]
    sh2 = mod_ref[3:4, :]
    sc2 = mod_ref[4:5, :]
    r = alpha * x_ref[...] + (1.0 + g1) * y
    mu = jnp.mean(r, axis=1, keepdims=True)
    rc = r - mu
    var = jnp.mean(rc * rc, axis=1, keepdims=True)
    xn = rc * lax.rsqrt(var + 1e-5) * lng_ref[...] + lnb_ref[...]
    xo_ref[...] = xn
    h2 = xn * (1.0 + sc2) + sh2
    h_ref[...] = h2.astype(BF16)
    h_hi, h_mid, _ = _split3(h2)
    rw = rwt_ref[...]
    w_hi, w_mid, _ = _split3(rw)
    logits_t = _dot_nt(w_hi, h_hi) + _dot_nt(w_hi, h_mid) + _dot_nt(w_mid, h_hi) + rb_ref[...]
    group_id, gates = _route(logits_t)
    sel = [jnp.where(group_id == float(gi), 1.0, 0.0) for gi in range(N_GROUPS)]
    onehot = jnp.concatenate(sel + [jnp.zeros((8 - N_GROUPS, tm), F32)], axis=0).astype(BF16)
    src = lax.broadcasted_iota(jnp.int32, (tm, tm), 0)
    dst = lax.broadcasted_iota(jnp.int32, (tm, tm), 1)
    earlier = jnp.where(src < dst, 1.0, 0.0).astype(BF16)
    counts = _dot(onehot, earlier)
    rank = sel[0] * counts[0:1, :]
    for gi in range(1, N_GROUPS):
        rank = rank + sel[gi] * counts[gi:gi + 1, :]
    info = jnp.concatenate(gates + [group_id, rank], axis=0)
    row_ref[...] = jnp.concatenate(
        [group_id, rank, jnp.zeros((8 - 2, tm), F32)], axis=0)
    pad = jnp.zeros((LANES - info.shape[0], tm), F32)
    col_ref[...] = jnp.concatenate([info, pad], axis=0).T


def _outproj(a, b, x, mod_l, w_bf16, ln_g, ln_b, router_w, router_b, alpha):
    bsz, t, d = x.shape
    half = a.shape[2]
    tm = min(T_BLK, t)
    tok = lambda width: pl.BlockSpec((None, tm, width), lambda bi, i: (bi, i, 0))
    full = lambda r, c: pl.BlockSpec((r, c), lambda bi, i: (0, 0))
    return pl.pallas_call(
        functools.partial(_outproj_kernel, alpha=alpha),
        grid=(bsz, t // tm),
        in_specs=[
            tok(half), tok(half), tok(d),
            pl.BlockSpec((None, 6, d), lambda bi, i: (bi, 0, 0)),
            full(2 * half, d), full(1, d), full(1, d), full(N_EXPERTS, d), full(N_EXPERTS, 1),
        ],
        out_specs=[tok(d), tok(d), pl.BlockSpec((None, 8, tm), lambda bi, i: (bi, 0, i)),
                   tok(LANES)],
        out_shape=[
            jax.ShapeDtypeStruct((bsz, t, d), F32),
            jax.ShapeDtypeStruct((bsz, t, d), BF16),
            jax.ShapeDtypeStruct((bsz, 8, t), F32),
            jax.ShapeDtypeStruct((bsz, t, LANES), F32),
        ],
        compiler_params=_params(("arbitrary", "arbitrary")),
        name="outproj_ln_route",
    )(a, b, x, mod_l, w_bf16, ln_g.reshape(1, d), ln_b.reshape(1, d), router_w.T,
      router_b.reshape(N_EXPERTS, 1))


def _slab_rows(tm):
    extra = -(-(tm - MOE_ROWS_MAIN) // MOE_ROWS_EXTRA)
    return MOE_ROWS_MAIN + max(extra, 0) * MOE_ROWS_EXTRA


def _extra_chunks(count):
    return (jnp.maximum(count - MOE_ROWS_MAIN, 0) + MOE_ROWS_EXTRA - 1) // MOE_ROWS_EXTRA


def _moe_expert_kernel(cnt_ref, h_ref, row_ref, col_ref, wg_ref, wu_ref, wd_ref, z_ref):
    g = pl.program_id(0)
    blk = pl.program_id(1)
    count = cnt_ref[g * pl.num_programs(1) + blk]
    mine = row_ref[0:1, :] == g.astype(F32)
    rank = row_ref[1:2, :]
    info = col_ref[...]
    info_hi = info.astype(BF16)
    info_lo = (info - info_hi.astype(F32)).astype(BF16)

    def run_rows(r0, m):
        rid = (lax.broadcasted_iota(jnp.int32, (m, 1), 0) + r0).astype(F32)
        pick = _onehot((rank == rid) & mine)
        xs = _dot(pick, h_ref[...]).astype(BF16)
        gm = _dot(pick, info_hi) + _dot(pick, info_lo)
        acc = None
        for j in range(E_PER_GROUP):
            a = _dot(xs, wg_ref[j])
            u = _dot(xs, wu_ref[j])
            he = (a * _sigmoid(a) * u * gm[:, j:j + 1]).astype(BF16)
            part = _dot(he, wd_ref[j])
            acc = part if acc is None else acc + part
        z_ref[pl.ds(r0, m), :] = acc.astype(z_ref.dtype)

    @pl.when(count <= MOE_ROWS_SMALL)
    def _():
        run_rows(0, MOE_ROWS_SMALL)
        z_ref[MOE_ROWS_SMALL:MOE_ROWS_MAIN, :] = jnp.zeros(
            (MOE_ROWS_MAIN - MOE_ROWS_SMALL, z_ref.shape[1]), z_ref.dtype)

    @pl.when(count > MOE_ROWS_SMALL)
    def _():
        run_rows(0, MOE_ROWS_MAIN)

    rest = z_ref.shape[0] - MOE_ROWS_MAIN
    if rest:
        z_ref[MOE_ROWS_MAIN:, :] = jnp.zeros((rest, z_ref.shape[1]), z_ref.dtype)

        def body(i, carry):
            run_rows(pl.multiple_of(MOE_ROWS_MAIN + i * MOE_ROWS_EXTRA, 16), MOE_ROWS_EXTRA)
            return carry

        lax.fori_loop(0, _extra_chunks(count), body, 0)


def _moe_combine_kernel(cnt_ref, z0_ref, z1_ref, z2_ref, z3_ref, col_ref, x_ref, mod_ref, lng_ref,
                        lnb_ref, o_ref, y_sc, *, alpha):
    blk = pl.program_id(0) * pl.num_programs(1) + pl.program_id(1)
    n_blk = pl.num_programs(0) * pl.num_programs(1)
    z_refs = (z0_ref, z1_ref, z2_ref, z3_ref)
    main = MOE_ROWS_MAIN
    grp = col_ref[:, E_PER_GROUP:E_PER_GROUP + 1]
    rank = col_ref[:, E_PER_GROUP + 1:E_PER_GROUP + 2]
    where_to = jnp.where(rank < float(main), grp * float(main) + rank, -1.0)
    lane = lax.broadcasted_iota(jnp.int32, (1, N_GROUPS * main), 1).astype(F32)
    pick = _onehot(where_to == lane)
    z_all = jnp.concatenate([zr[0:main, :] for zr in z_refs], axis=0)
    y_sc[...] = _dot(pick, z_all)
    lane_x = lax.broadcasted_iota(jnp.int32, (1, MOE_ROWS_EXTRA), 1).astype(F32)
    for gi in range(N_GROUPS):
        def body(i, carry, gi=gi):
            r0 = pl.multiple_of(main + i * MOE_ROWS_EXTRA, 16)
            hit = (grp == float(gi)) & ((rank - r0.astype(F32)) == lane_x)
            y_sc[...] += _dot(_onehot(hit), z_refs[gi][pl.ds(r0, MOE_ROWS_EXTRA), :])
            return carry

        lax.fori_loop(0, _extra_chunks(cnt_ref[gi * n_blk + blk]), body, 0)

    g2 = mod_ref[5:6, :]
    r = alpha * x_ref[...] + (1.0 + g2) * y_sc[...]
    mu = jnp.mean(r, axis=1, keepdims=True)
    rc = r - mu
    var = jnp.mean(rc * rc, axis=1, keepdims=True)
    o_ref[...] = rc * lax.rsqrt(var + 1e-5) * lng_ref[...] + lnb_ref[...]


def _moe(h2, rowinfo, colinfo, x, mod_l, wg, wu, wd, ln_g, ln_b, alpha):
    bsz, t, d = x.shape
    tm = min(T_BLK, t)
    nb = t // tm
    n_blk = bsz * nb
    dff = wg.shape[2]
    slab = _slab_rows(tm)
    group_of = rowinfo[:, 0, :].reshape(1, n_blk, tm)
    counts = jnp.sum(group_of == jnp.arange(N_GROUPS, dtype=F32).reshape(N_GROUPS, 1, 1), axis=2)
    counts = counts.astype(jnp.int32).reshape(N_GROUPS * n_blk)

    z = pl.pallas_call(
        _moe_expert_kernel,
        grid_spec=pltpu.PrefetchScalarGridSpec(
            num_scalar_prefetch=1,
            grid=(N_GROUPS, n_blk),
            in_specs=[
                pl.BlockSpec((None, tm, d), lambda g, i, c: (i // nb, i % nb, 0)),
                pl.BlockSpec((None, 8, tm), lambda g, i, c: (i // nb, 0, i % nb)),
                pl.BlockSpec((None, tm, LANES), lambda g, i, c: (i // nb, i % nb, 0)),
                pl.BlockSpec((E_PER_GROUP, d, dff), lambda g, i, c: (g, 0, 0)),
                pl.BlockSpec((E_PER_GROUP, d, dff), lambda g, i, c: (g, 0, 0)),
                pl.BlockSpec((E_PER_GROUP, dff, d), lambda g, i, c: (g, 0, 0)),
            ],
            out_specs=pl.BlockSpec((None, None, slab, d), lambda g, i, c: (g, i, 0, 0)),
        ),
        out_shape=jax.ShapeDtypeStruct((N_GROUPS, n_blk, slab, d), BF16),
        compiler_params=_params(("arbitrary", "arbitrary")),
        name="moe_experts",
    )(counts, h2, rowinfo, colinfo, wg, wu, wd)

    zspec = lambda gi: pl.BlockSpec((None, None, slab, d), lambda b, i, c: (gi, b * nb + i, 0, 0))
    tok = lambda width: pl.BlockSpec((None, tm, width), lambda b, i, c: (b, i, 0))
    return pl.pallas_call(
        functools.partial(_moe_combine_kernel, alpha=alpha),
        grid_spec=pltpu.PrefetchScalarGridSpec(
            num_scalar_prefetch=1,
            grid=(bsz, nb),
            in_specs=[
                zspec(0), zspec(1), zspec(2), zspec(3), tok(LANES), tok(d),
                pl.BlockSpec((None, 6, d), lambda b, i, c: (b, 0, 0)),
                pl.BlockSpec((1, d), lambda b, i, c: (0, 0)),
                pl.BlockSpec((1, d), lambda b, i, c: (0, 0)),
            ],
            out_specs=tok(d),
            scratch_shapes=[pltpu.VMEM((tm, d), F32)],
        ),
        out_shape=jax.ShapeDtypeStruct((bsz, t, d), F32),
        compiler_params=_params(("arbitrary", "arbitrary")),
        name="moe_combine_ln",
    )(counts, z, z, z, z, colinfo, x, mod_l, ln_g.reshape(1, d), ln_b.reshape(1, d))


def _lru_kernel(x_ref, g_ref, cw_ref, cb_ref, wa_ref, ba_ref, wx_ref, bx_ref, lam_ref,
                o_ref, xpad_sc, h_sc):
    tt = x_ref.shape[0]
    pad = 8

    @pl.when(pl.program_id(1) == 0)
    def _():
        xpad_sc[0:pad, :] = jnp.zeros((pad, xpad_sc.shape[1]), F32)
        h_sc[...] = jnp.zeros(h_sc.shape, F32)

    xpad_sc[pad:pad + tt, :] = x_ref[...].astype(F32)
    xc = cb_ref[...] + jnp.zeros((tt, x_ref.shape[1]), F32)
    for j in range(CONV_WIDTH):
        off = pad - (CONV_WIDTH - 1) + j
        xc = xc + cw_ref[j:j + 1, :] * xpad_sc[off:off + tt, :]
    xpad_sc[0:pad, :] = xpad_sc[tt:tt + pad, :]

    xb = xc.astype(BF16)
    r = _sigmoid(_dot(xb, wa_ref[...]) + ba_ref[...])
    i = _sigmoid(_dot(xb, wx_ref[...]) + bx_ref[...])
    lam = lam_ref[...].astype(F32)
    softplus_neg = jnp.maximum(-lam, 0.0) + jnp.log(1.0 + jnp.exp(-jnp.abs(lam)))
    log_a = -LRU_C * r * softplus_neg
    a = jnp.exp(log_a)
    gain_sq = jnp.maximum(1.0 - jnp.exp(2.0 * log_a), 1e-12)
    u = gain_sq * lax.rsqrt(gain_sq) * (i * xc)

    groups = (tt // SUBLANES, SUBLANES, a.shape[1])
    a = a.reshape(groups)
    u = u.reshape(groups)
    rowi = lax.broadcasted_iota(jnp.int32, (1, SUBLANES, 1), 1)
    d = 1
    while d < SUBLANES:
        a_sh = jnp.where(rowi >= d, pltpu.roll(a, d, 1), 1.0)
        u_sh = jnp.where(rowi >= d, pltpu.roll(u, d, 1), 0.0)
        u = u + a * u_sh
        a = a * a_sh
        d *= 2
    a = a.reshape(tt, groups[2])
    u = u.reshape(tt, groups[2])
    gr = g_ref[...].astype(F32)
    gelu = 0.5 * gr * (1.0 + jnp.tanh(0.7978845608028654 * (gr + 0.044715 * gr * gr * gr)))
    h_prev = h_sc[...]
    out = []
    for grp in range(tt // SUBLANES):
        rows = slice(grp * SUBLANES, (grp + 1) * SUBLANES)
        h_grp = u[rows, :] + a[rows, :] * h_prev
        out.append(gelu[rows, :] * h_grp)
        h_prev = h_grp[SUBLANES - 1:SUBLANES, :]
    h_sc[...] = h_prev
    o_ref[...] = jnp.concatenate(out, axis=0).astype(o_ref.dtype)


def _rg_lru(proj, conv_w, conv_b, wa_dense, ba, wx_dense, bx, lam):
    bsz, t, _ = proj.shape
    tt = min(T_LRU, t)
    w = LRU_WIDTH
    nblk = w // LANES
    row = lambda a: a.reshape(1, w)
    full = lambda r, c: pl.BlockSpec((r, c), lambda b, i: (0, 0))
    return pl.pallas_call(
        _lru_kernel,
        grid=(bsz, t // tt),
        in_specs=[
            pl.BlockSpec((None, tt, w), lambda b, i: (b, i, 0)),
            pl.BlockSpec((None, tt, w), lambda b, i: (b, i, 1)),
            full(CONV_WIDTH, w), full(1, w), full(w, w), full(1, w), full(w, w), full(1, w),
            full(1, w),
        ],
        out_specs=pl.BlockSpec((None, tt, w), lambda b, i: (b, i, 0)),
        out_shape=jax.ShapeDtypeStruct((bsz, t, w), BF16),
        scratch_shapes=[pltpu.VMEM((tt + SUBLANES, w), F32), pltpu.VMEM((1, w), F32)],
        compiler_params=_params(("arbitrary", "arbitrary")),
        name="rg_lru",
    )(proj, proj, conv_w, row(conv_b), wa_dense, row(ba), wx_dense, row(bx), row(lam))


def _sb_kernel(q_ref, k_ref, v_ref, o_ref, r_sc, acc_sc, *bufs, tile):
    qi = pl.program_id(2)
    d = SB_HEAD_DIM
    tk = tile // 2
    z_bufs, lb_bufs, l_bufs, w_bufs = (bufs[i * SB_SETS:(i + 1) * SB_SETS] for i in range(4))
    lane = lax.broadcasted_iota(jnp.int32, (1, 2 * d), 1)
    q = q_ref[...]
    zero = jnp.zeros_like(q)
    q2 = jnp.concatenate([jnp.where(lane < d, q, zero), jnp.where(lane >= d, q, zero)], axis=0)
    rj = lax.broadcasted_iota(jnp.int32, (tk, tk), 0)
    cs = lax.broadcasted_iota(jnp.int32, (tk, tk), 1)
    upper = jnp.where(rj > cs, 1.0, 0.0).astype(BF16)

    r_sc[...] = jnp.zeros(r_sc.shape, F32)
    acc_sc[...] = jnp.zeros(acc_sc.shape, F32)
    n_sub = 2 * qi + 2
    every = slice(0, 2 * tile)
    per_head = (slice(0, tile), slice(tile, 2 * tile))

    def key_start(j):
        return pl.multiple_of((n_sub - 1 - j) * tk, tk)

    def strict_mask(j, rows):
        n_rows = rows.stop - rows.start
        rowp = (lax.broadcasted_iota(jnp.int32, (n_rows, tk), 0) + rows.start) & (tile - 1)
        colp = lax.broadcasted_iota(jnp.int32, (n_rows, tk), 1) + (1 - j) * tk
        return colp < rowp

    def logits(j, b, rows=every):
        z_bufs[b][rows, :] = _dot_nt(q2[rows, :], k_ref[pl.ds(key_start(j), tk), :])

    def gates(j, b, rows=every, masked=False):
        z = z_bufs[b][rows, :]
        log_1m = jnp.log(1.0 + jnp.exp2(-jnp.abs(z))) * (-LOG2E) - jnp.maximum(z, 0.0)
        lb_bufs[b][rows, :] = z + log_1m
        if masked:
            log_1m = jnp.where(strict_mask(j, rows), log_1m, 0.0)
        l_bufs[b][rows, :] = log_1m.astype(BF16)

    def weights(j, b, rows=every, masked=False):
        log_1m = l_bufs[b][rows, :]
        after = _dot(log_1m, upper) + r_sc[rows, :]
        w = jnp.exp2(lb_bufs[b][rows, :] + after)
        if masked:
            w = jnp.where(strict_mask(j, rows), w, 0.0)
        w_bufs[b][rows, :] = w.astype(BF16)
        r_sc[rows, :] = after[:, 0:1] + log_1m[:, 0:1].astype(F32)

    def values(j, b, rows=every):
        acc_sc[rows, :] += _dot(w_bufs[b][rows, :], v_ref[pl.ds(key_start(j), tk), :])

    late = tuple(slice(r.start + tile // 2, r.stop) for r in per_head)
    early = tuple(slice(r.start, r.start + tile // 2) for r in per_head)

    def first(stage, **kw):
        for rows in late:
            stage(0, 0, rows, **kw)

    @pl.when(qi == 0)
    def _():
        first(logits)
        logits(1, 1)
        first(gates, masked=True)
        gates(1, 1, masked=True)
        first(weights, masked=True)
        weights(1, 1, masked=True)
        first(values)
        values(1, 1)

    @pl.when(qi > 0)
    def _():
        def alive(rows=every):
            return (jnp.max(r_sc[rows, :]) > -SB_DEAD_LOG2).astype(jnp.int32)

        def third(stage, rows_set):
            for rows in rows_set:
                stage(2, 2, rows)

        first(logits)
        logits(1, 1)
        first(gates, masked=True)
        third(logits, early)
        gates(1, 1, masked=True)
        first(weights, masked=True)
        third(gates, early)
        weights(1, 1, masked=True)
        first(values)
        third(weights, early)
        values(1, 1)
        third(values, early)

        @pl.when(jnp.maximum(alive(late[0]), alive(late[1])) > 0)
        def _():
            for stage in (logits, gates, weights, values):
                third(stage, late)

        def cond(carry):
            j, live = carry
            return (j < n_sub) & (live > 0)

        def body(carry):
            j, _ = carry
            for rows in per_head:
                logits(j, 0, rows)
            for rows in per_head:
                gates(j, 0, rows)
            for rows in per_head:
                weights(j, 0, rows)
            for rows in per_head:
                values(j, 0, rows)
            return j + 1, alive()

        lax.while_loop(cond, body, (jnp.int32(3), alive()))

    acc = acc_sc[...]
    o_ref[...] = jnp.where(lane < d, acc[0:tile, :], acc[tile:2 * tile, :]).astype(o_ref.dtype)


def _sb_attention(proj):
    bsz, t, _ = proj.shape
    tq = min(T_SB, t)
    pairs = SB_HEADS // 2
    wblk = 2 * SB_HEAD_DIM
    base = 2 * LRU_WIDTH // wblk
    return pl.pallas_call(
        functools.partial(_sb_kernel, tile=tq),
        grid=(bsz, pairs, t // tq),
        in_specs=[
            pl.BlockSpec((None, tq, wblk), lambda b, h, i: (b, i, base + h)),
            pl.BlockSpec((None, t, wblk), lambda b, h, i: (b, 0, base + pairs + h)),
            pl.BlockSpec((None, t, wblk), lambda b, h, i: (b, 0, base + 2 * pairs + h)),
        ],
        out_specs=pl.BlockSpec((None, tq, wblk), lambda b, h, i: (b, i, h)),
        out_shape=jax.ShapeDtypeStruct((bsz, t, SB_HEADS * SB_HEAD_DIM), BF16),
        scratch_shapes=[pltpu.VMEM((2 * tq, 1), F32), pltpu.VMEM((2 * tq, wblk), F32)]
        + [pltpu.VMEM((2 * tq, tq // 2), F32)] * (2 * SB_SETS)
        + [pltpu.VMEM((2 * tq, tq // 2), BF16)] * (2 * SB_SETS),
        compiler_params=_params(("arbitrary", "arbitrary", "arbitrary")),
        name="sb_attn",
    )(proj, proj, proj)


def _block_diag(w):
    g, n, _ = w.shape
    eye = jnp.eye(g, dtype=w.dtype)
    return (eye[:, None, :, None] * w[:, :, None, :]).reshape(g * n, g * n)


def kernel(x, c, ada_w, ada_b, ln_g, ln_b, even_w_in, even_w_out, diff_lambda, diff_gain, hgrn_gamma, hgrn_gain, odd_w_in, odd_w_out, conv_w, conv_b, lru_wa, lru_ba, lru_wx, lru_bx, lru_lambda, router_w, router_b, moe_w_gate, moe_w_up, moe_w_down):
    depth = ada_w.shape[0]
    bsz, t, d = x.shape
    alpha = (2.0 * depth) ** 0.25
    mod = _ada_mod(c, ada_w, ada_b).reshape(depth, bsz, 6, d)
    for l in range(depth):
        j = l // 2
        mod_l = mod[l]
        if l % 2 == 0:
            lam_init = 0.8 - 0.6 * math.exp(-0.3 * l)
            proj = _inproj(x, mod_l, even_w_in[j].astype(BF16), q_chunk=0)
            mix_a = _diff_attention(proj, diff_lambda[j], diff_gain[j], lam_init)
            mix_b = _hgrn2(proj, hgrn_gamma, hgrn_gain[j], l)
            w_out = even_w_out[j]
        else:
            proj = _inproj(x, mod_l, odd_w_in[j].astype(BF16), q_chunk=2 * LRU_WIDTH // PROJ_CHUNK)
            mix_a = _rg_lru(proj, conv_w[j], conv_b[j], _block_diag(lru_wa[j]).astype(BF16),
                            lru_ba[j], _block_diag(lru_wx[j]).astype(BF16), lru_bx[j],
                            lru_lambda[j])
            mix_b = _sb_attention(proj)
            w_out = odd_w_out[j]
        x, h2, rowinfo, colinfo = _outproj(mix_a, mix_b, x, mod_l, w_out.astype(BF16),
                                           ln_g[l, 0], ln_b[l, 0], router_w, router_b, alpha)
        x = _moe(h2, rowinfo, colinfo, x, mod_l, moe_w_gate[l].astype(BF16),
                 moe_w_up[l].astype(BF16), moe_w_down[l].astype(BF16), ln_g[l, 1], ln_b[l, 1],
                 alpha)
    return x
```

```python
import functools
import math

import jax
import jax.numpy as jnp
from jax import lax
from jax.experimental import pallas as pl
from jax.experimental.pallas import tpu as pltpu

F32 = jnp.float32
BF16 = jnp.bfloat16

DA_HEADS = 4
DA_HEAD_DIM = 64
HG_HEADS = 4
HG_DK = 128
HG_CHUNK = 64
LRU_WIDTH = 512
LRU_BLOCKS = 8
CONV_WIDTH = 4
LRU_C = 8.0
SB_HEADS = 8
SB_HEAD_DIM = 64
N_EXPERTS = 16
N_GROUPS = 4
E_PER_GROUP = N_EXPERTS // N_GROUPS
D_FF = 512

LANES = 128
SUBLANES = 8
NEG_BIG = -1e30
LOG2E = 1.4426950408889634
Q_PRESCALE = DA_HEAD_DIM ** -0.5 * LOG2E
PROJ_CHUNK = 512
VMEM_LIMIT = 56 * 1024 * 1024

TM_PROJ = 1024
TQ_ATT = 512
T_SB = 512
SB_SETS = 3
SB_DEAD_LOG2 = 160.0
DA_DEAD_LOG2 = 152.0
DA_FREEZE_LOG2 = 64.0
T_HG = 512
T_LRU = 256
T_BLK = 1024
MOE_ROWS_MAIN = 320
MOE_ROWS_SMALL = 256
MOE_ROWS_EXTRA = 128


def _params(sem):
    return pltpu.CompilerParams(dimension_semantics=sem, vmem_limit_bytes=VMEM_LIMIT)


def _sigmoid(x):
    return 1.0 / (1.0 + jnp.exp(-x))


def _dot(a, b):
    return jnp.dot(a, b, preferred_element_type=F32)


def _dot_nt(a, b):
    return lax.dot_general(a, b, (((1,), (1,)), ((), ())), preferred_element_type=F32)


def _onehot(mask):
    return jnp.where(mask, 1.0, 0.0).astype(BF16)


def _split3(x):
    hi = x.astype(BF16)
    r1 = x - hi.astype(F32)
    mid = r1.astype(BF16)
    lo = (r1 - mid.astype(F32)).astype(BF16)
    return hi, mid, lo


def _ada_kernel(c_ref, w_ref, b_ref, o_ref):
    c = c_ref[...]
    cond = c * _sigmoid(c)
    hi, mid, _ = _split3(cond)
    w = w_ref[...].astype(BF16)
    o_ref[...] = _dot(hi, w) + _dot(mid, w) + b_ref[...]


def _ada_mod(c, ada_w, ada_b):
    depth, d, d6 = ada_w.shape
    bsz = c.shape[0]
    n_col = d6 // d
    return pl.pallas_call(
        _ada_kernel,
        grid=(depth, n_col),
        in_specs=[
            pl.BlockSpec((bsz, d), lambda l, j: (0, 0)),
            pl.BlockSpec((None, d, d), lambda l, j: (l, 0, j)),
            pl.BlockSpec((None, 1, d), lambda l, j: (l, 0, j)),
        ],
        out_specs=pl.BlockSpec((None, bsz, d), lambda l, j: (l, 0, j)),
        out_shape=jax.ShapeDtypeStruct((depth, bsz, d6), F32),
        compiler_params=_params(("arbitrary", "arbitrary")),
        name="ada_mod",
    )(c, ada_w, ada_b.reshape(depth, 1, d6))


def _inproj_kernel(x_ref, mod_ref, w_ref, o_ref, *, col_chunk, q_chunk):
    sh = mod_ref[0:1, :]
    sc = mod_ref[1:2, :]
    h = (x_ref[...] * (1.0 + sc) + sh).astype(BF16)
    for j in range(o_ref.shape[1] // col_chunk):
        cols = slice(j * col_chunk, (j + 1) * col_chunk)
        y = _dot(h, w_ref[:, cols])
        if j == q_chunk:
            y = y * Q_PRESCALE
        o_ref[:, cols] = y.astype(o_ref.dtype)


def _inproj(x, mod_l, w_bf16, q_chunk):
    bsz, t, d = x.shape
    width = w_bf16.shape[1]
    tm = min(TM_PROJ, t)
    return pl.pallas_call(
        functools.partial(_inproj_kernel, col_chunk=PROJ_CHUNK, q_chunk=q_chunk),
        grid=(bsz, t // tm),
        in_specs=[
            pl.BlockSpec((None, tm, d), lambda b, i: (b, i, 0)),
            pl.BlockSpec((None, 6, d), lambda b, i: (b, 0, 0)),
            pl.BlockSpec((d, width), lambda b, i: (0, 0)),
        ],
        out_specs=pl.BlockSpec((None, tm, width), lambda b, i: (b, i, 0)),
        out_shape=jax.ShapeDtypeStruct((bsz, t, width), BF16),
        compiler_params=_params(("arbitrary", "arbitrary")),
        name="inproj",
    )(x, mod_l, w_bf16)


def _diffattn_kernel(q_ref, k_ref, v_ref, lam_ref, gain_ref, o_ref, m_sc, acc_sc, s0_sc, s1_sc,
                     p0_sc, p1_sc, a0_sc, a1_sc, kn_sc, *, tile, lam_init):
    h = pl.program_id(1)
    qi = pl.program_id(2)
    dh = DA_HEAD_DIM
    hd = 2 * dh
    tk = tile // 2
    reps = tk // LANES
    s_bufs, p_bufs, a_bufs = (s0_sc, s1_sc), (p0_sc, p1_sc), (a0_sc, a1_sc)

    lane = lax.broadcasted_iota(jnp.int32, (1, hd), 1)
    q = q_ref[...]
    zero = jnp.zeros_like(q)
    q2 = jnp.concatenate([jnp.where(lane < dh, q, zero), jnp.where(lane >= dh, q, zero)], axis=0)

    hf = jnp.full((1, 1), h + 1, jnp.int32).astype(F32)
    slope = jnp.exp2(hf * (-8.0 / DA_HEADS)) * LOG2E
    col = lax.broadcasted_iota(jnp.int32, (1, tk), 1)
    ones = jnp.ones((tk, hd), BF16)

    m_sc[...] = jnp.full(m_sc.shape, NEG_BIG, F32)
    acc_sc[...] = jnp.zeros(acc_sc.shape, F32)

    def max_half_norms(x):
        xf = x.astype(F32)
        sq = xf * xf
        out = []
        for keep in (lane < dh, lane >= dh):
            rows = jnp.sum(jnp.where(keep, sq, 0.0), axis=1, keepdims=True)
            out.append(jnp.sqrt(jnp.max(rows, axis=0, keepdims=True)))
        return out

    @pl.when(qi == 0)
    def _():
        kn_sc[...] = jnp.concatenate([jnp.broadcast_to(n, (4, LANES))
                                      for n in max_half_norms(k_ref[...])], axis=0)

    n_sub = 2 * qi + 2

    def key_start(j):
        return pl.multiple_of((n_sub - 1 - j) * tk, tk)

    every = slice(0, 2 * tile)

    def scores(j, slot, rows=every):
        ks = key_start(j)
        bias = (col + (ks - qi * tile)).astype(F32) * slope
        s_bufs[slot][rows, :] = _dot_nt(q2[rows, :], k_ref[pl.ds(ks, tk), :]) + bias

    def softmax(j, slot, masked, rows=every):
        s = s_bufs[slot][rows, :]
        if masked:
            n_rows = rows.stop - rows.start
            rowp = ((lax.broadcasted_iota(jnp.int32, (n_rows, tk), 0) + rows.start) & (tile - 1)) \
                + qi * tile
            colp = lax.broadcasted_iota(jnp.int32, (n_rows, tk), 1) + key_start(j)
            s = jnp.where(colp <= rowp, s, NEG_BIG)
        m_old = m_sc[rows, :]
        m_new = jnp.maximum(m_old, jnp.max(s, axis=1, keepdims=True))
        p_bufs[slot][rows, :] = jnp.exp2(s - jnp.concatenate([m_new] * reps, axis=1)).astype(BF16)
        a_bufs[slot][rows, :] = jnp.exp2(m_old - m_new)
        m_sc[rows, :] = m_new

    def values(j, slot, rows=every):
        v_aug = jnp.concatenate([v_ref[pl.ds(key_start(j), tk), :], ones], axis=1)
        alpha = a_bufs[slot][rows, :]
        acc_sc[rows, :] = (jnp.concatenate([alpha, alpha], axis=1) * acc_sc[rows, :]
                           + _dot(p_bufs[slot][rows, :], v_aug))

    late = (slice(tile // 2, tile), slice(tile + tile // 2, 2 * tile))

    def first(stage, *args):
        for rows in late:
            stage(0, 0, *args, rows)

    def softmax_frozen(slot, rows=every):
        m_rep = jnp.concatenate([m_sc[rows, :]] * reps, axis=1)
        p_bufs[slot][rows, :] = jnp.exp2(s_bufs[slot][rows, :] - m_rep).astype(BF16)

    def values_frozen(j, slot, rows=every):
        v_aug = jnp.concatenate([v_ref[pl.ds(key_start(j), tk), :], ones], axis=1)
        acc_sc[rows, :] += _dot(p_bufs[slot][rows, :], v_aug)

    first(scores)
    scores(1, 1)
    first(softmax, True)

    @pl.when(qi == 0)
    def _():
        softmax(1, 1, True)
        first(values)
        values(1, 1)

    @pl.when(qi > 0)
    def _():
        scores(2, 0)
        softmax(1, 1, True)
        first(values)
        scores(3, 1)
        softmax(2, 0, False)
        values(1, 1)
        qn = max_half_norms(q)
        qk_max = jnp.maximum(qn[0] * kn_sc[0:1, 0:1], qn[1] * kn_sc[4:5, 0:1])
        m_min = jnp.min(m_sc[...], axis=0, keepdims=True)[:, 0:1]
        reach = (qk_max - m_min + DA_DEAD_LOG2) / slope
        first_dead = jnp.floor((reach - 1.0) / tk) + 3.0
        first_dead = jnp.max(jnp.clip(first_dead, 0.0, 1e6)).astype(jnp.int32)
        pairs_end = jnp.maximum(2, jnp.minimum(qi + 1, (first_dead + 1) // 2))

        freeze = jnp.max(jnp.where(qk_max - m_min <= DA_FREEZE_LOG2, 1.0, 0.0)) > 0.5
        freeze = jnp.logical_and(freeze, pairs_end > 2)

        @pl.when(jnp.logical_not(freeze))
        def _():
            def body(i, carry):
                t = 2 * i
                scores(t, 0)
                softmax(t - 1, 1, False)
                values(t - 2, 0)
                scores(t + 1, 1)
                softmax(t, 0, False)
                values(t - 1, 1)
                return carry

            lax.fori_loop(2, pairs_end, body, 0)
            t = 2 * pairs_end
            softmax(t - 1, 1, False)
            values(t - 2, 0)
            values(t - 1, 1)

        @pl.when(freeze)
        def _():
            scores(4, 0)
            softmax_frozen(1)
            values(2, 0)
            scores(5, 1)
            softmax_frozen(0)
            values_frozen(3, 1)

            def body(i, carry):
                t = 2 * i
                scores(t, 0)
                softmax_frozen(1)
                values_frozen(t - 2, 0)
                scores(t + 1, 1)
                softmax_frozen(0)
                values_frozen(t - 1, 1)
                return carry

            lax.fori_loop(3, pairs_end, body, 0)
            t = 2 * pairs_end
            softmax_frozen(1)
            values_frozen(t - 2, 0)
            values_frozen(t - 1, 1)

    lv = lam_ref[...].astype(F32)
    dots = jnp.sum(lv[0:1, :] * lv[1:2, :], axis=1, keepdims=True)
    dots2 = jnp.sum(lv[2:3, :] * lv[3:4, :], axis=1, keepdims=True)
    lam = jnp.exp(dots) - jnp.exp(dots2) + lam_init
    acc = acc_sc[...]
    o0 = acc[0:tile, 0:hd] / acc[0:tile, hd:2 * hd]
    o1 = acc[tile:2 * tile, 0:hd] / acc[tile:2 * tile, hd:2 * hd]
    o = o0 - lam * o1
    ms = jnp.mean(o * o, axis=1, keepdims=True)
    o = o * lax.rsqrt(ms + 1e-6) * gain_ref[...] * (1.0 - lam_init)
    o_ref[...] = o.astype(o_ref.dtype)


def _diff_attention(proj, diff_lambda, diff_gain, lam_init):
    bsz, t, _ = proj.shape
    tile = min(TQ_ATT, t)
    hd = 2 * DA_HEAD_DIM
    kern = functools.partial(_diffattn_kernel, tile=tile, lam_init=lam_init)
    return pl.pallas_call(
        kern,
        grid=(bsz, DA_HEADS, t // tile),
        in_specs=[
            pl.BlockSpec((None, tile, hd), lambda b, h, i: (b, i, h)),
            pl.BlockSpec((None, t, hd), lambda b, h, i: (b, 0, DA_HEADS + h)),
            pl.BlockSpec((None, t, hd), lambda b, h, i: (b, 0, 2 * DA_HEADS + h)),
            pl.BlockSpec((4, DA_HEAD_DIM), lambda b, h, i: (0, 0)),
            pl.BlockSpec((1, hd), lambda b, h, i: (0, 0)),
        ],
        out_specs=pl.BlockSpec((None, tile, hd), lambda b, h, i: (b, i, h)),
        out_shape=jax.ShapeDtypeStruct((bsz, t, DA_HEADS * hd), BF16),
        scratch_shapes=[
            pltpu.VMEM((2 * tile, LANES), F32),
            pltpu.VMEM((2 * tile, 2 * hd), F32),
            pltpu.VMEM((2 * tile, tile // 2), F32),
            pltpu.VMEM((2 * tile, tile // 2), F32),
            pltpu.VMEM((2 * tile, tile // 2), BF16),
            pltpu.VMEM((2 * tile, tile // 2), BF16),
            pltpu.VMEM((2 * tile, LANES), F32),
            pltpu.VMEM((2 * tile, LANES), F32),
            pltpu.VMEM((8, LANES), F32),
        ],
        compiler_params=_params(("arbitrary", "arbitrary", "arbitrary")),
        name="diff_attn",
    )(proj, proj, proj, diff_lambda, diff_gain.reshape(1, hd))


def _hgrn_kernel(q_ref, f_ref, i_ref, g_ref, gamma_ref, gain_ref, o_ref, st_sc, *, layer):
    @pl.when(pl.program_id(1) == 0)
    def _():
        st_sc[...] = jnp.zeros(st_sc.shape, F32)

    gam = gamma_ref[...].astype(F32)
    e = jnp.exp(gam - jnp.max(gam, axis=0, keepdims=True))
    sm = e / jnp.sum(e, axis=0, keepdims=True)
    lb_all = jnp.sum(sm[0:layer + 1, :], axis=0, keepdims=True)

    c = HG_CHUNK
    row = lax.broadcasted_iota(jnp.int32, (c, c), 0)
    col = lax.broadcasted_iota(jnp.int32, (c, c), 1)
    tril = col <= row
    tril_bf = jnp.where(tril, 1.0, 0.0).astype(BF16)
    gain = gain_ref[...]

    heads = range(HG_HEADS)
    hcols = [slice(h * HG_DK, (h + 1) * HG_DK) for h in heads]
    lbs = [lb_all[:, hc] for hc in hcols]
    for n in range(q_ref.shape[0] // c):
        rows = slice(n * c, (n + 1) * c)
        sig = [_sigmoid(f_ref[rows, hc].astype(F32)) for hc in hcols]
        logf = [jnp.log(lbs[h] + (1.0 - lbs[h]) * sig[h]) for h in heads]
        kk = [(1.0 - lbs[h]) * (1.0 - sig[h]) for h in heads]
        parts = [_split3(x) for x in logf]
        b = [_dot(tril_bf, p[0]) + _dot(tril_bf, p[1]) + _dot(tril_bf, p[2]) for p in parts]
        b_mid = [x[c // 2 - 1:c // 2, :] for x in b]
        b_last = [x[c - 1:c, :] for x in b]
        qh = [q_ref[rows, hc].astype(F32) for hc in hcols]
        qs = [x * _sigmoid(x) for x in qh]
        v = [i_ref[rows, hc] for hc in hcols]
        att = [_dot_nt((qs[h] * jnp.exp(b[h] - b_mid[h])).astype(BF16),
                       (kk[h] * jnp.exp(b_mid[h] - b[h])).astype(BF16)) for h in heads]
        att = [jnp.where(tril, x, 0.0).astype(BF16) for x in att]
        o_intra = [_dot(att[h], v[h]) for h in heads]
        kd = [(kk[h] * jnp.exp(b_last[h] - b[h])).astype(BF16) for h in heads]
        ds_t = [_dot(v[h].T, kd[h]) for h in heads]
        st = [st_sc[h] for h in heads]
        o_inter = [_dot_nt((qs[h] * jnp.exp(b[h])).astype(BF16), st[h].astype(BF16))
                   for h in heads]
        for h in heads:
            st_sc[h] = st[h] * jnp.exp(b_last[h]) + ds_t[h]
        for h in heads:
            o = o_intra[h] + o_inter[h]
            gh = g_ref[rows, hcols[h]].astype(F32)
            ms = jnp.mean(o * o, axis=1, keepdims=True)
            o = o * lax.rsqrt(ms + 1e-6) * gain * (gh * _sigmoid(gh))
            o_ref[rows, hcols[h]] = o.astype(o_ref.dtype)


def _hgrn2(proj, hgrn_gamma, hgrn_gain, layer):
    bsz, t, _ = proj.shape
    tt = min(T_HG, t)
    width = HG_HEADS * HG_DK
    base = 3 * DA_HEADS * 2 * DA_HEAD_DIM // width
    spec = lambda k: pl.BlockSpec((None, tt, width), lambda b, i: (b, i, base + k))
    return pl.pallas_call(
        functools.partial(_hgrn_kernel, layer=layer),
        grid=(bsz, t // tt),
        in_specs=[
            spec(0), spec(1), spec(2), spec(3),
            pl.BlockSpec((hgrn_gamma.shape[0], width), lambda b, i: (0, 0)),
            pl.BlockSpec((1, HG_DK), lambda b, i: (0, 0)),
        ],
        out_specs=pl.BlockSpec((None, tt, width), lambda b, i: (b, i, 0)),
        out_shape=jax.ShapeDtypeStruct((bsz, t, width), BF16),
        scratch_shapes=[pltpu.VMEM((HG_HEADS, HG_DK, HG_DK), F32)],
        compiler_params=_params(("arbitrary", "arbitrary")),
        name="hgrn2",
    )(proj, proj, proj, proj, hgrn_gamma, hgrn_gain.reshape(1, HG_DK))


def _route(logits_t):
    mx = jnp.max(logits_t, axis=0, keepdims=True)
    ex = jnp.exp(logits_t - mx)
    probs = ex / jnp.sum(ex, axis=0, keepdims=True)
    p = [probs[e:e + 1, :] for e in range(N_EXPERTS)]
    g = E_PER_GROUP
    scores = []
    for gi in range(N_GROUPS):
        pg = p[gi * g:(gi + 1) * g]
        best = None
        for a in range(g):
            for b in range(a + 1, g):
                pair = pg[a] + pg[b]
                best = pair if best is None else jnp.maximum(best, pair)
        scores.append(best)
    group_id = jnp.zeros_like(p[0])
    gates = [jnp.zeros_like(p[0]) for _ in range(g)]
    for gi in range(N_GROUPS):
        sel = None
        for gj in range(N_GROUPS):
            if gj == gi:
                continue
            cond = (scores[gi] > scores[gj]) if gj < gi else (scores[gi] >= scores[gj])
            sel = cond if sel is None else (sel & cond)
        group_id = jnp.where(sel, float(gi), group_id)
        pg = p[gi * g:(gi + 1) * g]
        chosen = []
        for a in range(g):
            rank = jnp.zeros_like(pg[a])
            for b in range(g):
                if b == a:
                    continue
                ahead = (pg[b] >= pg[a]) if b < a else (pg[b] > pg[a])
                rank = rank + jnp.where(ahead, 1.0, 0.0)
            chosen.append(sel & (rank < 2.0))
        denom = None
        for a in range(g):
            term = jnp.where(chosen[a], pg[a], 0.0)
            denom = term if denom is None else denom + term
        for a in range(g):
            gates[a] = jnp.where(chosen[a], pg[a] / denom, gates[a])
    return group_id, gates


def _outproj_kernel(a_ref, b_ref, x_ref, mod_ref, w_ref, lng_ref, lnb_ref, rwt_ref, rb_ref,
                    xo_ref, h_ref, row_ref, col_ref, *, alpha):
    half = a_ref.shape[1]
    tm = a_ref.shape[0]
    y = _dot(a_ref[...], w_ref[0:half, :]) + _dot(b_ref[...], w_ref[half:2 * half, :])
    g1 = mod_ref[2:3, :]
    sh2 = mod_ref[3:4, :]
    sc2 = mod_ref[4:5, :]
    r = alpha * x_ref[...] + (1.0 + g1) * y
    mu = jnp.mean(r, axis=1, keepdims=True)
    rc = r - mu
    var = jnp.mean(rc * rc, axis=1, keepdims=True)
    xn = rc * lax.rsqrt(var + 1e-5) * lng_ref[...] + lnb_ref[...]
    xo_ref[...] = xn
    h2 = xn * (1.0 + sc2) + sh2
    h_ref[...] = h2.astype(BF16)
    h_hi, h_mid, _ = _split3(h2)
    rw = rwt_ref[...]
    w_hi, w_mid, _ = _split3(rw)
    logits_t = _dot_nt(w_hi, h_hi) + _dot_nt(w_hi, h_mid) + _dot_nt(w_mid, h_hi) + rb_ref[...]
    group_id, gates = _route(logits_t)
    sel = [jnp.where(group_id == float(gi), 1.0, 0.0) for gi in range(N_GROUPS)]
    onehot = jnp.concatenate(sel + [jnp.zeros((8 - N_GROUPS, tm), F32)], axis=0).astype(BF16)
    src = lax.broadcasted_iota(jnp.int32, (tm, tm), 0)
    dst = lax.broadcasted_iota(jnp.int32, (tm, tm), 1)
    earlier = jnp.where(src < dst, 1.0, 0.0).astype(BF16)
    counts = _dot(onehot, earlier)
    rank = sel[0] * counts[0:1, :]
    for gi in range(1, N_GROUPS):
        rank = rank + sel[gi] * counts[gi:gi + 1, :]
    info = jnp.concatenate(gates + [group_id, rank], axis=0)
    row_ref[...] = jnp.concatenate(
        [group_id, rank, jnp.zeros((8 - 2, tm), F32)], axis=0)
    pad = jnp.zeros((LANES - info.shape[0], tm), F32)
    col_ref[...] = jnp.concatenate([info, pad], axis=0).T


def _outproj(a, b, x, mod_l, w_bf16, ln_g, ln_b, router_w, router_b, alpha):
    bsz, t, d = x.shape
    half = a.shape[2]
    tm = min(T_BLK, t)
    tok = lambda width: pl.BlockSpec((None, tm, width), lambda bi, i: (bi, i, 0))
    full = lambda r, c: pl.BlockSpec((r, c), lambda bi, i: (0, 0))
    return pl.pallas_call(
        functools.partial(_outproj_kernel, alpha=alpha),
        grid=(bsz, t // tm),
        in_specs=[
            tok(half), tok(half), tok(d),
            pl.BlockSpec((None, 6, d), lambda bi, i: (bi, 0, 0)),
            full(2 * half, d), full(1, d), full(1, d), full(N_EXPERTS, d), full(N_EXPERTS, 1),
        ],
        out_specs=[tok(d), tok(d), pl.BlockSpec((None, 8, tm), lambda bi, i: (bi, 0, i)),
                   tok(LANES)],
        out_shape=[
            jax.ShapeDtypeStruct((bsz, t, d), F32),
            jax.ShapeDtypeStruct((bsz, t, d), BF16),
            jax.ShapeDtypeStruct((bsz, 8, t), F32),
            jax.ShapeDtypeStruct((bsz, t, LANES), F32),
        ],
        compiler_params=_params(("arbitrary", "arbitrary")),
        name="outproj_ln_route",
    )(a, b, x, mod_l, w_bf16, ln_g.reshape(1, d), ln_b.reshape(1, d), router_w.T,
      router_b.reshape(N_EXPERTS, 1))


def _slab_rows(tm):
    extra = -(-(tm - MOE_ROWS_MAIN) // MOE_ROWS_EXTRA)
    return MOE_ROWS_MAIN + max(extra, 0) * MOE_ROWS_EXTRA


def _extra_chunks(count):
    return (jnp.maximum(count - MOE_ROWS_MAIN, 0) + MOE_ROWS_EXTRA - 1) // MOE_ROWS_EXTRA


def _moe_expert_kernel(cnt_ref, h_ref, row_ref, col_ref, wg_ref, wu_ref, wd_ref, z_ref):
    g = pl.program_id(0)
    blk = pl.program_id(1)
    count = cnt_ref[g * pl.num_programs(1) + blk]
    mine = row_ref[0:1, :] == g.astype(F32)
    rank = row_ref[1:2, :]
    info = col_ref[...]
    info_hi = info.astype(BF16)
    info_lo = (info - info_hi.astype(F32)).astype(BF16)

    def run_rows(r0, m):
        rid = (lax.broadcasted_iota(jnp.int32, (m, 1), 0) + r0).astype(F32)
        pick = _onehot((rank == rid) & mine)
        xs = _dot(pick, h_ref[...]).astype(BF16)
        gm = _dot(pick, info_hi) + _dot(pick, info_lo)
        acc = None
        for j in range(E_PER_GROUP):
            a = _dot(xs, wg_ref[j])
            u = _dot(xs, wu_ref[j])
            he = (a * _sigmoid(a) * u * gm[:, j:j + 1]).astype(BF16)
            part = _dot(he, wd_ref[j])
            acc = part if acc is None else acc + part
        z_ref[pl.ds(r0, m), :] = acc.astype(z_ref.dtype)

    @pl.when(count <= MOE_ROWS_SMALL)
    def _():
        run_rows(0, MOE_ROWS_SMALL)
        z_ref[MOE_ROWS_SMALL:MOE_ROWS_MAIN, :] = jnp.zeros(
            (MOE_ROWS_MAIN - MOE_ROWS_SMALL, z_ref.shape[1]), z_ref.dtype)

    @pl.when(count > MOE_ROWS_SMALL)
    def _():
        run_rows(0, MOE_ROWS_MAIN)

    rest = z_ref.shape[0] - MOE_ROWS_MAIN
    if rest:
        z_ref[MOE_ROWS_MAIN:, :] = jnp.zeros((rest, z_ref.shape[1]), z_ref.dtype)

        def body(i, carry):
            run_rows(pl.multiple_of(MOE_ROWS_MAIN + i * MOE_ROWS_EXTRA, 16), MOE_ROWS_EXTRA)
            return carry

        lax.fori_loop(0, _extra_chunks(count), body, 0)


def _moe_combine_kernel(cnt_ref, z0_ref, z1_ref, z2_ref, z3_ref, col_ref, x_ref, mod_ref, lng_ref,
                        lnb_ref, o_ref, y_sc, *, alpha):
    blk = pl.program_id(0) * pl.num_programs(1) + pl.program_id(1)
    n_blk = pl.num_programs(0) * pl.num_programs(1)
    z_refs = (z0_ref, z1_ref, z2_ref, z3_ref)
    main = MOE_ROWS_MAIN
    grp = col_ref[:, E_PER_GROUP:E_PER_GROUP + 1]
    rank = col_ref[:, E_PER_GROUP + 1:E_PER_GROUP + 2]
    where_to = jnp.where(rank < float(main), grp * float(main) + rank, -1.0)
    lane = lax.broadcasted_iota(jnp.int32, (1, N_GROUPS * main), 1).astype(F32)
    pick = _onehot(where_to == lane)
    z_all = jnp.concatenate([zr[0:main, :] for zr in z_refs], axis=0)
    y_sc[...] = _dot(pick, z_all)
    lane_x = lax.broadcasted_iota(jnp.int32, (1, MOE_ROWS_EXTRA), 1).astype(F32)
    for gi in range(N_GROUPS):
        def body(i, carry, gi=gi):
            r0 = pl.multiple_of(main + i * MOE_ROWS_EXTRA, 16)
            hit = (grp == float(gi)) & ((rank - r0.astype(F32)) == lane_x)
            y_sc[...] += _dot(_onehot(hit), z_refs[gi][pl.ds(r0, MOE_ROWS_EXTRA), :])
            return carry

        lax.fori_loop(0, _extra_chunks(cnt_ref[gi * n_blk + blk]), body, 0)

    g2 = mod_ref[5:6, :]
    r = alpha * x_ref[...] + (1.0 + g2) * y_sc[...]
    mu = jnp.mean(r, axis=1, keepdims=True)
    rc = r - mu
    var = jnp.mean(rc * rc, axis=1, keepdims=True)
    o_ref[...] = rc * lax.rsqrt(var + 1e-5) * lng_ref[...] + lnb_ref[...]


def _moe(h2, rowinfo, colinfo, x, mod_l, layer, wg, wu, wd, ln_g, ln_b, alpha):
    bsz, t, d = x.shape
    tm = min(T_BLK, t)
    nb = t // tm
    n_blk = bsz * nb
    dff = wg.shape[3]
    slab = _slab_rows(tm)
    group_of = rowinfo[:, 0, :].reshape(1, n_blk, tm)
    counts = jnp.sum(group_of == jnp.arange(N_GROUPS, dtype=F32).reshape(N_GROUPS, 1, 1), axis=2)
    counts = counts.astype(jnp.int32).reshape(N_GROUPS * n_blk)

    z = pl.pallas_call(
        _moe_expert_kernel,
        grid_spec=pltpu.PrefetchScalarGridSpec(
            num_scalar_prefetch=1,
            grid=(N_GROUPS, n_blk),
            in_specs=[
                pl.BlockSpec((None, tm, d), lambda g, i, c: (i // nb, i % nb, 0)),
                pl.BlockSpec((None, 8, tm), lambda g, i, c: (i // nb, 0, i % nb)),
                pl.BlockSpec((None, tm, LANES), lambda g, i, c: (i // nb, i % nb, 0)),
                pl.BlockSpec((None, E_PER_GROUP, d, dff), lambda g, i, c: (layer, g, 0, 0)),
                pl.BlockSpec((None, E_PER_GROUP, d, dff), lambda g, i, c: (layer, g, 0, 0)),
                pl.BlockSpec((None, E_PER_GROUP, dff, d), lambda g, i, c: (layer, g, 0, 0)),
            ],
            out_specs=pl.BlockSpec((None, None, slab, d), lambda g, i, c: (g, i, 0, 0)),
        ),
        out_shape=jax.ShapeDtypeStruct((N_GROUPS, n_blk, slab, d), BF16),
        compiler_params=_params(("arbitrary", "arbitrary")),
        name="moe_experts",
    )(counts, h2, rowinfo, colinfo, wg, wu, wd)

    zspec = lambda gi: pl.BlockSpec((None, None, slab, d), lambda b, i, c: (gi, b * nb + i, 0, 0))
    tok = lambda width: pl.BlockSpec((None, tm, width), lambda b, i, c: (b, i, 0))
    return pl.pallas_call(
        functools.partial(_moe_combine_kernel, alpha=alpha),
        grid_spec=pltpu.PrefetchScalarGridSpec(
            num_scalar_prefetch=1,
            grid=(bsz, nb),
            in_specs=[
                zspec(0), zspec(1), zspec(2), zspec(3), tok(LANES), tok(d),
                pl.BlockSpec((None, 6, d), lambda b, i, c: (b, 0, 0)),
                pl.BlockSpec((1, d), lambda b, i, c: (0, 0)),
                pl.BlockSpec((1, d), lambda b, i, c: (0, 0)),
            ],
            out_specs=tok(d),
            scratch_shapes=[pltpu.VMEM((tm, d), F32)],
        ),
        out_shape=jax.ShapeDtypeStruct((bsz, t, d), F32),
        compiler_params=_params(("arbitrary", "arbitrary")),
        name="moe_combine_ln",
    )(counts, z, z, z, z, colinfo, x, mod_l, ln_g.reshape(1, d), ln_b.reshape(1, d))


def _lru_kernel(x_ref, g_ref, cw_ref, cb_ref, wa_ref, ba_ref, wx_ref, bx_ref, lam_ref,
                o_ref, xpad_sc, h_sc):
    tt = x_ref.shape[0]
    pad = 8

    @pl.when(pl.program_id(1) == 0)
    def _():
        xpad_sc[0:pad, :] = jnp.zeros((pad, xpad_sc.shape[1]), F32)
        h_sc[...] = jnp.zeros(h_sc.shape, F32)

    xpad_sc[pad:pad + tt, :] = x_ref[...].astype(F32)
    xc = cb_ref[...] + jnp.zeros((tt, x_ref.shape[1]), F32)
    for j in range(CONV_WIDTH):
        off = pad - (CONV_WIDTH - 1) + j
        xc = xc + cw_ref[j:j + 1, :] * xpad_sc[off:off + tt, :]
    xpad_sc[0:pad, :] = xpad_sc[tt:tt + pad, :]

    xb = xc.astype(BF16)
    r = _sigmoid(_dot(xb, wa_ref[...]) + ba_ref[...])
    i = _sigmoid(_dot(xb, wx_ref[...]) + bx_ref[...])
    lam = lam_ref[...].astype(F32)
    softplus_neg = jnp.maximum(-lam, 0.0) + jnp.log(1.0 + jnp.exp(-jnp.abs(lam)))
    log_a = -LRU_C * r * softplus_neg
    a = jnp.exp(log_a)
    gain_sq = jnp.maximum(1.0 - jnp.exp(2.0 * log_a), 1e-12)
    u = gain_sq * lax.rsqrt(gain_sq) * (i * xc)

    groups = (tt // SUBLANES, SUBLANES, a.shape[1])
    a = a.reshape(groups)
    u = u.reshape(groups)
    rowi = lax.broadcasted_iota(jnp.int32, (1, SUBLANES, 1), 1)
    d = 1
    while d < SUBLANES:
        a_sh = jnp.where(rowi >= d, pltpu.roll(a, d, 1), 1.0)
        u_sh = jnp.where(rowi >= d, pltpu.roll(u, d, 1), 0.0)
        u = u + a * u_sh
        a = a * a_sh
        d *= 2
    a = a.reshape(tt, groups[2])
    u = u.reshape(tt, groups[2])
    gr = g_ref[...].astype(F32)
    gelu = 0.5 * gr * (1.0 + jnp.tanh(0.7978845608028654 * (gr + 0.044715 * gr * gr * gr)))
    h_prev = h_sc[...]
    out = []
    for grp in range(tt // SUBLANES):
        rows = slice(grp * SUBLANES, (grp + 1) * SUBLANES)
        h_grp = u[rows, :] + a[rows, :] * h_prev
        out.append(gelu[rows, :] * h_grp)
        h_prev = h_grp[SUBLANES - 1:SUBLANES, :]
    h_sc[...] = h_prev
    o_ref[...] = jnp.concatenate(out, axis=0).astype(o_ref.dtype)


def _rg_lru(proj, conv_w, conv_b, wa_dense, ba, wx_dense, bx, lam):
    bsz, t, _ = proj.shape
    tt = min(T_LRU, t)
    w = LRU_WIDTH
    nblk = w // LANES
    row = lambda a: a.reshape(1, w)
    full = lambda r, c: pl.BlockSpec((r, c), lambda b, i: (0, 0))
    return pl.pallas_call(
        _lru_kernel,
        grid=(bsz, t // tt),
        in_specs=[
            pl.BlockSpec((None, tt, w), lambda b, i: (b, i, 0)),
            pl.BlockSpec((None, tt, w), lambda b, i: (b, i, 1)),
            full(CONV_WIDTH, w), full(1, w), full(w, w), full(1, w), full(w, w), full(1, w),
            full(1, w),
        ],
        out_specs=pl.BlockSpec((None, tt, w), lambda b, i: (b, i, 0)),
        out_shape=jax.ShapeDtypeStruct((bsz, t, w), BF16),
        scratch_shapes=[pltpu.VMEM((tt + SUBLANES, w), F32), pltpu.VMEM((1, w), F32)],
        compiler_params=_params(("arbitrary", "arbitrary")),
        name="rg_lru",
    )(proj, proj, conv_w, row(conv_b), wa_dense, row(ba), wx_dense, row(bx), row(lam))


def _sb_kernel(q_ref, k_ref, v_ref, o_ref, r_sc, acc_sc, *bufs, tile):
    qi = pl.program_id(2)
    d = SB_HEAD_DIM
    tk = tile // 2
    z_bufs, lb_bufs, l_bufs, w_bufs = (bufs[i * SB_SETS:(i + 1) * SB_SETS] for i in range(4))
    lane = lax.broadcasted_iota(jnp.int32, (1, 2 * d), 1)
    q = q_ref[...]
    zero = jnp.zeros_like(q)
    q2 = jnp.concatenate([jnp.where(lane < d, q, zero), jnp.where(lane >= d, q, zero)], axis=0)
    rj = lax.broadcasted_iota(jnp.int32, (tk, tk), 0)
    cs = lax.broadcasted_iota(jnp.int32, (tk, tk), 1)
    upper = jnp.where(rj > cs, 1.0, 0.0).astype(BF16)

    r_sc[...] = jnp.zeros(r_sc.shape, F32)
    acc_sc[...] = jnp.zeros(acc_sc.shape, F32)
    n_sub = 2 * qi + 2
    every = slice(0, 2 * tile)
    per_head = (slice(0, tile), slice(tile, 2 * tile))

    def key_start(j):
        return pl.multiple_of((n_sub - 1 - j) * tk, tk)

    def strict_mask(j, rows):
        n_rows = rows.stop - rows.start
        rowp = (lax.broadcasted_iota(jnp.int32, (n_rows, tk), 0) + rows.start) & (tile - 1)
        colp = lax.broadcasted_iota(jnp.int32, (n_rows, tk), 1) + (1 - j) * tk
        return colp < rowp

    def logits(j, b, rows=every):
        z_bufs[b][rows, :] = _dot_nt(q2[rows, :], k_ref[pl.ds(key_start(j), tk), :])

    def gates(j, b, rows=every, masked=False):
        z = z_bufs[b][rows, :]
        log_1m = jnp.log(1.0 + jnp.exp2(-jnp.abs(z))) * (-LOG2E) - jnp.maximum(z, 0.0)
        lb_bufs[b][rows, :] = z + log_1m
        if masked:
            log_1m = jnp.where(strict_mask(j, rows), log_1m, 0.0)
        l_bufs[b][rows, :] = log_1m.astype(BF16)

    def weights(j, b, rows=every, masked=False):
        log_1m = l_bufs[b][rows, :]
        after = _dot(log_1m, upper) + r_sc[rows, :]
        w = jnp.exp2(lb_bufs[b][rows, :] + after)
        if masked:
            w = jnp.where(strict_mask(j, rows), w, 0.0)
        w_bufs[b][rows, :] = w.astype(BF16)
        r_sc[rows, :] = after[:, 0:1] + log_1m[:, 0:1].astype(F32)

    def values(j, b, rows=every):
        acc_sc[rows, :] += _dot(w_bufs[b][rows, :], v_ref[pl.ds(key_start(j), tk), :])

    late = tuple(slice(r.start + tile // 2, r.stop) for r in per_head)
    early = tuple(slice(r.start, r.start + tile // 2) for r in per_head)

    def first(stage, **kw):
        for rows in late:
            stage(0, 0, rows, **kw)

    @pl.when(qi == 0)
    def _():
        first(logits)
        logits(1, 1)
        first(gates, masked=True)
        gates(1, 1, masked=True)
        first(weights, masked=True)
        weights(1, 1, masked=True)
        first(values)
        values(1, 1)

    @pl.when(qi > 0)
    def _():
        def alive(rows=every):
            return (jnp.max(r_sc[rows, :]) > -SB_DEAD_LOG2).astype(jnp.int32)

        def third(stage, rows_set):
            for rows in rows_set:
                stage(2, 2, rows)

        first(logits)
        logits(1, 1)
        first(gates, masked=True)
        third(logits, early)
        gates(1, 1, masked=True)
        first(weights, masked=True)
        third(gates, early)
        weights(1, 1, masked=True)
        first(values)
        third(weights, early)
        values(1, 1)
        third(values, early)

        @pl.when(jnp.maximum(alive(late[0]), alive(late[1])) > 0)
        def _():
            for stage in (logits, gates, weights, values):
                third(stage, late)

        def cond(carry):
            j, live = carry
            return (j < n_sub) & (live > 0)

        def body(carry):
            j, _ = carry
            for rows in per_head:
                logits(j, 0, rows)
            for rows in per_head:
                gates(j, 0, rows)
            for rows in per_head:
                weights(j, 0, rows)
            for rows in per_head:
                values(j, 0, rows)
            return j + 1, alive()

        lax.while_loop(cond, body, (jnp.int32(3), alive()))

    acc = acc_sc[...]
    o_ref[...] = jnp.where(lane < d, acc[0:tile, :], acc[tile:2 * tile, :]).astype(o_ref.dtype)


def _sb_attention(proj):
    bsz, t, _ = proj.shape
    tq = min(T_SB, t)
    pairs = SB_HEADS // 2
    wblk = 2 * SB_HEAD_DIM
    base = 2 * LRU_WIDTH // wblk
    return pl.pallas_call(
        functools.partial(_sb_kernel, tile=tq),
        grid=(bsz, pairs, t // tq),
        in_specs=[
            pl.BlockSpec((None, tq, wblk), lambda b, h, i: (b, i, base + h)),
            pl.BlockSpec((None, t, wblk), lambda b, h, i: (b, 0, base + pairs + h)),
            pl.BlockSpec((None, t, wblk), lambda b, h, i: (b, 0, base + 2 * pairs + h)),
        ],
        out_specs=pl.BlockSpec((None, tq, wblk), lambda b, h, i: (b, i, h)),
        out_shape=jax.ShapeDtypeStruct((bsz, t, SB_HEADS * SB_HEAD_DIM), BF16),
        scratch_shapes=[pltpu.VMEM((2 * tq, 1), F32), pltpu.VMEM((2 * tq, wblk), F32)]
        + [pltpu.VMEM((2 * tq, tq // 2), F32)] * (2 * SB_SETS)
        + [pltpu.VMEM((2 * tq, tq // 2), BF16)] * (2 * SB_SETS),
        compiler_params=_params(("arbitrary", "arbitrary", "arbitrary")),
        name="sb_attn",
    )(proj, proj, proj)


def _block_diag(w):
    g, n, _ = w.shape
    eye = jnp.eye(g, dtype=w.dtype)
    return (eye[:, None, :, None] * w[:, :, None, :]).reshape(g * n, g * n)


def kernel(x, c, ada_w, ada_b, ln_g, ln_b, even_w_in, even_w_out, diff_lambda, diff_gain, hgrn_gamma, hgrn_gain, odd_w_in, odd_w_out, conv_w, conv_b, lru_wa, lru_ba, lru_wx, lru_bx, lru_lambda, router_w, router_b, moe_w_gate, moe_w_up, moe_w_down):
    depth = ada_w.shape[0]
    bsz, t, d = x.shape
    alpha = (2.0 * depth) ** 0.25
    mod = _ada_mod(c, ada_w, ada_b).reshape(depth, bsz, 6, d)
    w_gate, w_up, w_down = (w.astype(BF16) for w in (moe_w_gate, moe_w_up, moe_w_down))
    for l in range(depth):
        j = l // 2
        mod_l = mod[l]
        if l % 2 == 0:
            lam_init = 0.8 - 0.6 * math.exp(-0.3 * l)
            proj = _inproj(x, mod_l, even_w_in[j].astype(BF16), q_chunk=0)
            mix_a = _diff_attention(proj, diff_lambda[j], diff_gain[j], lam_init)
            mix_b = _hgrn2(proj, hgrn_gamma, hgrn_gain[j], l)
            w_out = even_w_out[j]
        else:
            proj = _inproj(x, mod_l, odd_w_in[j].astype(BF16), q_chunk=2 * LRU_WIDTH // PROJ_CHUNK)
            mix_a = _rg_lru(proj, conv_w[j], conv_b[j], _block_diag(lru_wa[j]).astype(BF16),
                            lru_ba[j], _block_diag(lru_wx[j]).astype(BF16), lru_bx[j],
                            lru_lambda[j])
            mix_b = _sb_attention(proj)
            w_out = odd_w_out[j]
        x, h2, rowinfo, colinfo = _outproj(mix_a, mix_b, x, mod_l, w_out.astype(BF16),
                                           ln_g[l, 0], ln_b[l, 0], router_w, router_b, alpha)
        x = _moe(h2, rowinfo, colinfo, x, mod_l, l, w_gate, w_up, w_down, ln_g[l, 1], ln_b[l, 1],
                 alpha)
    return x
```

```python
import functools
import math

import jax
import jax.numpy as jnp
from jax import lax
from jax.experimental import pallas as pl
from jax.experimental.pallas import tpu as pltpu

F32 = jnp.float32
BF16 = jnp.bfloat16

DA_HEADS = 4
DA_HEAD_DIM = 64
HG_HEADS = 4
HG_DK = 128
HG_CHUNK = 64
LRU_WIDTH = 512
LRU_BLOCKS = 8
CONV_WIDTH = 4
LRU_C = 8.0
SB_HEADS = 8
SB_HEAD_DIM = 64
N_EXPERTS = 16
N_GROUPS = 4
E_PER_GROUP = N_EXPERTS // N_GROUPS
D_FF = 512

LANES = 128
SUBLANES = 8
NEG_BIG = -1e30
LOG2E = 1.4426950408889634
Q_PRESCALE = DA_HEAD_DIM ** -0.5 * LOG2E
PROJ_CHUNK = 512
VMEM_LIMIT = 56 * 1024 * 1024

TM_PROJ = 1024
TQ_ATT = 512
T_SB = 512
SB_SETS = 3
SB_DEAD_LOG2 = 160.0
DA_DEAD_LOG2 = 152.0
DA_FREEZE_LOG2 = 64.0
T_HG = 512
T_LRU = 256
T_BLK = 1024
OUTPROJ_CHUNKS = 4
MOE_ROWS_MAIN = 320
MOE_ROWS_SMALL = 256
MOE_ROWS_EXTRA = 128


def _params(sem):
    return pltpu.CompilerParams(dimension_semantics=sem, vmem_limit_bytes=VMEM_LIMIT)


def _sigmoid(x):
    return 0.5 * jnp.tanh(0.5 * x) + 0.5


def _dot(a, b):
    return jnp.dot(a, b, preferred_element_type=F32)


def _dot_nt(a, b):
    return lax.dot_general(a, b, (((1,), (1,)), ((), ())), preferred_element_type=F32)


def _onehot(mask):
    return jnp.where(mask, 1.0, 0.0).astype(BF16)


def _split3(x):
    hi = x.astype(BF16)
    r1 = x - hi.astype(F32)
    mid = r1.astype(BF16)
    lo = (r1 - mid.astype(F32)).astype(BF16)
    return hi, mid, lo


def _ada_kernel(c_ref, w_ref, b_ref, o_ref):
    c = c_ref[...]
    cond = c * _sigmoid(c)
    hi, mid, _ = _split3(cond)
    w = w_ref[...].astype(BF16)
    o_ref[...] = _dot(hi, w) + _dot(mid, w) + b_ref[...]


def _ada_mod(c, ada_w, ada_b):
    depth, d, d6 = ada_w.shape
    bsz = c.shape[0]
    n_col = d6 // d
    return pl.pallas_call(
        _ada_kernel,
        grid=(depth, n_col),
        in_specs=[
            pl.BlockSpec((bsz, d), lambda l, j: (0, 0)),
            pl.BlockSpec((None, d, d), lambda l, j: (l, 0, j)),
            pl.BlockSpec((None, 1, d), lambda l, j: (l, 0, j)),
        ],
        out_specs=pl.BlockSpec((None, bsz, d), lambda l, j: (l, 0, j)),
        out_shape=jax.ShapeDtypeStruct((depth, bsz, d6), F32),
        compiler_params=_params(("arbitrary", "arbitrary")),
        name="ada_mod",
    )(c, ada_w, ada_b.reshape(depth, 1, d6))


def _inproj_kernel(x_ref, mod_ref, w_ref, o_ref, *, col_chunk, q_chunk):
    sh = mod_ref[0:1, :]
    sc = mod_ref[1:2, :]
    h = (x_ref[...] * (1.0 + sc) + sh).astype(BF16)
    for j in range(o_ref.shape[1] // col_chunk):
        cols = slice(j * col_chunk, (j + 1) * col_chunk)
        y = _dot(h, w_ref[:, cols])
        if j == q_chunk:
            y = y * Q_PRESCALE
        o_ref[:, cols] = y.astype(o_ref.dtype)


def _inproj(x, mod_l, w_bf16, q_chunk):
    bsz, t, d = x.shape
    width = w_bf16.shape[1]
    tm = min(TM_PROJ, t)
    return pl.pallas_call(
        functools.partial(_inproj_kernel, col_chunk=PROJ_CHUNK, q_chunk=q_chunk),
        grid=(bsz, t // tm),
        in_specs=[
            pl.BlockSpec((None, tm, d), lambda b, i: (b, i, 0)),
            pl.BlockSpec((None, 6, d), lambda b, i: (b, 0, 0)),
            pl.BlockSpec((d, width), lambda b, i: (0, 0)),
        ],
        out_specs=pl.BlockSpec((None, tm, width), lambda b, i: (b, i, 0)),
        out_shape=jax.ShapeDtypeStruct((bsz, t, width), BF16),
        compiler_params=_params(("arbitrary", "arbitrary")),
        name="inproj",
    )(x, mod_l, w_bf16)


def _diffattn_kernel(q_ref, k_ref, v_ref, lam_ref, gain_ref, o_ref, m_sc, acc_sc, s0_sc, s1_sc,
                     p0_sc, p1_sc, a0_sc, a1_sc, kn_sc, *, tile, lam_init):
    h = pl.program_id(1)
    qi = pl.program_id(2)
    dh = DA_HEAD_DIM
    hd = 2 * dh
    tk = tile // 2
    reps = tk // LANES
    s_bufs, p_bufs, a_bufs = (s0_sc, s1_sc), (p0_sc, p1_sc), (a0_sc, a1_sc)

    lane = lax.broadcasted_iota(jnp.int32, (1, hd), 1)
    q = q_ref[...]
    zero = jnp.zeros_like(q)
    q2 = jnp.concatenate([jnp.where(lane < dh, q, zero), jnp.where(lane >= dh, q, zero)], axis=0)

    hf = jnp.full((1, 1), h + 1, jnp.int32).astype(F32)
    slope = jnp.exp2(hf * (-8.0 / DA_HEADS)) * LOG2E
    col = lax.broadcasted_iota(jnp.int32, (1, tk), 1)
    ones = jnp.ones((tk, hd), BF16)

    m_sc[...] = jnp.full(m_sc.shape, NEG_BIG, F32)
    acc_sc[...] = jnp.zeros(acc_sc.shape, F32)

    def max_half_norms(x):
        xf = x.astype(F32)
        sq = xf * xf
        out = []
        for keep in (lane < dh, lane >= dh):
            rows = jnp.sum(jnp.where(keep, sq, 0.0), axis=1, keepdims=True)
            out.append(jnp.sqrt(jnp.max(rows, axis=0, keepdims=True)))
        return out

    @pl.when(qi == 0)
    def _():
        kn_sc[...] = jnp.concatenate([jnp.broadcast_to(n, (4, LANES))
                                      for n in max_half_norms(k_ref[...])], axis=0)

    n_sub = 2 * qi + 2

    def key_start(j):
        return pl.multiple_of((n_sub - 1 - j) * tk, tk)

    every = slice(0, 2 * tile)

    def scores(j, slot, rows=every):
        ks = key_start(j)
        bias = (col + (ks - qi * tile)).astype(F32) * slope
        s_bufs[slot][rows, :] = _dot_nt(q2[rows, :], k_ref[pl.ds(ks, tk), :]) + bias

    def softmax(j, slot, masked, rows=every):
        s = s_bufs[slot][rows, :]
        if masked:
            n_rows = rows.stop - rows.start
            rowp = ((lax.broadcasted_iota(jnp.int32, (n_rows, tk), 0) + rows.start) & (tile - 1)) \
                + qi * tile
            colp = lax.broadcasted_iota(jnp.int32, (n_rows, tk), 1) + key_start(j)
            s = jnp.where(colp <= rowp, s, NEG_BIG)
        m_old = m_sc[rows, :]
        m_new = jnp.maximum(m_old, jnp.max(s, axis=1, keepdims=True))
        p_bufs[slot][rows, :] = jnp.exp2(s - jnp.concatenate([m_new] * reps, axis=1)).astype(BF16)
        a_bufs[slot][rows, :] = jnp.exp2(m_old - m_new)
        m_sc[rows, :] = m_new

    def values(j, slot, rows=every):
        v_aug = jnp.concatenate([v_ref[pl.ds(key_start(j), tk), :], ones], axis=1)
        alpha = a_bufs[slot][rows, :]
        acc_sc[rows, :] = (jnp.concatenate([alpha, alpha], axis=1) * acc_sc[rows, :]
                           + _dot(p_bufs[slot][rows, :], v_aug))

    late = (slice(tile // 2, tile), slice(tile + tile // 2, 2 * tile))

    def first(stage, *args):
        for rows in late:
            stage(0, 0, *args, rows)

    def softmax_frozen(slot, rows=every):
        m_rep = jnp.concatenate([m_sc[rows, :]] * reps, axis=1)
        p_bufs[slot][rows, :] = jnp.exp2(s_bufs[slot][rows, :] - m_rep).astype(BF16)

    def values_frozen(j, slot, rows=every):
        v_aug = jnp.concatenate([v_ref[pl.ds(key_start(j), tk), :], ones], axis=1)
        acc_sc[rows, :] += _dot(p_bufs[slot][rows, :], v_aug)

    first(scores)
    scores(1, 1)
    first(softmax, True)

    @pl.when(qi == 0)
    def _():
        softmax(1, 1, True)
        first(values)
        values(1, 1)

    @pl.when(qi > 0)
    def _():
        scores(2, 0)
        softmax(1, 1, True)
        first(values)
        scores(3, 1)
        softmax(2, 0, False)
        values(1, 1)
        qn = max_half_norms(q)
        qk_max = jnp.maximum(qn[0] * kn_sc[0:1, 0:1], qn[1] * kn_sc[4:5, 0:1])
        m_min = jnp.min(m_sc[...], axis=0, keepdims=True)[:, 0:1]
        reach = (qk_max - m_min + DA_DEAD_LOG2) / slope
        first_dead = jnp.floor((reach - 1.0) / tk) + 3.0
        first_dead = jnp.max(jnp.clip(first_dead, 0.0, 1e6)).astype(jnp.int32)
        pairs_end = jnp.maximum(2, jnp.minimum(qi + 1, (first_dead + 1) // 2))

        freeze = jnp.max(jnp.where(qk_max - m_min <= DA_FREEZE_LOG2, 1.0, 0.0)) > 0.5
        freeze = jnp.logical_and(freeze, pairs_end > 2)

        @pl.when(jnp.logical_not(freeze))
        def _():
            def body(i, carry):
                t = 2 * i
                scores(t, 0)
                softmax(t - 1, 1, False)
                values(t - 2, 0)
                scores(t + 1, 1)
                softmax(t, 0, False)
                values(t - 1, 1)
                return carry

            lax.fori_loop(2, pairs_end, body, 0)
            t = 2 * pairs_end
            softmax(t - 1, 1, False)
            values(t - 2, 0)
            values(t - 1, 1)

        @pl.when(freeze)
        def _():
            scores(4, 0)
            softmax_frozen(1)
            values(2, 0)
            scores(5, 1)
            softmax_frozen(0)
            values_frozen(3, 1)

            def body(i, carry):
                t = 2 * i
                scores(t, 0)
                softmax_frozen(1)
                values_frozen(t - 2, 0)
                scores(t + 1, 1)
                softmax_frozen(0)
                values_frozen(t - 1, 1)
                return carry

            lax.fori_loop(3, pairs_end, body, 0)
            t = 2 * pairs_end
            softmax_frozen(1)
            values_frozen(t - 2, 0)
            values_frozen(t - 1, 1)

    lv = lam_ref[...].astype(F32)
    dots = jnp.sum(lv[0:1, :] * lv[1:2, :], axis=1, keepdims=True)
    dots2 = jnp.sum(lv[2:3, :] * lv[3:4, :], axis=1, keepdims=True)
    lam = jnp.exp(dots) - jnp.exp(dots2) + lam_init
    acc = acc_sc[...]
    o0 = acc[0:tile, 0:hd] / acc[0:tile, hd:2 * hd]
    o1 = acc[tile:2 * tile, 0:hd] / acc[tile:2 * tile, hd:2 * hd]
    o = o0 - lam * o1
    ms = jnp.mean(o * o, axis=1, keepdims=True)
    o = o * lax.rsqrt(ms + 1e-6) * gain_ref[...] * (1.0 - lam_init)
    o_ref[...] = o.astype(o_ref.dtype)


def _diff_attention(proj, diff_lambda, diff_gain, lam_init):
    bsz, t, _ = proj.shape
    tile = min(TQ_ATT, t)
    hd = 2 * DA_HEAD_DIM
    kern = functools.partial(_diffattn_kernel, tile=tile, lam_init=lam_init)
    return pl.pallas_call(
        kern,
        grid=(bsz, DA_HEADS, t // tile),
        in_specs=[
            pl.BlockSpec((None, tile, hd), lambda b, h, i: (b, i, h)),
            pl.BlockSpec((None, t, hd), lambda b, h, i: (b, 0, DA_HEADS + h)),
            pl.BlockSpec((None, t, hd), lambda b, h, i: (b, 0, 2 * DA_HEADS + h)),
            pl.BlockSpec((4, DA_HEAD_DIM), lambda b, h, i: (0, 0)),
            pl.BlockSpec((1, hd), lambda b, h, i: (0, 0)),
        ],
        out_specs=pl.BlockSpec((None, tile, hd), lambda b, h, i: (b, i, h)),
        out_shape=jax.ShapeDtypeStruct((bsz, t, DA_HEADS * hd), BF16),
        scratch_shapes=[
            pltpu.VMEM((2 * tile, LANES), F32),
            pltpu.VMEM((2 * tile, 2 * hd), F32),
            pltpu.VMEM((2 * tile, tile // 2), F32),
            pltpu.VMEM((2 * tile, tile // 2), F32),
            pltpu.VMEM((2 * tile, tile // 2), BF16),
            pltpu.VMEM((2 * tile, tile // 2), BF16),
            pltpu.VMEM((2 * tile, LANES), F32),
            pltpu.VMEM((2 * tile, LANES), F32),
            pltpu.VMEM((8, LANES), F32),
        ],
        compiler_params=_params(("arbitrary", "arbitrary", "arbitrary")),
        name="diff_attn",
    )(proj, proj, proj, diff_lambda, diff_gain.reshape(1, hd))


def _hgrn_kernel(q_ref, f_ref, i_ref, g_ref, gamma_ref, gain_ref, o_ref, st_sc, *, layer):
    @pl.when(pl.program_id(1) == 0)
    def _():
        st_sc[...] = jnp.zeros(st_sc.shape, F32)

    gam = gamma_ref[...].astype(F32)
    e = jnp.exp(gam - jnp.max(gam, axis=0, keepdims=True))
    sm = e / jnp.sum(e, axis=0, keepdims=True)
    lb_all = jnp.sum(sm[0:layer + 1, :], axis=0, keepdims=True)

    c = HG_CHUNK
    row = lax.broadcasted_iota(jnp.int32, (c, c), 0)
    col = lax.broadcasted_iota(jnp.int32, (c, c), 1)
    tril = col <= row
    tril_bf = jnp.where(tril, 1.0, 0.0).astype(BF16)
    gain = gain_ref[...]

    heads = range(HG_HEADS)
    hcols = [slice(h * HG_DK, (h + 1) * HG_DK) for h in heads]
    lbs = [lb_all[:, hc] for hc in hcols]
    for n in range(q_ref.shape[0] // c):
        rows = slice(n * c, (n + 1) * c)
        sig = [_sigmoid(f_ref[rows, hc].astype(F32)) for hc in hcols]
        logf = [jnp.log(lbs[h] + (1.0 - lbs[h]) * sig[h]) for h in heads]
        kk = [(1.0 - lbs[h]) * (1.0 - sig[h]) for h in heads]
        parts = [_split3(x) for x in logf]
        b = [_dot(tril_bf, p[0]) + _dot(tril_bf, p[1]) for p in parts]
        b_mid = [x[c // 2 - 1:c // 2, :] for x in b]
        b_last = [x[c - 1:c, :] for x in b]
        qh = [q_ref[rows, hc].astype(F32) for hc in hcols]
        qs = [x * _sigmoid(x) for x in qh]
        v = [i_ref[rows, hc] for hc in hcols]
        qa = [qs[h] * jnp.exp(b[h] - b_mid[h]) for h in heads]
        ka = [kk[h] * jnp.exp(b_mid[h] - b[h]) for h in heads]
        att = [_dot_nt(qa[h].astype(BF16), ka[h].astype(BF16)) for h in heads]
        att = [jnp.where(tril, x, 0.0).astype(BF16) for x in att]
        o_intra = [_dot(att[h], v[h]) for h in heads]
        kd = [(ka[h] * jnp.exp(b_last[h] - b_mid[h])).astype(BF16) for h in heads]
        ds_t = [_dot(v[h].T, kd[h]) for h in heads]
        st = [st_sc[h] for h in heads]
        o_inter = [_dot_nt((qa[h] * jnp.exp(b_mid[h])).astype(BF16), st[h].astype(BF16))
                   for h in heads]
        for h in heads:
            st_sc[h] = st[h] * jnp.exp(b_last[h]) + ds_t[h]
        for h in heads:
            o = o_intra[h] + o_inter[h]
            gh = g_ref[rows, hcols[h]].astype(F32)
            ms = jnp.mean(o * o, axis=1, keepdims=True)
            o = o * lax.rsqrt(ms + 1e-6) * gain * (gh * _sigmoid(gh))
            o_ref[rows, hcols[h]] = o.astype(o_ref.dtype)


def _hgrn2(proj, hgrn_gamma, hgrn_gain, layer):
    bsz, t, _ = proj.shape
    tt = min(T_HG, t)
    width = HG_HEADS * HG_DK
    base = 3 * DA_HEADS * 2 * DA_HEAD_DIM // width
    spec = lambda k: pl.BlockSpec((None, tt, width), lambda b, i: (b, i, base + k))
    return pl.pallas_call(
        functools.partial(_hgrn_kernel, layer=layer),
        grid=(bsz, t // tt),
        in_specs=[
            spec(0), spec(1), spec(2), spec(3),
            pl.BlockSpec((hgrn_gamma.shape[0], width), lambda b, i: (0, 0)),
            pl.BlockSpec((1, HG_DK), lambda b, i: (0, 0)),
        ],
        out_specs=pl.BlockSpec((None, tt, width), lambda b, i: (b, i, 0)),
        out_shape=jax.ShapeDtypeStruct((bsz, t, width), BF16),
        scratch_shapes=[pltpu.VMEM((HG_HEADS, HG_DK, HG_DK), F32)],
        compiler_params=_params(("arbitrary", "arbitrary")),
        name="hgrn2",
    )(proj, proj, proj, proj, hgrn_gamma, hgrn_gain.reshape(1, HG_DK))


def _route(logits_t):
    mx = jnp.max(logits_t, axis=0, keepdims=True)
    ex = jnp.exp(logits_t - mx)
    probs = ex / jnp.sum(ex, axis=0, keepdims=True)
    p = [probs[e:e + 1, :] for e in range(N_EXPERTS)]
    g = E_PER_GROUP
    scores = []
    for gi in range(N_GROUPS):
        pg = p[gi * g:(gi + 1) * g]
        best = None
        for a in range(g):
            for b in range(a + 1, g):
                pair = pg[a] + pg[b]
                best = pair if best is None else jnp.maximum(best, pair)
        scores.append(best)
    group_id = jnp.zeros_like(p[0])
    gates = [jnp.zeros_like(p[0]) for _ in range(g)]
    for gi in range(N_GROUPS):
        sel = None
        for gj in range(N_GROUPS):
            if gj == gi:
                continue
            cond = (scores[gi] > scores[gj]) if gj < gi else (scores[gi] >= scores[gj])
            sel = cond if sel is None else (sel & cond)
        group_id = jnp.where(sel, float(gi), group_id)
        pg = p[gi * g:(gi + 1) * g]
        chosen = []
        for a in range(g):
            rank = jnp.zeros_like(pg[a])
            for b in range(g):
                if b == a:
                    continue
                ahead = (pg[b] >= pg[a]) if b < a else (pg[b] > pg[a])
                rank = rank + jnp.where(ahead, 1.0, 0.0)
            chosen.append(sel & (rank < 2.0))
        denom = None
        for a in range(g):
            term = jnp.where(chosen[a], pg[a], 0.0)
            denom = term if denom is None else denom + term
        for a in range(g):
            gates[a] = jnp.where(chosen[a], pg[a] / denom, gates[a])
    return group_id, gates


def _outproj_kernel(a_ref, b_ref, x_ref, mod_ref, w_ref, lng_ref, lnb_ref, rwt_ref, rb_ref,
                    xo_ref, h_ref, row_ref, col_ref, *, alpha):
    half = a_ref.shape[1]
    tm = a_ref.shape[0]
    g1 = mod_ref[2:3, :]
    sh2 = mod_ref[3:4, :]
    sc2 = mod_ref[4:5, :]
    rw = rwt_ref[...]
    w_hi, w_mid, _ = _split3(rw)
    chunks = [slice(i * tm // OUTPROJ_CHUNKS, (i + 1) * tm // OUTPROJ_CHUNKS)
              for i in range(OUTPROJ_CHUNKS)]
    y = [_dot(a_ref[rs, :], w_ref[0:half, :]) + _dot(b_ref[rs, :], w_ref[half:2 * half, :])
         for rs in chunks]
    r = [alpha * x_ref[rs, :] + (1.0 + g1) * yy for rs, yy in zip(chunks, y)]
    rc = [rr - jnp.mean(rr, axis=1, keepdims=True) for rr in r]
    var = [jnp.mean(cc * cc, axis=1, keepdims=True) for cc in rc]
    xn = [cc * lax.rsqrt(vv + 1e-5) * lng_ref[...] + lnb_ref[...] for cc, vv in zip(rc, var)]
    h2 = [xx * (1.0 + sc2) + sh2 for xx in xn]
    for rs, xx, hh in zip(chunks, xn, h2):
        xo_ref[rs, :] = xx
        h_ref[rs, :] = hh.astype(BF16)
    split = [_split3(hh) for hh in h2]
    logits_t = jnp.concatenate(
        [_dot_nt(w_hi, s[0]) + _dot_nt(w_hi, s[1]) + _dot_nt(w_mid, s[0]) for s in split],
        axis=1) + rb_ref[...]
    group_id, gates = _route(logits_t)
    sel = [jnp.where(group_id == float(gi), 1.0, 0.0) for gi in range(N_GROUPS)]
    onehot = jnp.concatenate(sel + [jnp.zeros((8 - N_GROUPS, tm), F32)], axis=0).astype(BF16)
    src = lax.broadcasted_iota(jnp.int32, (tm, tm), 0)
    dst = lax.broadcasted_iota(jnp.int32, (tm, tm), 1)
    earlier = jnp.where(src < dst, 1.0, 0.0).astype(BF16)
    counts = _dot(onehot, earlier)
    rank = sel[0] * counts[0:1, :]
    for gi in range(1, N_GROUPS):
        rank = rank + sel[gi] * counts[gi:gi + 1, :]
    info = jnp.concatenate(gates + [group_id, rank], axis=0)
    row_ref[...] = jnp.concatenate(
        [group_id, rank, jnp.zeros((8 - 2, tm), F32)], axis=0)
    pad = jnp.zeros((LANES - info.shape[0], tm), F32)
    col_ref[...] = jnp.concatenate([info, pad], axis=0).T


def _outproj(a, b, x, mod_l, w_bf16, ln_g, ln_b, router_w, router_b, alpha):
    bsz, t, d = x.shape
    half = a.shape[2]
    tm = min(T_BLK, t)
    tok = lambda width: pl.BlockSpec((None, tm, width), lambda bi, i: (bi, i, 0))
    full = lambda r, c: pl.BlockSpec((r, c), lambda bi, i: (0, 0))
    return pl.pallas_call(
        functools.partial(_outproj_kernel, alpha=alpha),
        grid=(bsz, t // tm),
        in_specs=[
            tok(half), tok(half), tok(d),
            pl.BlockSpec((None, 6, d), lambda bi, i: (bi, 0, 0)),
            full(2 * half, d), full(1, d), full(1, d), full(N_EXPERTS, d), full(N_EXPERTS, 1),
        ],
        out_specs=[tok(d), tok(d), pl.BlockSpec((None, 8, tm), lambda bi, i: (bi, 0, i)),
                   tok(LANES)],
        out_shape=[
            jax.ShapeDtypeStruct((bsz, t, d), F32),
            jax.ShapeDtypeStruct((bsz, t, d), BF16),
            jax.ShapeDtypeStruct((bsz, 8, t), F32),
            jax.ShapeDtypeStruct((bsz, t, LANES), F32),
        ],
        compiler_params=_params(("arbitrary", "arbitrary")),
        name="outproj_ln_route",
    )(a, b, x, mod_l, w_bf16, ln_g.reshape(1, d), ln_b.reshape(1, d), router_w.T,
      router_b.reshape(N_EXPERTS, 1))


def _slab_rows(tm):
    extra = -(-(tm - MOE_ROWS_MAIN) // MOE_ROWS_EXTRA)
    return MOE_ROWS_MAIN + max(extra, 0) * MOE_ROWS_EXTRA


def _extra_chunks(count):
    return (jnp.maximum(count - MOE_ROWS_MAIN, 0) + MOE_ROWS_EXTRA - 1) // MOE_ROWS_EXTRA


def _moe_expert_kernel(cnt_ref, h_ref, row_ref, col_ref, wg_ref, wu_ref, wd_ref, z_ref):
    g = pl.program_id(0)
    blk = pl.program_id(1)
    count = cnt_ref[g * pl.num_programs(1) + blk]
    mine = row_ref[0:1, :] == g.astype(F32)
    rank = row_ref[1:2, :]
    info = col_ref[...]
    info_hi = info.astype(BF16)
    info_lo = (info - info_hi.astype(F32)).astype(BF16)

    def run_rows(r0, m):
        rid = (lax.broadcasted_iota(jnp.int32, (m, 1), 0) + r0).astype(F32)
        pick = _onehot((rank == rid) & mine)
        xs = _dot(pick, h_ref[...]).astype(BF16)
        gm = _dot(pick, info_hi) + _dot(pick, info_lo)
        acc = None
        for j in range(E_PER_GROUP):
            a = _dot(xs, wg_ref[j])
            u = _dot(xs, wu_ref[j])
            he = (a * _sigmoid(a) * u * gm[:, j:j + 1]).astype(BF16)
            part = _dot(he, wd_ref[j])
            acc = part if acc is None else acc + part
        z_ref[pl.ds(r0, m), :] = acc.astype(z_ref.dtype)

    @pl.when(count <= MOE_ROWS_SMALL)
    def _():
        run_rows(0, MOE_ROWS_SMALL)
        z_ref[MOE_ROWS_SMALL:MOE_ROWS_MAIN, :] = jnp.zeros(
            (MOE_ROWS_MAIN - MOE_ROWS_SMALL, z_ref.shape[1]), z_ref.dtype)

    @pl.when(count > MOE_ROWS_SMALL)
    def _():
        run_rows(0, MOE_ROWS_MAIN)

    rest = z_ref.shape[0] - MOE_ROWS_MAIN
    if rest:
        z_ref[MOE_ROWS_MAIN:, :] = jnp.zeros((rest, z_ref.shape[1]), z_ref.dtype)

        def body(i, carry):
            run_rows(pl.multiple_of(MOE_ROWS_MAIN + i * MOE_ROWS_EXTRA, 16), MOE_ROWS_EXTRA)
            return carry

        lax.fori_loop(0, _extra_chunks(count), body, 0)


def _moe_combine_kernel(cnt_ref, z0_ref, z1_ref, z2_ref, z3_ref, col_ref, x_ref, mod_ref, lng_ref,
                        lnb_ref, o_ref, y_sc, *, alpha):
    blk = pl.program_id(0) * pl.num_programs(1) + pl.program_id(1)
    n_blk = pl.num_programs(0) * pl.num_programs(1)
    z_refs = (z0_ref, z1_ref, z2_ref, z3_ref)
    main = MOE_ROWS_MAIN
    grp = col_ref[:, E_PER_GROUP:E_PER_GROUP + 1]
    rank = col_ref[:, E_PER_GROUP + 1:E_PER_GROUP + 2]
    y_sc[...] = jnp.zeros(y_sc.shape, F32)
    lane_x = lax.broadcasted_iota(jnp.int32, (1, MOE_ROWS_EXTRA), 1).astype(F32)
    for gi in range(N_GROUPS):
        def body(i, carry, gi=gi):
            r0 = pl.multiple_of(main + i * MOE_ROWS_EXTRA, 16)
            hit = (grp == float(gi)) & ((rank - r0.astype(F32)) == lane_x)
            y_sc[...] += _dot(_onehot(hit), z_refs[gi][pl.ds(r0, MOE_ROWS_EXTRA), :])
            return carry

        lax.fori_loop(0, _extra_chunks(cnt_ref[gi * n_blk + blk]), body, 0)

    where_to = jnp.where(rank < float(main), grp * float(main) + rank, -1.0)
    lane = lax.broadcasted_iota(jnp.int32, (1, N_GROUPS * main), 1).astype(F32)
    z_all = jnp.concatenate([zr[0:main, :] for zr in z_refs], axis=0)
    g2 = mod_ref[5:6, :]
    tm = x_ref.shape[0]
    chunks = [slice(i * tm // OUTPROJ_CHUNKS, (i + 1) * tm // OUTPROJ_CHUNKS)
              for i in range(OUTPROJ_CHUNKS)]
    y = [y_sc[rs, :] + _dot(_onehot(where_to[rs, :] == lane), z_all) for rs in chunks]
    r = [alpha * x_ref[rs, :] + (1.0 + g2) * yy for rs, yy in zip(chunks, y)]
    rc = [rr - jnp.mean(rr, axis=1, keepdims=True) for rr in r]
    var = [jnp.mean(cc * cc, axis=1, keepdims=True) for cc in rc]
    for rs, cc, vv in zip(chunks, rc, var):
        o_ref[rs, :] = cc * lax.rsqrt(vv + 1e-5) * lng_ref[...] + lnb_ref[...]


def _moe(h2, rowinfo, colinfo, x, mod_l, layer, wg, wu, wd, ln_g, ln_b, alpha):
    bsz, t, d = x.shape
    tm = min(T_BLK, t)
    nb = t // tm
    n_blk = bsz * nb
    dff = wg.shape[3]
    slab = _slab_rows(tm)
    group_of = rowinfo[:, 0, :].reshape(1, n_blk, tm)
    counts = jnp.sum(group_of == jnp.arange(N_GROUPS, dtype=F32).reshape(N_GROUPS, 1, 1), axis=2)
    counts = counts.astype(jnp.int32).reshape(N_GROUPS * n_blk)

    z = pl.pallas_call(
        _moe_expert_kernel,
        grid_spec=pltpu.PrefetchScalarGridSpec(
            num_scalar_prefetch=1,
            grid=(N_GROUPS, n_blk),
            in_specs=[
                pl.BlockSpec((None, tm, d), lambda g, i, c: (i // nb, i % nb, 0)),
                pl.BlockSpec((None, 8, tm), lambda g, i, c: (i // nb, 0, i % nb)),
                pl.BlockSpec((None, tm, LANES), lambda g, i, c: (i // nb, i % nb, 0)),
                pl.BlockSpec((None, E_PER_GROUP, d, dff), lambda g, i, c: (layer, g, 0, 0)),
                pl.BlockSpec((None, E_PER_GROUP, d, dff), lambda g, i, c: (layer, g, 0, 0)),
                pl.BlockSpec((None, E_PER_GROUP, dff, d), lambda g, i, c: (layer, g, 0, 0)),
            ],
            out_specs=pl.BlockSpec((None, None, slab, d), lambda g, i, c: (g, i, 0, 0)),
        ),
        out_shape=jax.ShapeDtypeStruct((N_GROUPS, n_blk, slab, d), BF16),
        compiler_params=_params(("arbitrary", "arbitrary")),
        name="moe_experts",
    )(counts, h2, rowinfo, colinfo, wg, wu, wd)

    zspec = lambda gi: pl.BlockSpec((None, None, slab, d), lambda b, i, c: (gi, b * nb + i, 0, 0))
    tok = lambda width: pl.BlockSpec((None, tm, width), lambda b, i, c: (b, i, 0))
    return pl.pallas_call(
        functools.partial(_moe_combine_kernel, alpha=alpha),
        grid_spec=pltpu.PrefetchScalarGridSpec(
            num_scalar_prefetch=1,
            grid=(bsz, nb),
            in_specs=[
                zspec(0), zspec(1), zspec(2), zspec(3), tok(LANES), tok(d),
                pl.BlockSpec((None, 6, d), lambda b, i, c: (b, 0, 0)),
                pl.BlockSpec((1, d), lambda b, i, c: (0, 0)),
                pl.BlockSpec((1, d), lambda b, i, c: (0, 0)),
            ],
            out_specs=tok(d),
            scratch_shapes=[pltpu.VMEM((tm, d), F32)],
        ),
        out_shape=jax.ShapeDtypeStruct((bsz, t, d), F32),
        compiler_params=_params(("arbitrary", "arbitrary")),
        name="moe_combine_ln",
    )(counts, z, z, z, z, colinfo, x, mod_l, ln_g.reshape(1, d), ln_b.reshape(1, d))


def _lru_kernel(x_ref, g_ref, cw_ref, cb_ref, wa_ref, ba_ref, wx_ref, bx_ref, lam_ref,
                o_ref, xpad_sc, h_sc):
    tt = x_ref.shape[0]
    pad = 8

    @pl.when(pl.program_id(1) == 0)
    def _():
        xpad_sc[0:pad, :] = jnp.zeros((pad, xpad_sc.shape[1]), F32)
        h_sc[...] = jnp.zeros(h_sc.shape, F32)

    xpad_sc[pad:pad + tt, :] = x_ref[...].astype(F32)
    xc = cb_ref[...] + jnp.zeros((tt, x_ref.shape[1]), F32)
    for j in range(CONV_WIDTH):
        off = pad - (CONV_WIDTH - 1) + j
        xc = xc + cw_ref[j:j + 1, :] * xpad_sc[off:off + tt, :]
    xpad_sc[0:pad, :] = xpad_sc[tt:tt + pad, :]

    xb = xc.astype(BF16)
    r = _sigmoid(_dot(xb, wa_ref[...]) + ba_ref[...])
    i = _sigmoid(_dot(xb, wx_ref[...]) + bx_ref[...])
    lam = lam_ref[...].astype(F32)
    softplus_neg = jnp.maximum(-lam, 0.0) + jnp.log(1.0 + jnp.exp(-jnp.abs(lam)))
    log_a = -LRU_C * r * softplus_neg
    a = jnp.exp(log_a)
    gain_sq = jnp.maximum(1.0 - jnp.exp(2.0 * log_a), 1e-12)
    u = gain_sq * lax.rsqrt(gain_sq) * (i * xc)

    groups = (tt // SUBLANES, SUBLANES, a.shape[1])
    a = a.reshape(groups)
    u = u.reshape(groups)
    rowi = lax.broadcasted_iota(jnp.int32, (1, SUBLANES, 1), 1)
    d = 1
    while d < SUBLANES:
        a_sh = jnp.where(rowi >= d, pltpu.roll(a, d, 1), 1.0)
        u_sh = jnp.where(rowi >= d, pltpu.roll(u, d, 1), 0.0)
        u = u + a * u_sh
        a = a * a_sh
        d *= 2
    a = a.reshape(tt, groups[2])
    u = u.reshape(tt, groups[2])
    gr = g_ref[...].astype(F32)
    gelu = 0.5 * gr * (1.0 + jnp.tanh(0.7978845608028654 * (gr + 0.044715 * gr * gr * gr)))
    h_prev = h_sc[...]
    out = []
    for grp in range(tt // SUBLANES):
        rows = slice(grp * SUBLANES, (grp + 1) * SUBLANES)
        h_grp = u[rows, :] + a[rows, :] * h_prev
        out.append(gelu[rows, :] * h_grp)
        h_prev = h_grp[SUBLANES - 1:SUBLANES, :]
    h_sc[...] = h_prev
    o_ref[...] = jnp.concatenate(out, axis=0).astype(o_ref.dtype)


def _rg_lru(proj, conv_w, conv_b, wa_dense, ba, wx_dense, bx, lam):
    bsz, t, _ = proj.shape
    tt = min(T_LRU, t)
    w = LRU_WIDTH
    nblk = w // LANES
    row = lambda a: a.reshape(1, w)
    full = lambda r, c: pl.BlockSpec((r, c), lambda b, i: (0, 0))
    return pl.pallas_call(
        _lru_kernel,
        grid=(bsz, t // tt),
        in_specs=[
            pl.BlockSpec((None, tt, w), lambda b, i: (b, i, 0)),
            pl.BlockSpec((None, tt, w), lambda b, i: (b, i, 1)),
            full(CONV_WIDTH, w), full(1, w), full(w, w), full(1, w), full(w, w), full(1, w),
            full(1, w),
        ],
        out_specs=pl.BlockSpec((None, tt, w), lambda b, i: (b, i, 0)),
        out_shape=jax.ShapeDtypeStruct((bsz, t, w), BF16),
        scratch_shapes=[pltpu.VMEM((tt + SUBLANES, w), F32), pltpu.VMEM((1, w), F32)],
        compiler_params=_params(("arbitrary", "arbitrary")),
        name="rg_lru",
    )(proj, proj, conv_w, row(conv_b), wa_dense, row(ba), wx_dense, row(bx), row(lam))


def _sb_kernel(q_ref, k_ref, v_ref, o_ref, r_sc, acc_sc, *bufs, tile):
    qi = pl.program_id(2)
    d = SB_HEAD_DIM
    tk = tile // 2
    z_bufs, lb_bufs, l_bufs, w_bufs = (bufs[i * SB_SETS:(i + 1) * SB_SETS] for i in range(4))
    lane = lax.broadcasted_iota(jnp.int32, (1, 2 * d), 1)
    q = q_ref[...]
    zero = jnp.zeros_like(q)
    q2 = jnp.concatenate([jnp.where(lane < d, q, zero), jnp.where(lane >= d, q, zero)], axis=0)
    rj = lax.broadcasted_iota(jnp.int32, (tk, tk), 0)
    cs = lax.broadcasted_iota(jnp.int32, (tk, tk), 1)
    upper = jnp.where(rj > cs, 1.0, 0.0).astype(BF16)

    r_sc[...] = jnp.zeros(r_sc.shape, F32)
    acc_sc[...] = jnp.zeros(acc_sc.shape, F32)
    n_sub = 2 * qi + 2
    every = slice(0, 2 * tile)
    per_head = (slice(0, tile), slice(tile, 2 * tile))

    def key_start(j):
        return pl.multiple_of((n_sub - 1 - j) * tk, tk)

    def strict_mask(j, rows):
        n_rows = rows.stop - rows.start
        rowp = (lax.broadcasted_iota(jnp.int32, (n_rows, tk), 0) + rows.start) & (tile - 1)
        colp = lax.broadcasted_iota(jnp.int32, (n_rows, tk), 1) + (1 - j) * tk
        return colp < rowp

    def logits(j, b, rows=every):
        z_bufs[b][rows, :] = _dot_nt(q2[rows, :], k_ref[pl.ds(key_start(j), tk), :])

    def gates(j, b, rows=every, masked=False):
        z = z_bufs[b][rows, :]
        log_1m = jnp.log(1.0 + jnp.exp2(-jnp.abs(z))) * (-LOG2E) - jnp.maximum(z, 0.0)
        lb_bufs[b][rows, :] = z + log_1m
        if masked:
            log_1m = jnp.where(strict_mask(j, rows), log_1m, 0.0)
        l_bufs[b][rows, :] = log_1m.astype(BF16)

    def weights(j, b, rows=every, masked=False):
        log_1m = l_bufs[b][rows, :]
        after = _dot(log_1m, upper) + r_sc[rows, :]
        w = jnp.exp2(lb_bufs[b][rows, :] + after)
        if masked:
            w = jnp.where(strict_mask(j, rows), w, 0.0)
        w_bufs[b][rows, :] = w.astype(BF16)
        r_sc[rows, :] = after[:, 0:1] + log_1m[:, 0:1].astype(F32)

    def values(j, b, rows=every):
        acc_sc[rows, :] += _dot(w_bufs[b][rows, :], v_ref[pl.ds(key_start(j), tk), :])

    late = tuple(slice(r.start + tile // 2, r.stop) for r in per_head)
    early = tuple(slice(r.start, r.start + tile // 2) for r in per_head)

    def first(stage, **kw):
        for rows in late:
            stage(0, 0, rows, **kw)

    @pl.when(qi == 0)
    def _():
        first(logits)
        logits(1, 1)
        first(gates, masked=True)
        gates(1, 1, masked=True)
        first(weights, masked=True)
        weights(1, 1, masked=True)
        first(values)
        values(1, 1)

    @pl.when(qi > 0)
    def _():
        def alive(rows=every):
            return (jnp.max(r_sc[rows, :]) > -SB_DEAD_LOG2).astype(jnp.int32)

        def third(stage, rows_set):
            for rows in rows_set:
                stage(2, 2, rows)

        first(logits)
        logits(1, 1)
        first(gates, masked=True)
        third(logits, early)
        gates(1, 1, masked=True)
        first(weights, masked=True)
        third(gates, early)
        weights(1, 1, masked=True)
        first(values)
        third(weights, early)
        values(1, 1)
        third(values, early)

        @pl.when(jnp.maximum(alive(late[0]), alive(late[1])) > 0)
        def _():
            for stage in (logits, gates, weights, values):
                third(stage, late)

        def cond(carry):
            j, live = carry
            return (j < n_sub) & (live > 0)

        def body(carry):
            j, _ = carry
            for rows in per_head:
                logits(j, 0, rows)
            for rows in per_head:
                gates(j, 0, rows)
            for rows in per_head:
                weights(j, 0, rows)
            for rows in per_head:
                values(j, 0, rows)
            return j + 1, alive()

        lax.while_loop(cond, body, (jnp.int32(3), alive()))

    acc = acc_sc[...]
    o_ref[...] = jnp.where(lane < d, acc[0:tile, :], acc[tile:2 * tile, :]).astype(o_ref.dtype)


def _sb_attention(proj):
    bsz, t, _ = proj.shape
    tq = min(T_SB, t)
    pairs = SB_HEADS // 2
    wblk = 2 * SB_HEAD_DIM
    base = 2 * LRU_WIDTH // wblk
    return pl.pallas_call(
        functools.partial(_sb_kernel, tile=tq),
        grid=(bsz, pairs, t // tq),
        in_specs=[
            pl.BlockSpec((None, tq, wblk), lambda b, h, i: (b, i, base + h)),
            pl.BlockSpec((None, t, wblk), lambda b, h, i: (b, 0, base + pairs + h)),
            pl.BlockSpec((None, t, wblk), lambda b, h, i: (b, 0, base + 2 * pairs + h)),
        ],
        out_specs=pl.BlockSpec((None, tq, wblk), lambda b, h, i: (b, i, h)),
        out_shape=jax.ShapeDtypeStruct((bsz, t, SB_HEADS * SB_HEAD_DIM), BF16),
        scratch_shapes=[pltpu.VMEM((2 * tq, 1), F32), pltpu.VMEM((2 * tq, wblk), F32)]
        + [pltpu.VMEM((2 * tq, tq // 2), F32)] * (2 * SB_SETS)
        + [pltpu.VMEM((2 * tq, tq // 2), BF16)] * (2 * SB_SETS),
        compiler_params=_params(("arbitrary", "arbitrary", "arbitrary")),
        name="sb_attn",
    )(proj, proj, proj)


def _block_diag(w):
    g, n, _ = w.shape
    eye = jnp.eye(g, dtype=w.dtype)
    return (eye[:, None, :, None] * w[:, :, None, :]).reshape(g * n, g * n)


def kernel(x, c, ada_w, ada_b, ln_g, ln_b, even_w_in, even_w_out, diff_lambda, diff_gain, hgrn_gamma, hgrn_gain, odd_w_in, odd_w_out, conv_w, conv_b, lru_wa, lru_ba, lru_wx, lru_bx, lru_lambda, router_w, router_b, moe_w_gate, moe_w_up, moe_w_down):
    depth = ada_w.shape[0]
    bsz, t, d = x.shape
    alpha = (2.0 * depth) ** 0.25
    mod = _ada_mod(c, ada_w, ada_b).reshape(depth, bsz, 6, d)
    w_gate, w_up, w_down = (w.astype(BF16) for w in (moe_w_gate, moe_w_up, moe_w_down))
    for l in range(depth):
        j = l // 2
        mod_l = mod[l]
        if l % 2 == 0:
            lam_init = 0.8 - 0.6 * math.exp(-0.3 * l)
            proj = _inproj(x, mod_l, even_w_in[j].astype(BF16), q_chunk=0)
            mix_a = _diff_attention(proj, diff_lambda[j], diff_gain[j], lam_init)
            mix_b = _hgrn2(proj, hgrn_gamma, hgrn_gain[j], l)
            w_out = even_w_out[j]
        else:
            proj = _inproj(x, mod_l, odd_w_in[j].astype(BF16), q_chunk=2 * LRU_WIDTH // PROJ_CHUNK)
            mix_a = _rg_lru(proj, conv_w[j], conv_b[j], _block_diag(lru_wa[j]).astype(BF16),
                            lru_ba[j], _block_diag(lru_wx[j]).astype(BF16), lru_bx[j],
                            lru_lambda[j])
            mix_b = _sb_attention(proj)
            w_out = odd_w_out[j]
        x, h2, rowinfo, colinfo = _outproj(mix_a, mix_b, x, mod_l, w_out.astype(BF16),
                                           ln_g[l, 0], ln_b[l, 0], router_w, router_b, alpha)
        x = _moe(h2, rowinfo, colinfo, x, mod_l, l, w_gate, w_up, w_down, ln_g[l, 1], ln_b[l, 1],
                 alpha)
    return x
```

```python
import functools
import math

import jax
import jax.numpy as jnp
from jax import lax
from jax.experimental import pallas as pl
from jax.experimental.pallas import tpu as pltpu

F32 = jnp.float32
BF16 = jnp.bfloat16

DA_HEADS = 4
DA_HEAD_DIM = 64
HG_HEADS = 4
HG_DK = 128
HG_CHUNK = 64
LRU_WIDTH = 512
LRU_BLOCKS = 8
CONV_WIDTH = 4
LRU_C = 8.0
SB_HEADS = 8
SB_HEAD_DIM = 64
N_EXPERTS = 16
N_GROUPS = 4
E_PER_GROUP = N_EXPERTS // N_GROUPS
D_FF = 512

LANES = 128
SUBLANES = 8
NEG_BIG = -1e30
LOG2E = 1.4426950408889634
Q_PRESCALE = DA_HEAD_DIM ** -0.5 * LOG2E
PROJ_CHUNK = 512
VMEM_LIMIT = 56 * 1024 * 1024

TM_PROJ = 1024
TQ_ATT = 512
T_SB = 512
SB_SETS = 3
SB_DEAD_LOG2 = 160.0
DA_DEAD_LOG2 = 152.0
DA_FREEZE_LOG2 = 64.0
T_HG = 512
T_LRU = 256
T_BLK = 1024
OUTPROJ_CHUNKS = 4
MOE_ROWS_MAIN = 320
MOE_ROWS_SMALL = 256
MOE_ROWS_EXTRA = 128


def _params(sem):
    return pltpu.CompilerParams(dimension_semantics=sem, vmem_limit_bytes=VMEM_LIMIT)


def _sigmoid(x):
    return 0.5 * jnp.tanh(0.5 * x) + 0.5


def _dot(a, b):
    return jnp.dot(a, b, preferred_element_type=F32)


def _dot_nt(a, b):
    return lax.dot_general(a, b, (((1,), (1,)), ((), ())), preferred_element_type=F32)


def _onehot(mask):
    return jnp.where(mask, 1.0, 0.0).astype(BF16)


def _split3(x):
    hi = x.astype(BF16)
    r1 = x - hi.astype(F32)
    mid = r1.astype(BF16)
    lo = (r1 - mid.astype(F32)).astype(BF16)
    return hi, mid, lo


def _ada_kernel(c_ref, w_ref, b_ref, o_ref):
    c = c_ref[...]
    cond = c * _sigmoid(c)
    hi, mid, _ = _split3(cond)
    w = w_ref[...].astype(BF16)
    o_ref[...] = _dot(hi, w) + _dot(mid, w) + b_ref[...]


def _ada_mod(c, ada_w, ada_b):
    depth, d, d6 = ada_w.shape
    bsz = c.shape[0]
    n_col = d6 // d
    return pl.pallas_call(
        _ada_kernel,
        grid=(depth, n_col),
        in_specs=[
            pl.BlockSpec((bsz, d), lambda l, j: (0, 0)),
            pl.BlockSpec((None, d, d), lambda l, j: (l, 0, j)),
            pl.BlockSpec((None, 1, d), lambda l, j: (l, 0, j)),
        ],
        out_specs=pl.BlockSpec((None, bsz, d), lambda l, j: (l, 0, j)),
        out_shape=jax.ShapeDtypeStruct((depth, bsz, d6), F32),
        compiler_params=_params(("arbitrary", "arbitrary")),
        name="ada_mod",
    )(c, ada_w, ada_b.reshape(depth, 1, d6))


def _inproj_kernel(x_ref, mod_ref, w_ref, o_ref, *, col_chunk, q_chunk):
    sh = mod_ref[0:1, :]
    sc = mod_ref[1:2, :]
    h = (x_ref[...] * (1.0 + sc) + sh).astype(BF16)
    for j in range(o_ref.shape[1] // col_chunk):
        cols = slice(j * col_chunk, (j + 1) * col_chunk)
        y = _dot(h, w_ref[:, cols])
        if j == q_chunk:
            y = y * Q_PRESCALE
        o_ref[:, cols] = y.astype(o_ref.dtype)


def _inproj(x, mod_l, w_bf16, q_chunk):
    bsz, t, d = x.shape
    width = w_bf16.shape[1]
    tm = min(TM_PROJ, t)
    return pl.pallas_call(
        functools.partial(_inproj_kernel, col_chunk=PROJ_CHUNK, q_chunk=q_chunk),
        grid=(bsz, t // tm),
        in_specs=[
            pl.BlockSpec((None, tm, d), lambda b, i: (b, i, 0)),
            pl.BlockSpec((None, 6, d), lambda b, i: (b, 0, 0)),
            pl.BlockSpec((d, width), lambda b, i: (0, 0)),
        ],
        out_specs=pl.BlockSpec((None, tm, width), lambda b, i: (b, i, 0)),
        out_shape=jax.ShapeDtypeStruct((bsz, t, width), BF16),
        compiler_params=_params(("arbitrary", "arbitrary")),
        name="inproj",
    )(x, mod_l, w_bf16)


def _diffattn_kernel(q_ref, k_ref, v_ref, lam_ref, gain_ref, o_ref, m_sc, acc_sc, s0_sc, s1_sc,
                     p0_sc, p1_sc, a0_sc, a1_sc, kn_sc, *, tile, lam_init):
    h = pl.program_id(1)
    qi = pl.program_id(2)
    dh = DA_HEAD_DIM
    hd = 2 * dh
    tk = tile // 2
    reps = tk // LANES
    s_bufs, p_bufs, a_bufs = (s0_sc, s1_sc), (p0_sc, p1_sc), (a0_sc, a1_sc)

    lane = lax.broadcasted_iota(jnp.int32, (1, hd), 1)
    q = q_ref[...]
    zero = jnp.zeros_like(q)
    q2 = jnp.concatenate([jnp.where(lane < dh, q, zero), jnp.where(lane >= dh, q, zero)], axis=0)

    hf = jnp.full((1, 1), h + 1, jnp.int32).astype(F32)
    slope = jnp.exp2(hf * (-8.0 / DA_HEADS)) * LOG2E
    col = lax.broadcasted_iota(jnp.int32, (1, tk), 1)
    ones = jnp.ones((tk, hd), BF16)

    m_sc[...] = jnp.full(m_sc.shape, NEG_BIG, F32)
    acc_sc[...] = jnp.zeros(acc_sc.shape, F32)

    def max_half_norms(x):
        xf = x.astype(F32)
        sq = xf * xf
        out = []
        for keep in (lane < dh, lane >= dh):
            rows = jnp.sum(jnp.where(keep, sq, 0.0), axis=1, keepdims=True)
            out.append(jnp.sqrt(jnp.max(rows, axis=0, keepdims=True)))
        return out

    @pl.when(qi == 0)
    def _():
        kn_sc[...] = jnp.concatenate([jnp.broadcast_to(n, (4, LANES))
                                      for n in max_half_norms(k_ref[...])], axis=0)

    n_sub = 2 * qi + 2

    def key_start(j):
        return pl.multiple_of((n_sub - 1 - j) * tk, tk)

    every = slice(0, 2 * tile)

    def scores(j, slot, rows=every):
        ks = key_start(j)
        bias = (col + (ks - qi * tile)).astype(F32) * slope
        s_bufs[slot][rows, :] = _dot_nt(q2[rows, :], k_ref[pl.ds(ks, tk), :]) + bias

    def softmax(j, slot, masked, rows=every):
        s = s_bufs[slot][rows, :]
        if masked:
            n_rows = rows.stop - rows.start
            rowp = ((lax.broadcasted_iota(jnp.int32, (n_rows, tk), 0) + rows.start) & (tile - 1)) \
                + qi * tile
            colp = lax.broadcasted_iota(jnp.int32, (n_rows, tk), 1) + key_start(j)
            s = jnp.where(colp <= rowp, s, NEG_BIG)
        m_old = m_sc[rows, :]
        m_new = jnp.maximum(m_old, jnp.max(s, axis=1, keepdims=True))
        p_bufs[slot][rows, :] = jnp.exp2(s - jnp.concatenate([m_new] * reps, axis=1)).astype(BF16)
        a_bufs[slot][rows, :] = jnp.exp2(m_old - m_new)
        m_sc[rows, :] = m_new

    def values(j, slot, rows=every):
        v_aug = jnp.concatenate([v_ref[pl.ds(key_start(j), tk), :], ones], axis=1)
        alpha = a_bufs[slot][rows, :]
        acc_sc[rows, :] = (jnp.concatenate([alpha, alpha], axis=1) * acc_sc[rows, :]
                           + _dot(p_bufs[slot][rows, :], v_aug))

    late = (slice(tile // 2, tile), slice(tile + tile // 2, 2 * tile))

    def first(stage, *args):
        for rows in late:
            stage(0, 0, *args, rows)

    def softmax_frozen(slot, rows=every):
        m_rep = jnp.concatenate([m_sc[rows, :]] * reps, axis=1)
        p_bufs[slot][rows, :] = jnp.exp2(s_bufs[slot][rows, :] - m_rep).astype(BF16)

    def values_frozen(j, slot, rows=every):
        v_aug = jnp.concatenate([v_ref[pl.ds(key_start(j), tk), :], ones], axis=1)
        acc_sc[rows, :] += _dot(p_bufs[slot][rows, :], v_aug)

    first(scores)
    scores(1, 1)
    first(softmax, True)

    @pl.when(qi == 0)
    def _():
        softmax(1, 1, True)
        first(values)
        values(1, 1)

    @pl.when(qi > 0)
    def _():
        scores(2, 0)
        softmax(1, 1, True)
        first(values)
        scores(3, 1)
        softmax(2, 0, False)
        values(1, 1)
        qn = max_half_norms(q)
        qk_max = jnp.maximum(qn[0] * kn_sc[0:1, 0:1], qn[1] * kn_sc[4:5, 0:1])
        m_min = jnp.min(m_sc[...], axis=0, keepdims=True)[:, 0:1]
        reach = (qk_max - m_min + DA_DEAD_LOG2) / slope
        first_dead = jnp.floor((reach - 1.0) / tk) + 3.0
        first_dead = jnp.max(jnp.clip(first_dead, 0.0, 1e6)).astype(jnp.int32)
        pairs_end = jnp.maximum(2, jnp.minimum(qi + 1, (first_dead + 1) // 2))

        freeze = jnp.max(jnp.where(qk_max - m_min <= DA_FREEZE_LOG2, 1.0, 0.0)) > 0.5
        freeze = jnp.logical_and(freeze, pairs_end > 2)

        @pl.when(jnp.logical_not(freeze))
        def _():
            def body(i, carry):
                t = 2 * i
                scores(t, 0)
                softmax(t - 1, 1, False)
                values(t - 2, 0)
                scores(t + 1, 1)
                softmax(t, 0, False)
                values(t - 1, 1)
                return carry

            lax.fori_loop(2, pairs_end, body, 0)
            t = 2 * pairs_end
            softmax(t - 1, 1, False)
            values(t - 2, 0)
            values(t - 1, 1)

        @pl.when(freeze)
        def _():
            scores(4, 0)
            softmax_frozen(1)
            values(2, 0)
            scores(5, 1)
            softmax_frozen(0)
            values_frozen(3, 1)

            def body(i, carry):
                t = 2 * i
                scores(t, 0)
                softmax_frozen(1)
                values_frozen(t - 2, 0)
                scores(t + 1, 1)
                softmax_frozen(0)
                values_frozen(t - 1, 1)
                return carry

            lax.fori_loop(3, pairs_end, body, 0)
            t = 2 * pairs_end
            softmax_frozen(1)
            values_frozen(t - 2, 0)
            values_frozen(t - 1, 1)

    lv = lam_ref[...].astype(F32)
    dots = jnp.sum(lv[0:1, :] * lv[1:2, :], axis=1, keepdims=True)
    dots2 = jnp.sum(lv[2:3, :] * lv[3:4, :], axis=1, keepdims=True)
    lam = jnp.exp(dots) - jnp.exp(dots2) + lam_init
    acc = acc_sc[...]
    o0 = acc[0:tile, 0:hd] / acc[0:tile, hd:2 * hd]
    o1 = acc[tile:2 * tile, 0:hd] / acc[tile:2 * tile, hd:2 * hd]
    o = o0 - lam * o1
    ms = jnp.mean(o * o, axis=1, keepdims=True)
    o = o * lax.rsqrt(ms + 1e-6) * gain_ref[...] * (1.0 - lam_init)
    o_ref[...] = o.astype(o_ref.dtype)


def _diff_attention(proj, diff_lambda, diff_gain, lam_init):
    bsz, t, _ = proj.shape
    tile = min(TQ_ATT, t)
    hd = 2 * DA_HEAD_DIM
    kern = functools.partial(_diffattn_kernel, tile=tile, lam_init=lam_init)
    return pl.pallas_call(
        kern,
        grid=(bsz, DA_HEADS, t // tile),
        in_specs=[
            pl.BlockSpec((None, tile, hd), lambda b, h, i: (b, i, h)),
            pl.BlockSpec((None, t, hd), lambda b, h, i: (b, 0, DA_HEADS + h)),
            pl.BlockSpec((None, t, hd), lambda b, h, i: (b, 0, 2 * DA_HEADS + h)),
            pl.BlockSpec((4, DA_HEAD_DIM), lambda b, h, i: (0, 0)),
            pl.BlockSpec((1, hd), lambda b, h, i: (0, 0)),
        ],
        out_specs=pl.BlockSpec((None, tile, hd), lambda b, h, i: (b, i, h)),
        out_shape=jax.ShapeDtypeStruct((bsz, t, DA_HEADS * hd), BF16),
        scratch_shapes=[
            pltpu.VMEM((2 * tile, LANES), F32),
            pltpu.VMEM((2 * tile, 2 * hd), F32),
            pltpu.VMEM((2 * tile, tile // 2), F32),
            pltpu.VMEM((2 * tile, tile // 2), F32),
            pltpu.VMEM((2 * tile, tile // 2), BF16),
            pltpu.VMEM((2 * tile, tile // 2), BF16),
            pltpu.VMEM((2 * tile, LANES), F32),
            pltpu.VMEM((2 * tile, LANES), F32),
            pltpu.VMEM((8, LANES), F32),
        ],
        compiler_params=_params(("arbitrary", "arbitrary", "arbitrary")),
        name="diff_attn",
    )(proj, proj, proj, diff_lambda, diff_gain.reshape(1, hd))


def _hgrn_kernel(q_ref, f_ref, i_ref, g_ref, gamma_ref, gain_ref, o_ref, st_sc, *, layer):
    @pl.when(pl.program_id(1) == 0)
    def _():
        st_sc[...] = jnp.zeros(st_sc.shape, F32)

    gam = gamma_ref[...].astype(F32)
    e = jnp.exp(gam - jnp.max(gam, axis=0, keepdims=True))
    sm = e / jnp.sum(e, axis=0, keepdims=True)
    lb_all = jnp.sum(sm[0:layer + 1, :], axis=0, keepdims=True)

    c = HG_CHUNK
    row = lax.broadcasted_iota(jnp.int32, (c, c), 0)
    col = lax.broadcasted_iota(jnp.int32, (c, c), 1)
    tril = col <= row
    tril_bf = jnp.where(tril, 1.0, 0.0).astype(BF16)
    gain = gain_ref[...]

    heads = range(HG_HEADS)
    hcols = [slice(h * HG_DK, (h + 1) * HG_DK) for h in heads]
    lbs = [lb_all[:, hc] for hc in hcols]
    for n in range(q_ref.shape[0] // c):
        rows = slice(n * c, (n + 1) * c)
        sig = [_sigmoid(f_ref[rows, hc].astype(F32)) for hc in hcols]
        logf = [jnp.log(lbs[h] + (1.0 - lbs[h]) * sig[h]) for h in heads]
        kk = [(1.0 - lbs[h]) * (1.0 - sig[h]) for h in heads]
        parts = [_split3(x) for x in logf]
        b = [_dot(tril_bf, p[0]) + _dot(tril_bf, p[1]) for p in parts]
        b_mid = [x[c // 2 - 1:c // 2, :] for x in b]
        b_last = [x[c - 1:c, :] for x in b]
        qh = [q_ref[rows, hc].astype(F32) for hc in hcols]
        qs = [x * _sigmoid(x) for x in qh]
        v = [i_ref[rows, hc] for hc in hcols]
        qa = [qs[h] * jnp.exp(b[h] - b_mid[h]) for h in heads]
        ka = [kk[h] * jnp.exp(b_mid[h] - b[h]) for h in heads]
        att = [_dot_nt(qa[h].astype(BF16), ka[h].astype(BF16)) for h in heads]
        att = [jnp.where(tril, x, 0.0).astype(BF16) for x in att]
        o_intra = [_dot(att[h], v[h]) for h in heads]
        kd = [(ka[h] * jnp.exp(b_last[h] - b_mid[h])).astype(BF16) for h in heads]
        ds_t = [_dot(v[h].T, kd[h]) for h in heads]
        st = [st_sc[h] for h in heads]
        o_inter = [_dot_nt((qa[h] * jnp.exp(b_mid[h])).astype(BF16), st[h].astype(BF16))
                   for h in heads]
        for h in heads:
            st_sc[h] = st[h] * jnp.exp(b_last[h]) + ds_t[h]
        for h in heads:
            o = o_intra[h] + o_inter[h]
            gh = g_ref[rows, hcols[h]].astype(F32)
            ms = jnp.mean(o * o, axis=1, keepdims=True)
            o = o * lax.rsqrt(ms + 1e-6) * gain * (gh * _sigmoid(gh))
            o_ref[rows, hcols[h]] = o.astype(o_ref.dtype)


def _hgrn2(proj, hgrn_gamma, hgrn_gain, layer):
    bsz, t, _ = proj.shape
    tt = min(T_HG, t)
    width = HG_HEADS * HG_DK
    base = 3 * DA_HEADS * 2 * DA_HEAD_DIM // width
    spec = lambda k: pl.BlockSpec((None, tt, width), lambda b, i: (b, i, base + k))
    return pl.pallas_call(
        functools.partial(_hgrn_kernel, layer=layer),
        grid=(bsz, t // tt),
        in_specs=[
            spec(0), spec(1), spec(2), spec(3),
            pl.BlockSpec((hgrn_gamma.shape[0], width), lambda b, i: (0, 0)),
            pl.BlockSpec((1, HG_DK), lambda b, i: (0, 0)),
        ],
        out_specs=pl.BlockSpec((None, tt, width), lambda b, i: (b, i, 0)),
        out_shape=jax.ShapeDtypeStruct((bsz, t, width), BF16),
        scratch_shapes=[pltpu.VMEM((HG_HEADS, HG_DK, HG_DK), F32)],
        compiler_params=_params(("arbitrary", "arbitrary")),
        name="hgrn2",
    )(proj, proj, proj, proj, hgrn_gamma, hgrn_gain.reshape(1, HG_DK))


def _route(logits_t):
    mx = jnp.max(logits_t, axis=0, keepdims=True)
    ex = jnp.exp(logits_t - mx)
    probs = ex / jnp.sum(ex, axis=0, keepdims=True)
    p = [probs[e:e + 1, :] for e in range(N_EXPERTS)]
    g = E_PER_GROUP
    scores = []
    for gi in range(N_GROUPS):
        pg = p[gi * g:(gi + 1) * g]
        best = None
        for a in range(g):
            for b in range(a + 1, g):
                pair = pg[a] + pg[b]
                best = pair if best is None else jnp.maximum(best, pair)
        scores.append(best)
    group_id = jnp.zeros_like(p[0])
    gates = [jnp.zeros_like(p[0]) for _ in range(g)]
    for gi in range(N_GROUPS):
        sel = None
        for gj in range(N_GROUPS):
            if gj == gi:
                continue
            cond = (scores[gi] > scores[gj]) if gj < gi else (scores[gi] >= scores[gj])
            sel = cond if sel is None else (sel & cond)
        group_id = jnp.where(sel, float(gi), group_id)
        pg = p[gi * g:(gi + 1) * g]
        chosen = []
        for a in range(g):
            rank = jnp.zeros_like(pg[a])
            for b in range(g):
                if b == a:
                    continue
                ahead = (pg[b] >= pg[a]) if b < a else (pg[b] > pg[a])
                rank = rank + jnp.where(ahead, 1.0, 0.0)
            chosen.append(sel & (rank < 2.0))
        denom = None
        for a in range(g):
            term = jnp.where(chosen[a], pg[a], 0.0)
            denom = term if denom is None else denom + term
        for a in range(g):
            gates[a] = jnp.where(chosen[a], pg[a] / denom, gates[a])
    return group_id, gates


def _outproj_kernel(a_ref, b_ref, x_ref, mod_ref, w_ref, lng_ref, lnb_ref, rwt_ref, rb_ref,
                    xo_ref, h_ref, row_ref, col_ref, *, alpha):
    half = a_ref.shape[1]
    tm = a_ref.shape[0]
    g1 = mod_ref[2:3, :]
    sh2 = mod_ref[3:4, :]
    sc2 = mod_ref[4:5, :]
    rw = rwt_ref[...]
    w_hi, w_mid, _ = _split3(rw)
    chunks = [slice(i * tm // OUTPROJ_CHUNKS, (i + 1) * tm // OUTPROJ_CHUNKS)
              for i in range(OUTPROJ_CHUNKS)]
    y = [_dot(a_ref[rs, :], w_ref[0:half, :]) + _dot(b_ref[rs, :], w_ref[half:2 * half, :])
         for rs in chunks]
    r = [alpha * x_ref[rs, :] + (1.0 + g1) * yy for rs, yy in zip(chunks, y)]
    rc = [rr - jnp.mean(rr, axis=1, keepdims=True) for rr in r]
    var = [jnp.mean(cc * cc, axis=1, keepdims=True) for cc in rc]
    xn = [cc * lax.rsqrt(vv + 1e-5) * lng_ref[...] + lnb_ref[...] for cc, vv in zip(rc, var)]
    h2 = [xx * (1.0 + sc2) + sh2 for xx in xn]
    for rs, xx, hh in zip(chunks, xn, h2):
        xo_ref[rs, :] = xx
        h_ref[rs, :] = hh.astype(BF16)
    split = [_split3(hh) for hh in h2]
    logits_t = jnp.concatenate(
        [_dot_nt(w_hi, s[0]) + _dot_nt(w_hi, s[1]) + _dot_nt(w_mid, s[0]) for s in split],
        axis=1) + rb_ref[...]
    group_id, gates = _route(logits_t)
    sel = [jnp.where(group_id == float(gi), 1.0, 0.0) for gi in range(N_GROUPS)]
    onehot = jnp.concatenate(sel + [jnp.zeros((8 - N_GROUPS, tm), F32)], axis=0).astype(BF16)
    src = lax.broadcasted_iota(jnp.int32, (tm, tm), 0)
    dst = lax.broadcasted_iota(jnp.int32, (tm, tm), 1)
    earlier = jnp.where(src < dst, 1.0, 0.0).astype(BF16)
    counts = _dot(onehot, earlier)
    rank = sel[0] * counts[0:1, :]
    for gi in range(1, N_GROUPS):
        rank = rank + sel[gi] * counts[gi:gi + 1, :]
    info = jnp.concatenate(gates + [group_id, rank], axis=0)
    row_ref[...] = jnp.concatenate(
        [group_id, rank, jnp.zeros((8 - 2, tm), F32)], axis=0)
    pad = jnp.zeros((LANES - info.shape[0], tm), F32)
    col_ref[...] = jnp.concatenate([info, pad], axis=0).T


def _outproj(a, b, x, mod_l, w_bf16, ln_g, ln_b, router_w, router_b, alpha):
    bsz, t, d = x.shape
    half = a.shape[2]
    tm = min(T_BLK, t)
    tok = lambda width: pl.BlockSpec((None, tm, width), lambda bi, i: (bi, i, 0))
    full = lambda r, c: pl.BlockSpec((r, c), lambda bi, i: (0, 0))
    return pl.pallas_call(
        functools.partial(_outproj_kernel, alpha=alpha),
        grid=(bsz, t // tm),
        in_specs=[
            tok(half), tok(half), tok(d),
            pl.BlockSpec((None, 6, d), lambda bi, i: (bi, 0, 0)),
            full(2 * half, d), full(1, d), full(1, d), full(N_EXPERTS, d), full(N_EXPERTS, 1),
        ],
        out_specs=[tok(d), tok(d), pl.BlockSpec((None, 8, tm), lambda bi, i: (bi, 0, i)),
                   tok(LANES)],
        out_shape=[
            jax.ShapeDtypeStruct((bsz, t, d), F32),
            jax.ShapeDtypeStruct((bsz, t, d), BF16),
            jax.ShapeDtypeStruct((bsz, 8, t), F32),
            jax.ShapeDtypeStruct((bsz, t, LANES), F32),
        ],
        compiler_params=_params(("arbitrary", "arbitrary")),
        name="outproj_ln_route",
    )(a, b, x, mod_l, w_bf16, ln_g.reshape(1, d), ln_b.reshape(1, d), router_w.T,
      router_b.reshape(N_EXPERTS, 1))


def _slab_rows(tm):
    extra = -(-(tm - MOE_ROWS_MAIN) // MOE_ROWS_EXTRA)
    return MOE_ROWS_MAIN + max(extra, 0) * MOE_ROWS_EXTRA


def _extra_chunks(count):
    return (jnp.maximum(count - MOE_ROWS_MAIN, 0) + MOE_ROWS_EXTRA - 1) // MOE_ROWS_EXTRA


def _moe_expert_kernel(cnt_ref, h_ref, row_ref, col_ref, wg_ref, wu_ref, wd_ref, zm_ref, zx_ref):
    g = pl.program_id(0)
    blk = pl.program_id(1)
    count = cnt_ref[g * pl.num_programs(1) + blk]
    mine = row_ref[0:1, :] == g.astype(F32)
    rank = row_ref[1:2, :]
    info = col_ref[...]
    info_hi = info.astype(BF16)
    info_lo = (info - info_hi.astype(F32)).astype(BF16)

    def run_rows(r0, m, out_ref, out_r0):
        rid = (lax.broadcasted_iota(jnp.int32, (m, 1), 0) + r0).astype(F32)
        pick = _onehot((rank == rid) & mine)
        xs = _dot(pick, h_ref[...]).astype(BF16)
        gm = _dot(pick, info_hi) + _dot(pick, info_lo)
        acc = None
        for j in range(E_PER_GROUP):
            a = _dot(xs, wg_ref[j])
            u = _dot(xs, wu_ref[j])
            he = (a * _sigmoid(a) * u * gm[:, j:j + 1]).astype(BF16)
            part = _dot(he, wd_ref[j])
            acc = part if acc is None else acc + part
        out_ref[pl.ds(out_r0, m), :] = acc.astype(out_ref.dtype)

    @pl.when(count <= MOE_ROWS_SMALL)
    def _():
        run_rows(0, MOE_ROWS_SMALL, zm_ref, 0)
        zm_ref[MOE_ROWS_SMALL:MOE_ROWS_MAIN, :] = jnp.zeros(
            (MOE_ROWS_MAIN - MOE_ROWS_SMALL, zm_ref.shape[1]), zm_ref.dtype)

    @pl.when(count > MOE_ROWS_SMALL)
    def _():
        run_rows(0, MOE_ROWS_MAIN, zm_ref, 0)

    zx_ref[...] = jnp.zeros(zx_ref.shape, zx_ref.dtype)

    def body(i, carry):
        off = pl.multiple_of(i * MOE_ROWS_EXTRA, 16)
        run_rows(MOE_ROWS_MAIN + off, MOE_ROWS_EXTRA, zx_ref, off)
        return carry

    lax.fori_loop(0, _extra_chunks(count), body, 0)


def _moe_combine_kernel(cnt_ref, xidx_ref, zm0_ref, zm1_ref, zm2_ref, zm3_ref, zx0_ref, zx1_ref,
                        zx2_ref, zx3_ref, col_ref, x_ref, mod_ref, lng_ref, lnb_ref, o_ref, y_sc,
                        *, alpha):
    del xidx_ref
    blk = pl.program_id(0) * pl.num_programs(1) + pl.program_id(1)
    n_blk = pl.num_programs(0) * pl.num_programs(1)
    zm_refs = (zm0_ref, zm1_ref, zm2_ref, zm3_ref)
    zx_refs = (zx0_ref, zx1_ref, zx2_ref, zx3_ref)
    main = MOE_ROWS_MAIN
    grp = col_ref[:, E_PER_GROUP:E_PER_GROUP + 1]
    rank = col_ref[:, E_PER_GROUP + 1:E_PER_GROUP + 2]
    y_sc[...] = jnp.zeros(y_sc.shape, F32)
    lane_x = lax.broadcasted_iota(jnp.int32, (1, MOE_ROWS_EXTRA), 1).astype(F32)
    for gi in range(N_GROUPS):
        def body(i, carry, gi=gi):
            off = pl.multiple_of(i * MOE_ROWS_EXTRA, 16)
            hit = (grp == float(gi)) & ((rank - (main + off).astype(F32)) == lane_x)
            y_sc[...] += _dot(_onehot(hit), zx_refs[gi][pl.ds(off, MOE_ROWS_EXTRA), :])
            return carry

        lax.fori_loop(0, _extra_chunks(cnt_ref[gi * n_blk + blk]), body, 0)

    where_to = jnp.where(rank < float(main), grp * float(main) + rank, -1.0)
    lane = lax.broadcasted_iota(jnp.int32, (1, N_GROUPS * main), 1).astype(F32)
    z_all = jnp.concatenate([zr[...] for zr in zm_refs], axis=0)
    g2 = mod_ref[5:6, :]
    tm = x_ref.shape[0]
    chunks = [slice(i * tm // OUTPROJ_CHUNKS, (i + 1) * tm // OUTPROJ_CHUNKS)
              for i in range(OUTPROJ_CHUNKS)]
    y = [y_sc[rs, :] + _dot(_onehot(where_to[rs, :] == lane), z_all) for rs in chunks]
    r = [alpha * x_ref[rs, :] + (1.0 + g2) * yy for rs, yy in zip(chunks, y)]
    rc = [rr - jnp.mean(rr, axis=1, keepdims=True) for rr in r]
    var = [jnp.mean(cc * cc, axis=1, keepdims=True) for cc in rc]
    for rs, cc, vv in zip(chunks, rc, var):
        o_ref[rs, :] = cc * lax.rsqrt(vv + 1e-5) * lng_ref[...] + lnb_ref[...]


def _moe(h2, rowinfo, colinfo, x, mod_l, layer, wg, wu, wd, ln_g, ln_b, alpha):
    bsz, t, d = x.shape
    tm = min(T_BLK, t)
    nb = t // tm
    n_blk = bsz * nb
    dff = wg.shape[3]
    slab = _slab_rows(tm)
    group_of = rowinfo[:, 0, :].reshape(1, n_blk, tm)
    counts = jnp.sum(group_of == jnp.arange(N_GROUPS, dtype=F32).reshape(N_GROUPS, 1, 1), axis=2)
    counts = counts.astype(jnp.int32)
    needed = jnp.where(counts > MOE_ROWS_MAIN, jnp.arange(n_blk, dtype=jnp.int32), 0)
    extra_block = lax.cummax(needed, axis=1).reshape(N_GROUPS * n_blk)
    counts = counts.reshape(N_GROUPS * n_blk)
    extra = slab - MOE_ROWS_MAIN

    z_main, z_extra = pl.pallas_call(
        _moe_expert_kernel,
        grid_spec=pltpu.PrefetchScalarGridSpec(
            num_scalar_prefetch=1,
            grid=(N_GROUPS, n_blk),
            in_specs=[
                pl.BlockSpec((None, tm, d), lambda g, i, c: (i // nb, i % nb, 0)),
                pl.BlockSpec((None, 8, tm), lambda g, i, c: (i // nb, 0, i % nb)),
                pl.BlockSpec((None, tm, LANES), lambda g, i, c: (i // nb, i % nb, 0)),
                pl.BlockSpec((None, E_PER_GROUP, d, dff), lambda g, i, c: (layer, g, 0, 0)),
                pl.BlockSpec((None, E_PER_GROUP, d, dff), lambda g, i, c: (layer, g, 0, 0)),
                pl.BlockSpec((None, E_PER_GROUP, dff, d), lambda g, i, c: (layer, g, 0, 0)),
            ],
            out_specs=[
                pl.BlockSpec((None, None, MOE_ROWS_MAIN, d), lambda g, i, c: (g, i, 0, 0)),
                pl.BlockSpec((None, None, extra, d), lambda g, i, c: (g, i, 0, 0)),
            ],
        ),
        out_shape=[
            jax.ShapeDtypeStruct((N_GROUPS, n_blk, MOE_ROWS_MAIN, d), BF16),
            jax.ShapeDtypeStruct((N_GROUPS, n_blk, extra, d), BF16),
        ],
        compiler_params=_params(("arbitrary", "arbitrary")),
        name="moe_experts",
    )(counts, h2, rowinfo, colinfo, wg, wu, wd)

    zm_spec = lambda gi: pl.BlockSpec((None, None, MOE_ROWS_MAIN, d),
                                      lambda b, i, c, xb: (gi, b * nb + i, 0, 0))
    zx_spec = lambda gi: pl.BlockSpec((None, None, extra, d),
                                      lambda b, i, c, xb: (gi, xb[gi * n_blk + b * nb + i], 0, 0))
    tok = lambda width: pl.BlockSpec((None, tm, width), lambda b, i, c, xb: (b, i, 0))
    return pl.pallas_call(
        functools.partial(_moe_combine_kernel, alpha=alpha),
        grid_spec=pltpu.PrefetchScalarGridSpec(
            num_scalar_prefetch=2,
            grid=(bsz, nb),
            in_specs=[
                zm_spec(0), zm_spec(1), zm_spec(2), zm_spec(3),
                zx_spec(0), zx_spec(1), zx_spec(2), zx_spec(3), tok(LANES), tok(d),
                pl.BlockSpec((None, 6, d), lambda b, i, c, xb: (b, 0, 0)),
                pl.BlockSpec((1, d), lambda b, i, c, xb: (0, 0)),
                pl.BlockSpec((1, d), lambda b, i, c, xb: (0, 0)),
            ],
            out_specs=tok(d),
            scratch_shapes=[pltpu.VMEM((tm, d), F32)],
        ),
        out_shape=jax.ShapeDtypeStruct((bsz, t, d), F32),
        compiler_params=_params(("arbitrary", "arbitrary")),
        name="moe_combine_ln",
    )(counts, extra_block, z_main, z_main, z_main, z_main, z_extra, z_extra, z_extra, z_extra,
      colinfo, x, mod_l, ln_g.reshape(1, d), ln_b.reshape(1, d))


def _lru_kernel(x_ref, g_ref, cw_ref, cb_ref, wa_ref, ba_ref, wx_ref, bx_ref, lam_ref,
                o_ref, xpad_sc, h_sc):
    tt = x_ref.shape[0]
    pad = 8

    @pl.when(pl.program_id(1) == 0)
    def _():
        xpad_sc[0:pad, :] = jnp.zeros((pad, xpad_sc.shape[1]), F32)
        h_sc[...] = jnp.zeros(h_sc.shape, F32)

    xpad_sc[pad:pad + tt, :] = x_ref[...].astype(F32)
    xc = cb_ref[...] + jnp.zeros((tt, x_ref.shape[1]), F32)
    for j in range(CONV_WIDTH):
        off = pad - (CONV_WIDTH - 1) + j
        xc = xc + cw_ref[j:j + 1, :] * xpad_sc[off:off + tt, :]
    xpad_sc[0:pad, :] = xpad_sc[tt:tt + pad, :]

    xb = xc.astype(BF16)
    r = _sigmoid(_dot(xb, wa_ref[...]) + ba_ref[...])
    i = _sigmoid(_dot(xb, wx_ref[...]) + bx_ref[...])
    lam = lam_ref[...].astype(F32)
    softplus_neg = jnp.maximum(-lam, 0.0) + jnp.log(1.0 + jnp.exp(-jnp.abs(lam)))
    log_a = -LRU_C * r * softplus_neg
    a = jnp.exp(log_a)
    gain_sq = jnp.maximum(1.0 - jnp.exp(2.0 * log_a), 1e-12)
    u = gain_sq * lax.rsqrt(gain_sq) * (i * xc)

    groups = (tt // SUBLANES, SUBLANES, a.shape[1])
    a = a.reshape(groups)
    u = u.reshape(groups)
    rowi = lax.broadcasted_iota(jnp.int32, (1, SUBLANES, 1), 1)
    d = 1
    while d < SUBLANES:
        a_sh = jnp.where(rowi >= d, pltpu.roll(a, d, 1), 1.0)
        u_sh = jnp.where(rowi >= d, pltpu.roll(u, d, 1), 0.0)
        u = u + a * u_sh
        a = a * a_sh
        d *= 2
    a = a.reshape(tt, groups[2])
    u = u.reshape(tt, groups[2])
    gr = g_ref[...].astype(F32)
    gelu = 0.5 * gr * (1.0 + jnp.tanh(0.7978845608028654 * (gr + 0.044715 * gr * gr * gr)))
    h_prev = h_sc[...]
    out = []
    for grp in range(tt // SUBLANES):
        rows = slice(grp * SUBLANES, (grp + 1) * SUBLANES)
        h_grp = u[rows, :] + a[rows, :] * h_prev
        out.append(gelu[rows, :] * h_grp)
        h_prev = h_grp[SUBLANES - 1:SUBLANES, :]
    h_sc[...] = h_prev
    o_ref[...] = jnp.concatenate(out, axis=0).astype(o_ref.dtype)


def _rg_lru(proj, conv_w, conv_b, wa_dense, ba, wx_dense, bx, lam):
    bsz, t, _ = proj.shape
    tt = min(T_LRU, t)
    w = LRU_WIDTH
    nblk = w // LANES
    row = lambda a: a.reshape(1, w)
    full = lambda r, c: pl.BlockSpec((r, c), lambda b, i: (0, 0))
    return pl.pallas_call(
        _lru_kernel,
        grid=(bsz, t // tt),
        in_specs=[
            pl.BlockSpec((None, tt, w), lambda b, i: (b, i, 0)),
            pl.BlockSpec((None, tt, w), lambda b, i: (b, i, 1)),
            full(CONV_WIDTH, w), full(1, w), full(w, w), full(1, w), full(w, w), full(1, w),
            full(1, w),
        ],
        out_specs=pl.BlockSpec((None, tt, w), lambda b, i: (b, i, 0)),
        out_shape=jax.ShapeDtypeStruct((bsz, t, w), BF16),
        scratch_shapes=[pltpu.VMEM((tt + SUBLANES, w), F32), pltpu.VMEM((1, w), F32)],
        compiler_params=_params(("arbitrary", "arbitrary")),
        name="rg_lru",
    )(proj, proj, conv_w, row(conv_b), wa_dense, row(ba), wx_dense, row(bx), row(lam))


def _sb_kernel(q_ref, k_ref, v_ref, o_ref, r_sc, acc_sc, *bufs, tile):
    qi = pl.program_id(2)
    d = SB_HEAD_DIM
    tk = tile // 2
    z_bufs, lb_bufs, l_bufs, w_bufs = (bufs[i * SB_SETS:(i + 1) * SB_SETS] for i in range(4))
    lane = lax.broadcasted_iota(jnp.int32, (1, 2 * d), 1)
    q = q_ref[...]
    zero = jnp.zeros_like(q)
    q2 = jnp.concatenate([jnp.where(lane < d, q, zero), jnp.where(lane >= d, q, zero)], axis=0)
    rj = lax.broadcasted_iota(jnp.int32, (tk, tk), 0)
    cs = lax.broadcasted_iota(jnp.int32, (tk, tk), 1)
    upper = jnp.where(rj > cs, 1.0, 0.0).astype(BF16)

    r_sc[...] = jnp.zeros(r_sc.shape, F32)
    acc_sc[...] = jnp.zeros(acc_sc.shape, F32)
    n_sub = 2 * qi + 2
    every = slice(0, 2 * tile)
    per_head = (slice(0, tile), slice(tile, 2 * tile))

    def key_start(j):
        return pl.multiple_of((n_sub - 1 - j) * tk, tk)

    def strict_mask(j, rows):
        n_rows = rows.stop - rows.start
        rowp = (lax.broadcasted_iota(jnp.int32, (n_rows, tk), 0) + rows.start) & (tile - 1)
        colp = lax.broadcasted_iota(jnp.int32, (n_rows, tk), 1) + (1 - j) * tk
        return colp < rowp

    def logits(j, b, rows=every):
        z_bufs[b][rows, :] = _dot_nt(q2[rows, :], k_ref[pl.ds(key_start(j), tk), :])

    def gates(j, b, rows=every, masked=False):
        z = z_bufs[b][rows, :]
        log_1m = jnp.log(1.0 + jnp.exp2(-jnp.abs(z))) * (-LOG2E) - jnp.maximum(z, 0.0)
        lb_bufs[b][rows, :] = z + log_1m
        if masked:
            log_1m = jnp.where(strict_mask(j, rows), log_1m, 0.0)
        l_bufs[b][rows, :] = log_1m.astype(BF16)

    def weights(j, b, rows=every, masked=False):
        log_1m = l_bufs[b][rows, :]
        after = _dot(log_1m, upper) + r_sc[rows, :]
        w = jnp.exp2(lb_bufs[b][rows, :] + after)
        if masked:
            w = jnp.where(strict_mask(j, rows), w, 0.0)
        w_bufs[b][rows, :] = w.astype(BF16)
        r_sc[rows, :] = after[:, 0:1] + log_1m[:, 0:1].astype(F32)

    def values(j, b, rows=every):
        acc_sc[rows, :] += _dot(w_bufs[b][rows, :], v_ref[pl.ds(key_start(j), tk), :])

    late = tuple(slice(r.start + tile // 2, r.stop) for r in per_head)
    early = tuple(slice(r.start, r.start + tile // 2) for r in per_head)

    def first(stage, **kw):
        for rows in late:
            stage(0, 0, rows, **kw)

    @pl.when(qi == 0)
    def _():
        first(logits)
        logits(1, 1)
        first(gates, masked=True)
        gates(1, 1, masked=True)
        first(weights, masked=True)
        weights(1, 1, masked=True)
        first(values)
        values(1, 1)

    @pl.when(qi > 0)
    def _():
        def alive(rows=every):
            return (jnp.max(r_sc[rows, :]) > -SB_DEAD_LOG2).astype(jnp.int32)

        def third(stage, rows_set):
            for rows in rows_set:
                stage(2, 2, rows)

        first(logits)
        logits(1, 1)
        first(gates, masked=True)
        third(logits, early)
        gates(1, 1, masked=True)
        first(weights, masked=True)
        third(gates, early)
        weights(1, 1, masked=True)
        first(values)
        third(weights, early)
        values(1, 1)
        third(values, early)

        @pl.when(jnp.maximum(alive(late[0]), alive(late[1])) > 0)
        def _():
            for stage in (logits, gates, weights, values):
                third(stage, late)

        def cond(carry):
            j, live = carry
            return (j < n_sub) & (live > 0)

        def body(carry):
            j, _ = carry
            for rows in per_head:
                logits(j, 0, rows)
            for rows in per_head:
                gates(j, 0, rows)
            for rows in per_head:
                weights(j, 0, rows)
            for rows in per_head:
                values(j, 0, rows)
            return j + 1, alive()

        lax.while_loop(cond, body, (jnp.int32(3), alive()))

    acc = acc_sc[...]
    o_ref[...] = jnp.where(lane < d, acc[0:tile, :], acc[tile:2 * tile, :]).astype(o_ref.dtype)


def _sb_attention(proj):
    bsz, t, _ = proj.shape
    tq = min(T_SB, t)
    pairs = SB_HEADS // 2
    wblk = 2 * SB_HEAD_DIM
    base = 2 * LRU_WIDTH // wblk
    return pl.pallas_call(
        functools.partial(_sb_kernel, tile=tq),
        grid=(bsz, pairs, t // tq),
        in_specs=[
            pl.BlockSpec((None, tq, wblk), lambda b, h, i: (b, i, base + h)),
            pl.BlockSpec((None, t, wblk), lambda b, h, i: (b, 0, base + pairs + h)),
            pl.BlockSpec((None, t, wblk), lambda b, h, i: (b, 0, base + 2 * pairs + h)),
        ],
        out_specs=pl.BlockSpec((None, tq, wblk), lambda b, h, i: (b, i, h)),
        out_shape=jax.ShapeDtypeStruct((bsz, t, SB_HEADS * SB_HEAD_DIM), BF16),
        scratch_shapes=[pltpu.VMEM((2 * tq, 1), F32), pltpu.VMEM((2 * tq, wblk), F32)]
        + [pltpu.VMEM((2 * tq, tq // 2), F32)] * (2 * SB_SETS)
        + [pltpu.VMEM((2 * tq, tq // 2), BF16)] * (2 * SB_SETS),
        compiler_params=_params(("arbitrary", "arbitrary", "arbitrary")),
        name="sb_attn",
    )(proj, proj, proj)


def _block_diag(w):
    g, n, _ = w.shape
    eye = jnp.eye(g, dtype=w.dtype)
    return (eye[:, None, :, None] * w[:, :, None, :]).reshape(g * n, g * n)


def kernel(x, c, ada_w, ada_b, ln_g, ln_b, even_w_in, even_w_out, diff_lambda, diff_gain, hgrn_gamma, hgrn_gain, odd_w_in, odd_w_out, conv_w, conv_b, lru_wa, lru_ba, lru_wx, lru_bx, lru_lambda, router_w, router_b, moe_w_gate, moe_w_up, moe_w_down):
    depth = ada_w.shape[0]
    bsz, t, d = x.shape
    alpha = (2.0 * depth) ** 0.25
    mod = _ada_mod(c, ada_w, ada_b).reshape(depth, bsz, 6, d)
    w_gate, w_up, w_down = (w.astype(BF16) for w in (moe_w_gate, moe_w_up, moe_w_down))
    for l in range(depth):
        j = l // 2
        mod_l = mod[l]
        if l % 2 == 0:
            lam_init = 0.8 - 0.6 * math.exp(-0.3 * l)
            proj = _inproj(x, mod_l, even_w_in[j].astype(BF16), q_chunk=0)
            mix_a = _diff_attention(proj, diff_lambda[j], diff_gain[j], lam_init)
            mix_b = _hgrn2(proj, hgrn_gamma, hgrn_gain[j], l)
            w_out = even_w_out[j]
        else:
            proj = _inproj(x, mod_l, odd_w_in[j].astype(BF16), q_chunk=2 * LRU_WIDTH // PROJ_CHUNK)
            mix_a = _rg_lru(proj, conv_w[j], conv_b[j], _block_diag(lru_wa[j]).astype(BF16),
                            lru_ba[j], _block_diag(lru_wx[j]).astype(BF16), lru_bx[j],
                            lru_lambda[j])
            mix_b = _sb_attention(proj)
            w_out = odd_w_out[j]
        x, h2, rowinfo, colinfo = _outproj(mix_a, mix_b, x, mod_l, w_out.astype(BF16),
                                           ln_g[l, 0], ln_b[l, 0], router_w, router_b, alpha)
        x = _moe(h2, rowinfo, colinfo, x, mod_l, l, w_gate, w_up, w_down, ln_g[l, 1], ln_b[l, 1],
                 alpha)
    return x
```

```python
import functools
import math

import jax
import jax.numpy as jnp
from jax import lax
from jax.experimental import pallas as pl
from jax.experimental.pallas import tpu as pltpu

F32 = jnp.float32
BF16 = jnp.bfloat16

DA_HEADS = 4
DA_HEAD_DIM = 64
HG_HEADS = 4
HG_DK = 128
HG_CHUNK = 64
LRU_WIDTH = 512
CONV_WIDTH = 4
LRU_C = 8.0
SB_HEADS = 8
SB_HEAD_DIM = 64
N_EXPERTS = 16
N_GROUPS = 4
E_PER_GROUP = N_EXPERTS // N_GROUPS

LANES = 128
SUBLANES = 8
NEG_BIG = -1e30
LOG2E = 1.4426950408889634
Q_PRESCALE = DA_HEAD_DIM ** -0.5 * LOG2E
PROJ_CHUNK = 512
VMEM_LIMIT = 56 * 1024 * 1024

TM_PROJ = 1024
TQ_ATT = 512
T_SB = 512
SB_SETS = 3
SB_DEAD_LOG2 = 160.0
DA_DEAD_LOG2 = 152.0
DA_FREEZE_LOG2 = 64.0
T_HG = 512
T_LRU = 256
T_BLK = 1024
ROW_CHUNKS = 8
MOE_ROWS_MAIN = 320
MOE_ROWS_SMALL = 256
MOE_ROWS_EXTRA = 128


def _params(sem):
    return pltpu.CompilerParams(dimension_semantics=sem, vmem_limit_bytes=VMEM_LIMIT)


def _sigmoid(x):
    return 0.5 * jnp.tanh(0.5 * x) + 0.5


def _dot(a, b):
    return jnp.dot(a, b, preferred_element_type=F32)


def _dot_nt(a, b):
    return lax.dot_general(a, b, (((1,), (1,)), ((), ())), preferred_element_type=F32)


def _onehot(mask):
    return jnp.where(mask, 1.0, 0.0).astype(BF16)


def _split3(x):
    hi = x.astype(BF16)
    r1 = x - hi.astype(F32)
    mid = r1.astype(BF16)
    lo = (r1 - mid.astype(F32)).astype(BF16)
    return hi, mid, lo


def _ada_kernel(c_ref, w_ref, b_ref, o_ref):
    c = c_ref[...]
    cond = c * _sigmoid(c)
    hi, mid, _ = _split3(cond)
    w = w_ref[...].astype(BF16)
    o_ref[...] = _dot(hi, w) + _dot(mid, w) + b_ref[...]


def _ada_mod(c, ada_w, ada_b):
    depth, d, d6 = ada_w.shape
    bsz = c.shape[0]
    n_col = d6 // d
    return pl.pallas_call(
        _ada_kernel,
        grid=(depth, n_col),
        in_specs=[
            pl.BlockSpec((bsz, d), lambda l, j: (0, 0)),
            pl.BlockSpec((None, d, d), lambda l, j: (l, 0, j)),
            pl.BlockSpec((None, 1, d), lambda l, j: (l, 0, j)),
        ],
        out_specs=pl.BlockSpec((None, bsz, d), lambda l, j: (l, 0, j)),
        out_shape=jax.ShapeDtypeStruct((depth, bsz, d6), F32),
        compiler_params=_params(("arbitrary", "arbitrary")),
        name="ada_mod",
    )(c, ada_w, ada_b.reshape(depth, 1, d6))


def _inproj_kernel(x_ref, mod_ref, w_ref, o_ref, *, col_chunk, q_chunk):
    sh = mod_ref[0:1, :]
    sc = mod_ref[1:2, :]
    h = (x_ref[...] * (1.0 + sc) + sh).astype(BF16)
    for j in range(o_ref.shape[1] // col_chunk):
        cols = slice(j * col_chunk, (j + 1) * col_chunk)
        y = _dot(h, w_ref[:, cols])
        if j == q_chunk:
            y = y * Q_PRESCALE
        o_ref[:, cols] = y.astype(o_ref.dtype)


def _inproj(x, mod_l, w_bf16, q_chunk):
    bsz, t, d = x.shape
    width = w_bf16.shape[1]
    tm = min(TM_PROJ, t)
    return pl.pallas_call(
        functools.partial(_inproj_kernel, col_chunk=PROJ_CHUNK, q_chunk=q_chunk),
        grid=(bsz, t // tm),
        in_specs=[
            pl.BlockSpec((None, tm, d), lambda b, i: (b, i, 0)),
            pl.BlockSpec((None, 6, d), lambda b, i: (b, 0, 0)),
            pl.BlockSpec((d, width), lambda b, i: (0, 0)),
        ],
        out_specs=pl.BlockSpec((None, tm, width), lambda b, i: (b, i, 0)),
        out_shape=jax.ShapeDtypeStruct((bsz, t, width), BF16),
        compiler_params=_params(("arbitrary", "arbitrary")),
        name="inproj",
    )(x, mod_l, w_bf16)


def _diffattn_kernel(q_ref, k_ref, v_ref, lam_ref, gain_ref, o_ref, m_sc, acc_sc, s0_sc, s1_sc,
                     p0_sc, p1_sc, a0_sc, a1_sc, kn_sc, *, tile, lam_init):
    h = pl.program_id(1)
    qi = pl.program_id(2)
    dh = DA_HEAD_DIM
    hd = 2 * dh
    tk = tile // 2
    reps = tk // LANES
    s_bufs, p_bufs, a_bufs = (s0_sc, s1_sc), (p0_sc, p1_sc), (a0_sc, a1_sc)

    lane = lax.broadcasted_iota(jnp.int32, (1, hd), 1)
    q = q_ref[...]
    zero = jnp.zeros_like(q)
    q2 = jnp.concatenate([jnp.where(lane < dh, q, zero), jnp.where(lane >= dh, q, zero)], axis=0)

    hf = jnp.full((1, 1), h + 1, jnp.int32).astype(F32)
    slope = jnp.exp2(hf * (-8.0 / DA_HEADS)) * LOG2E
    col = lax.broadcasted_iota(jnp.int32, (1, tk), 1)
    ones = jnp.ones((tk, hd), BF16)

    m_sc[...] = jnp.full(m_sc.shape, NEG_BIG, F32)
    acc_sc[...] = jnp.zeros(acc_sc.shape, F32)

    def max_half_norms(x):
        xf = x.astype(F32)
        sq = xf * xf
        out = []
        for keep in (lane < dh, lane >= dh):
            rows = jnp.sum(jnp.where(keep, sq, 0.0), axis=1, keepdims=True)
            out.append(jnp.sqrt(jnp.max(rows, axis=0, keepdims=True)))
        return out

    @pl.when(qi == 0)
    def _():
        kn_sc[...] = jnp.concatenate([jnp.broadcast_to(n, (4, LANES))
                                      for n in max_half_norms(k_ref[...])], axis=0)

    n_sub = 2 * qi + 2

    def key_start(j):
        return pl.multiple_of((n_sub - 1 - j) * tk, tk)

    every = slice(0, 2 * tile)

    def scores(j, slot, rows=every):
        ks = key_start(j)
        bias = (col + (ks - qi * tile)).astype(F32) * slope
        s_bufs[slot][rows, :] = _dot_nt(q2[rows, :], k_ref[pl.ds(ks, tk), :]) + bias

    def softmax(j, slot, masked, rows=every):
        s = s_bufs[slot][rows, :]
        if masked:
            n_rows = rows.stop - rows.start
            rowp = ((lax.broadcasted_iota(jnp.int32, (n_rows, tk), 0) + rows.start) & (tile - 1)) \
                + qi * tile
            colp = lax.broadcasted_iota(jnp.int32, (n_rows, tk), 1) + key_start(j)
            s = jnp.where(colp <= rowp, s, NEG_BIG)
        m_old = m_sc[rows, :]
        m_new = jnp.maximum(m_old, jnp.max(s, axis=1, keepdims=True))
        p_bufs[slot][rows, :] = jnp.exp2(s - jnp.concatenate([m_new] * reps, axis=1)).astype(BF16)
        a_bufs[slot][rows, :] = jnp.exp2(m_old - m_new)
        m_sc[rows, :] = m_new

    def values(j, slot, rows=every):
        v_aug = jnp.concatenate([v_ref[pl.ds(key_start(j), tk), :], ones], axis=1)
        alpha = a_bufs[slot][rows, :]
        acc_sc[rows, :] = (jnp.concatenate([alpha, alpha], axis=1) * acc_sc[rows, :]
                           + _dot(p_bufs[slot][rows, :], v_aug))

    late = (slice(tile // 2, tile), slice(tile + tile // 2, 2 * tile))

    def first(stage, *args):
        for rows in late:
            stage(0, 0, *args, rows)

    def softmax_frozen(slot, rows=every):
        m_rep = jnp.concatenate([m_sc[rows, :]] * reps, axis=1)
        p_bufs[slot][rows, :] = jnp.exp2(s_bufs[slot][rows, :] - m_rep).astype(BF16)

    def values_frozen(j, slot, rows=every):
        v_aug = jnp.concatenate([v_ref[pl.ds(key_start(j), tk), :], ones], axis=1)
        acc_sc[rows, :] += _dot(p_bufs[slot][rows, :], v_aug)

    first(scores)
    scores(1, 1)
    first(softmax, True)

    @pl.when(qi == 0)
    def _():
        softmax(1, 1, True)
        first(values)
        values(1, 1)

    @pl.when(qi > 0)
    def _():
        scores(2, 0)
        softmax(1, 1, True)
        first(values)
        scores(3, 1)
        softmax(2, 0, False)
        values(1, 1)
        qn = max_half_norms(q)
        qk_max = jnp.maximum(qn[0] * kn_sc[0:1, 0:1], qn[1] * kn_sc[4:5, 0:1])
        m_min = jnp.min(m_sc[...], axis=0, keepdims=True)[:, 0:1]
        reach = (qk_max - m_min + DA_DEAD_LOG2) / slope
        first_dead = jnp.floor((reach - 1.0) / tk) + 3.0
        first_dead = jnp.max(jnp.clip(first_dead, 0.0, 1e6)).astype(jnp.int32)
        pairs_end = jnp.maximum(2, jnp.minimum(qi + 1, (first_dead + 1) // 2))

        freeze = jnp.max(jnp.where(qk_max - m_min <= DA_FREEZE_LOG2, 1.0, 0.0)) > 0.5
        freeze = jnp.logical_and(freeze, pairs_end > 2)

        @pl.when(jnp.logical_not(freeze))
        def _():
            def body(i, carry):
                t = 2 * i
                scores(t, 0)
                softmax(t - 1, 1, False)
                values(t - 2, 0)
                scores(t + 1, 1)
                softmax(t, 0, False)
                values(t - 1, 1)
                return carry

            lax.fori_loop(2, pairs_end, body, 0)
            t = 2 * pairs_end
            softmax(t - 1, 1, False)
            values(t - 2, 0)
            values(t - 1, 1)

        @pl.when(freeze)
        def _():
            scores(4, 0)
            softmax_frozen(1)
            values(2, 0)
            scores(5, 1)
            softmax_frozen(0)
            values_frozen(3, 1)

            def body(i, carry):
                t = 2 * i
                scores(t, 0)
                softmax_frozen(1)
                values_frozen(t - 2, 0)
                scores(t + 1, 1)
                softmax_frozen(0)
                values_frozen(t - 1, 1)
                return carry

            lax.fori_loop(3, pairs_end, body, 0)
            t = 2 * pairs_end
            softmax_frozen(1)
            values_frozen(t - 2, 0)
            values_frozen(t - 1, 1)

    lv = lam_ref[...].astype(F32)
    dots = jnp.sum(lv[0:1, :] * lv[1:2, :], axis=1, keepdims=True)
    dots2 = jnp.sum(lv[2:3, :] * lv[3:4, :], axis=1, keepdims=True)
    lam = jnp.exp(dots) - jnp.exp(dots2) + lam_init
    acc = acc_sc[...]
    o0 = acc[0:tile, 0:hd] / acc[0:tile, hd:2 * hd]
    o1 = acc[tile:2 * tile, 0:hd] / acc[tile:2 * tile, hd:2 * hd]
    o = o0 - lam * o1
    ms = jnp.mean(o * o, axis=1, keepdims=True)
    o = o * lax.rsqrt(ms + 1e-6) * gain_ref[...] * (1.0 - lam_init)
    o_ref[...] = o.astype(o_ref.dtype)


def _diff_attention(proj, diff_lambda, diff_gain, lam_init):
    bsz, t, _ = proj.shape
    tile = min(TQ_ATT, t)
    hd = 2 * DA_HEAD_DIM
    kern = functools.partial(_diffattn_kernel, tile=tile, lam_init=lam_init)
    return pl.pallas_call(
        kern,
        grid=(bsz, DA_HEADS, t // tile),
        in_specs=[
            pl.BlockSpec((None, tile, hd), lambda b, h, i: (b, i, h)),
            pl.BlockSpec((None, t, hd), lambda b, h, i: (b, 0, DA_HEADS + h)),
            pl.BlockSpec((None, t, hd), lambda b, h, i: (b, 0, 2 * DA_HEADS + h)),
            pl.BlockSpec((4, DA_HEAD_DIM), lambda b, h, i: (0, 0)),
            pl.BlockSpec((1, hd), lambda b, h, i: (0, 0)),
        ],
        out_specs=pl.BlockSpec((None, tile, hd), lambda b, h, i: (b, i, h)),
        out_shape=jax.ShapeDtypeStruct((bsz, t, DA_HEADS * hd), BF16),
        scratch_shapes=[
            pltpu.VMEM((2 * tile, LANES), F32),
            pltpu.VMEM((2 * tile, 2 * hd), F32),
            pltpu.VMEM((2 * tile, tile // 2), F32),
            pltpu.VMEM((2 * tile, tile // 2), F32),
            pltpu.VMEM((2 * tile, tile // 2), BF16),
            pltpu.VMEM((2 * tile, tile // 2), BF16),
            pltpu.VMEM((2 * tile, LANES), F32),
            pltpu.VMEM((2 * tile, LANES), F32),
            pltpu.VMEM((8, LANES), F32),
        ],
        compiler_params=_params(("arbitrary", "arbitrary", "arbitrary")),
        name="diff_attn",
    )(proj, proj, proj, diff_lambda, diff_gain.reshape(1, hd))


def _hgrn_kernel(q_ref, f_ref, i_ref, g_ref, gamma_ref, gain_ref, o_ref, st_sc, *, layer):
    @pl.when(pl.program_id(1) == 0)
    def _():
        st_sc[...] = jnp.zeros(st_sc.shape, F32)

    gam = gamma_ref[...].astype(F32)
    e = jnp.exp(gam - jnp.max(gam, axis=0, keepdims=True))
    sm = e / jnp.sum(e, axis=0, keepdims=True)
    lb_all = jnp.sum(sm[0:layer + 1, :], axis=0, keepdims=True)

    c = HG_CHUNK
    row = lax.broadcasted_iota(jnp.int32, (c, c), 0)
    col = lax.broadcasted_iota(jnp.int32, (c, c), 1)
    tril = col <= row
    tril_bf = jnp.where(tril, 1.0, 0.0).astype(BF16)
    gain = gain_ref[...]

    heads = range(HG_HEADS)
    hcols = [slice(h * HG_DK, (h + 1) * HG_DK) for h in heads]
    lbs = [lb_all[:, hc] for hc in hcols]
    for n in range(q_ref.shape[0] // c):
        rows = slice(n * c, (n + 1) * c)
        sig = [_sigmoid(f_ref[rows, hc].astype(F32)) for hc in hcols]
        logf = [jnp.log(lbs[h] + (1.0 - lbs[h]) * sig[h]) for h in heads]
        kk = [(1.0 - lbs[h]) * (1.0 - sig[h]) for h in heads]
        parts = [_split3(x) for x in logf]
        b = [_dot(tril_bf, p[0]) + _dot(tril_bf, p[1]) for p in parts]
        b_mid = [x[c // 2 - 1:c // 2, :] for x in b]
        b_last = [x[c - 1:c, :] for x in b]
        qh = [q_ref[rows, hc].astype(F32) for hc in hcols]
        qs = [x * _sigmoid(x) for x in qh]
        v = [i_ref[rows, hc] for hc in hcols]
        qa = [qs[h] * jnp.exp(b[h] - b_mid[h]) for h in heads]
        ka = [kk[h] * jnp.exp(b_mid[h] - b[h]) for h in heads]
        att = [_dot_nt(qa[h].astype(BF16), ka[h].astype(BF16)) for h in heads]
        att = [jnp.where(tril, x, 0.0).astype(BF16) for x in att]
        o_intra = [_dot(att[h], v[h]) for h in heads]
        kd = [(ka[h] * jnp.exp(b_last[h] - b_mid[h])).astype(BF16) for h in heads]
        ds_t = [_dot(v[h].T, kd[h]) for h in heads]
        st = [st_sc[h] for h in heads]
        o_inter = [_dot_nt((qa[h] * jnp.exp(b_mid[h])).astype(BF16), st[h].astype(BF16))
                   for h in heads]
        for h in heads:
            st_sc[h] = st[h] * jnp.exp(b_last[h]) + ds_t[h]
        for h in heads:
            o = o_intra[h] + o_inter[h]
            gh = g_ref[rows, hcols[h]].astype(F32)
            ms = jnp.mean(o * o, axis=1, keepdims=True)
            o = o * lax.rsqrt(ms + 1e-6) * gain * (gh * _sigmoid(gh))
            o_ref[rows, hcols[h]] = o.astype(o_ref.dtype)


def _hgrn2(proj, hgrn_gamma, hgrn_gain, layer):
    bsz, t, _ = proj.shape
    tt = min(T_HG, t)
    width = HG_HEADS * HG_DK
    base = 3 * DA_HEADS * 2 * DA_HEAD_DIM // width
    spec = lambda k: pl.BlockSpec((None, tt, width), lambda b, i: (b, i, base + k))
    return pl.pallas_call(
        functools.partial(_hgrn_kernel, layer=layer),
        grid=(bsz, t // tt),
        in_specs=[
            spec(0), spec(1), spec(2), spec(3),
            pl.BlockSpec((hgrn_gamma.shape[0], width), lambda b, i: (0, 0)),
            pl.BlockSpec((1, HG_DK), lambda b, i: (0, 0)),
        ],
        out_specs=pl.BlockSpec((None, tt, width), lambda b, i: (b, i, 0)),
        out_shape=jax.ShapeDtypeStruct((bsz, t, width), BF16),
        scratch_shapes=[pltpu.VMEM((HG_HEADS, HG_DK, HG_DK), F32)],
        compiler_params=_params(("arbitrary", "arbitrary")),
        name="hgrn2",
    )(proj, proj, proj, proj, hgrn_gamma, hgrn_gain.reshape(1, HG_DK))


def _route(logits_t):
    mx = jnp.max(logits_t, axis=0, keepdims=True)
    ex = jnp.exp(logits_t - mx)
    probs = ex / jnp.sum(ex, axis=0, keepdims=True)
    p = [probs[e:e + 1, :] for e in range(N_EXPERTS)]
    g = E_PER_GROUP
    scores = []
    for gi in range(N_GROUPS):
        pg = p[gi * g:(gi + 1) * g]
        best = None
        for a in range(g):
            for b in range(a + 1, g):
                pair = pg[a] + pg[b]
                best = pair if best is None else jnp.maximum(best, pair)
        scores.append(best)
    group_id = jnp.zeros_like(p[0])
    gates = [jnp.zeros_like(p[0]) for _ in range(g)]
    for gi in range(N_GROUPS):
        sel = None
        for gj in range(N_GROUPS):
            if gj == gi:
                continue
            cond = (scores[gi] > scores[gj]) if gj < gi else (scores[gi] >= scores[gj])
            sel = cond if sel is None else (sel & cond)
        group_id = jnp.where(sel, float(gi), group_id)
        pg = p[gi * g:(gi + 1) * g]
        chosen = []
        for a in range(g):
            rank = jnp.zeros_like(pg[a])
            for b in range(g):
                if b == a:
                    continue
                ahead = (pg[b] >= pg[a]) if b < a else (pg[b] > pg[a])
                rank = rank + jnp.where(ahead, 1.0, 0.0)
            chosen.append(sel & (rank < 2.0))
        denom = None
        for a in range(g):
            term = jnp.where(chosen[a], pg[a], 0.0)
            denom = term if denom is None else denom + term
        for a in range(g):
            gates[a] = jnp.where(chosen[a], pg[a] / denom, gates[a])
    return group_id, gates


def _outproj_kernel(a_ref, b_ref, x_ref, mod_ref, w_ref, lng_ref, lnb_ref, rwt_ref, rb_ref,
                    xo_ref, h_ref, row_ref, col_ref, *, alpha):
    half = a_ref.shape[1]
    tm = a_ref.shape[0]
    g1 = mod_ref[2:3, :]
    sh2 = mod_ref[3:4, :]
    sc2 = mod_ref[4:5, :]
    rw = rwt_ref[...]
    w_hi, w_mid, _ = _split3(rw)
    chunks = [slice(i * tm // ROW_CHUNKS, (i + 1) * tm // ROW_CHUNKS)
              for i in range(ROW_CHUNKS)]
    y = [_dot(a_ref[rs, :], w_ref[0:half, :]) + _dot(b_ref[rs, :], w_ref[half:2 * half, :])
         for rs in chunks]
    r = [alpha * x_ref[rs, :] + (1.0 + g1) * yy for rs, yy in zip(chunks, y)]
    rc = [rr - jnp.mean(rr, axis=1, keepdims=True) for rr in r]
    var = [jnp.mean(cc * cc, axis=1, keepdims=True) for cc in rc]
    xn = [cc * lax.rsqrt(vv + 1e-5) * lng_ref[...] + lnb_ref[...] for cc, vv in zip(rc, var)]
    h2 = [xx * (1.0 + sc2) + sh2 for xx in xn]
    for rs, xx, hh in zip(chunks, xn, h2):
        xo_ref[rs, :] = xx
        h_ref[rs, :] = hh.astype(BF16)
    split = [_split3(hh) for hh in h2]
    logits_t = jnp.concatenate(
        [_dot_nt(w_hi, s[0]) + _dot_nt(w_hi, s[1]) + _dot_nt(w_mid, s[0]) for s in split],
        axis=1) + rb_ref[...]
    group_id, gates = _route(logits_t)
    sel = [jnp.where(group_id == float(gi), 1.0, 0.0) for gi in range(N_GROUPS)]
    onehot = jnp.concatenate(sel + [jnp.zeros((8 - N_GROUPS, tm), F32)], axis=0).astype(BF16)
    src = lax.broadcasted_iota(jnp.int32, (tm, tm), 0)
    dst = lax.broadcasted_iota(jnp.int32, (tm, tm), 1)
    earlier = jnp.where(src < dst, 1.0, 0.0).astype(BF16)
    counts = _dot(onehot, earlier)
    rank = sel[0] * counts[0:1, :]
    for gi in range(1, N_GROUPS):
        rank = rank + sel[gi] * counts[gi:gi + 1, :]
    info = jnp.concatenate(gates + [group_id, rank], axis=0)
    row_ref[...] = jnp.concatenate(
        [group_id, rank, jnp.zeros((8 - 2, tm), F32)], axis=0)
    pad = jnp.zeros((LANES - info.shape[0], tm), F32)
    col_ref[...] = jnp.concatenate([info, pad], axis=0).T


def _outproj(a, b, x, mod_l, w_bf16, ln_g, ln_b, router_w, router_b, alpha):
    bsz, t, d = x.shape
    half = a.shape[2]
    tm = min(T_BLK, t)
    tok = lambda width: pl.BlockSpec((None, tm, width), lambda bi, i: (bi, i, 0))
    full = lambda r, c: pl.BlockSpec((r, c), lambda bi, i: (0, 0))
    return pl.pallas_call(
        functools.partial(_outproj_kernel, alpha=alpha),
        grid=(bsz, t // tm),
        in_specs=[
            tok(half), tok(half), tok(d),
            pl.BlockSpec((None, 6, d), lambda bi, i: (bi, 0, 0)),
            full(2 * half, d), full(1, d), full(1, d), full(N_EXPERTS, d), full(N_EXPERTS, 1),
        ],
        out_specs=[tok(d), tok(d), pl.BlockSpec((None, 8, tm), lambda bi, i: (bi, 0, i)),
                   tok(LANES)],
        out_shape=[
            jax.ShapeDtypeStruct((bsz, t, d), F32),
            jax.ShapeDtypeStruct((bsz, t, d), BF16),
            jax.ShapeDtypeStruct((bsz, 8, t), F32),
            jax.ShapeDtypeStruct((bsz, t, LANES), F32),
        ],
        compiler_params=_params(("arbitrary", "arbitrary")),
        name="outproj_ln_route",
    )(a, b, x, mod_l, w_bf16, ln_g.reshape(1, d), ln_b.reshape(1, d), router_w.T,
      router_b.reshape(N_EXPERTS, 1))


def _slab_rows(tm):
    extra = -(-(tm - MOE_ROWS_MAIN) // MOE_ROWS_EXTRA)
    return MOE_ROWS_MAIN + max(extra, 0) * MOE_ROWS_EXTRA


def _extra_chunks(count):
    return (jnp.maximum(count - MOE_ROWS_MAIN, 0) + MOE_ROWS_EXTRA - 1) // MOE_ROWS_EXTRA


def _moe_expert_kernel(cnt_ref, h_ref, row_ref, col_ref, wg_ref, wu_ref, wd_ref, zm_ref, zx_ref):
    g = pl.program_id(0)
    blk = pl.program_id(1)
    count = cnt_ref[g * pl.num_programs(1) + blk]
    mine = row_ref[0:1, :] == g.astype(F32)
    rank = row_ref[1:2, :]
    info = col_ref[...]
    info_hi = info.astype(BF16)
    info_lo = (info - info_hi.astype(F32)).astype(BF16)

    def run_rows(r0, m, out_ref, out_r0):
        rid = (lax.broadcasted_iota(jnp.int32, (m, 1), 0) + r0).astype(F32)
        pick = _onehot((rank == rid) & mine)
        xs = _dot(pick, h_ref[...]).astype(BF16)
        gm = _dot(pick, info_hi) + _dot(pick, info_lo)
        acc = None
        for j in range(E_PER_GROUP):
            a = _dot(xs, wg_ref[j])
            u = _dot(xs, wu_ref[j])
            he = (a * _sigmoid(a) * u * gm[:, j:j + 1]).astype(BF16)
            part = _dot(he, wd_ref[j])
            acc = part if acc is None else acc + part
        out_ref[pl.ds(out_r0, m), :] = acc.astype(out_ref.dtype)

    @pl.when(count <= MOE_ROWS_SMALL)
    def _():
        run_rows(0, MOE_ROWS_SMALL, zm_ref, 0)
        zm_ref[MOE_ROWS_SMALL:MOE_ROWS_MAIN, :] = jnp.zeros(
            (MOE_ROWS_MAIN - MOE_ROWS_SMALL, zm_ref.shape[1]), zm_ref.dtype)

    @pl.when(count > MOE_ROWS_SMALL)
    def _():
        run_rows(0, MOE_ROWS_MAIN, zm_ref, 0)

    zx_ref[...] = jnp.zeros(zx_ref.shape, zx_ref.dtype)

    def body(i, carry):
        off = pl.multiple_of(i * MOE_ROWS_EXTRA, 16)
        run_rows(MOE_ROWS_MAIN + off, MOE_ROWS_EXTRA, zx_ref, off)
        return carry

    lax.fori_loop(0, _extra_chunks(count), body, 0)


def _moe_combine_kernel(cnt_ref, xidx_ref, zm0_ref, zm1_ref, zm2_ref, zm3_ref, zx0_ref, zx1_ref,
                        zx2_ref, zx3_ref, col_ref, x_ref, mod_ref, lng_ref, lnb_ref, o_ref, y_sc,
                        *, alpha):
    del xidx_ref
    blk = pl.program_id(0) * pl.num_programs(1) + pl.program_id(1)
    n_blk = pl.num_programs(0) * pl.num_programs(1)
    zm_refs = (zm0_ref, zm1_ref, zm2_ref, zm3_ref)
    zx_refs = (zx0_ref, zx1_ref, zx2_ref, zx3_ref)
    main = MOE_ROWS_MAIN
    grp = col_ref[:, E_PER_GROUP:E_PER_GROUP + 1]
    rank = col_ref[:, E_PER_GROUP + 1:E_PER_GROUP + 2]
    y_sc[...] = jnp.zeros(y_sc.shape, F32)
    lane_x = lax.broadcasted_iota(jnp.int32, (1, MOE_ROWS_EXTRA), 1).astype(F32)
    for gi in range(N_GROUPS):
        def body(i, carry, gi=gi):
            off = pl.multiple_of(i * MOE_ROWS_EXTRA, 16)
            hit = (grp == float(gi)) & ((rank - (main + off).astype(F32)) == lane_x)
            y_sc[...] += _dot(_onehot(hit), zx_refs[gi][pl.ds(off, MOE_ROWS_EXTRA), :])
            return carry

        lax.fori_loop(0, _extra_chunks(cnt_ref[gi * n_blk + blk]), body, 0)

    where_to = jnp.where(rank < float(main), grp * float(main) + rank, -1.0)
    lane = lax.broadcasted_iota(jnp.int32, (1, N_GROUPS * main), 1).astype(F32)
    z_all = jnp.concatenate([zr[...] for zr in zm_refs], axis=0)
    g2 = mod_ref[5:6, :]
    tm = x_ref.shape[0]
    chunks = [slice(i * tm // ROW_CHUNKS, (i + 1) * tm // ROW_CHUNKS)
              for i in range(ROW_CHUNKS)]
    y = [y_sc[rs, :] + _dot(_onehot(where_to[rs, :] == lane), z_all) for rs in chunks]
    r = [alpha * x_ref[rs, :] + (1.0 + g2) * yy for rs, yy in zip(chunks, y)]
    rc = [rr - jnp.mean(rr, axis=1, keepdims=True) for rr in r]
    var = [jnp.mean(cc * cc, axis=1, keepdims=True) for cc in rc]
    for rs, cc, vv in zip(chunks, rc, var):
        o_ref[rs, :] = cc * lax.rsqrt(vv + 1e-5) * lng_ref[...] + lnb_ref[...]


def _moe(h2, rowinfo, colinfo, x, mod_l, layer, wg, wu, wd, ln_g, ln_b, alpha):
    bsz, t, d = x.shape
    tm = min(T_BLK, t)
    nb = t // tm
    n_blk = bsz * nb
    dff = wg.shape[3]
    slab = _slab_rows(tm)
    group_of = rowinfo[:, 0, :].reshape(1, n_blk, tm)
    counts = jnp.sum(group_of == jnp.arange(N_GROUPS, dtype=F32).reshape(N_GROUPS, 1, 1), axis=2)
    counts = counts.astype(jnp.int32)
    needed = jnp.where(counts > MOE_ROWS_MAIN, jnp.arange(n_blk, dtype=jnp.int32), 0)
    extra_block = lax.cummax(needed, axis=1).reshape(N_GROUPS * n_blk)
    counts = counts.reshape(N_GROUPS * n_blk)
    extra = slab - MOE_ROWS_MAIN

    z_main, z_extra = pl.pallas_call(
        _moe_expert_kernel,
        grid_spec=pltpu.PrefetchScalarGridSpec(
            num_scalar_prefetch=1,
            grid=(N_GROUPS, n_blk),
            in_specs=[
                pl.BlockSpec((None, tm, d), lambda g, i, c: (i // nb, i % nb, 0)),
                pl.BlockSpec((None, 8, tm), lambda g, i, c: (i // nb, 0, i % nb)),
                pl.BlockSpec((None, tm, LANES), lambda g, i, c: (i // nb, i % nb, 0)),
                pl.BlockSpec((None, E_PER_GROUP, d, dff), lambda g, i, c: (layer, g, 0, 0)),
                pl.BlockSpec((None, E_PER_GROUP, d, dff), lambda g, i, c: (layer, g, 0, 0)),
                pl.BlockSpec((None, E_PER_GROUP, dff, d), lambda g, i, c: (layer, g, 0, 0)),
            ],
            out_specs=[
                pl.BlockSpec((None, None, MOE_ROWS_MAIN, d), lambda g, i, c: (g, i, 0, 0)),
                pl.BlockSpec((None, None, extra, d), lambda g, i, c: (g, i, 0, 0)),
            ],
        ),
        out_shape=[
            jax.ShapeDtypeStruct((N_GROUPS, n_blk, MOE_ROWS_MAIN, d), BF16),
            jax.ShapeDtypeStruct((N_GROUPS, n_blk, extra, d), BF16),
        ],
        compiler_params=_params(("arbitrary", "arbitrary")),
        name="moe_experts",
    )(counts, h2, rowinfo, colinfo, wg, wu, wd)

    zm_spec = lambda gi: pl.BlockSpec((None, None, MOE_ROWS_MAIN, d),
                                      lambda b, i, c, xb: (gi, b * nb + i, 0, 0))
    zx_spec = lambda gi: pl.BlockSpec((None, None, extra, d),
                                      lambda b, i, c, xb: (gi, xb[gi * n_blk + b * nb + i], 0, 0))
    tok = lambda width: pl.BlockSpec((None, tm, width), lambda b, i, c, xb: (b, i, 0))
    return pl.pallas_call(
        functools.partial(_moe_combine_kernel, alpha=alpha),
        grid_spec=pltpu.PrefetchScalarGridSpec(
            num_scalar_prefetch=2,
            grid=(bsz, nb),
            in_specs=[
                zm_spec(0), zm_spec(1), zm_spec(2), zm_spec(3),
                zx_spec(0), zx_spec(1), zx_spec(2), zx_spec(3), tok(LANES), tok(d),
                pl.BlockSpec((None, 6, d), lambda b, i, c, xb: (b, 0, 0)),
                pl.BlockSpec((1, d), lambda b, i, c, xb: (0, 0)),
                pl.BlockSpec((1, d), lambda b, i, c, xb: (0, 0)),
            ],
            out_specs=tok(d),
            scratch_shapes=[pltpu.VMEM((tm, d), F32)],
        ),
        out_shape=jax.ShapeDtypeStruct((bsz, t, d), F32),
        compiler_params=_params(("arbitrary", "arbitrary")),
        name="moe_combine_ln",
    )(counts, extra_block, z_main, z_main, z_main, z_main, z_extra, z_extra, z_extra, z_extra,
      colinfo, x, mod_l, ln_g.reshape(1, d), ln_b.reshape(1, d))


def _lru_kernel(x_ref, g_ref, cw_ref, cb_ref, wa_ref, ba_ref, wx_ref, bx_ref, lam_ref,
                o_ref, xpad_sc, h_sc):
    tt = x_ref.shape[0]
    pad = 8

    @pl.when(pl.program_id(1) == 0)
    def _():
        xpad_sc[0:pad, :] = jnp.zeros((pad, xpad_sc.shape[1]), F32)
        h_sc[...] = jnp.zeros(h_sc.shape, F32)

    xpad_sc[pad:pad + tt, :] = x_ref[...].astype(F32)
    xc = cb_ref[...] + jnp.zeros((tt, x_ref.shape[1]), F32)
    for j in range(CONV_WIDTH):
        off = pad - (CONV_WIDTH - 1) + j
        xc = xc + cw_ref[j:j + 1, :] * xpad_sc[off:off + tt, :]
    xpad_sc[0:pad, :] = xpad_sc[tt:tt + pad, :]

    xb = xc.astype(BF16)
    r = _sigmoid(_dot(xb, wa_ref[...]) + ba_ref[...])
    i = _sigmoid(_dot(xb, wx_ref[...]) + bx_ref[...])
    lam = lam_ref[...].astype(F32)
    softplus_neg = jnp.maximum(-lam, 0.0) + jnp.log(1.0 + jnp.exp(-jnp.abs(lam)))
    log_a = -LRU_C * r * softplus_neg
    a = jnp.exp(log_a)
    gain_sq = jnp.maximum(1.0 - jnp.exp(2.0 * log_a), 1e-12)
    u = gain_sq * lax.rsqrt(gain_sq) * (i * xc)

    groups = (tt // SUBLANES, SUBLANES, a.shape[1])
    a = a.reshape(groups)
    u = u.reshape(groups)
    rowi = lax.broadcasted_iota(jnp.int32, (1, SUBLANES, 1), 1)
    d = 1
    while d < SUBLANES:
        a_sh = jnp.where(rowi >= d, pltpu.roll(a, d, 1), 1.0)
        u_sh = jnp.where(rowi >= d, pltpu.roll(u, d, 1), 0.0)
        u = u + a * u_sh
        a = a * a_sh
        d *= 2
    a = a.reshape(tt, groups[2])
    u = u.reshape(tt, groups[2])
    gr = g_ref[...].astype(F32)
    gelu = 0.5 * gr * (1.0 + jnp.tanh(0.7978845608028654 * (gr + 0.044715 * gr * gr * gr)))
    h_prev = h_sc[...]
    out = []
    for grp in range(tt // SUBLANES):
        rows = slice(grp * SUBLANES, (grp + 1) * SUBLANES)
        h_grp = u[rows, :] + a[rows, :] * h_prev
        out.append(gelu[rows, :] * h_grp)
        h_prev = h_grp[SUBLANES - 1:SUBLANES, :]
    h_sc[...] = h_prev
    o_ref[...] = jnp.concatenate(out, axis=0).astype(o_ref.dtype)


def _rg_lru(proj, conv_w, conv_b, wa_dense, ba, wx_dense, bx, lam):
    bsz, t, _ = proj.shape
    tt = min(T_LRU, t)
    w = LRU_WIDTH
    nblk = w // LANES
    row = lambda a: a.reshape(1, w)
    full = lambda r, c: pl.BlockSpec((r, c), lambda b, i: (0, 0))
    return pl.pallas_call(
        _lru_kernel,
        grid=(bsz, t // tt),
        in_specs=[
            pl.BlockSpec((None, tt, w), lambda b, i: (b, i, 0)),
            pl.BlockSpec((None, tt, w), lambda b, i: (b, i, 1)),
            full(CONV_WIDTH, w), full(1, w), full(w, w), full(1, w), full(w, w), full(1, w),
            full(1, w),
        ],
        out_specs=pl.BlockSpec((None, tt, w), lambda b, i: (b, i, 0)),
        out_shape=jax.ShapeDtypeStruct((bsz, t, w), BF16),
        scratch_shapes=[pltpu.VMEM((tt + SUBLANES, w), F32), pltpu.VMEM((1, w), F32)],
        compiler_params=_params(("arbitrary", "arbitrary")),
        name="rg_lru",
    )(proj, proj, conv_w, row(conv_b), wa_dense, row(ba), wx_dense, row(bx), row(lam))


def _sb_kernel(q_ref, k_ref, v_ref, o_ref, r_sc, acc_sc, *bufs, tile):
    qi = pl.program_id(2)
    d = SB_HEAD_DIM
    tk = tile // 2
    z_bufs, lb_bufs, l_bufs, w_bufs = (bufs[i * SB_SETS:(i + 1) * SB_SETS] for i in range(4))
    lane = lax.broadcasted_iota(jnp.int32, (1, 2 * d), 1)
    q = q_ref[...]
    zero = jnp.zeros_like(q)
    q2 = jnp.concatenate([jnp.where(lane < d, q, zero), jnp.where(lane >= d, q, zero)], axis=0)
    rj = lax.broadcasted_iota(jnp.int32, (tk, tk), 0)
    cs = lax.broadcasted_iota(jnp.int32, (tk, tk), 1)
    upper = jnp.where(rj > cs, 1.0, 0.0).astype(BF16)

    r_sc[...] = jnp.zeros(r_sc.shape, F32)
    acc_sc[...] = jnp.zeros(acc_sc.shape, F32)
    n_sub = 2 * qi + 2
    every = slice(0, 2 * tile)
    per_head = (slice(0, tile), slice(tile, 2 * tile))

    def key_start(j):
        return pl.multiple_of((n_sub - 1 - j) * tk, tk)

    def strict_mask(j, rows):
        n_rows = rows.stop - rows.start
        rowp = (lax.broadcasted_iota(jnp.int32, (n_rows, tk), 0) + rows.start) & (tile - 1)
        colp = lax.broadcasted_iota(jnp.int32, (n_rows, tk), 1) + (1 - j) * tk
        return colp < rowp

    def logits(j, b, rows=every):
        z_bufs[b][rows, :] = _dot_nt(q2[rows, :], k_ref[pl.ds(key_start(j), tk), :])

    def gates(j, b, rows=every, masked=False):
        z = z_bufs[b][rows, :]
        log_1m = jnp.log(1.0 + jnp.exp2(-jnp.abs(z))) * (-LOG2E) - jnp.maximum(z, 0.0)
        lb_bufs[b][rows, :] = z + log_1m
        if masked:
            log_1m = jnp.where(strict_mask(j, rows), log_1m, 0.0)
        l_bufs[b][rows, :] = log_1m.astype(BF16)

    def weights(j, b, rows=every, masked=False):
        log_1m = l_bufs[b][rows, :]
        after = _dot(log_1m, upper) + r_sc[rows, :]
        w = jnp.exp2(lb_bufs[b][rows, :] + after)
        if masked:
            w = jnp.where(strict_mask(j, rows), w, 0.0)
        w_bufs[b][rows, :] = w.astype(BF16)
        r_sc[rows, :] = after[:, 0:1] + log_1m[:, 0:1].astype(F32)

    def values(j, b, rows=every):
        acc_sc[rows, :] += _dot(w_bufs[b][rows, :], v_ref[pl.ds(key_start(j), tk), :])

    late = tuple(slice(r.start + tile // 2, r.stop) for r in per_head)
    early = tuple(slice(r.start, r.start + tile // 2) for r in per_head)

    def first(stage, **kw):
        for rows in late:
            stage(0, 0, rows, **kw)

    @pl.when(qi == 0)
    def _():
        first(logits)
        logits(1, 1)
        first(gates, masked=True)
        gates(1, 1, masked=True)
        first(weights, masked=True)
        weights(1, 1, masked=True)
        first(values)
        values(1, 1)

    @pl.when(qi > 0)
    def _():
        def alive(rows=every):
            return (jnp.max(r_sc[rows, :]) > -SB_DEAD_LOG2).astype(jnp.int32)

        def third(stage, rows_set):
            for rows in rows_set:
                stage(2, 2, rows)

        first(logits)
        logits(1, 1)
        first(gates, masked=True)
        third(logits, early)
        gates(1, 1, masked=True)
        first(weights, masked=True)
        third(gates, early)
        weights(1, 1, masked=True)
        first(values)
        third(weights, early)
        values(1, 1)
        third(values, early)

        @pl.when(jnp.maximum(alive(late[0]), alive(late[1])) > 0)
        def _():
            for stage in (logits, gates, weights, values):
                third(stage, late)

        def cond(carry):
            j, live = carry
            return (j < n_sub) & (live > 0)

        def body(carry):
            j, _ = carry
            for rows in per_head:
                logits(j, 0, rows)
            for rows in per_head:
                gates(j, 0, rows)
            for rows in per_head:
                weights(j, 0, rows)
            for rows in per_head:
                values(j, 0, rows)
            return j + 1, alive()

        lax.while_loop(cond, body, (jnp.int32(3), alive()))

    acc = acc_sc[...]
    o_ref[...] = jnp.where(lane < d, acc[0:tile, :], acc[tile:2 * tile, :]).astype(o_ref.dtype)


def _sb_attention(proj):
    bsz, t, _ = proj.shape
    tq = min(T_SB, t)
    pairs = SB_HEADS // 2
    wblk = 2 * SB_HEAD_DIM
    base = 2 * LRU_WIDTH // wblk
    return pl.pallas_call(
        functools.partial(_sb_kernel, tile=tq),
        grid=(bsz, pairs, t // tq),
        in_specs=[
            pl.BlockSpec((None, tq, wblk), lambda b, h, i: (b, i, base + h)),
            pl.BlockSpec((None, t, wblk), lambda b, h, i: (b, 0, base + pairs + h)),
            pl.BlockSpec((None, t, wblk), lambda b, h, i: (b, 0, base + 2 * pairs + h)),
        ],
        out_specs=pl.BlockSpec((None, tq, wblk), lambda b, h, i: (b, i, h)),
        out_shape=jax.ShapeDtypeStruct((bsz, t, SB_HEADS * SB_HEAD_DIM), BF16),
        scratch_shapes=[pltpu.VMEM((2 * tq, 1), F32), pltpu.VMEM((2 * tq, wblk), F32)]
        + [pltpu.VMEM((2 * tq, tq // 2), F32)] * (2 * SB_SETS)
        + [pltpu.VMEM((2 * tq, tq // 2), BF16)] * (2 * SB_SETS),
        compiler_params=_params(("arbitrary", "arbitrary", "arbitrary")),
        name="sb_attn",
    )(proj, proj, proj)


def _block_diag(w):
    g, n, _ = w.shape
    eye = jnp.eye(g, dtype=w.dtype)
    return (eye[:, None, :, None] * w[:, :, None, :]).reshape(g * n, g * n)


def kernel(x, c, ada_w, ada_b, ln_g, ln_b, even_w_in, even_w_out, diff_lambda, diff_gain, hgrn_gamma, hgrn_gain, odd_w_in, odd_w_out, conv_w, conv_b, lru_wa, lru_ba, lru_wx, lru_bx, lru_lambda, router_w, router_b, moe_w_gate, moe_w_up, moe_w_down):
    depth = ada_w.shape[0]
    bsz, t, d = x.shape
    alpha = (2.0 * depth) ** 0.25
    mod = _ada_mod(c, ada_w, ada_b).reshape(depth, bsz, 6, d)
    w_gate, w_up, w_down = (w.astype(BF16) for w in (moe_w_gate, moe_w_up, moe_w_down))
    for l in range(depth):
        j = l // 2
        mod_l = mod[l]
        if l % 2 == 0:
            lam_init = 0.8 - 0.6 * math.exp(-0.3 * l)
            proj = _inproj(x, mod_l, even_w_in[j].astype(BF16), q_chunk=0)
            mix_a = _diff_attention(proj, diff_lambda[j], diff_gain[j], lam_init)
            mix_b = _hgrn2(proj, hgrn_gamma, hgrn_gain[j], l)
            w_out = even_w_out[j]
        else:
            proj = _inproj(x, mod_l, odd_w_in[j].astype(BF16), q_chunk=2 * LRU_WIDTH // PROJ_CHUNK)
            mix_a = _rg_lru(proj, conv_w[j], conv_b[j], _block_diag(lru_wa[j]).astype(BF16),
                            lru_ba[j], _block_diag(lru_wx[j]).astype(BF16), lru_bx[j],
                            lru_lambda[j])
            mix_b = _sb_attention(proj)
            w_out = odd_w_out[j]
        x, h2, rowinfo, colinfo = _outproj(mix_a, mix_b, x, mod_l, w_out.astype(BF16),
                                           ln_g[l, 0], ln_b[l, 0], router_w, router_b, alpha)
        x = _moe(h2, rowinfo, colinfo, x, mod_l, l, w_gate, w_up, w_down, ln_g[l, 1], ln_b[l, 1],
                 alpha)
    return x
```

```python
import functools
import math

import jax
import jax.numpy as jnp
from jax import lax
from jax.experimental import pallas as pl
from jax.experimental.pallas import tpu as pltpu

F32 = jnp.float32
BF16 = jnp.bfloat16

DA_HEADS = 4
DA_HEAD_DIM = 64
HG_HEADS = 4
HG_DK = 128
HG_CHUNK = 64
LRU_WIDTH = 512
CONV_WIDTH = 4
LRU_C = 8.0
SB_HEADS = 8
SB_HEAD_DIM = 64
N_EXPERTS = 16
N_GROUPS = 4
E_PER_GROUP = N_EXPERTS // N_GROUPS

LANES = 128
SUBLANES = 8
NEG_BIG = -1e30
LOG2E = 1.4426950408889634
Q_PRESCALE = DA_HEAD_DIM ** -0.5 * LOG2E
PROJ_CHUNK = 512
VMEM_LIMIT = 56 * 1024 * 1024

TM_PROJ = 1024
TQ_ATT = 512
T_SB = 512
SB_SETS = 3
SB_DEAD_LOG2 = 160.0
DA_DEAD_LOG2 = 152.0
DA_FREEZE_LOG2 = 64.0
T_HG = 512
T_LRU = 512
T_BLK = 1024
ROW_CHUNKS = 8
MOE_ROWS_MAIN = 320
MOE_ROWS_SMALL = 256
MOE_ROWS_EXTRA = 128


def _params(sem):
    return pltpu.CompilerParams(dimension_semantics=sem, vmem_limit_bytes=VMEM_LIMIT)


def _sigmoid(x):
    return 0.5 * jnp.tanh(0.5 * x) + 0.5


def _dot(a, b):
    return jnp.dot(a, b, preferred_element_type=F32)


def _dot_nt(a, b):
    return lax.dot_general(a, b, (((1,), (1,)), ((), ())), preferred_element_type=F32)


def _onehot(mask):
    return jnp.where(mask, 1.0, 0.0).astype(BF16)


def _split3(x):
    hi = x.astype(BF16)
    r1 = x - hi.astype(F32)
    mid = r1.astype(BF16)
    lo = (r1 - mid.astype(F32)).astype(BF16)
    return hi, mid, lo


def _ada_kernel(c_ref, w_ref, b_ref, o_ref):
    c = c_ref[...]
    cond = c * _sigmoid(c)
    hi, mid, _ = _split3(cond)
    w = w_ref[...].astype(BF16)
    o_ref[...] = _dot(hi, w) + _dot(mid, w) + b_ref[...]


def _ada_mod(c, ada_w, ada_b):
    depth, d, d6 = ada_w.shape
    bsz = c.shape[0]
    n_col = d6 // d
    return pl.pallas_call(
        _ada_kernel,
        grid=(depth, n_col),
        in_specs=[
            pl.BlockSpec((bsz, d), lambda l, j: (0, 0)),
            pl.BlockSpec((None, d, d), lambda l, j: (l, 0, j)),
            pl.BlockSpec((None, 1, d), lambda l, j: (l, 0, j)),
        ],
        out_specs=pl.BlockSpec((None, bsz, d), lambda l, j: (l, 0, j)),
        out_shape=jax.ShapeDtypeStruct((depth, bsz, d6), F32),
        compiler_params=_params(("arbitrary", "arbitrary")),
        name="ada_mod",
    )(c, ada_w, ada_b.reshape(depth, 1, d6))


def _inproj_kernel(x_ref, mod_ref, w_ref, o_ref, *, col_chunk, q_chunk):
    sh = mod_ref[0:1, :]
    sc = mod_ref[1:2, :]
    h = (x_ref[...] * (1.0 + sc) + sh).astype(BF16)
    for j in range(o_ref.shape[1] // col_chunk):
        cols = slice(j * col_chunk, (j + 1) * col_chunk)
        y = _dot(h, w_ref[:, cols])
        if j == q_chunk:
            y = y * Q_PRESCALE
        o_ref[:, cols] = y.astype(o_ref.dtype)


def _inproj(x, mod_l, w_bf16, q_chunk):
    bsz, t, d = x.shape
    width = w_bf16.shape[1]
    tm = min(TM_PROJ, t)
    return pl.pallas_call(
        functools.partial(_inproj_kernel, col_chunk=PROJ_CHUNK, q_chunk=q_chunk),
        grid=(bsz, t // tm),
        in_specs=[
            pl.BlockSpec((None, tm, d), lambda b, i: (b, i, 0)),
            pl.BlockSpec((None, 6, d), lambda b, i: (b, 0, 0)),
            pl.BlockSpec((d, width), lambda b, i: (0, 0)),
        ],
        out_specs=pl.BlockSpec((None, tm, width), lambda b, i: (b, i, 0)),
        out_shape=jax.ShapeDtypeStruct((bsz, t, width), BF16),
        compiler_params=_params(("arbitrary", "arbitrary")),
        name="inproj",
    )(x, mod_l, w_bf16)


def _diffattn_kernel(q_ref, k_ref, v_ref, lam_ref, gain_ref, o_ref, m_sc, acc_sc, s0_sc, s1_sc,
                     p0_sc, p1_sc, a0_sc, a1_sc, kn_sc, *, tile, lam_init):
    h = pl.program_id(1)
    qi = pl.program_id(2)
    dh = DA_HEAD_DIM
    hd = 2 * dh
    tk = tile // 2
    reps = tk // LANES
    s_bufs, p_bufs, a_bufs = (s0_sc, s1_sc), (p0_sc, p1_sc), (a0_sc, a1_sc)

    lane = lax.broadcasted_iota(jnp.int32, (1, hd), 1)
    q = q_ref[...]
    zero = jnp.zeros_like(q)
    q2 = jnp.concatenate([jnp.where(lane < dh, q, zero), jnp.where(lane >= dh, q, zero)], axis=0)

    hf = jnp.full((1, 1), h + 1, jnp.int32).astype(F32)
    slope = jnp.exp2(hf * (-8.0 / DA_HEADS)) * LOG2E
    col = lax.broadcasted_iota(jnp.int32, (1, tk), 1)
    ones = jnp.ones((tk, hd), BF16)

    m_sc[...] = jnp.full(m_sc.shape, NEG_BIG, F32)
    acc_sc[...] = jnp.zeros(acc_sc.shape, F32)

    def max_half_norms(x):
        xf = x.astype(F32)
        sq = xf * xf
        out = []
        for keep in (lane < dh, lane >= dh):
            rows = jnp.sum(jnp.where(keep, sq, 0.0), axis=1, keepdims=True)
            out.append(jnp.sqrt(jnp.max(rows, axis=0, keepdims=True)))
        return out

    @pl.when(qi == 0)
    def _():
        kn_sc[...] = jnp.concatenate([jnp.broadcast_to(n, (4, LANES))
                                      for n in max_half_norms(k_ref[...])], axis=0)

    n_sub = 2 * qi + 2

    def key_start(j):
        return pl.multiple_of((n_sub - 1 - j) * tk, tk)

    every = slice(0, 2 * tile)

    def scores(j, slot, rows=every):
        ks = key_start(j)
        bias = (col + (ks - qi * tile)).astype(F32) * slope
        s_bufs[slot][rows, :] = _dot_nt(q2[rows, :], k_ref[pl.ds(ks, tk), :]) + bias

    def softmax(j, slot, masked, rows=every):
        s = s_bufs[slot][rows, :]
        if masked:
            n_rows = rows.stop - rows.start
            rowp = ((lax.broadcasted_iota(jnp.int32, (n_rows, tk), 0) + rows.start) & (tile - 1)) \
                + qi * tile
            colp = lax.broadcasted_iota(jnp.int32, (n_rows, tk), 1) + key_start(j)
            s = jnp.where(colp <= rowp, s, NEG_BIG)
        m_old = m_sc[rows, :]
        m_new = jnp.maximum(m_old, jnp.max(s, axis=1, keepdims=True))
        p_bufs[slot][rows, :] = jnp.exp2(s - jnp.concatenate([m_new] * reps, axis=1)).astype(BF16)
        a_bufs[slot][rows, :] = jnp.exp2(m_old - m_new)
        m_sc[rows, :] = m_new

    def values(j, slot, rows=every):
        v_aug = jnp.concatenate([v_ref[pl.ds(key_start(j), tk), :], ones], axis=1)
        alpha = a_bufs[slot][rows, :]
        acc_sc[rows, :] = (jnp.concatenate([alpha, alpha], axis=1) * acc_sc[rows, :]
                           + _dot(p_bufs[slot][rows, :], v_aug))

    late = (slice(tile // 2, tile), slice(tile + tile // 2, 2 * tile))

    def first(stage, *args):
        for rows in late:
            stage(0, 0, *args, rows)

    def softmax_frozen(slot, rows=every):
        m_rep = jnp.concatenate([m_sc[rows, :]] * reps, axis=1)
        p_bufs[slot][rows, :] = jnp.exp2(s_bufs[slot][rows, :] - m_rep).astype(BF16)

    def values_frozen(j, slot, rows=every):
        v_aug = jnp.concatenate([v_ref[pl.ds(key_start(j), tk), :], ones], axis=1)
        acc_sc[rows, :] += _dot(p_bufs[slot][rows, :], v_aug)

    first(scores)
    scores(1, 1)
    first(softmax, True)

    @pl.when(qi == 0)
    def _():
        softmax(1, 1, True)
        first(values)
        values(1, 1)

    @pl.when(qi > 0)
    def _():
        scores(2, 0)
        softmax(1, 1, True)
        first(values)
        scores(3, 1)
        softmax(2, 0, False)
        values(1, 1)
        qn = max_half_norms(q)
        qk_max = jnp.maximum(qn[0] * kn_sc[0:1, 0:1], qn[1] * kn_sc[4:5, 0:1])
        m_min = jnp.min(m_sc[...], axis=0, keepdims=True)[:, 0:1]
        reach = (qk_max - m_min + DA_DEAD_LOG2) / slope
        first_dead = jnp.floor((reach - 1.0) / tk) + 3.0
        first_dead = jnp.max(jnp.clip(first_dead, 0.0, 1e6)).astype(jnp.int32)
        pairs_end = jnp.maximum(2, jnp.minimum(qi + 1, (first_dead + 1) // 2))

        freeze = jnp.max(jnp.where(qk_max - m_min <= DA_FREEZE_LOG2, 1.0, 0.0)) > 0.5
        freeze = jnp.logical_and(freeze, pairs_end > 2)

        @pl.when(jnp.logical_not(freeze))
        def _():
            def body(i, carry):
                t = 2 * i
                scores(t, 0)
                softmax(t - 1, 1, False)
                values(t - 2, 0)
                scores(t + 1, 1)
                softmax(t, 0, False)
                values(t - 1, 1)
                return carry

            lax.fori_loop(2, pairs_end, body, 0)
            t = 2 * pairs_end
            softmax(t - 1, 1, False)
            values(t - 2, 0)
            values(t - 1, 1)

        @pl.when(freeze)
        def _():
            scores(4, 0)
            softmax_frozen(1)
            values(2, 0)
            scores(5, 1)
            softmax_frozen(0)
            values_frozen(3, 1)

            def body(i, carry):
                t = 2 * i
                scores(t, 0)
                softmax_frozen(1)
                values_frozen(t - 2, 0)
                scores(t + 1, 1)
                softmax_frozen(0)
                values_frozen(t - 1, 1)
                return carry

            lax.fori_loop(3, pairs_end, body, 0)
            t = 2 * pairs_end
            softmax_frozen(1)
            values_frozen(t - 2, 0)
            values_frozen(t - 1, 1)

    lv = lam_ref[...].astype(F32)
    dots = jnp.sum(lv[0:1, :] * lv[1:2, :], axis=1, keepdims=True)
    dots2 = jnp.sum(lv[2:3, :] * lv[3:4, :], axis=1, keepdims=True)
    lam = jnp.exp(dots) - jnp.exp(dots2) + lam_init
    acc = acc_sc[...]
    o0 = acc[0:tile, 0:hd] / acc[0:tile, hd:2 * hd]
    o1 = acc[tile:2 * tile, 0:hd] / acc[tile:2 * tile, hd:2 * hd]
    o = o0 - lam * o1
    ms = jnp.mean(o * o, axis=1, keepdims=True)
    o = o * lax.rsqrt(ms + 1e-6) * gain_ref[...] * (1.0 - lam_init)
    o_ref[...] = o.astype(o_ref.dtype)


def _diff_attention(proj, diff_lambda, diff_gain, lam_init):
    bsz, t, _ = proj.shape
    tile = min(TQ_ATT, t)
    hd = 2 * DA_HEAD_DIM
    kern = functools.partial(_diffattn_kernel, tile=tile, lam_init=lam_init)
    return pl.pallas_call(
        kern,
        grid=(bsz, DA_HEADS, t // tile),
        in_specs=[
            pl.BlockSpec((None, tile, hd), lambda b, h, i: (b, i, h)),
            pl.BlockSpec((None, t, hd), lambda b, h, i: (b, 0, DA_HEADS + h)),
            pl.BlockSpec((None, t, hd), lambda b, h, i: (b, 0, 2 * DA_HEADS + h)),
            pl.BlockSpec((4, DA_HEAD_DIM), lambda b, h, i: (0, 0)),
            pl.BlockSpec((1, hd), lambda b, h, i: (0, 0)),
        ],
        out_specs=pl.BlockSpec((None, tile, hd), lambda b, h, i: (b, i, h)),
        out_shape=jax.ShapeDtypeStruct((bsz, t, DA_HEADS * hd), BF16),
        scratch_shapes=[
            pltpu.VMEM((2 * tile, LANES), F32),
            pltpu.VMEM((2 * tile, 2 * hd), F32),
            pltpu.VMEM((2 * tile, tile // 2), F32),
            pltpu.VMEM((2 * tile, tile // 2), F32),
            pltpu.VMEM((2 * tile, tile // 2), BF16),
            pltpu.VMEM((2 * tile, tile // 2), BF16),
            pltpu.VMEM((2 * tile, LANES), F32),
            pltpu.VMEM((2 * tile, LANES), F32),
            pltpu.VMEM((8, LANES), F32),
        ],
        compiler_params=_params(("arbitrary", "arbitrary", "arbitrary")),
        name="diff_attn",
    )(proj, proj, proj, diff_lambda, diff_gain.reshape(1, hd))


def _hgrn_kernel(q_ref, f_ref, i_ref, g_ref, gamma_ref, gain_ref, o_ref, st_sc, *, layer):
    @pl.when(pl.program_id(1) == 0)
    def _():
        st_sc[...] = jnp.zeros(st_sc.shape, F32)

    gam = gamma_ref[...].astype(F32)
    e = jnp.exp(gam - jnp.max(gam, axis=0, keepdims=True))
    sm = e / jnp.sum(e, axis=0, keepdims=True)
    lb_all = jnp.sum(sm[0:layer + 1, :], axis=0, keepdims=True)

    c = HG_CHUNK
    row = lax.broadcasted_iota(jnp.int32, (c, c), 0)
    col = lax.broadcasted_iota(jnp.int32, (c, c), 1)
    tril = col <= row
    tril_bf = jnp.where(tril, 1.0, 0.0).astype(BF16)
    gain = gain_ref[...]

    heads = range(HG_HEADS)
    hcols = [slice(h * HG_DK, (h + 1) * HG_DK) for h in heads]
    lbs = [lb_all[:, hc] for hc in hcols]
    for n in range(q_ref.shape[0] // c):
        rows = slice(n * c, (n + 1) * c)
        sig = [_sigmoid(f_ref[rows, hc].astype(F32)) for hc in hcols]
        logf = [jnp.log(lbs[h] + (1.0 - lbs[h]) * sig[h]) for h in heads]
        kk = [(1.0 - lbs[h]) * (1.0 - sig[h]) for h in heads]
        parts = [_split3(x) for x in logf]
        b = [_dot(tril_bf, p[0]) + _dot(tril_bf, p[1]) for p in parts]
        b_mid = [x[c // 2 - 1:c // 2, :] for x in b]
        b_last = [x[c - 1:c, :] for x in b]
        qh = [q_ref[rows, hc].astype(F32) for hc in hcols]
        qs = [x * _sigmoid(x) for x in qh]
        v = [i_ref[rows, hc] for hc in hcols]
        qa = [qs[h] * jnp.exp(b[h] - b_mid[h]) for h in heads]
        ka = [kk[h] * jnp.exp(b_mid[h] - b[h]) for h in heads]
        att = [_dot_nt(qa[h].astype(BF16), ka[h].astype(BF16)) for h in heads]
        att = [jnp.where(tril, x, 0.0).astype(BF16) for x in att]
        o_intra = [_dot(att[h], v[h]) for h in heads]
        kd = [(ka[h] * jnp.exp(b_last[h] - b_mid[h])).astype(BF16) for h in heads]
        ds_t = [_dot(v[h].T, kd[h]) for h in heads]
        st = [st_sc[h] for h in heads]
        o_inter = [_dot_nt((qa[h] * jnp.exp(b_mid[h])).astype(BF16), st[h].astype(BF16))
                   for h in heads]
        for h in heads:
            st_sc[h] = st[h] * jnp.exp(b_last[h]) + ds_t[h]
        for h in heads:
            o = o_intra[h] + o_inter[h]
            gh = g_ref[rows, hcols[h]].astype(F32)
            ms = jnp.mean(o * o, axis=1, keepdims=True)
            o = o * lax.rsqrt(ms + 1e-6) * gain * (gh * _sigmoid(gh))
            o_ref[rows, hcols[h]] = o.astype(o_ref.dtype)


def _hgrn2(proj, hgrn_gamma, hgrn_gain, layer):
    bsz, t, _ = proj.shape
    tt = min(T_HG, t)
    width = HG_HEADS * HG_DK
    base = 3 * DA_HEADS * 2 * DA_HEAD_DIM // width
    spec = lambda k: pl.BlockSpec((None, tt, width), lambda b, i: (b, i, base + k))
    return pl.pallas_call(
        functools.partial(_hgrn_kernel, layer=layer),
        grid=(bsz, t // tt),
        in_specs=[
            spec(0), spec(1), spec(2), spec(3),
            pl.BlockSpec((hgrn_gamma.shape[0], width), lambda b, i: (0, 0)),
            pl.BlockSpec((1, HG_DK), lambda b, i: (0, 0)),
        ],
        out_specs=pl.BlockSpec((None, tt, width), lambda b, i: (b, i, 0)),
        out_shape=jax.ShapeDtypeStruct((bsz, t, width), BF16),
        scratch_shapes=[pltpu.VMEM((HG_HEADS, HG_DK, HG_DK), F32)],
        compiler_params=_params(("arbitrary", "arbitrary")),
        name="hgrn2",
    )(proj, proj, proj, proj, hgrn_gamma, hgrn_gain.reshape(1, HG_DK))


def _route(logits_t):
    mx = jnp.max(logits_t, axis=0, keepdims=True)
    ex = jnp.exp(logits_t - mx)
    probs = ex / jnp.sum(ex, axis=0, keepdims=True)
    p = [probs[e:e + 1, :] for e in range(N_EXPERTS)]
    g = E_PER_GROUP
    scores = []
    for gi in range(N_GROUPS):
        pg = p[gi * g:(gi + 1) * g]
        best = None
        for a in range(g):
            for b in range(a + 1, g):
                pair = pg[a] + pg[b]
                best = pair if best is None else jnp.maximum(best, pair)
        scores.append(best)
    group_id = jnp.zeros_like(p[0])
    gates = [jnp.zeros_like(p[0]) for _ in range(g)]
    for gi in range(N_GROUPS):
        sel = None
        for gj in range(N_GROUPS):
            if gj == gi:
                continue
            cond = (scores[gi] > scores[gj]) if gj < gi else (scores[gi] >= scores[gj])
            sel = cond if sel is None else (sel & cond)
        group_id = jnp.where(sel, float(gi), group_id)
        pg = p[gi * g:(gi + 1) * g]
        chosen = []
        for a in range(g):
            rank = jnp.zeros_like(pg[a])
            for b in range(g):
                if b == a:
                    continue
                ahead = (pg[b] >= pg[a]) if b < a else (pg[b] > pg[a])
                rank = rank + jnp.where(ahead, 1.0, 0.0)
            chosen.append(sel & (rank < 2.0))
        denom = None
        for a in range(g):
            term = jnp.where(chosen[a], pg[a], 0.0)
            denom = term if denom is None else denom + term
        for a in range(g):
            gates[a] = jnp.where(chosen[a], pg[a] / denom, gates[a])
    return group_id, gates


def _outproj_kernel(a_ref, b_ref, x_ref, mod_ref, w_ref, lng_ref, lnb_ref, rwt_ref, rb_ref,
                    xo_ref, h_ref, row_ref, col_ref, *, alpha):
    half = a_ref.shape[1]
    tm = a_ref.shape[0]
    g1 = mod_ref[2:3, :]
    sh2 = mod_ref[3:4, :]
    sc2 = mod_ref[4:5, :]
    rw = rwt_ref[...]
    w_hi, w_mid, _ = _split3(rw)
    chunks = [slice(i * tm // ROW_CHUNKS, (i + 1) * tm // ROW_CHUNKS)
              for i in range(ROW_CHUNKS)]
    y = [_dot(a_ref[rs, :], w_ref[0:half, :]) + _dot(b_ref[rs, :], w_ref[half:2 * half, :])
         for rs in chunks]
    r = [alpha * x_ref[rs, :] + (1.0 + g1) * yy for rs, yy in zip(chunks, y)]
    rc = [rr - jnp.mean(rr, axis=1, keepdims=True) for rr in r]
    var = [jnp.mean(cc * cc, axis=1, keepdims=True) for cc in rc]
    xn = [cc * lax.rsqrt(vv + 1e-5) * lng_ref[...] + lnb_ref[...] for cc, vv in zip(rc, var)]
    h2 = [xx * (1.0 + sc2) + sh2 for xx in xn]
    for rs, xx, hh in zip(chunks, xn, h2):
        xo_ref[rs, :] = xx
        h_ref[rs, :] = hh.astype(BF16)
    split = [_split3(hh) for hh in h2]
    logits_t = jnp.concatenate(
        [_dot_nt(w_hi, s[0]) + _dot_nt(w_hi, s[1]) + _dot_nt(w_mid, s[0]) for s in split],
        axis=1) + rb_ref[...]
    group_id, gates = _route(logits_t)
    sel = [jnp.where(group_id == float(gi), 1.0, 0.0) for gi in range(N_GROUPS)]
    onehot = jnp.concatenate(sel + [jnp.zeros((8 - N_GROUPS, tm), F32)], axis=0).astype(BF16)
    src = lax.broadcasted_iota(jnp.int32, (tm, tm), 0)
    dst = lax.broadcasted_iota(jnp.int32, (tm, tm), 1)
    earlier = jnp.where(src < dst, 1.0, 0.0).astype(BF16)
    counts = _dot(onehot, earlier)
    rank = sel[0] * counts[0:1, :]
    for gi in range(1, N_GROUPS):
        rank = rank + sel[gi] * counts[gi:gi + 1, :]
    info = jnp.concatenate(gates + [group_id, rank], axis=0)
    row_ref[...] = jnp.concatenate(
        [group_id, rank, jnp.zeros((8 - 2, tm), F32)], axis=0)
    pad = jnp.zeros((LANES - info.shape[0], tm), F32)
    col_ref[...] = jnp.concatenate([info, pad], axis=0).T


def _outproj(a, b, x, mod_l, w_bf16, ln_g, ln_b, router_w, router_b, alpha):
    bsz, t, d = x.shape
    half = a.shape[2]
    tm = min(T_BLK, t)
    tok = lambda width: pl.BlockSpec((None, tm, width), lambda bi, i: (bi, i, 0))
    full = lambda r, c: pl.BlockSpec((r, c), lambda bi, i: (0, 0))
    return pl.pallas_call(
        functools.partial(_outproj_kernel, alpha=alpha),
        grid=(bsz, t // tm),
        in_specs=[
            tok(half), tok(half), tok(d),
            pl.BlockSpec((None, 6, d), lambda bi, i: (bi, 0, 0)),
            full(2 * half, d), full(1, d), full(1, d), full(N_EXPERTS, d), full(N_EXPERTS, 1),
        ],
        out_specs=[tok(d), tok(d), pl.BlockSpec((None, 8, tm), lambda bi, i: (bi, 0, i)),
                   tok(LANES)],
        out_shape=[
            jax.ShapeDtypeStruct((bsz, t, d), F32),
            jax.ShapeDtypeStruct((bsz, t, d), BF16),
            jax.ShapeDtypeStruct((bsz, 8, t), F32),
            jax.ShapeDtypeStruct((bsz, t, LANES), F32),
        ],
        compiler_params=_params(("arbitrary", "arbitrary")),
        name="outproj_ln_route",
    )(a, b, x, mod_l, w_bf16, ln_g.reshape(1, d), ln_b.reshape(1, d), router_w.T,
      router_b.reshape(N_EXPERTS, 1))


def _slab_rows(tm):
    extra = -(-(tm - MOE_ROWS_MAIN) // MOE_ROWS_EXTRA)
    return MOE_ROWS_MAIN + max(extra, 0) * MOE_ROWS_EXTRA


def _extra_chunks(count):
    return (jnp.maximum(count - MOE_ROWS_MAIN, 0) + MOE_ROWS_EXTRA - 1) // MOE_ROWS_EXTRA


def _moe_expert_kernel(cnt_ref, h_ref, row_ref, col_ref, wg_ref, wu_ref, wd_ref, zm_ref, zx_ref):
    g = pl.program_id(0)
    blk = pl.program_id(1)
    count = cnt_ref[g * pl.num_programs(1) + blk]
    mine = row_ref[0:1, :] == g.astype(F32)
    rank = row_ref[1:2, :]
    info = col_ref[...]
    info_hi = info.astype(BF16)
    info_lo = (info - info_hi.astype(F32)).astype(BF16)

    def run_rows(r0, m, out_ref, out_r0):
        rid = (lax.broadcasted_iota(jnp.int32, (m, 1), 0) + r0).astype(F32)
        pick = _onehot((rank == rid) & mine)
        xs = _dot(pick, h_ref[...]).astype(BF16)
        gm = _dot(pick, info_hi) + _dot(pick, info_lo)
        acc = None
        for j in range(E_PER_GROUP):
            a = _dot(xs, wg_ref[j])
            u = _dot(xs, wu_ref[j])
            he = (a * _sigmoid(a) * u * gm[:, j:j + 1]).astype(BF16)
            part = _dot(he, wd_ref[j])
            acc = part if acc is None else acc + part
        out_ref[pl.ds(out_r0, m), :] = acc.astype(out_ref.dtype)

    @pl.when(count <= MOE_ROWS_SMALL)
    def _():
        run_rows(0, MOE_ROWS_SMALL, zm_ref, 0)
        zm_ref[MOE_ROWS_SMALL:MOE_ROWS_MAIN, :] = jnp.zeros(
            (MOE_ROWS_MAIN - MOE_ROWS_SMALL, zm_ref.shape[1]), zm_ref.dtype)

    @pl.when(count > MOE_ROWS_SMALL)
    def _():
        run_rows(0, MOE_ROWS_MAIN, zm_ref, 0)

    zx_ref[...] = jnp.zeros(zx_ref.shape, zx_ref.dtype)

    def body(i, carry):
        off = pl.multiple_of(i * MOE_ROWS_EXTRA, 16)
        run_rows(MOE_ROWS_MAIN + off, MOE_ROWS_EXTRA, zx_ref, off)
        return carry

    lax.fori_loop(0, _extra_chunks(count), body, 0)


def _moe_combine_kernel(cnt_ref, xidx_ref, zm0_ref, zm1_ref, zm2_ref, zm3_ref, zx0_ref, zx1_ref,
                        zx2_ref, zx3_ref, col_ref, x_ref, mod_ref, lng_ref, lnb_ref, o_ref, y_sc,
                        *, alpha):
    del xidx_ref
    blk = pl.program_id(0) * pl.num_programs(1) + pl.program_id(1)
    n_blk = pl.num_programs(0) * pl.num_programs(1)
    zm_refs = (zm0_ref, zm1_ref, zm2_ref, zm3_ref)
    zx_refs = (zx0_ref, zx1_ref, zx2_ref, zx3_ref)
    main = MOE_ROWS_MAIN
    grp = col_ref[:, E_PER_GROUP:E_PER_GROUP + 1]
    rank = col_ref[:, E_PER_GROUP + 1:E_PER_GROUP + 2]
    y_sc[...] = jnp.zeros(y_sc.shape, F32)
    lane_x = lax.broadcasted_iota(jnp.int32, (1, MOE_ROWS_EXTRA), 1).astype(F32)
    for gi in range(N_GROUPS):
        def body(i, carry, gi=gi):
            off = pl.multiple_of(i * MOE_ROWS_EXTRA, 16)
            hit = (grp == float(gi)) & ((rank - (main + off).astype(F32)) == lane_x)
            y_sc[...] += _dot(_onehot(hit), zx_refs[gi][pl.ds(off, MOE_ROWS_EXTRA), :])
            return carry

        lax.fori_loop(0, _extra_chunks(cnt_ref[gi * n_blk + blk]), body, 0)

    where_to = jnp.where(rank < float(main), grp * float(main) + rank, -1.0)
    lane = lax.broadcasted_iota(jnp.int32, (1, N_GROUPS * main), 1).astype(F32)
    z_all = jnp.concatenate([zr[...] for zr in zm_refs], axis=0)
    g2 = mod_ref[5:6, :]
    tm = x_ref.shape[0]
    chunks = [slice(i * tm // ROW_CHUNKS, (i + 1) * tm // ROW_CHUNKS)
              for i in range(ROW_CHUNKS)]
    y = [y_sc[rs, :] + _dot(_onehot(where_to[rs, :] == lane), z_all) for rs in chunks]
    r = [alpha * x_ref[rs, :] + (1.0 + g2) * yy for rs, yy in zip(chunks, y)]
    rc = [rr - jnp.mean(rr, axis=1, keepdims=True) for rr in r]
    var = [jnp.mean(cc * cc, axis=1, keepdims=True) for cc in rc]
    for rs, cc, vv in zip(chunks, rc, var):
        o_ref[rs, :] = cc * lax.rsqrt(vv + 1e-5) * lng_ref[...] + lnb_ref[...]


def _moe(h2, rowinfo, colinfo, x, mod_l, layer, wg, wu, wd, ln_g, ln_b, alpha):
    bsz, t, d = x.shape
    tm = min(T_BLK, t)
    nb = t // tm
    n_blk = bsz * nb
    dff = wg.shape[3]
    slab = _slab_rows(tm)
    group_of = rowinfo[:, 0, :].reshape(1, n_blk, tm)
    counts = jnp.sum(group_of == jnp.arange(N_GROUPS, dtype=F32).reshape(N_GROUPS, 1, 1), axis=2)
    counts = counts.astype(jnp.int32)
    needed = jnp.where(counts > MOE_ROWS_MAIN, jnp.arange(n_blk, dtype=jnp.int32), 0)
    extra_block = lax.cummax(needed, axis=1).reshape(N_GROUPS * n_blk)
    counts = counts.reshape(N_GROUPS * n_blk)
    extra = slab - MOE_ROWS_MAIN

    z_main, z_extra = pl.pallas_call(
        _moe_expert_kernel,
        grid_spec=pltpu.PrefetchScalarGridSpec(
            num_scalar_prefetch=1,
            grid=(N_GROUPS, n_blk),
            in_specs=[
                pl.BlockSpec((None, tm, d), lambda g, i, c: (i // nb, i % nb, 0)),
                pl.BlockSpec((None, 8, tm), lambda g, i, c: (i // nb, 0, i % nb)),
                pl.BlockSpec((None, tm, LANES), lambda g, i, c: (i // nb, i % nb, 0)),
                pl.BlockSpec((None, E_PER_GROUP, d, dff), lambda g, i, c: (layer, g, 0, 0)),
                pl.BlockSpec((None, E_PER_GROUP, d, dff), lambda g, i, c: (layer, g, 0, 0)),
                pl.BlockSpec((None, E_PER_GROUP, dff, d), lambda g, i, c: (layer, g, 0, 0)),
            ],
            out_specs=[
                pl.BlockSpec((None, None, MOE_ROWS_MAIN, d), lambda g, i, c: (g, i, 0, 0)),
                pl.BlockSpec((None, None, extra, d), lambda g, i, c: (g, i, 0, 0)),
            ],
        ),
        out_shape=[
            jax.ShapeDtypeStruct((N_GROUPS, n_blk, MOE_ROWS_MAIN, d), BF16),
            jax.ShapeDtypeStruct((N_GROUPS, n_blk, extra, d), BF16),
        ],
        compiler_params=_params(("arbitrary", "arbitrary")),
        name="moe_experts",
    )(counts, h2, rowinfo, colinfo, wg, wu, wd)

    zm_spec = lambda gi: pl.BlockSpec((None, None, MOE_ROWS_MAIN, d),
                                      lambda b, i, c, xb: (gi, b * nb + i, 0, 0))
    zx_spec = lambda gi: pl.BlockSpec((None, None, extra, d),
                                      lambda b, i, c, xb: (gi, xb[gi * n_blk + b * nb + i], 0, 0))
    tok = lambda width: pl.BlockSpec((None, tm, width), lambda b, i, c, xb: (b, i, 0))
    return pl.pallas_call(
        functools.partial(_moe_combine_kernel, alpha=alpha),
        grid_spec=pltpu.PrefetchScalarGridSpec(
            num_scalar_prefetch=2,
            grid=(bsz, nb),
            in_specs=[
                zm_spec(0), zm_spec(1), zm_spec(2), zm_spec(3),
                zx_spec(0), zx_spec(1), zx_spec(2), zx_spec(3), tok(LANES), tok(d),
                pl.BlockSpec((None, 6, d), lambda b, i, c, xb: (b, 0, 0)),
                pl.BlockSpec((1, d), lambda b, i, c, xb: (0, 0)),
                pl.BlockSpec((1, d), lambda b, i, c, xb: (0, 0)),
            ],
            out_specs=tok(d),
            scratch_shapes=[pltpu.VMEM((tm, d), F32)],
        ),
        out_shape=jax.ShapeDtypeStruct((bsz, t, d), F32),
        compiler_params=_params(("arbitrary", "arbitrary")),
        name="moe_combine_ln",
    )(counts, extra_block, z_main, z_main, z_main, z_main, z_extra, z_extra, z_extra, z_extra,
      colinfo, x, mod_l, ln_g.reshape(1, d), ln_b.reshape(1, d))


def _lru_kernel(x_ref, g_ref, cw_ref, cb_ref, wa_ref, ba_ref, wx_ref, bx_ref, lam_ref,
                o_ref, xpad_sc, h_sc):
    tt = x_ref.shape[0]
    pad = 8

    @pl.when(pl.program_id(1) == 0)
    def _():
        xpad_sc[0:pad, :] = jnp.zeros((pad, xpad_sc.shape[1]), F32)
        h_sc[...] = jnp.zeros(h_sc.shape, F32)

    xpad_sc[pad:pad + tt, :] = x_ref[...].astype(F32)
    xc = cb_ref[...] + jnp.zeros((tt, x_ref.shape[1]), F32)
    for j in range(CONV_WIDTH):
        off = pad - (CONV_WIDTH - 1) + j
        xc = xc + cw_ref[j:j + 1, :] * xpad_sc[off:off + tt, :]
    xpad_sc[0:pad, :] = xpad_sc[tt:tt + pad, :]

    xb = xc.astype(BF16)
    r = _sigmoid(_dot(xb, wa_ref[...]) + ba_ref[...])
    i = _sigmoid(_dot(xb, wx_ref[...]) + bx_ref[...])
    lam = lam_ref[...].astype(F32)
    softplus_neg = jnp.maximum(-lam, 0.0) + jnp.log(1.0 + jnp.exp(-jnp.abs(lam)))
    log_a = -LRU_C * r * softplus_neg
    a = jnp.exp(log_a)
    gain_sq = jnp.maximum(1.0 - jnp.exp(2.0 * log_a), 1e-12)
    u = gain_sq * lax.rsqrt(gain_sq) * (i * xc)

    groups = (tt // SUBLANES, SUBLANES, a.shape[1])
    a = a.reshape(groups)
    u = u.reshape(groups)
    rowi = lax.broadcasted_iota(jnp.int32, (1, SUBLANES, 1), 1)
    d = 1
    while d < SUBLANES:
        a_sh = jnp.where(rowi >= d, pltpu.roll(a, d, 1), 1.0)
        u_sh = jnp.where(rowi >= d, pltpu.roll(u, d, 1), 0.0)
        u = u + a * u_sh
        a = a * a_sh
        d *= 2
    a = a.reshape(tt, groups[2])
    u = u.reshape(tt, groups[2])
    gr = g_ref[...].astype(F32)
    gelu = 0.5 * gr * (1.0 + jnp.tanh(0.7978845608028654 * (gr + 0.044715 * gr * gr * gr)))
    h_prev = h_sc[...]
    out = []
    for grp in range(tt // SUBLANES):
        rows = slice(grp * SUBLANES, (grp + 1) * SUBLANES)
        h_grp = u[rows, :] + a[rows, :] * h_prev
        out.append(gelu[rows, :] * h_grp)
        h_prev = h_grp[SUBLANES - 1:SUBLANES, :]
    h_sc[...] = h_prev
    o_ref[...] = jnp.concatenate(out, axis=0).astype(o_ref.dtype)


def _rg_lru(proj, conv_w, conv_b, wa_dense, ba, wx_dense, bx, lam):
    bsz, t, _ = proj.shape
    tt = min(T_LRU, t)
    w = LRU_WIDTH
    nblk = w // LANES
    row = lambda a: a.reshape(1, w)
    full = lambda r, c: pl.BlockSpec((r, c), lambda b, i: (0, 0))
    return pl.pallas_call(
        _lru_kernel,
        grid=(bsz, t // tt),
        in_specs=[
            pl.BlockSpec((None, tt, w), lambda b, i: (b, i, 0)),
            pl.BlockSpec((None, tt, w), lambda b, i: (b, i, 1)),
            full(CONV_WIDTH, w), full(1, w), full(w, w), full(1, w), full(w, w), full(1, w),
            full(1, w),
        ],
        out_specs=pl.BlockSpec((None, tt, w), lambda b, i: (b, i, 0)),
        out_shape=jax.ShapeDtypeStruct((bsz, t, w), BF16),
        scratch_shapes=[pltpu.VMEM((tt + SUBLANES, w), F32), pltpu.VMEM((1, w), F32)],
        compiler_params=_params(("arbitrary", "arbitrary")),
        name="rg_lru",
    )(proj, proj, conv_w, row(conv_b), wa_dense, row(ba), wx_dense, row(bx), row(lam))


def _sb_kernel(q_ref, k_ref, v_ref, o_ref, r_sc, acc_sc, *bufs, tile):
    qi = pl.program_id(2)
    d = SB_HEAD_DIM
    tk = tile // 2
    z_bufs, lb_bufs, l_bufs, w_bufs = (bufs[i * SB_SETS:(i + 1) * SB_SETS] for i in range(4))
    lane = lax.broadcasted_iota(jnp.int32, (1, 2 * d), 1)
    q = q_ref[...]
    zero = jnp.zeros_like(q)
    q2 = jnp.concatenate([jnp.where(lane < d, q, zero), jnp.where(lane >= d, q, zero)], axis=0)
    rj = lax.broadcasted_iota(jnp.int32, (tk, tk), 0)
    cs = lax.broadcasted_iota(jnp.int32, (tk, tk), 1)
    upper = jnp.where(rj > cs, 1.0, 0.0).astype(BF16)

    r_sc[...] = jnp.zeros(r_sc.shape, F32)
    acc_sc[...] = jnp.zeros(acc_sc.shape, F32)
    n_sub = 2 * qi + 2
    every = slice(0, 2 * tile)
    per_head = (slice(0, tile), slice(tile, 2 * tile))

    def key_start(j):
        return pl.multiple_of((n_sub - 1 - j) * tk, tk)

    def strict_mask(j, rows):
        n_rows = rows.stop - rows.start
        rowp = (lax.broadcasted_iota(jnp.int32, (n_rows, tk), 0) + rows.start) & (tile - 1)
        colp = lax.broadcasted_iota(jnp.int32, (n_rows, tk), 1) + (1 - j) * tk
        return colp < rowp

    def logits(j, b, rows=every):
        z_bufs[b][rows, :] = _dot_nt(q2[rows, :], k_ref[pl.ds(key_start(j), tk), :])

    def gates(j, b, rows=every, masked=False):
        z = z_bufs[b][rows, :]
        log_1m = jnp.log(1.0 + jnp.exp2(-jnp.abs(z))) * (-LOG2E) - jnp.maximum(z, 0.0)
        lb_bufs[b][rows, :] = z + log_1m
        if masked:
            log_1m = jnp.where(strict_mask(j, rows), log_1m, 0.0)
        l_bufs[b][rows, :] = log_1m.astype(BF16)

    def weights(j, b, rows=every, masked=False):
        log_1m = l_bufs[b][rows, :]
        after = _dot(log_1m, upper) + r_sc[rows, :]
        w = jnp.exp2(lb_bufs[b][rows, :] + after)
        if masked:
            w = jnp.where(strict_mask(j, rows), w, 0.0)
        w_bufs[b][rows, :] = w.astype(BF16)
        r_sc[rows, :] = after[:, 0:1] + log_1m[:, 0:1].astype(F32)

    def values(j, b, rows=every):
        acc_sc[rows, :] += _dot(w_bufs[b][rows, :], v_ref[pl.ds(key_start(j), tk), :])

    late = tuple(slice(r.start + tile // 2, r.stop) for r in per_head)
    early = tuple(slice(r.start, r.start + tile // 2) for r in per_head)

    def first(stage, **kw):
        for rows in late:
            stage(0, 0, rows, **kw)

    @pl.when(qi == 0)
    def _():
        first(logits)
        logits(1, 1)
        first(gates, masked=True)
        gates(1, 1, masked=True)
        first(weights, masked=True)
        weights(1, 1, masked=True)
        first(values)
        values(1, 1)

    @pl.when(qi > 0)
    def _():
        def alive(rows=every):
            return (jnp.max(r_sc[rows, :]) > -SB_DEAD_LOG2).astype(jnp.int32)

        def third(stage, rows_set):
            for rows in rows_set:
                stage(2, 2, rows)

        first(logits)
        logits(1, 1)
        first(gates, masked=True)
        third(logits, early)
        gates(1, 1, masked=True)
        first(weights, masked=True)
        third(gates, early)
        weights(1, 1, masked=True)
        first(values)
        third(weights, early)
        values(1, 1)
        third(values, early)

        @pl.when(jnp.maximum(alive(late[0]), alive(late[1])) > 0)
        def _():
            for stage in (logits, gates, weights, values):
                third(stage, late)

        def cond(carry):
            j, live = carry
            return (j < n_sub) & (live > 0)

        def body(carry):
            j, _ = carry
            for rows in per_head:
                logits(j, 0, rows)
            for rows in per_head:
                gates(j, 0, rows)
            for rows in per_head:
                weights(j, 0, rows)
            for rows in per_head:
                values(j, 0, rows)
            return j + 1, alive()

        lax.while_loop(cond, body, (jnp.int32(3), alive()))

    acc = acc_sc[...]
    o_ref[...] = jnp.where(lane < d, acc[0:tile, :], acc[tile:2 * tile, :]).astype(o_ref.dtype)


def _sb_attention(proj):
    bsz, t, _ = proj.shape
    tq = min(T_SB, t)
    pairs = SB_HEADS // 2
    wblk = 2 * SB_HEAD_DIM
    base = 2 * LRU_WIDTH // wblk
    return pl.pallas_call(
        functools.partial(_sb_kernel, tile=tq),
        grid=(bsz, pairs, t // tq),
        in_specs=[
            pl.BlockSpec((None, tq, wblk), lambda b, h, i: (b, i, base + h)),
            pl.BlockSpec((None, t, wblk), lambda b, h, i: (b, 0, base + pairs + h)),
            pl.BlockSpec((None, t, wblk), lambda b, h, i: (b, 0, base + 2 * pairs + h)),
        ],
        out_specs=pl.BlockSpec((None, tq, wblk), lambda b, h, i: (b, i, h)),
        out_shape=jax.ShapeDtypeStruct((bsz, t, SB_HEADS * SB_HEAD_DIM), BF16),
        scratch_shapes=[pltpu.VMEM((2 * tq, 1), F32), pltpu.VMEM((2 * tq, wblk), F32)]
        + [pltpu.VMEM((2 * tq, tq // 2), F32)] * (2 * SB_SETS)
        + [pltpu.VMEM((2 * tq, tq // 2), BF16)] * (2 * SB_SETS),
        compiler_params=_params(("arbitrary", "arbitrary", "arbitrary")),
        name="sb_attn",
    )(proj, proj, proj)


def _block_diag(w):
    g, n, _ = w.shape
    eye = jnp.eye(g, dtype=w.dtype)
    return (eye[:, None, :, None] * w[:, :, None, :]).reshape(g * n, g * n)


def kernel(x, c, ada_w, ada_b, ln_g, ln_b, even_w_in, even_w_out, diff_lambda, diff_gain, hgrn_gamma, hgrn_gain, odd_w_in, odd_w_out, conv_w, conv_b, lru_wa, lru_ba, lru_wx, lru_bx, lru_lambda, router_w, router_b, moe_w_gate, moe_w_up, moe_w_down):
    depth = ada_w.shape[0]
    bsz, t, d = x.shape
    alpha = (2.0 * depth) ** 0.25
    mod = _ada_mod(c, ada_w, ada_b).reshape(depth, bsz, 6, d)
    w_gate, w_up, w_down = (w.astype(BF16) for w in (moe_w_gate, moe_w_up, moe_w_down))
    for l in range(depth):
        j = l // 2
        mod_l = mod[l]
        if l % 2 == 0:
            lam_init = 0.8 - 0.6 * math.exp(-0.3 * l)
            proj = _inproj(x, mod_l, even_w_in[j].astype(BF16), q_chunk=0)
            mix_a = _diff_attention(proj, diff_lambda[j], diff_gain[j], lam_init)
            mix_b = _hgrn2(proj, hgrn_gamma, hgrn_gain[j], l)
            w_out = even_w_out[j]
        else:
            proj = _inproj(x, mod_l, odd_w_in[j].astype(BF16), q_chunk=2 * LRU_WIDTH // PROJ_CHUNK)
            mix_a = _rg_lru(proj, conv_w[j], conv_b[j], _block_diag(lru_wa[j]).astype(BF16),
                            lru_ba[j], _block_diag(lru_wx[j]).astype(BF16), lru_bx[j],
                            lru_lambda[j])
            mix_b = _sb_attention(proj)
            w_out = odd_w_out[j]
        x, h2, rowinfo, colinfo = _outproj(mix_a, mix_b, x, mod_l, w_out.astype(BF16),
                                           ln_g[l, 0], ln_b[l, 0], router_w, router_b, alpha)
        x = _moe(h2, rowinfo, colinfo, x, mod_l, l, w_gate, w_up, w_down, ln_g[l, 1], ln_b[l, 1],
                 alpha)
    return x
```

```python
import functools
import math

import jax
import jax.numpy as jnp
from jax import lax
from jax.experimental import pallas as pl
from jax.experimental.pallas import tpu as pltpu

F32 = jnp.float32
BF16 = jnp.bfloat16

DA_HEADS = 4
DA_HEAD_DIM = 64
HG_HEADS = 4
HG_DK = 128
HG_CHUNK = 64
LRU_WIDTH = 512
CONV_WIDTH = 4
LRU_C = 8.0
SB_HEADS = 8
SB_HEAD_DIM = 64
N_EXPERTS = 16
N_GROUPS = 4
E_PER_GROUP = N_EXPERTS // N_GROUPS

LANES = 128
SUBLANES = 8
NEG_BIG = -1e30
LOG2E = 1.4426950408889634
Q_PRESCALE = DA_HEAD_DIM ** -0.5 * LOG2E
PROJ_CHUNK = 512
VMEM_LIMIT = 56 * 1024 * 1024

TM_PROJ = 1024
TQ_ATT = 512
T_SB = 512
SB_SETS = 3
SB_DEAD_LOG2 = 160.0
DA_DEAD_LOG2 = 152.0
DA_FREEZE_LOG2 = 64.0
T_HG = 512
T_LRU = 256
T_BLK = 1024
ROW_CHUNKS = 8
MOE_ROWS_MAIN = 320
MOE_ROWS_SMALL = 256
MOE_ROWS_MID = 288
MOE_ROWS_EXTRA = 128


def _params(sem):
    return pltpu.CompilerParams(dimension_semantics=sem, vmem_limit_bytes=VMEM_LIMIT)


def _sigmoid(x):
    return 0.5 * jnp.tanh(0.5 * x) + 0.5


def _dot(a, b):
    return jnp.dot(a, b, preferred_element_type=F32)


def _dot_nt(a, b):
    return lax.dot_general(a, b, (((1,), (1,)), ((), ())), preferred_element_type=F32)


def _onehot(mask):
    return jnp.where(mask, 1.0, 0.0).astype(BF16)


def _split3(x):
    hi = x.astype(BF16)
    r1 = x - hi.astype(F32)
    mid = r1.astype(BF16)
    lo = (r1 - mid.astype(F32)).astype(BF16)
    return hi, mid, lo


def _ada_kernel(c_ref, w_ref, b_ref, o_ref):
    c = c_ref[...]
    cond = c * _sigmoid(c)
    hi, mid, _ = _split3(cond)
    w = w_ref[...].astype(BF16)
    o_ref[...] = _dot(hi, w) + _dot(mid, w) + b_ref[...]


def _ada_mod(c, ada_w, ada_b):
    depth, d, d6 = ada_w.shape
    bsz = c.shape[0]
    n_col = d6 // d
    return pl.pallas_call(
        _ada_kernel,
        grid=(depth, n_col),
        in_specs=[
            pl.BlockSpec((bsz, d), lambda l, j: (0, 0)),
            pl.BlockSpec((None, d, d), lambda l, j: (l, 0, j)),
            pl.BlockSpec((None, 1, d), lambda l, j: (l, 0, j)),
        ],
        out_specs=pl.BlockSpec((None, bsz, d), lambda l, j: (l, 0, j)),
        out_shape=jax.ShapeDtypeStruct((depth, bsz, d6), F32),
        compiler_params=_params(("arbitrary", "arbitrary")),
        name="ada_mod",
    )(c, ada_w, ada_b.reshape(depth, 1, d6))


def _inproj_kernel(x_ref, mod_ref, w_ref, o_ref, *, col_chunk, q_chunk):
    sh = mod_ref[0:1, :]
    sc = mod_ref[1:2, :]
    h = (x_ref[...] * (1.0 + sc) + sh).astype(BF16)
    for j in range(o_ref.shape[1] // col_chunk):
        cols = slice(j * col_chunk, (j + 1) * col_chunk)
        y = _dot(h, w_ref[:, cols])
        if j == q_chunk:
            y = y * Q_PRESCALE
        o_ref[:, cols] = y.astype(o_ref.dtype)


def _inproj(x, mod_l, w_bf16, q_chunk):
    bsz, t, d = x.shape
    width = w_bf16.shape[1]
    tm = min(TM_PROJ, t)
    return pl.pallas_call(
        functools.partial(_inproj_kernel, col_chunk=PROJ_CHUNK, q_chunk=q_chunk),
        grid=(bsz, t // tm),
        in_specs=[
            pl.BlockSpec((None, tm, d), lambda b, i: (b, i, 0)),
            pl.BlockSpec((None, 6, d), lambda b, i: (b, 0, 0)),
            pl.BlockSpec((d, width), lambda b, i: (0, 0)),
        ],
        out_specs=pl.BlockSpec((None, tm, width), lambda b, i: (b, i, 0)),
        out_shape=jax.ShapeDtypeStruct((bsz, t, width), BF16),
        compiler_params=_params(("arbitrary", "arbitrary")),
        name="inproj",
    )(x, mod_l, w_bf16)


def _diffattn_kernel(q_ref, k_ref, v_ref, lam_ref, gain_ref, o_ref, m_sc, acc_sc, s0_sc, s1_sc,
                     p0_sc, p1_sc, a0_sc, a1_sc, kn_sc, *, tile, lam_init):
    h = pl.program_id(1)
    qi = pl.program_id(2)
    dh = DA_HEAD_DIM
    hd = 2 * dh
    tk = tile // 2
    reps = tk // LANES
    s_bufs, p_bufs, a_bufs = (s0_sc, s1_sc), (p0_sc, p1_sc), (a0_sc, a1_sc)

    lane = lax.broadcasted_iota(jnp.int32, (1, hd), 1)
    q = q_ref[...]
    zero = jnp.zeros_like(q)
    q2 = jnp.concatenate([jnp.where(lane < dh, q, zero), jnp.where(lane >= dh, q, zero)], axis=0)

    hf = jnp.full((1, 1), h + 1, jnp.int32).astype(F32)
    slope = jnp.exp2(hf * (-8.0 / DA_HEADS)) * LOG2E
    col = lax.broadcasted_iota(jnp.int32, (1, tk), 1)
    ones = jnp.ones((tk, hd), BF16)

    m_sc[...] = jnp.full(m_sc.shape, NEG_BIG, F32)
    acc_sc[...] = jnp.zeros(acc_sc.shape, F32)

    def max_half_norms(x):
        xf = x.astype(F32)
        sq = xf * xf
        out = []
        for keep in (lane < dh, lane >= dh):
            rows = jnp.sum(jnp.where(keep, sq, 0.0), axis=1, keepdims=True)
            out.append(jnp.sqrt(jnp.max(rows, axis=0, keepdims=True)))
        return out

    @pl.when(qi == 0)
    def _():
        kn_sc[...] = jnp.concatenate([jnp.broadcast_to(n, (4, LANES))
                                      for n in max_half_norms(k_ref[...])], axis=0)

    n_sub = 2 * qi + 2

    def key_start(j):
        return pl.multiple_of((n_sub - 1 - j) * tk, tk)

    every = slice(0, 2 * tile)

    def scores(j, slot, rows=every):
        ks = key_start(j)
        bias = (col + (ks - qi * tile)).astype(F32) * slope
        s_bufs[slot][rows, :] = _dot_nt(q2[rows, :], k_ref[pl.ds(ks, tk), :]) + bias

    def softmax(j, slot, masked, rows=every):
        s = s_bufs[slot][rows, :]
        if masked:
            n_rows = rows.stop - rows.start
            rowp = ((lax.broadcasted_iota(jnp.int32, (n_rows, tk), 0) + rows.start) & (tile - 1)) \
                + qi * tile
            colp = lax.broadcasted_iota(jnp.int32, (n_rows, tk), 1) + key_start(j)
            s = jnp.where(colp <= rowp, s, NEG_BIG)
        m_old = m_sc[rows, :]
        m_new = jnp.maximum(m_old, jnp.max(s, axis=1, keepdims=True))
        p_bufs[slot][rows, :] = jnp.exp2(s - jnp.concatenate([m_new] * reps, axis=1)).astype(BF16)
        a_bufs[slot][rows, :] = jnp.exp2(m_old - m_new)
        m_sc[rows, :] = m_new

    def values(j, slot, rows=every):
        v_aug = jnp.concatenate([v_ref[pl.ds(key_start(j), tk), :], ones], axis=1)
        alpha = a_bufs[slot][rows, :]
        acc_sc[rows, :] = (jnp.concatenate([alpha, alpha], axis=1) * acc_sc[rows, :]
                           + _dot(p_bufs[slot][rows, :], v_aug))

    late = (slice(tile // 2, tile), slice(tile + tile // 2, 2 * tile))

    def first(stage, *args):
        for rows in late:
            stage(0, 0, *args, rows)

    def softmax_frozen(slot, rows=every):
        m_rep = jnp.concatenate([m_sc[rows, :]] * reps, axis=1)
        p_bufs[slot][rows, :] = jnp.exp2(s_bufs[slot][rows, :] - m_rep).astype(BF16)

    def values_frozen(j, slot, rows=every):
        v_aug = jnp.concatenate([v_ref[pl.ds(key_start(j), tk), :], ones], axis=1)
        acc_sc[rows, :] += _dot(p_bufs[slot][rows, :], v_aug)

    first(scores)
    scores(1, 1)
    first(softmax, True)

    @pl.when(qi == 0)
    def _():
        softmax(1, 1, True)
        first(values)
        values(1, 1)

    @pl.when(qi > 0)
    def _():
        scores(2, 0)
        softmax(1, 1, True)
        first(values)
        scores(3, 1)
        softmax(2, 0, False)
        values(1, 1)
        qn = max_half_norms(q)
        qk_max = jnp.maximum(qn[0] * kn_sc[0:1, 0:1], qn[1] * kn_sc[4:5, 0:1])
        m_min = jnp.min(m_sc[...], axis=0, keepdims=True)[:, 0:1]
        reach = (qk_max - m_min + DA_DEAD_LOG2) / slope
        first_dead = jnp.floor((reach - 1.0) / tk) + 3.0
        first_dead = jnp.max(jnp.clip(first_dead, 0.0, 1e6)).astype(jnp.int32)
        pairs_end = jnp.maximum(2, jnp.minimum(qi + 1, (first_dead + 1) // 2))

        freeze = jnp.max(jnp.where(qk_max - m_min <= DA_FREEZE_LOG2, 1.0, 0.0)) > 0.5
        freeze = jnp.logical_and(freeze, pairs_end > 2)

        @pl.when(jnp.logical_not(freeze))
        def _():
            def body(i, carry):
                t = 2 * i
                scores(t, 0)
                softmax(t - 1, 1, False)
                values(t - 2, 0)
                scores(t + 1, 1)
                softmax(t, 0, False)
                values(t - 1, 1)
                return carry

            lax.fori_loop(2, pairs_end, body, 0)
            t = 2 * pairs_end
            softmax(t - 1, 1, False)
            values(t - 2, 0)
            values(t - 1, 1)

        @pl.when(freeze)
        def _():
            scores(4, 0)
            softmax_frozen(1)
            values(2, 0)
            scores(5, 1)
            softmax_frozen(0)
            values_frozen(3, 1)

            def body(i, carry):
                t = 2 * i
                scores(t, 0)
                softmax_frozen(1)
                values_frozen(t - 2, 0)
                scores(t + 1, 1)
                softmax_frozen(0)
                values_frozen(t - 1, 1)
                return carry

            lax.fori_loop(3, pairs_end, body, 0)
            t = 2 * pairs_end
            softmax_frozen(1)
            values_frozen(t - 2, 0)
            values_frozen(t - 1, 1)

    lv = lam_ref[...].astype(F32)
    dots = jnp.sum(lv[0:1, :] * lv[1:2, :], axis=1, keepdims=True)
    dots2 = jnp.sum(lv[2:3, :] * lv[3:4, :], axis=1, keepdims=True)
    lam = jnp.exp(dots) - jnp.exp(dots2) + lam_init
    acc = acc_sc[...]
    o0 = acc[0:tile, 0:hd] / acc[0:tile, hd:2 * hd]
    o1 = acc[tile:2 * tile, 0:hd] / acc[tile:2 * tile, hd:2 * hd]
    o = o0 - lam * o1
    ms = jnp.mean(o * o, axis=1, keepdims=True)
    o = o * lax.rsqrt(ms + 1e-6) * gain_ref[...] * (1.0 - lam_init)
    o_ref[...] = o.astype(o_ref.dtype)


def _diff_attention(proj, diff_lambda, diff_gain, lam_init):
    bsz, t, _ = proj.shape
    tile = min(TQ_ATT, t)
    hd = 2 * DA_HEAD_DIM
    kern = functools.partial(_diffattn_kernel, tile=tile, lam_init=lam_init)
    return pl.pallas_call(
        kern,
        grid=(bsz, DA_HEADS, t // tile),
        in_specs=[
            pl.BlockSpec((None, tile, hd), lambda b, h, i: (b, i, h)),
            pl.BlockSpec((None, t, hd), lambda b, h, i: (b, 0, DA_HEADS + h)),
            pl.BlockSpec((None, t, hd), lambda b, h, i: (b, 0, 2 * DA_HEADS + h)),
            pl.BlockSpec((4, DA_HEAD_DIM), lambda b, h, i: (0, 0)),
            pl.BlockSpec((1, hd), lambda b, h, i: (0, 0)),
        ],
        out_specs=pl.BlockSpec((None, tile, hd), lambda b, h, i: (b, i, h)),
        out_shape=jax.ShapeDtypeStruct((bsz, t, DA_HEADS * hd), BF16),
        scratch_shapes=[
            pltpu.VMEM((2 * tile, LANES), F32),
            pltpu.VMEM((2 * tile, 2 * hd), F32),
            pltpu.VMEM((2 * tile, tile // 2), F32),
            pltpu.VMEM((2 * tile, tile // 2), F32),
            pltpu.VMEM((2 * tile, tile // 2), BF16),
            pltpu.VMEM((2 * tile, tile // 2), BF16),
            pltpu.VMEM((2 * tile, LANES), F32),
            pltpu.VMEM((2 * tile, LANES), F32),
            pltpu.VMEM((8, LANES), F32),
        ],
        compiler_params=_params(("arbitrary", "arbitrary", "arbitrary")),
        name="diff_attn",
    )(proj, proj, proj, diff_lambda, diff_gain.reshape(1, hd))


def _hgrn_kernel(q_ref, f_ref, i_ref, g_ref, gamma_ref, gain_ref, o_ref, st_sc, *, layer):
    @pl.when(pl.program_id(1) == 0)
    def _():
        st_sc[...] = jnp.zeros(st_sc.shape, F32)

    gam = gamma_ref[...].astype(F32)
    e = jnp.exp(gam - jnp.max(gam, axis=0, keepdims=True))
    sm = e / jnp.sum(e, axis=0, keepdims=True)
    lb_all = jnp.sum(sm[0:layer + 1, :], axis=0, keepdims=True)

    c = HG_CHUNK
    row = lax.broadcasted_iota(jnp.int32, (c, c), 0)
    col = lax.broadcasted_iota(jnp.int32, (c, c), 1)
    tril = col <= row
    tril_bf = jnp.where(tril, 1.0, 0.0).astype(BF16)
    gain = gain_ref[...]

    heads = range(HG_HEADS)
    hcols = [slice(h * HG_DK, (h + 1) * HG_DK) for h in heads]
    lbs = [lb_all[:, hc] for hc in hcols]
    for n in range(q_ref.shape[0] // c):
        rows = slice(n * c, (n + 1) * c)
        sig = [_sigmoid(f_ref[rows, hc].astype(F32)) for hc in hcols]
        logf = [jnp.log(lbs[h] + (1.0 - lbs[h]) * sig[h]) for h in heads]
        kk = [(1.0 - lbs[h]) * (1.0 - sig[h]) for h in heads]
        parts = [_split3(x) for x in logf]
        b = [_dot(tril_bf, p[0]) + _dot(tril_bf, p[1]) for p in parts]
        b_mid = [x[c // 2 - 1:c // 2, :] for x in b]
        b_last = [x[c - 1:c, :] for x in b]
        qh = [q_ref[rows, hc].astype(F32) for hc in hcols]
        qs = [x * _sigmoid(x) for x in qh]
        v = [i_ref[rows, hc] for hc in hcols]
        qa = [qs[h] * jnp.exp(b[h] - b_mid[h]) for h in heads]
        ka = [kk[h] * jnp.exp(b_mid[h] - b[h]) for h in heads]
        att = [_dot_nt(qa[h].astype(BF16), ka[h].astype(BF16)) for h in heads]
        att = [jnp.where(tril, x, 0.0).astype(BF16) for x in att]
        o_intra = [_dot(att[h], v[h]) for h in heads]
        kd = [(ka[h] * jnp.exp(b_last[h] - b_mid[h])).astype(BF16) for h in heads]
        ds_t = [_dot(v[h].T, kd[h]) for h in heads]
        st = [st_sc[h] for h in heads]
        o_inter = [_dot_nt((qa[h] * jnp.exp(b_mid[h])).astype(BF16), st[h].astype(BF16))
                   for h in heads]
        for h in heads:
            st_sc[h] = st[h] * jnp.exp(b_last[h]) + ds_t[h]
        for h in heads:
            o = o_intra[h] + o_inter[h]
            gh = g_ref[rows, hcols[h]].astype(F32)
            ms = jnp.mean(o * o, axis=1, keepdims=True)
            o = o * lax.rsqrt(ms + 1e-6) * gain * (gh * _sigmoid(gh))
            o_ref[rows, hcols[h]] = o.astype(o_ref.dtype)


def _hgrn2(proj, hgrn_gamma, hgrn_gain, layer):
    bsz, t, _ = proj.shape
    tt = min(T_HG, t)
    width = HG_HEADS * HG_DK
    base = 3 * DA_HEADS * 2 * DA_HEAD_DIM // width
    spec = lambda k: pl.BlockSpec((None, tt, width), lambda b, i: (b, i, base + k))
    return pl.pallas_call(
        functools.partial(_hgrn_kernel, layer=layer),
        grid=(bsz, t // tt),
        in_specs=[
            spec(0), spec(1), spec(2), spec(3),
            pl.BlockSpec((hgrn_gamma.shape[0], width), lambda b, i: (0, 0)),
            pl.BlockSpec((1, HG_DK), lambda b, i: (0, 0)),
        ],
        out_specs=pl.BlockSpec((None, tt, width), lambda b, i: (b, i, 0)),
        out_shape=jax.ShapeDtypeStruct((bsz, t, width), BF16),
        scratch_shapes=[pltpu.VMEM((HG_HEADS, HG_DK, HG_DK), F32)],
        compiler_params=_params(("arbitrary", "arbitrary")),
        name="hgrn2",
    )(proj, proj, proj, proj, hgrn_gamma, hgrn_gain.reshape(1, HG_DK))


def _route(logits_t):
    mx = jnp.max(logits_t, axis=0, keepdims=True)
    ex = jnp.exp(logits_t - mx)
    probs = ex / jnp.sum(ex, axis=0, keepdims=True)
    p = [probs[e:e + 1, :] for e in range(N_EXPERTS)]
    g = E_PER_GROUP
    scores = []
    for gi in range(N_GROUPS):
        pg = p[gi * g:(gi + 1) * g]
        best = None
        for a in range(g):
            for b in range(a + 1, g):
                pair = pg[a] + pg[b]
                best = pair if best is None else jnp.maximum(best, pair)
        scores.append(best)
    group_id = jnp.zeros_like(p[0])
    gates = [jnp.zeros_like(p[0]) for _ in range(g)]
    for gi in range(N_GROUPS):
        sel = None
        for gj in range(N_GROUPS):
            if gj == gi:
                continue
            cond = (scores[gi] > scores[gj]) if gj < gi else (scores[gi] >= scores[gj])
            sel = cond if sel is None else (sel & cond)
        group_id = jnp.where(sel, float(gi), group_id)
        pg = p[gi * g:(gi + 1) * g]
        chosen = []
        for a in range(g):
            rank = jnp.zeros_like(pg[a])
            for b in range(g):
                if b == a:
                    continue
                ahead = (pg[b] >= pg[a]) if b < a else (pg[b] > pg[a])
                rank = rank + jnp.where(ahead, 1.0, 0.0)
            chosen.append(sel & (rank < 2.0))
        denom = None
        for a in range(g):
            term = jnp.where(chosen[a], pg[a], 0.0)
            denom = term if denom is None else denom + term
        for a in range(g):
            gates[a] = jnp.where(chosen[a], pg[a] / denom, gates[a])
    return group_id, gates


def _outproj_kernel(a_ref, b_ref, x_ref, mod_ref, w_ref, lng_ref, lnb_ref, rwt_ref, rb_ref,
                    xo_ref, h_ref, row_ref, col_ref, *, alpha):
    half = a_ref.shape[1]
    tm = a_ref.shape[0]
    g1 = mod_ref[2:3, :]
    sh2 = mod_ref[3:4, :]
    sc2 = mod_ref[4:5, :]
    rw = rwt_ref[...]
    w_hi, w_mid, _ = _split3(rw)
    chunks = [slice(i * tm // ROW_CHUNKS, (i + 1) * tm // ROW_CHUNKS)
              for i in range(ROW_CHUNKS)]
    y = [_dot(a_ref[rs, :], w_ref[0:half, :]) + _dot(b_ref[rs, :], w_ref[half:2 * half, :])
         for rs in chunks]
    r = [alpha * x_ref[rs, :] + (1.0 + g1) * yy for rs, yy in zip(chunks, y)]
    rc = [rr - jnp.mean(rr, axis=1, keepdims=True) for rr in r]
    var = [jnp.mean(cc * cc, axis=1, keepdims=True) for cc in rc]
    xn = [cc * lax.rsqrt(vv + 1e-5) * lng_ref[...] + lnb_ref[...] for cc, vv in zip(rc, var)]
    h2 = [xx * (1.0 + sc2) + sh2 for xx in xn]
    for rs, xx, hh in zip(chunks, xn, h2):
        xo_ref[rs, :] = xx
        h_ref[rs, :] = hh.astype(BF16)
    split = [_split3(hh) for hh in h2]
    logits_t = jnp.concatenate(
        [_dot_nt(w_hi, s[0]) + _dot_nt(w_hi, s[1]) + _dot_nt(w_mid, s[0]) for s in split],
        axis=1) + rb_ref[...]
    group_id, gates = _route(logits_t)
    sel = [jnp.where(group_id == float(gi), 1.0, 0.0) for gi in range(N_GROUPS)]
    onehot = jnp.concatenate(sel + [jnp.zeros((8 - N_GROUPS, tm), F32)], axis=0).astype(BF16)
    src = lax.broadcasted_iota(jnp.int32, (tm, tm), 0)
    dst = lax.broadcasted_iota(jnp.int32, (tm, tm), 1)
    earlier = jnp.where(src < dst, 1.0, 0.0).astype(BF16)
    counts = _dot(onehot, earlier)
    rank = sel[0] * counts[0:1, :]
    for gi in range(1, N_GROUPS):
        rank = rank + sel[gi] * counts[gi:gi + 1, :]
    info = jnp.concatenate(gates + [group_id, rank], axis=0)
    row_ref[...] = jnp.concatenate(
        [group_id, rank, jnp.zeros((8 - 2, tm), F32)], axis=0)
    pad = jnp.zeros((LANES - info.shape[0], tm), F32)
    col_ref[...] = jnp.concatenate([info, pad], axis=0).T


def _outproj(a, b, x, mod_l, w_bf16, ln_g, ln_b, router_w, router_b, alpha):
    bsz, t, d = x.shape
    half = a.shape[2]
    tm = min(T_BLK, t)
    tok = lambda width: pl.BlockSpec((None, tm, width), lambda bi, i: (bi, i, 0))
    full = lambda r, c: pl.BlockSpec((r, c), lambda bi, i: (0, 0))
    return pl.pallas_call(
        functools.partial(_outproj_kernel, alpha=alpha),
        grid=(bsz, t // tm),
        in_specs=[
            tok(half), tok(half), tok(d),
            pl.BlockSpec((None, 6, d), lambda bi, i: (bi, 0, 0)),
            full(2 * half, d), full(1, d), full(1, d), full(N_EXPERTS, d), full(N_EXPERTS, 1),
        ],
        out_specs=[tok(d), tok(d), pl.BlockSpec((None, 8, tm), lambda bi, i: (bi, 0, i)),
                   tok(LANES)],
        out_shape=[
            jax.ShapeDtypeStruct((bsz, t, d), F32),
            jax.ShapeDtypeStruct((bsz, t, d), BF16),
            jax.ShapeDtypeStruct((bsz, 8, t), F32),
            jax.ShapeDtypeStruct((bsz, t, LANES), F32),
        ],
        compiler_params=_params(("arbitrary", "arbitrary")),
        name="outproj_ln_route",
    )(a, b, x, mod_l, w_bf16, ln_g.reshape(1, d), ln_b.reshape(1, d), router_w.T,
      router_b.reshape(N_EXPERTS, 1))


def _slab_rows(tm):
    extra = -(-(tm - MOE_ROWS_MAIN) // MOE_ROWS_EXTRA)
    return MOE_ROWS_MAIN + max(extra, 0) * MOE_ROWS_EXTRA


def _extra_chunks(count):
    return (jnp.maximum(count - MOE_ROWS_MAIN, 0) + MOE_ROWS_EXTRA - 1) // MOE_ROWS_EXTRA


def _moe_expert_kernel(cnt_ref, h_ref, row_ref, col_ref, wg_ref, wu_ref, wd_ref, zm_ref, zx_ref):
    g = pl.program_id(0)
    blk = pl.program_id(1)
    count = cnt_ref[g * pl.num_programs(1) + blk]
    mine = row_ref[0:1, :] == g.astype(F32)
    rank = row_ref[1:2, :]
    info = col_ref[...]
    info_hi = info.astype(BF16)
    info_lo = (info - info_hi.astype(F32)).astype(BF16)

    def run_rows(r0, m, out_ref, out_r0):
        rid = (lax.broadcasted_iota(jnp.int32, (m, 1), 0) + r0).astype(F32)
        pick = _onehot((rank == rid) & mine)
        xs = _dot(pick, h_ref[...]).astype(BF16)
        gm = _dot(pick, info_hi) + _dot(pick, info_lo)
        acc = None
        for j in range(E_PER_GROUP):
            a = _dot(xs, wg_ref[j])
            u = _dot(xs, wu_ref[j])
            he = (a * _sigmoid(a) * u * gm[:, j:j + 1]).astype(BF16)
            part = _dot(he, wd_ref[j])
            acc = part if acc is None else acc + part
        out_ref[pl.ds(out_r0, m), :] = acc.astype(out_ref.dtype)

    sizes = (MOE_ROWS_SMALL, MOE_ROWS_MID, MOE_ROWS_MAIN)
    for below, size in zip((-1,) + sizes[:-1], sizes):
        fits = count > below
        if size != MOE_ROWS_MAIN:
            fits = jnp.logical_and(fits, count <= size)

        @pl.when(fits)
        def _(size=size):
            run_rows(0, size, zm_ref, 0)
            if size != MOE_ROWS_MAIN:
                zm_ref[size:MOE_ROWS_MAIN, :] = jnp.zeros(
                    (MOE_ROWS_MAIN - size, zm_ref.shape[1]), zm_ref.dtype)

    zx_ref[...] = jnp.zeros(zx_ref.shape, zx_ref.dtype)

    def body(i, carry):
        off = pl.multiple_of(i * MOE_ROWS_EXTRA, 16)
        run_rows(MOE_ROWS_MAIN + off, MOE_ROWS_EXTRA, zx_ref, off)
        return carry

    lax.fori_loop(0, _extra_chunks(count), body, 0)


def _moe_combine_kernel(cnt_ref, xidx_ref, zm0_ref, zm1_ref, zm2_ref, zm3_ref, zx0_ref, zx1_ref,
                        zx2_ref, zx3_ref, col_ref, x_ref, mod_ref, lng_ref, lnb_ref, o_ref, y_sc,
                        *, alpha):
    del xidx_ref
    blk = pl.program_id(0) * pl.num_programs(1) + pl.program_id(1)
    n_blk = pl.num_programs(0) * pl.num_programs(1)
    zm_refs = (zm0_ref, zm1_ref, zm2_ref, zm3_ref)
    zx_refs = (zx0_ref, zx1_ref, zx2_ref, zx3_ref)
    main = MOE_ROWS_MAIN
    grp = col_ref[:, E_PER_GROUP:E_PER_GROUP + 1]
    rank = col_ref[:, E_PER_GROUP + 1:E_PER_GROUP + 2]
    y_sc[...] = jnp.zeros(y_sc.shape, F32)
    lane_x = lax.broadcasted_iota(jnp.int32, (1, MOE_ROWS_EXTRA), 1).astype(F32)
    for gi in range(N_GROUPS):
        def body(i, carry, gi=gi):
            off = pl.multiple_of(i * MOE_ROWS_EXTRA, 16)
            hit = (grp == float(gi)) & ((rank - (main + off).astype(F32)) == lane_x)
            y_sc[...] += _dot(_onehot(hit), zx_refs[gi][pl.ds(off, MOE_ROWS_EXTRA), :])
            return carry

        lax.fori_loop(0, _extra_chunks(cnt_ref[gi * n_blk + blk]), body, 0)

    where_to = jnp.where(rank < float(main), grp * float(main) + rank, -1.0)
    lane = lax.broadcasted_iota(jnp.int32, (1, N_GROUPS * main), 1).astype(F32)
    z_all = jnp.concatenate([zr[...] for zr in zm_refs], axis=0)
    g2 = mod_ref[5:6, :]
    tm = x_ref.shape[0]
    chunks = [slice(i * tm // ROW_CHUNKS, (i + 1) * tm // ROW_CHUNKS)
              for i in range(ROW_CHUNKS)]
    y = [y_sc[rs, :] + _dot(_onehot(where_to[rs, :] == lane), z_all) for rs in chunks]
    r = [alpha * x_ref[rs, :] + (1.0 + g2) * yy for rs, yy in zip(chunks, y)]
    rc = [rr - jnp.mean(rr, axis=1, keepdims=True) for rr in r]
    var = [jnp.mean(cc * cc, axis=1, keepdims=True) for cc in rc]
    for rs, cc, vv in zip(chunks, rc, var):
        o_ref[rs, :] = cc * lax.rsqrt(vv + 1e-5) * lng_ref[...] + lnb_ref[...]


def _moe(h2, rowinfo, colinfo, x, mod_l, layer, wg, wu, wd, ln_g, ln_b, alpha):
    bsz, t, d = x.shape
    tm = min(T_BLK, t)
    nb = t // tm
    n_blk = bsz * nb
    dff = wg.shape[3]
    slab = _slab_rows(tm)
    group_of = rowinfo[:, 0, :].reshape(1, n_blk, tm)
    counts = jnp.sum(group_of == jnp.arange(N_GROUPS, dtype=F32).reshape(N_GROUPS, 1, 1), axis=2)
    counts = counts.astype(jnp.int32)
    needed = jnp.where(counts > MOE_ROWS_MAIN, jnp.arange(n_blk, dtype=jnp.int32), 0)
    extra_block = lax.cummax(needed, axis=1).reshape(N_GROUPS * n_blk)
    counts = counts.reshape(N_GROUPS * n_blk)
    extra = slab - MOE_ROWS_MAIN

    z_main, z_extra = pl.pallas_call(
        _moe_expert_kernel,
        grid_spec=pltpu.PrefetchScalarGridSpec(
            num_scalar_prefetch=1,
            grid=(N_GROUPS, n_blk),
            in_specs=[
                pl.BlockSpec((None, tm, d), lambda g, i, c: (i // nb, i % nb, 0)),
                pl.BlockSpec((None, 8, tm), lambda g, i, c: (i // nb, 0, i % nb)),
                pl.BlockSpec((None, tm, LANES), lambda g, i, c: (i // nb, i % nb, 0)),
                pl.BlockSpec((None, E_PER_GROUP, d, dff), lambda g, i, c: (layer, g, 0, 0)),
                pl.BlockSpec((None, E_PER_GROUP, d, dff), lambda g, i, c: (layer, g, 0, 0)),
                pl.BlockSpec((None, E_PER_GROUP, dff, d), lambda g, i, c: (layer, g, 0, 0)),
            ],
            out_specs=[
                pl.BlockSpec((None, None, MOE_ROWS_MAIN, d), lambda g, i, c: (g, i, 0, 0)),
                pl.BlockSpec((None, None, extra, d), lambda g, i, c: (g, i, 0, 0)),
            ],
        ),
        out_shape=[
            jax.ShapeDtypeStruct((N_GROUPS, n_blk, MOE_ROWS_MAIN, d), BF16),
            jax.ShapeDtypeStruct((N_GROUPS, n_blk, extra, d), BF16),
        ],
        compiler_params=_params(("arbitrary", "arbitrary")),
        name="moe_experts",
    )(counts, h2, rowinfo, colinfo, wg, wu, wd)

    zm_spec = lambda gi: pl.BlockSpec((None, None, MOE_ROWS_MAIN, d),
                                      lambda b, i, c, xb: (gi, b * nb + i, 0, 0))
    zx_spec = lambda gi: pl.BlockSpec((None, None, extra, d),
                                      lambda b, i, c, xb: (gi, xb[gi * n_blk + b * nb + i], 0, 0))
    tok = lambda width: pl.BlockSpec((None, tm, width), lambda b, i, c, xb: (b, i, 0))
    return pl.pallas_call(
        functools.partial(_moe_combine_kernel, alpha=alpha),
        grid_spec=pltpu.PrefetchScalarGridSpec(
            num_scalar_prefetch=2,
            grid=(bsz, nb),
            in_specs=[
                zm_spec(0), zm_spec(1), zm_spec(2), zm_spec(3),
                zx_spec(0), zx_spec(1), zx_spec(2), zx_spec(3), tok(LANES), tok(d),
                pl.BlockSpec((None, 6, d), lambda b, i, c, xb: (b, 0, 0)),
                pl.BlockSpec((1, d), lambda b, i, c, xb: (0, 0)),
                pl.BlockSpec((1, d), lambda b, i, c, xb: (0, 0)),
            ],
            out_specs=tok(d),
            scratch_shapes=[pltpu.VMEM((tm, d), F32)],
        ),
        out_shape=jax.ShapeDtypeStruct((bsz, t, d), F32),
        compiler_params=_params(("arbitrary", "arbitrary")),
        name="moe_combine_ln",
    )(counts, extra_block, z_main, z_main, z_main, z_main, z_extra, z_extra, z_extra, z_extra,
      colinfo, x, mod_l, ln_g.reshape(1, d), ln_b.reshape(1, d))


def _lru_kernel(x_ref, g_ref, cw_ref, cb_ref, wa_ref, ba_ref, wx_ref, bx_ref, lam_ref,
                o_ref, xpad_sc, h_sc):
    tt = x_ref.shape[0]
    pad = 8

    @pl.when(pl.program_id(1) == 0)
    def _():
        xpad_sc[0:pad, :] = jnp.zeros((pad, xpad_sc.shape[1]), F32)
        h_sc[...] = jnp.zeros(h_sc.shape, F32)

    xpad_sc[pad:pad + tt, :] = x_ref[...].astype(F32)
    xc = cb_ref[...] + jnp.zeros((tt, x_ref.shape[1]), F32)
    for j in range(CONV_WIDTH):
        off = pad - (CONV_WIDTH - 1) + j
        xc = xc + cw_ref[j:j + 1, :] * xpad_sc[off:off + tt, :]
    xpad_sc[0:pad, :] = xpad_sc[tt:tt + pad, :]

    xb = xc.astype(BF16)
    r = _sigmoid(_dot(xb, wa_ref[...]) + ba_ref[...])
    i = _sigmoid(_dot(xb, wx_ref[...]) + bx_ref[...])
    lam = lam_ref[...].astype(F32)
    softplus_neg = jnp.maximum(-lam, 0.0) + jnp.log(1.0 + jnp.exp(-jnp.abs(lam)))
    log_a = -LRU_C * r * softplus_neg
    a = jnp.exp(log_a)
    gain_sq = jnp.maximum(1.0 - jnp.exp(2.0 * log_a), 1e-12)
    u = gain_sq * lax.rsqrt(gain_sq) * (i * xc)

    groups = (tt // SUBLANES, SUBLANES, a.shape[1])
    a = a.reshape(groups)
    u = u.reshape(groups)
    rowi = lax.broadcasted_iota(jnp.int32, (1, SUBLANES, 1), 1)
    d = 1
    while d < SUBLANES:
        a_sh = jnp.where(rowi >= d, pltpu.roll(a, d, 1), 1.0)
        u_sh = jnp.where(rowi >= d, pltpu.roll(u, d, 1), 0.0)
        u = u + a * u_sh
        a = a * a_sh
        d *= 2
    a = a.reshape(tt, groups[2])
    u = u.reshape(tt, groups[2])
    gr = g_ref[...].astype(F32)
    gelu = 0.5 * gr * (1.0 + jnp.tanh(0.7978845608028654 * (gr + 0.044715 * gr * gr * gr)))
    h_prev = h_sc[...]
    out = []
    for grp in range(tt // SUBLANES):
        rows = slice(grp * SUBLANES, (grp + 1) * SUBLANES)
        h_grp = u[rows, :] + a[rows, :] * h_prev
        out.append(gelu[rows, :] * h_grp)
        h_prev = h_grp[SUBLANES - 1:SUBLANES, :]
    h_sc[...] = h_prev
    o_ref[...] = jnp.concatenate(out, axis=0).astype(o_ref.dtype)


def _rg_lru(proj, conv_w, conv_b, wa_dense, ba, wx_dense, bx, lam):
    bsz, t, _ = proj.shape
    tt = min(T_LRU, t)
    w = LRU_WIDTH
    nblk = w // LANES
    row = lambda a: a.reshape(1, w)
    full = lambda r, c: pl.BlockSpec((r, c), lambda b, i: (0, 0))
    return pl.pallas_call(
        _lru_kernel,
        grid=(bsz, t // tt),
        in_specs=[
            pl.BlockSpec((None, tt, w), lambda b, i: (b, i, 0)),
            pl.BlockSpec((None, tt, w), lambda b, i: (b, i, 1)),
            full(CONV_WIDTH, w), full(1, w), full(w, w), full(1, w), full(w, w), full(1, w),
            full(1, w),
        ],
        out_specs=pl.BlockSpec((None, tt, w), lambda b, i: (b, i, 0)),
        out_shape=jax.ShapeDtypeStruct((bsz, t, w), BF16),
        scratch_shapes=[pltpu.VMEM((tt + SUBLANES, w), F32), pltpu.VMEM((1, w), F32)],
        compiler_params=_params(("arbitrary", "arbitrary")),
        name="rg_lru",
    )(proj, proj, conv_w, row(conv_b), wa_dense, row(ba), wx_dense, row(bx), row(lam))


def _sb_kernel(q_ref, k_ref, v_ref, o_ref, r_sc, acc_sc, *bufs, tile):
    qi = pl.program_id(2)
    d = SB_HEAD_DIM
    tk = tile // 2
    z_bufs, lb_bufs, l_bufs, w_bufs = (bufs[i * SB_SETS:(i + 1) * SB_SETS] for i in range(4))
    lane = lax.broadcasted_iota(jnp.int32, (1, 2 * d), 1)
    q = q_ref[...]
    zero = jnp.zeros_like(q)
    q2 = jnp.concatenate([jnp.where(lane < d, q, zero), jnp.where(lane >= d, q, zero)], axis=0)
    rj = lax.broadcasted_iota(jnp.int32, (tk, tk), 0)
    cs = lax.broadcasted_iota(jnp.int32, (tk, tk), 1)
    upper = jnp.where(rj > cs, 1.0, 0.0).astype(BF16)

    r_sc[...] = jnp.zeros(r_sc.shape, F32)
    acc_sc[...] = jnp.zeros(acc_sc.shape, F32)
    n_sub = 2 * qi + 2
    every = slice(0, 2 * tile)
    per_head = (slice(0, tile), slice(tile, 2 * tile))

    def key_start(j):
        return pl.multiple_of((n_sub - 1 - j) * tk, tk)

    def strict_mask(j, rows):
        n_rows = rows.stop - rows.start
        rowp = (lax.broadcasted_iota(jnp.int32, (n_rows, tk), 0) + rows.start) & (tile - 1)
        colp = lax.broadcasted_iota(jnp.int32, (n_rows, tk), 1) + (1 - j) * tk
        return colp < rowp

    def logits(j, b, rows=every):
        z_bufs[b][rows, :] = _dot_nt(q2[rows, :], k_ref[pl.ds(key_start(j), tk), :])

    def gates(j, b, rows=every, masked=False):
        z = z_bufs[b][rows, :]
        log_1m = jnp.log(1.0 + jnp.exp2(-jnp.abs(z))) * (-LOG2E) - jnp.maximum(z, 0.0)
        lb_bufs[b][rows, :] = z + log_1m
        if masked:
            log_1m = jnp.where(strict_mask(j, rows), log_1m, 0.0)
        l_bufs[b][rows, :] = log_1m.astype(BF16)

    def weights(j, b, rows=every, masked=False):
        log_1m = l_bufs[b][rows, :]
        after = _dot(log_1m, upper) + r_sc[rows, :]
        w = jnp.exp2(lb_bufs[b][rows, :] + after)
        if masked:
            w = jnp.where(strict_mask(j, rows), w, 0.0)
        w_bufs[b][rows, :] = w.astype(BF16)
        r_sc[rows, :] = after[:, 0:1] + log_1m[:, 0:1].astype(F32)

    def values(j, b, rows=every):
        acc_sc[rows, :] += _dot(w_bufs[b][rows, :], v_ref[pl.ds(key_start(j), tk), :])

    late = tuple(slice(r.start + tile // 2, r.stop) for r in per_head)
    early = tuple(slice(r.start, r.start + tile // 2) for r in per_head)

    def first(stage, **kw):
        for rows in late:
            stage(0, 0, rows, **kw)

    @pl.when(qi == 0)
    def _():
        first(logits)
        logits(1, 1)
        first(gates, masked=True)
        gates(1, 1, masked=True)
        first(weights, masked=True)
        weights(1, 1, masked=True)
        first(values)
        values(1, 1)

    @pl.when(qi > 0)
    def _():
        def alive(rows=every):
            return (jnp.max(r_sc[rows, :]) > -SB_DEAD_LOG2).astype(jnp.int32)

        def third(stage, rows_set):
            for rows in rows_set:
                stage(2, 2, rows)

        first(logits)
        logits(1, 1)
        first(gates, masked=True)
        third(logits, early)
        gates(1, 1, masked=True)
        first(weights, masked=True)
        third(gates, early)
        weights(1, 1, masked=True)
        first(values)
        third(weights, early)
        values(1, 1)
        third(values, early)

        @pl.when(jnp.maximum(alive(late[0]), alive(late[1])) > 0)
        def _():
            for stage in (logits, gates, weights, values):
                third(stage, late)

        def cond(carry):
            j, live = carry
            return (j < n_sub) & (live > 0)

        def body(carry):
            j, _ = carry
            for rows in per_head:
                logits(j, 0, rows)
            for rows in per_head:
                gates(j, 0, rows)
            for rows in per_head:
                weights(j, 0, rows)
            for rows in per_head:
                values(j, 0, rows)
            return j + 1, alive()

        lax.while_loop(cond, body, (jnp.int32(3), alive()))

    acc = acc_sc[...]
    o_ref[...] = jnp.where(lane < d, acc[0:tile, :], acc[tile:2 * tile, :]).astype(o_ref.dtype)


def _sb_attention(proj):
    bsz, t, _ = proj.shape
    tq = min(T_SB, t)
    pairs = SB_HEADS // 2
    wblk = 2 * SB_HEAD_DIM
    base = 2 * LRU_WIDTH // wblk
    return pl.pallas_call(
        functools.partial(_sb_kernel, tile=tq),
        grid=(bsz, pairs, t // tq),
        in_specs=[
            pl.BlockSpec((None, tq, wblk), lambda b, h, i: (b, i, base + h)),
            pl.BlockSpec((None, t, wblk), lambda b, h, i: (b, 0, base + pairs + h)),
            pl.BlockSpec((None, t, wblk), lambda b, h, i: (b, 0, base + 2 * pairs + h)),
        ],
        out_specs=pl.BlockSpec((None, tq, wblk), lambda b, h, i: (b, i, h)),
        out_shape=jax.ShapeDtypeStruct((bsz, t, SB_HEADS * SB_HEAD_DIM), BF16),
        scratch_shapes=[pltpu.VMEM((2 * tq, 1), F32), pltpu.VMEM((2 * tq, wblk), F32)]
        + [pltpu.VMEM((2 * tq, tq // 2), F32)] * (2 * SB_SETS)
        + [pltpu.VMEM((2 * tq, tq // 2), BF16)] * (2 * SB_SETS),
        compiler_params=_params(("arbitrary", "arbitrary", "arbitrary")),
        name="sb_attn",
    )(proj, proj, proj)


def _block_diag(w):
    g, n, _ = w.shape
    eye = jnp.eye(g, dtype=w.dtype)
    return (eye[:, None, :, None] * w[:, :, None, :]).reshape(g * n, g * n)


def kernel(x, c, ada_w, ada_b, ln_g, ln_b, even_w_in, even_w_out, diff_lambda, diff_gain, hgrn_gamma, hgrn_gain, odd_w_in, odd_w_out, conv_w, conv_b, lru_wa, lru_ba, lru_wx, lru_bx, lru_lambda, router_w, router_b, moe_w_gate, moe_w_up, moe_w_down):
    depth = ada_w.shape[0]
    bsz, t, d = x.shape
    alpha = (2.0 * depth) ** 0.25
    mod = _ada_mod(c, ada_w, ada_b).reshape(depth, bsz, 6, d)
    w_gate, w_up, w_down = (w.astype(BF16) for w in (moe_w_gate, moe_w_up, moe_w_down))
    for l in range(depth):
        j = l // 2
        mod_l = mod[l]
        if l % 2 == 0:
            lam_init = 0.8 - 0.6 * math.exp(-0.3 * l)
            proj = _inproj(x, mod_l, even_w_in[j].astype(BF16), q_chunk=0)
            mix_a = _diff_attention(proj, diff_lambda[j], diff_gain[j], lam_init)
            mix_b = _hgrn2(proj, hgrn_gamma, hgrn_gain[j], l)
            w_out = even_w_out[j]
        else:
            proj = _inproj(x, mod_l, odd_w_in[j].astype(BF16), q_chunk=2 * LRU_WIDTH // PROJ_CHUNK)
            mix_a = _rg_lru(proj, conv_w[j], conv_b[j], _block_diag(lru_wa[j]).astype(BF16),
                            lru_ba[j], _block_diag(lru_wx[j]).astype(BF16), lru_bx[j],
                            lru_lambda[j])
            mix_b = _sb_attention(proj)
            w_out = odd_w_out[j]
        x, h2, rowinfo, colinfo = _outproj(mix_a, mix_b, x, mod_l, w_out.astype(BF16),
                                           ln_g[l, 0], ln_b[l, 0], router_w, router_b, alpha)
        x = _moe(h2, rowinfo, colinfo, x, mod_l, l, w_gate, w_up, w_down, ln_g[l, 1], ln_b[l, 1],
                 alpha)
    return x
```

```python
import functools
import math

import jax
import jax.numpy as jnp
from jax import lax
from jax.experimental import pallas as pl
from jax.experimental.pallas import tpu as pltpu

F32 = jnp.float32
BF16 = jnp.bfloat16

DA_HEADS = 4
DA_HEAD_DIM = 64
HG_HEADS = 4
HG_DK = 128
HG_CHUNK = 64
LRU_WIDTH = 512
CONV_WIDTH = 4
LRU_C = 8.0
SB_HEADS = 8
SB_HEAD_DIM = 64
N_EXPERTS = 16
N_GROUPS = 4
E_PER_GROUP = N_EXPERTS // N_GROUPS

LANES = 128
SUBLANES = 8
NEG_BIG = -1e30
LOG2E = 1.4426950408889634
Q_PRESCALE = DA_HEAD_DIM ** -0.5 * LOG2E
PROJ_CHUNK = 512
VMEM_LIMIT = 56 * 1024 * 1024

TM_PROJ = 1024
TQ_ATT = 512
T_SB = 512
SB_SETS = 3
SB_DEAD_LOG2 = 160.0
DA_DEAD_LOG2 = 152.0
DA_FREEZE_LOG2 = 64.0
T_HG = 512
T_LRU = 256
T_BLK = 1024
ROW_CHUNKS = 8
MOE_ROWS_MAIN = 320
MOE_ROWS_SMALL = 224
MOE_ROWS_STEP = 16
MOE_ROWS_EXTRA = 128


def _params(sem):
    return pltpu.CompilerParams(dimension_semantics=sem, vmem_limit_bytes=VMEM_LIMIT)


def _sigmoid(x):
    return 0.5 * jnp.tanh(0.5 * x) + 0.5


def _dot(a, b):
    return jnp.dot(a, b, preferred_element_type=F32)


def _dot_nt(a, b):
    return lax.dot_general(a, b, (((1,), (1,)), ((), ())), preferred_element_type=F32)


def _onehot(mask):
    return jnp.where(mask, 1.0, 0.0).astype(BF16)


def _split3(x):
    hi = x.astype(BF16)
    r1 = x - hi.astype(F32)
    mid = r1.astype(BF16)
    lo = (r1 - mid.astype(F32)).astype(BF16)
    return hi, mid, lo


def _ada_kernel(c_ref, w_ref, b_ref, o_ref):
    c = c_ref[...]
    cond = c * _sigmoid(c)
    hi, mid, _ = _split3(cond)
    w = w_ref[...].astype(BF16)
    o_ref[...] = _dot(hi, w) + _dot(mid, w) + b_ref[...]


def _ada_mod(c, ada_w, ada_b):
    depth, d, d6 = ada_w.shape
    bsz = c.shape[0]
    n_col = d6 // d
    return pl.pallas_call(
        _ada_kernel,
        grid=(depth, n_col),
        in_specs=[
            pl.BlockSpec((bsz, d), lambda l, j: (0, 0)),
            pl.BlockSpec((None, d, d), lambda l, j: (l, 0, j)),
            pl.BlockSpec((None, 1, d), lambda l, j: (l, 0, j)),
        ],
        out_specs=pl.BlockSpec((None, bsz, d), lambda l, j: (l, 0, j)),
        out_shape=jax.ShapeDtypeStruct((depth, bsz, d6), F32),
        compiler_params=_params(("arbitrary", "arbitrary")),
        name="ada_mod",
    )(c, ada_w, ada_b.reshape(depth, 1, d6))


def _inproj_kernel(x_ref, mod_ref, w_ref, o_ref, *, col_chunk, q_chunk):
    sh = mod_ref[0:1, :]
    sc = mod_ref[1:2, :]
    h = (x_ref[...] * (1.0 + sc) + sh).astype(BF16)
    for j in range(o_ref.shape[1] // col_chunk):
        cols = slice(j * col_chunk, (j + 1) * col_chunk)
        y = _dot(h, w_ref[:, cols])
        if j == q_chunk:
            y = y * Q_PRESCALE
        o_ref[:, cols] = y.astype(o_ref.dtype)


def _inproj(x, mod_l, w_bf16, q_chunk):
    bsz, t, d = x.shape
    width = w_bf16.shape[1]
    tm = min(TM_PROJ, t)
    return pl.pallas_call(
        functools.partial(_inproj_kernel, col_chunk=PROJ_CHUNK, q_chunk=q_chunk),
        grid=(bsz, t // tm),
        in_specs=[
            pl.BlockSpec((None, tm, d), lambda b, i: (b, i, 0)),
            pl.BlockSpec((None, 6, d), lambda b, i: (b, 0, 0)),
            pl.BlockSpec((d, width), lambda b, i: (0, 0)),
        ],
        out_specs=pl.BlockSpec((None, tm, width), lambda b, i: (b, i, 0)),
        out_shape=jax.ShapeDtypeStruct((bsz, t, width), BF16),
        compiler_params=_params(("arbitrary", "arbitrary")),
        name="inproj",
    )(x, mod_l, w_bf16)


def _diffattn_kernel(q_ref, k_ref, v_ref, lam_ref, gain_ref, o_ref, m_sc, acc_sc, s0_sc, s1_sc,
                     p0_sc, p1_sc, a0_sc, a1_sc, kn_sc, *, tile, lam_init):
    h = pl.program_id(1)
    qi = pl.program_id(2)
    dh = DA_HEAD_DIM
    hd = 2 * dh
    tk = tile // 2
    reps = tk // LANES
    s_bufs, p_bufs, a_bufs = (s0_sc, s1_sc), (p0_sc, p1_sc), (a0_sc, a1_sc)

    lane = lax.broadcasted_iota(jnp.int32, (1, hd), 1)
    q = q_ref[...]
    zero = jnp.zeros_like(q)
    q2 = jnp.concatenate([jnp.where(lane < dh, q, zero), jnp.where(lane >= dh, q, zero)], axis=0)

    hf = jnp.full((1, 1), h + 1, jnp.int32).astype(F32)
    slope = jnp.exp2(hf * (-8.0 / DA_HEADS)) * LOG2E
    col = lax.broadcasted_iota(jnp.int32, (1, tk), 1)
    ones = jnp.ones((tk, hd), BF16)

    m_sc[...] = jnp.full(m_sc.shape, NEG_BIG, F32)
    acc_sc[...] = jnp.zeros(acc_sc.shape, F32)

    def max_half_norms(x):
        xf = x.astype(F32)
        sq = xf * xf
        out = []
        for keep in (lane < dh, lane >= dh):
            rows = jnp.sum(jnp.where(keep, sq, 0.0), axis=1, keepdims=True)
            out.append(jnp.sqrt(jnp.max(rows, axis=0, keepdims=True)))
        return out

    @pl.when(qi == 0)
    def _():
        kn_sc[...] = jnp.concatenate([jnp.broadcast_to(n, (4, LANES))
                                      for n in max_half_norms(k_ref[...])], axis=0)

    n_sub = 2 * qi + 2

    def key_start(j):
        return pl.multiple_of((n_sub - 1 - j) * tk, tk)

    every = slice(0, 2 * tile)

    def scores(j, slot, rows=every):
        ks = key_start(j)
        bias = (col + (ks - qi * tile)).astype(F32) * slope
        s_bufs[slot][rows, :] = _dot_nt(q2[rows, :], k_ref[pl.ds(ks, tk), :]) + bias

    def softmax(j, slot, masked, rows=every):
        s = s_bufs[slot][rows, :]
        if masked:
            n_rows = rows.stop - rows.start
            rowp = ((lax.broadcasted_iota(jnp.int32, (n_rows, tk), 0) + rows.start) & (tile - 1)) \
                + qi * tile
            colp = lax.broadcasted_iota(jnp.int32, (n_rows, tk), 1) + key_start(j)
            s = jnp.where(colp <= rowp, s, NEG_BIG)
        m_old = m_sc[rows, :]
        m_new = jnp.maximum(m_old, jnp.max(s, axis=1, keepdims=True))
        p_bufs[slot][rows, :] = jnp.exp2(s - jnp.concatenate([m_new] * reps, axis=1)).astype(BF16)
        a_bufs[slot][rows, :] = jnp.exp2(m_old - m_new)
        m_sc[rows, :] = m_new

    def values(j, slot, rows=every):
        v_aug = jnp.concatenate([v_ref[pl.ds(key_start(j), tk), :], ones], axis=1)
        alpha = a_bufs[slot][rows, :]
        acc_sc[rows, :] = (jnp.concatenate([alpha, alpha], axis=1) * acc_sc[rows, :]
                           + _dot(p_bufs[slot][rows, :], v_aug))

    late = (slice(tile // 2, tile), slice(tile + tile // 2, 2 * tile))

    def first(stage, *args):
        for rows in late:
            stage(0, 0, *args, rows)

    def softmax_frozen(slot, rows=every):
        m_rep = jnp.concatenate([m_sc[rows, :]] * reps, axis=1)
        p_bufs[slot][rows, :] = jnp.exp2(s_bufs[slot][rows, :] - m_rep).astype(BF16)

    def values_frozen(j, slot, rows=every):
        v_aug = jnp.concatenate([v_ref[pl.ds(key_start(j), tk), :], ones], axis=1)
        acc_sc[rows, :] += _dot(p_bufs[slot][rows, :], v_aug)

    first(scores)
    scores(1, 1)
    first(softmax, True)

    @pl.when(qi == 0)
    def _():
        softmax(1, 1, True)
        first(values)
        values(1, 1)

    @pl.when(qi > 0)
    def _():
        scores(2, 0)
        softmax(1, 1, True)
        first(values)
        scores(3, 1)
        softmax(2, 0, False)
        values(1, 1)
        qn = max_half_norms(q)
        qk_max = jnp.maximum(qn[0] * kn_sc[0:1, 0:1], qn[1] * kn_sc[4:5, 0:1])
        m_min = jnp.min(m_sc[...], axis=0, keepdims=True)[:, 0:1]
        reach = (qk_max - m_min + DA_DEAD_LOG2) / slope
        first_dead = jnp.floor((reach - 1.0) / tk) + 3.0
        first_dead = jnp.max(jnp.clip(first_dead, 0.0, 1e6)).astype(jnp.int32)
        pairs_end = jnp.maximum(2, jnp.minimum(qi + 1, (first_dead + 1) // 2))

        freeze = jnp.max(jnp.where(qk_max - m_min <= DA_FREEZE_LOG2, 1.0, 0.0)) > 0.5
        freeze = jnp.logical_and(freeze, pairs_end > 2)

        @pl.when(jnp.logical_not(freeze))
        def _():
            def body(i, carry):
                t = 2 * i
                scores(t, 0)
                softmax(t - 1, 1, False)
                values(t - 2, 0)
                scores(t + 1, 1)
                softmax(t, 0, False)
                values(t - 1, 1)
                return carry

            lax.fori_loop(2, pairs_end, body, 0)
            t = 2 * pairs_end
            softmax(t - 1, 1, False)
            values(t - 2, 0)
            values(t - 1, 1)

        @pl.when(freeze)
        def _():
            scores(4, 0)
            softmax_frozen(1)
            values(2, 0)
            scores(5, 1)
            softmax_frozen(0)
            values_frozen(3, 1)

            def body(i, carry):
                t = 2 * i
                scores(t, 0)
                softmax_frozen(1)
                values_frozen(t - 2, 0)
                scores(t + 1, 1)
                softmax_frozen(0)
                values_frozen(t - 1, 1)
                return carry

            lax.fori_loop(3, pairs_end, body, 0)
            t = 2 * pairs_end
            softmax_frozen(1)
            values_frozen(t - 2, 0)
            values_frozen(t - 1, 1)

    lv = lam_ref[...].astype(F32)
    dots = jnp.sum(lv[0:1, :] * lv[1:2, :], axis=1, keepdims=True)
    dots2 = jnp.sum(lv[2:3, :] * lv[3:4, :], axis=1, keepdims=True)
    lam = jnp.exp(dots) - jnp.exp(dots2) + lam_init
    acc = acc_sc[...]
    o0 = acc[0:tile, 0:hd] / acc[0:tile, hd:2 * hd]
    o1 = acc[tile:2 * tile, 0:hd] / acc[tile:2 * tile, hd:2 * hd]
    o = o0 - lam * o1
    ms = jnp.mean(o * o, axis=1, keepdims=True)
    o = o * lax.rsqrt(ms + 1e-6) * gain_ref[...] * (1.0 - lam_init)
    o_ref[...] = o.astype(o_ref.dtype)


def _diff_attention(proj, diff_lambda, diff_gain, lam_init):
    bsz, t, _ = proj.shape
    tile = min(TQ_ATT, t)
    hd = 2 * DA_HEAD_DIM
    kern = functools.partial(_diffattn_kernel, tile=tile, lam_init=lam_init)
    return pl.pallas_call(
        kern,
        grid=(bsz, DA_HEADS, t // tile),
        in_specs=[
            pl.BlockSpec((None, tile, hd), lambda b, h, i: (b, i, h)),
            pl.BlockSpec((None, t, hd), lambda b, h, i: (b, 0, DA_HEADS + h)),
            pl.BlockSpec((None, t, hd), lambda b, h, i: (b, 0, 2 * DA_HEADS + h)),
            pl.BlockSpec((4, DA_HEAD_DIM), lambda b, h, i: (0, 0)),
            pl.BlockSpec((1, hd), lambda b, h, i: (0, 0)),
        ],
        out_specs=pl.BlockSpec((None, tile, hd), lambda b, h, i: (b, i, h)),
        out_shape=jax.ShapeDtypeStruct((bsz, t, DA_HEADS * hd), BF16),
        scratch_shapes=[
            pltpu.VMEM((2 * tile, LANES), F32),
            pltpu.VMEM((2 * tile, 2 * hd), F32),
            pltpu.VMEM((2 * tile, tile // 2), F32),
            pltpu.VMEM((2 * tile, tile // 2), F32),
            pltpu.VMEM((2 * tile, tile // 2), BF16),
            pltpu.VMEM((2 * tile, tile // 2), BF16),
            pltpu.VMEM((2 * tile, LANES), F32),
            pltpu.VMEM((2 * tile, LANES), F32),
            pltpu.VMEM((8, LANES), F32),
        ],
        compiler_params=_params(("arbitrary", "arbitrary", "arbitrary")),
        name="diff_attn",
    )(proj, proj, proj, diff_lambda, diff_gain.reshape(1, hd))


def _hgrn_kernel(q_ref, f_ref, i_ref, g_ref, gamma_ref, gain_ref, o_ref, st_sc, *, layer):
    @pl.when(pl.program_id(1) == 0)
    def _():
        st_sc[...] = jnp.zeros(st_sc.shape, F32)

    gam = gamma_ref[...].astype(F32)
    e = jnp.exp(gam - jnp.max(gam, axis=0, keepdims=True))
    sm = e / jnp.sum(e, axis=0, keepdims=True)
    lb_all = jnp.sum(sm[0:layer + 1, :], axis=0, keepdims=True)

    c = HG_CHUNK
    row = lax.broadcasted_iota(jnp.int32, (c, c), 0)
    col = lax.broadcasted_iota(jnp.int32, (c, c), 1)
    tril = col <= row
    tril_bf = jnp.where(tril, 1.0, 0.0).astype(BF16)
    gain = gain_ref[...]

    heads = range(HG_HEADS)
    hcols = [slice(h * HG_DK, (h + 1) * HG_DK) for h in heads]
    lbs = [lb_all[:, hc] for hc in hcols]
    for n in range(q_ref.shape[0] // c):
        rows = slice(n * c, (n + 1) * c)
        sig = [_sigmoid(f_ref[rows, hc].astype(F32)) for hc in hcols]
        logf = [jnp.log(lbs[h] + (1.0 - lbs[h]) * sig[h]) for h in heads]
        kk = [(1.0 - lbs[h]) * (1.0 - sig[h]) for h in heads]
        parts = [_split3(x) for x in logf]
        b = [_dot(tril_bf, p[0]) + _dot(tril_bf, p[1]) for p in parts]
        b_mid = [x[c // 2 - 1:c // 2, :] for x in b]
        b_last = [x[c - 1:c, :] for x in b]
        qh = [q_ref[rows, hc].astype(F32) for hc in hcols]
        qs = [x * _sigmoid(x) for x in qh]
        v = [i_ref[rows, hc] for hc in hcols]
        qa = [qs[h] * jnp.exp(b[h] - b_mid[h]) for h in heads]
        ka = [kk[h] * jnp.exp(b_mid[h] - b[h]) for h in heads]
        att = [_dot_nt(qa[h].astype(BF16), ka[h].astype(BF16)) for h in heads]
        att = [jnp.where(tril, x, 0.0).astype(BF16) for x in att]
        o_intra = [_dot(att[h], v[h]) for h in heads]
        kd = [(ka[h] * jnp.exp(b_last[h] - b_mid[h])).astype(BF16) for h in heads]
        ds_t = [_dot(v[h].T, kd[h]) for h in heads]
        st = [st_sc[h] for h in heads]
        o_inter = [_dot_nt((qa[h] * jnp.exp(b_mid[h])).astype(BF16), st[h].astype(BF16))
                   for h in heads]
        for h in heads:
            st_sc[h] = st[h] * jnp.exp(b_last[h]) + ds_t[h]
        for h in heads:
            o = o_intra[h] + o_inter[h]
            gh = g_ref[rows, hcols[h]].astype(F32)
            ms = jnp.mean(o * o, axis=1, keepdims=True)
            o = o * lax.rsqrt(ms + 1e-6) * gain * (gh * _sigmoid(gh))
            o_ref[rows, hcols[h]] = o.astype(o_ref.dtype)


def _hgrn2(proj, hgrn_gamma, hgrn_gain, layer):
    bsz, t, _ = proj.shape
    tt = min(T_HG, t)
    width = HG_HEADS * HG_DK
    base = 3 * DA_HEADS * 2 * DA_HEAD_DIM // width
    spec = lambda k: pl.BlockSpec((None, tt, width), lambda b, i: (b, i, base + k))
    return pl.pallas_call(
        functools.partial(_hgrn_kernel, layer=layer),
        grid=(bsz, t // tt),
        in_specs=[
            spec(0), spec(1), spec(2), spec(3),
            pl.BlockSpec((hgrn_gamma.shape[0], width), lambda b, i: (0, 0)),
            pl.BlockSpec((1, HG_DK), lambda b, i: (0, 0)),
        ],
        out_specs=pl.BlockSpec((None, tt, width), lambda b, i: (b, i, 0)),
        out_shape=jax.ShapeDtypeStruct((bsz, t, width), BF16),
        scratch_shapes=[pltpu.VMEM((HG_HEADS, HG_DK, HG_DK), F32)],
        compiler_params=_params(("arbitrary", "arbitrary")),
        name="hgrn2",
    )(proj, proj, proj, proj, hgrn_gamma, hgrn_gain.reshape(1, HG_DK))


def _route(logits_t):
    mx = jnp.max(logits_t, axis=0, keepdims=True)
    ex = jnp.exp(logits_t - mx)
    probs = ex / jnp.sum(ex, axis=0, keepdims=True)
    p = [probs[e:e + 1, :] for e in range(N_EXPERTS)]
    g = E_PER_GROUP
    scores = []
    for gi in range(N_GROUPS):
        pg = p[gi * g:(gi + 1) * g]
        best = None
        for a in range(g):
            for b in range(a + 1, g):
                pair = pg[a] + pg[b]
                best = pair if best is None else jnp.maximum(best, pair)
        scores.append(best)
    group_id = jnp.zeros_like(p[0])
    gates = [jnp.zeros_like(p[0]) for _ in range(g)]
    for gi in range(N_GROUPS):
        sel = None
        for gj in range(N_GROUPS):
            if gj == gi:
                continue
            cond = (scores[gi] > scores[gj]) if gj < gi else (scores[gi] >= scores[gj])
            sel = cond if sel is None else (sel & cond)
        group_id = jnp.where(sel, float(gi), group_id)
        pg = p[gi * g:(gi + 1) * g]
        chosen = []
        for a in range(g):
            rank = jnp.zeros_like(pg[a])
            for b in range(g):
                if b == a:
                    continue
                ahead = (pg[b] >= pg[a]) if b < a else (pg[b] > pg[a])
                rank = rank + jnp.where(ahead, 1.0, 0.0)
            chosen.append(sel & (rank < 2.0))
        denom = None
        for a in range(g):
            term = jnp.where(chosen[a], pg[a], 0.0)
            denom = term if denom is None else denom + term
        for a in range(g):
            gates[a] = jnp.where(chosen[a], pg[a] / denom, gates[a])
    return group_id, gates


def _outproj_kernel(a_ref, b_ref, x_ref, mod_ref, w_ref, lng_ref, lnb_ref, rwt_ref, rb_ref,
                    xo_ref, h_ref, row_ref, col_ref, *, alpha):
    half = a_ref.shape[1]
    tm = a_ref.shape[0]
    g1 = mod_ref[2:3, :]
    sh2 = mod_ref[3:4, :]
    sc2 = mod_ref[4:5, :]
    rw = rwt_ref[...]
    w_hi, w_mid, _ = _split3(rw)
    chunks = [slice(i * tm // ROW_CHUNKS, (i + 1) * tm // ROW_CHUNKS)
              for i in range(ROW_CHUNKS)]
    y = [_dot(a_ref[rs, :], w_ref[0:half, :]) + _dot(b_ref[rs, :], w_ref[half:2 * half, :])
         for rs in chunks]
    r = [alpha * x_ref[rs, :] + (1.0 + g1) * yy for rs, yy in zip(chunks, y)]
    rc = [rr - jnp.mean(rr, axis=1, keepdims=True) for rr in r]
    var = [jnp.mean(cc * cc, axis=1, keepdims=True) for cc in rc]
    xn = [cc * lax.rsqrt(vv + 1e-5) * lng_ref[...] + lnb_ref[...] for cc, vv in zip(rc, var)]
    h2 = [xx * (1.0 + sc2) + sh2 for xx in xn]
    for rs, xx, hh in zip(chunks, xn, h2):
        xo_ref[rs, :] = xx
        h_ref[rs, :] = hh.astype(BF16)
    split = [_split3(hh) for hh in h2]
    logits_t = jnp.concatenate(
        [_dot_nt(w_hi, s[0]) + _dot_nt(w_hi, s[1]) + _dot_nt(w_mid, s[0]) for s in split],
        axis=1) + rb_ref[...]
    group_id, gates = _route(logits_t)
    sel = [jnp.where(group_id == float(gi), 1.0, 0.0) for gi in range(N_GROUPS)]
    onehot = jnp.concatenate(sel + [jnp.zeros((8 - N_GROUPS, tm), F32)], axis=0).astype(BF16)
    src = lax.broadcasted_iota(jnp.int32, (tm, tm), 0)
    dst = lax.broadcasted_iota(jnp.int32, (tm, tm), 1)
    earlier = jnp.where(src < dst, 1.0, 0.0).astype(BF16)
    counts = _dot(onehot, earlier)
    rank = sel[0] * counts[0:1, :]
    for gi in range(1, N_GROUPS):
        rank = rank + sel[gi] * counts[gi:gi + 1, :]
    info = jnp.concatenate(gates + [group_id, rank], axis=0)
    row_ref[...] = jnp.concatenate(
        [group_id, rank, jnp.zeros((8 - 2, tm), F32)], axis=0)
    pad = jnp.zeros((LANES - info.shape[0], tm), F32)
    col_ref[...] = jnp.concatenate([info, pad], axis=0).T


def _outproj(a, b, x, mod_l, w_bf16, ln_g, ln_b, router_w, router_b, alpha):
    bsz, t, d = x.shape
    half = a.shape[2]
    tm = min(T_BLK, t)
    tok = lambda width: pl.BlockSpec((None, tm, width), lambda bi, i: (bi, i, 0))
    full = lambda r, c: pl.BlockSpec((r, c), lambda bi, i: (0, 0))
    return pl.pallas_call(
        functools.partial(_outproj_kernel, alpha=alpha),
        grid=(bsz, t // tm),
        in_specs=[
            tok(half), tok(half), tok(d),
            pl.BlockSpec((None, 6, d), lambda bi, i: (bi, 0, 0)),
            full(2 * half, d), full(1, d), full(1, d), full(N_EXPERTS, d), full(N_EXPERTS, 1),
        ],
        out_specs=[tok(d), tok(d), pl.BlockSpec((None, 8, tm), lambda bi, i: (bi, 0, i)),
                   tok(LANES)],
        out_shape=[
            jax.ShapeDtypeStruct((bsz, t, d), F32),
            jax.ShapeDtypeStruct((bsz, t, d), BF16),
            jax.ShapeDtypeStruct((bsz, 8, t), F32),
            jax.ShapeDtypeStruct((bsz, t, LANES), F32),
        ],
        compiler_params=_params(("arbitrary", "arbitrary")),
        name="outproj_ln_route",
    )(a, b, x, mod_l, w_bf16, ln_g.reshape(1, d), ln_b.reshape(1, d), router_w.T,
      router_b.reshape(N_EXPERTS, 1))


def _slab_rows(tm):
    extra = -(-(tm - MOE_ROWS_MAIN) // MOE_ROWS_EXTRA)
    return MOE_ROWS_MAIN + max(extra, 0) * MOE_ROWS_EXTRA


def _extra_chunks(count):
    return (jnp.maximum(count - MOE_ROWS_MAIN, 0) + MOE_ROWS_EXTRA - 1) // MOE_ROWS_EXTRA


def _moe_expert_kernel(cnt_ref, h_ref, row_ref, col_ref, wg_ref, wu_ref, wd_ref, zm_ref, zx_ref):
    g = pl.program_id(0)
    blk = pl.program_id(1)
    count = cnt_ref[g * pl.num_programs(1) + blk]
    mine = row_ref[0:1, :] == g.astype(F32)
    rank = row_ref[1:2, :]
    info = col_ref[...]
    info_hi = info.astype(BF16)
    info_lo = (info - info_hi.astype(F32)).astype(BF16)

    def run_rows(r0, m, out_ref, out_r0):
        rid = (lax.broadcasted_iota(jnp.int32, (m, 1), 0) + r0).astype(F32)
        pick = _onehot((rank == rid) & mine)
        xs = _dot(pick, h_ref[...]).astype(BF16)
        gm = _dot(pick, info_hi) + _dot(pick, info_lo)
        acc = None
        for j in range(E_PER_GROUP):
            a = _dot(xs, wg_ref[j])
            u = _dot(xs, wu_ref[j])
            he = (a * _sigmoid(a) * u * gm[:, j:j + 1]).astype(BF16)
            part = _dot(he, wd_ref[j])
            acc = part if acc is None else acc + part
        out_ref[pl.ds(out_r0, m), :] = acc.astype(out_ref.dtype)

    sizes = tuple(range(MOE_ROWS_SMALL, MOE_ROWS_MAIN + 1, MOE_ROWS_STEP))
    for below, size in zip((-1,) + sizes[:-1], sizes):
        fits = count > below
        if size != MOE_ROWS_MAIN:
            fits = jnp.logical_and(fits, count <= size)

        @pl.when(fits)
        def _(size=size):
            run_rows(0, size, zm_ref, 0)
            if size != MOE_ROWS_MAIN:
                zm_ref[size:MOE_ROWS_MAIN, :] = jnp.zeros(
                    (MOE_ROWS_MAIN - size, zm_ref.shape[1]), zm_ref.dtype)

    zx_ref[...] = jnp.zeros(zx_ref.shape, zx_ref.dtype)

    def body(i, carry):
        off = pl.multiple_of(i * MOE_ROWS_EXTRA, 16)
        run_rows(MOE_ROWS_MAIN + off, MOE_ROWS_EXTRA, zx_ref, off)
        return carry

    lax.fori_loop(0, _extra_chunks(count), body, 0)


def _moe_combine_kernel(cnt_ref, xidx_ref, zm0_ref, zm1_ref, zm2_ref, zm3_ref, zx0_ref, zx1_ref,
                        zx2_ref, zx3_ref, col_ref, x_ref, mod_ref, lng_ref, lnb_ref, o_ref, y_sc,
                        *, alpha):
    del xidx_ref
    blk = pl.program_id(0) * pl.num_programs(1) + pl.program_id(1)
    n_blk = pl.num_programs(0) * pl.num_programs(1)
    zm_refs = (zm0_ref, zm1_ref, zm2_ref, zm3_ref)
    zx_refs = (zx0_ref, zx1_ref, zx2_ref, zx3_ref)
    main = MOE_ROWS_MAIN
    grp = col_ref[:, E_PER_GROUP:E_PER_GROUP + 1]
    rank = col_ref[:, E_PER_GROUP + 1:E_PER_GROUP + 2]
    y_sc[...] = jnp.zeros(y_sc.shape, F32)
    lane_x = lax.broadcasted_iota(jnp.int32, (1, MOE_ROWS_EXTRA), 1).astype(F32)
    for gi in range(N_GROUPS):
        def body(i, carry, gi=gi):
            off = pl.multiple_of(i * MOE_ROWS_EXTRA, 16)
            hit = (grp == float(gi)) & ((rank - (main + off).astype(F32)) == lane_x)
            y_sc[...] += _dot(_onehot(hit), zx_refs[gi][pl.ds(off, MOE_ROWS_EXTRA), :])
            return carry

        lax.fori_loop(0, _extra_chunks(cnt_ref[gi * n_blk + blk]), body, 0)

    where_to = jnp.where(rank < float(main), grp * float(main) + rank, -1.0)
    lane = lax.broadcasted_iota(jnp.int32, (1, N_GROUPS * main), 1).astype(F32)
    z_all = jnp.concatenate([zr[...] for zr in zm_refs], axis=0)
    g2 = mod_ref[5:6, :]
    tm = x_ref.shape[0]
    chunks = [slice(i * tm // ROW_CHUNKS, (i + 1) * tm // ROW_CHUNKS)
              for i in range(ROW_CHUNKS)]
    y = [y_sc[rs, :] + _dot(_onehot(where_to[rs, :] == lane), z_all) for rs in chunks]
    r = [alpha * x_ref[rs, :] + (1.0 + g2) * yy for rs, yy in zip(chunks, y)]
    rc = [rr - jnp.mean(rr, axis=1, keepdims=True) for rr in r]
    var = [jnp.mean(cc * cc, axis=1, keepdims=True) for cc in rc]
    for rs, cc, vv in zip(chunks, rc, var):
        o_ref[rs, :] = cc * lax.rsqrt(vv + 1e-5) * lng_ref[...] + lnb_ref[...]


def _moe(h2, rowinfo, colinfo, x, mod_l, layer, wg, wu, wd, ln_g, ln_b, alpha):
    bsz, t, d = x.shape
    tm = min(T_BLK, t)
    nb = t // tm
    n_blk = bsz * nb
    dff = wg.shape[3]
    slab = _slab_rows(tm)
    group_of = rowinfo[:, 0, :].reshape(1, n_blk, tm)
    counts = jnp.sum(group_of == jnp.arange(N_GROUPS, dtype=F32).reshape(N_GROUPS, 1, 1), axis=2)
    counts = counts.astype(jnp.int32)
    needed = jnp.where(counts > MOE_ROWS_MAIN, jnp.arange(n_blk, dtype=jnp.int32), 0)
    extra_block = lax.cummax(needed, axis=1).reshape(N_GROUPS * n_blk)
    counts = counts.reshape(N_GROUPS * n_blk)
    extra = slab - MOE_ROWS_MAIN

    z_main, z_extra = pl.pallas_call(
        _moe_expert_kernel,
        grid_spec=pltpu.PrefetchScalarGridSpec(
            num_scalar_prefetch=1,
            grid=(N_GROUPS, n_blk),
            in_specs=[
                pl.BlockSpec((None, tm, d), lambda g, i, c: (i // nb, i % nb, 0)),
                pl.BlockSpec((None, 8, tm), lambda g, i, c: (i // nb, 0, i % nb)),
                pl.BlockSpec((None, tm, LANES), lambda g, i, c: (i // nb, i % nb, 0)),
                pl.BlockSpec((None, E_PER_GROUP, d, dff), lambda g, i, c: (layer, g, 0, 0)),
                pl.BlockSpec((None, E_PER_GROUP, d, dff), lambda g, i, c: (layer, g, 0, 0)),
                pl.BlockSpec((None, E_PER_GROUP, dff, d), lambda g, i, c: (layer, g, 0, 0)),
            ],
            out_specs=[
                pl.BlockSpec((None, None, MOE_ROWS_MAIN, d), lambda g, i, c: (g, i, 0, 0)),
                pl.BlockSpec((None, None, extra, d), lambda g, i, c: (g, i, 0, 0)),
            ],
        ),
        out_shape=[
            jax.ShapeDtypeStruct((N_GROUPS, n_blk, MOE_ROWS_MAIN, d), BF16),
            jax.ShapeDtypeStruct((N_GROUPS, n_blk, extra, d), BF16),
        ],
        compiler_params=_params(("arbitrary", "arbitrary")),
        name="moe_experts",
    )(counts, h2, rowinfo, colinfo, wg, wu, wd)

    zm_spec = lambda gi: pl.BlockSpec((None, None, MOE_ROWS_MAIN, d),
                                      lambda b, i, c, xb: (gi, b * nb + i, 0, 0))
    zx_spec = lambda gi: pl.BlockSpec((None, None, extra, d),
                                      lambda b, i, c, xb: (gi, xb[gi * n_blk + b * nb + i], 0, 0))
    tok = lambda width: pl.BlockSpec((None, tm, width), lambda b, i, c, xb: (b, i, 0))
    return pl.pallas_call(
        functools.partial(_moe_combine_kernel, alpha=alpha),
        grid_spec=pltpu.PrefetchScalarGridSpec(
            num_scalar_prefetch=2,
            grid=(bsz, nb),
            in_specs=[
                zm_spec(0), zm_spec(1), zm_spec(2), zm_spec(3),
                zx_spec(0), zx_spec(1), zx_spec(2), zx_spec(3), tok(LANES), tok(d),
                pl.BlockSpec((None, 6, d), lambda b, i, c, xb: (b, 0, 0)),
                pl.BlockSpec((1, d), lambda b, i, c, xb: (0, 0)),
                pl.BlockSpec((1, d), lambda b, i, c, xb: (0, 0)),
            ],
            out_specs=tok(d),
            scratch_shapes=[pltpu.VMEM((tm, d), F32)],
        ),
        out_shape=jax.ShapeDtypeStruct((bsz, t, d), F32),
        compiler_params=_params(("arbitrary", "arbitrary")),
        name="moe_combine_ln",
    )(counts, extra_block, z_main, z_main, z_main, z_main, z_extra, z_extra, z_extra, z_extra,
      colinfo, x, mod_l, ln_g.reshape(1, d), ln_b.reshape(1, d))


def _lru_kernel(x_ref, g_ref, cw_ref, cb_ref, wa_ref, ba_ref, wx_ref, bx_ref, lam_ref,
                o_ref, xpad_sc, h_sc):
    tt = x_ref.shape[0]
    pad = 8

    @pl.when(pl.program_id(1) == 0)
    def _():
        xpad_sc[0:pad, :] = jnp.zeros((pad, xpad_sc.shape[1]), F32)
        h_sc[...] = jnp.zeros(h_sc.shape, F32)

    xpad_sc[pad:pad + tt, :] = x_ref[...].astype(F32)
    xc = cb_ref[...] + jnp.zeros((tt, x_ref.shape[1]), F32)
    for j in range(CONV_WIDTH):
        off = pad - (CONV_WIDTH - 1) + j
        xc = xc + cw_ref[j:j + 1, :] * xpad_sc[off:off + tt, :]
    xpad_sc[0:pad, :] = xpad_sc[tt:tt + pad, :]

    xb = xc.astype(BF16)
    r = _sigmoid(_dot(xb, wa_ref[...]) + ba_ref[...])
    i = _sigmoid(_dot(xb, wx_ref[...]) + bx_ref[...])
    lam = lam_ref[...].astype(F32)
    softplus_neg = jnp.maximum(-lam, 0.0) + jnp.log(1.0 + jnp.exp(-jnp.abs(lam)))
    log_a = -LRU_C * r * softplus_neg
    a = jnp.exp(log_a)
    gain_sq = jnp.maximum(1.0 - jnp.exp(2.0 * log_a), 1e-12)
    u = gain_sq * lax.rsqrt(gain_sq) * (i * xc)

    groups = (tt // SUBLANES, SUBLANES, a.shape[1])
    a = a.reshape(groups)
    u = u.reshape(groups)
    rowi = lax.broadcasted_iota(jnp.int32, (1, SUBLANES, 1), 1)
    d = 1
    while d < SUBLANES:
        a_sh = jnp.where(rowi >= d, pltpu.roll(a, d, 1), 1.0)
        u_sh = jnp.where(rowi >= d, pltpu.roll(u, d, 1), 0.0)
        u = u + a * u_sh
        a = a * a_sh
        d *= 2
    a = a.reshape(tt, groups[2])
    u = u.reshape(tt, groups[2])
    gr = g_ref[...].astype(F32)
    gelu = 0.5 * gr * (1.0 + jnp.tanh(0.7978845608028654 * (gr + 0.044715 * gr * gr * gr)))
    h_prev = h_sc[...]
    out = []
    for grp in range(tt // SUBLANES):
        rows = slice(grp * SUBLANES, (grp + 1) * SUBLANES)
        h_grp = u[rows, :] + a[rows, :] * h_prev
        out.append(gelu[rows, :] * h_grp)
        h_prev = h_grp[SUBLANES - 1:SUBLANES, :]
    h_sc[...] = h_prev
    o_ref[...] = jnp.concatenate(out, axis=0).astype(o_ref.dtype)


def _rg_lru(proj, conv_w, conv_b, wa_dense, ba, wx_dense, bx, lam):
    bsz, t, _ = proj.shape
    tt = min(T_LRU, t)
    w = LRU_WIDTH
    nblk = w // LANES
    row = lambda a: a.reshape(1, w)
    full = lambda r, c: pl.BlockSpec((r, c), lambda b, i: (0, 0))
    return pl.pallas_call(
        _lru_kernel,
        grid=(bsz, t // tt),
        in_specs=[
            pl.BlockSpec((None, tt, w), lambda b, i: (b, i, 0)),
            pl.BlockSpec((None, tt, w), lambda b, i: (b, i, 1)),
            full(CONV_WIDTH, w), full(1, w), full(w, w), full(1, w), full(w, w), full(1, w),
            full(1, w),
        ],
        out_specs=pl.BlockSpec((None, tt, w), lambda b, i: (b, i, 0)),
        out_shape=jax.ShapeDtypeStruct((bsz, t, w), BF16),
        scratch_shapes=[pltpu.VMEM((tt + SUBLANES, w), F32), pltpu.VMEM((1, w), F32)],
        compiler_params=_params(("arbitrary", "arbitrary")),
        name="rg_lru",
    )(proj, proj, conv_w, row(conv_b), wa_dense, row(ba), wx_dense, row(bx), row(lam))


def _sb_kernel(q_ref, k_ref, v_ref, o_ref, r_sc, acc_sc, *bufs, tile):
    qi = pl.program_id(2)
    d = SB_HEAD_DIM
    tk = tile // 2
    z_bufs, lb_bufs, l_bufs, w_bufs = (bufs[i * SB_SETS:(i + 1) * SB_SETS] for i in range(4))
    lane = lax.broadcasted_iota(jnp.int32, (1, 2 * d), 1)
    q = q_ref[...]
    zero = jnp.zeros_like(q)
    q2 = jnp.concatenate([jnp.where(lane < d, q, zero), jnp.where(lane >= d, q, zero)], axis=0)
    rj = lax.broadcasted_iota(jnp.int32, (tk, tk), 0)
    cs = lax.broadcasted_iota(jnp.int32, (tk, tk), 1)
    upper = jnp.where(rj > cs, 1.0, 0.0).astype(BF16)

    r_sc[...] = jnp.zeros(r_sc.shape, F32)
    acc_sc[...] = jnp.zeros(acc_sc.shape, F32)
    n_sub = 2 * qi + 2
    every = slice(0, 2 * tile)
    per_head = (slice(0, tile), slice(tile, 2 * tile))

    def key_start(j):
        return pl.multiple_of((n_sub - 1 - j) * tk, tk)

    def strict_mask(j, rows):
        n_rows = rows.stop - rows.start
        rowp = (lax.broadcasted_iota(jnp.int32, (n_rows, tk), 0) + rows.start) & (tile - 1)
        colp = lax.broadcasted_iota(jnp.int32, (n_rows, tk), 1) + (1 - j) * tk
        return colp < rowp

    def logits(j, b, rows=every):
        z_bufs[b][rows, :] = _dot_nt(q2[rows, :], k_ref[pl.ds(key_start(j), tk), :])

    def gates(j, b, rows=every, masked=False):
        z = z_bufs[b][rows, :]
        log_1m = jnp.log(1.0 + jnp.exp2(-jnp.abs(z))) * (-LOG2E) - jnp.maximum(z, 0.0)
        lb_bufs[b][rows, :] = z + log_1m
        if masked:
            log_1m = jnp.where(strict_mask(j, rows), log_1m, 0.0)
        l_bufs[b][rows, :] = log_1m.astype(BF16)

    def weights(j, b, rows=every, masked=False):
        log_1m = l_bufs[b][rows, :]
        after = _dot(log_1m, upper) + r_sc[rows, :]
        w = jnp.exp2(lb_bufs[b][rows, :] + after)
        if masked:
            w = jnp.where(strict_mask(j, rows), w, 0.0)
        w_bufs[b][rows, :] = w.astype(BF16)
        r_sc[rows, :] = after[:, 0:1] + log_1m[:, 0:1].astype(F32)

    def values(j, b, rows=every):
        acc_sc[rows, :] += _dot(w_bufs[b][rows, :], v_ref[pl.ds(key_start(j), tk), :])

    late = tuple(slice(r.start + tile // 2, r.stop) for r in per_head)
    early = tuple(slice(r.start, r.start + tile // 2) for r in per_head)

    def first(stage, **kw):
        for rows in late:
            stage(0, 0, rows, **kw)

    @pl.when(qi == 0)
    def _():
        first(logits)
        logits(1, 1)
        first(gates, masked=True)
        gates(1, 1, masked=True)
        first(weights, masked=True)
        weights(1, 1, masked=True)
        first(values)
        values(1, 1)

    @pl.when(qi > 0)
    def _():
        def alive(rows=every):
            return (jnp.max(r_sc[rows, :]) > -SB_DEAD_LOG2).astype(jnp.int32)

        def third(stage, rows_set):
            for rows in rows_set:
                stage(2, 2, rows)

        first(logits)
        logits(1, 1)
        first(gates, masked=True)
        third(logits, early)
        gates(1, 1, masked=True)
        first(weights, masked=True)
        third(gates, early)
        weights(1, 1, masked=True)
        first(values)
        third(weights, early)
        values(1, 1)
        third(values, early)

        @pl.when(jnp.maximum(alive(late[0]), alive(late[1])) > 0)
        def _():
            for stage in (logits, gates, weights, values):
                third(stage, late)

        def cond(carry):
            j, live = carry
            return (j < n_sub) & (live > 0)

        def body(carry):
            j, _ = carry
            for rows in per_head:
                logits(j, 0, rows)
            for rows in per_head:
                gates(j, 0, rows)
            for rows in per_head:
                weights(j, 0, rows)
            for rows in per_head:
                values(j, 0, rows)
            return j + 1, alive()

        lax.while_loop(cond, body, (jnp.int32(3), alive()))

    acc = acc_sc[...]
    o_ref[...] = jnp.where(lane < d, acc[0:tile, :], acc[tile:2 * tile, :]).astype(o_ref.dtype)


def _sb_attention(proj):
    bsz, t, _ = proj.shape
    tq = min(T_SB, t)
    pairs = SB_HEADS // 2
    wblk = 2 * SB_HEAD_DIM
    base = 2 * LRU_WIDTH // wblk
    return pl.pallas_call(
        functools.partial(_sb_kernel, tile=tq),
        grid=(bsz, pairs, t // tq),
        in_specs=[
            pl.BlockSpec((None, tq, wblk), lambda b, h, i: (b, i, base + h)),
            pl.BlockSpec((None, t, wblk), lambda b, h, i: (b, 0, base + pairs + h)),
            pl.BlockSpec((None, t, wblk), lambda b, h, i: (b, 0, base + 2 * pairs + h)),
        ],
        out_specs=pl.BlockSpec((None, tq, wblk), lambda b, h, i: (b, i, h)),
        out_shape=jax.ShapeDtypeStruct((bsz, t, SB_HEADS * SB_HEAD_DIM), BF16),
        scratch_shapes=[pltpu.VMEM((2 * tq, 1), F32), pltpu.VMEM((2 * tq, wblk), F32)]
        + [pltpu.VMEM((2 * tq, tq // 2), F32)] * (2 * SB_SETS)
        + [pltpu.VMEM((2 * tq, tq // 2), BF16)] * (2 * SB_SETS),
        compiler_params=_params(("arbitrary", "arbitrary", "arbitrary")),
        name="sb_attn",
    )(proj, proj, proj)


def _block_diag(w):
    g, n, _ = w.shape
    eye = jnp.eye(g, dtype=w.dtype)
    return (eye[:, None, :, None] * w[:, :, None, :]).reshape(g * n, g * n)


def kernel(x, c, ada_w, ada_b, ln_g, ln_b, even_w_in, even_w_out, diff_lambda, diff_gain, hgrn_gamma, hgrn_gain, odd_w_in, odd_w_out, conv_w, conv_b, lru_wa, lru_ba, lru_wx, lru_bx, lru_lambda, router_w, router_b, moe_w_gate, moe_w_up, moe_w_down):
    depth = ada_w.shape[0]
    bsz, t, d = x.shape
    alpha = (2.0 * depth) ** 0.25
    mod = _ada_mod(c, ada_w, ada_b).reshape(depth, bsz, 6, d)
    w_gate, w_up, w_down = (w.astype(BF16) for w in (moe_w_gate, moe_w_up, moe_w_down))
    for l in range(depth):
        j = l // 2
        mod_l = mod[l]
        if l % 2 == 0:
            lam_init = 0.8 - 0.6 * math.exp(-0.3 * l)
            proj = _inproj(x, mod_l, even_w_in[j].astype(BF16), q_chunk=0)
            mix_a = _diff_attention(proj, diff_lambda[j], diff_gain[j], lam_init)
            mix_b = _hgrn2(proj, hgrn_gamma, hgrn_gain[j], l)
            w_out = even_w_out[j]
        else:
            proj = _inproj(x, mod_l, odd_w_in[j].astype(BF16), q_chunk=2 * LRU_WIDTH // PROJ_CHUNK)
            mix_a = _rg_lru(proj, conv_w[j], conv_b[j], _block_diag(lru_wa[j]).astype(BF16),
                            lru_ba[j], _block_diag(lru_wx[j]).astype(BF16), lru_bx[j],
                            lru_lambda[j])
            mix_b = _sb_attention(proj)
            w_out = odd_w_out[j]
        x, h2, rowinfo, colinfo = _outproj(mix_a, mix_b, x, mod_l, w_out.astype(BF16),
                                           ln_g[l, 0], ln_b[l, 0], router_w, router_b, alpha)
        x = _moe(h2, rowinfo, colinfo, x, mod_l, l, w_gate, w_up, w_down, ln_g[l, 1], ln_b[l, 1],
                 alpha)
    return x
```
